```python
import math
import jax, jax.numpy as jnp
from jax import lax
import numpy as np


D_MODEL = 1024
BATCH = 8
SEQ = 2048
DEPTH = 4

D_FF = 2816
DEEPNORM_ALPHA = (2.0 * DEPTH) ** 0.25
DEEPNORM_BETA = (8.0 * DEPTH) ** -0.25
LN_EPS = 1e-5
RMS_EPS = 1e-6
CONV_K = 4

GDN_QK_HEADS = 4
GDN_V_HEADS = 8
GDN_DK = 128
GDN_DV = 128
GDN_CHUNK = 64
GDN_QK_W = GDN_QK_HEADS * GDN_DK
GDN_V_W = GDN_V_HEADS * GDN_DV
GDN_CONV_DIM = 2 * GDN_QK_W + GDN_V_W

SSD_D_INNER = D_MODEL
SSD_HEADDIM = 64
SSD_HEADS = SSD_D_INNER // SSD_HEADDIM
SSD_GROUPS = 2
SSD_STATE = 128
SSD_CHUNK = 128
SSD_CONV_DIM = SSD_D_INNER + 2 * SSD_GROUPS * SSD_STATE

HYB_IN = GDN_CONV_DIM + GDN_V_W + 2 * GDN_V_HEADS + SSD_D_INNER + SSD_CONV_DIM + SSD_HEADS
HYB_OUT = GDN_V_W + SSD_D_INNER

NSA_HEADS = 16
NSA_GROUPS = 2
NSA_DK = 64
NSA_DV = 64
ROPE_DIM = NSA_DK // 4
ROPE_THETA = 500000.0
CMP_LEN = 32
CMP_STRIDE = 16
CMP_HIDDEN = 256
SEL_LEN = 64
SEL_TOPK = 8
SEL_LOCAL = 2
FORCE_SCORE = 1e4
WINDOW = 512
Q_BLOCK = 128
NSA_IN = NSA_HEADS * NSA_DK + 3 * NSA_GROUPS * (NSA_DK + NSA_DV) + 3 * NSA_HEADS

kernel_name = 'hybrid_gdn_ssd_nsa_macaron_deepnorm'


def layer_norm(x, g, b):
    xf = x.astype(jnp.float32)
    mu = jnp.mean(xf, -1, keepdims=True)
    var = jnp.mean(jnp.square(xf - mu), -1, keepdims=True)
    return ((xf - mu) * lax.rsqrt(var + LN_EPS) * g + b).astype(x.dtype)


def rms_normalize(x):
    xf = x.astype(jnp.float32)
    return xf * lax.rsqrt(jnp.mean(jnp.square(xf), -1, keepdims=True) + RMS_EPS)


def l2_normalize(x):
    xf = x.astype(jnp.float32)
    return xf * lax.rsqrt(jnp.sum(jnp.square(xf), -1, keepdims=True) + 1e-6)


def swiglu(x, w_in, w_out):
    gate, up = jnp.split(x @ w_in, 2, axis=-1)
    return (jax.nn.silu(gate) * up) @ w_out


def causal_conv(x, w):
    k, c = w.shape
    return lax.conv_general_dilated(x, w[:, None, :].astype(x.dtype), window_strides=(1,),
                                    padding=[(k - 1, 0)], dimension_numbers=('NWC', 'WIO', 'NWC'),
                                    feature_group_count=c)


def decay_matrix(cs):
    n = cs.shape[-1]
    mask = np.tril(np.ones((n, n), dtype=bool))
    return jnp.exp(jnp.where(mask, cs[..., :, None] - cs[..., None, :], -jnp.inf))


def masked_softmax(s, mask):
    s = jnp.where(mask, s.astype(jnp.float32), -jnp.inf)
    m = jnp.max(s, -1, keepdims=True)
    m = jnp.where(jnp.isfinite(m), m, 0.0)
    e = jnp.exp(s - m)
    return e / jnp.maximum(jnp.sum(e, -1, keepdims=True), 1e-30)


def partial_rope(x, positions):
    half = ROPE_DIM // 2
    inv_freq = jnp.asarray(ROPE_THETA ** (-np.arange(half) / half), jnp.float32)
    ang = positions.astype(jnp.float32)[..., None] * inv_freq
    cos = jnp.cos(ang)[:, :, None, :]
    sin = jnp.sin(ang)[:, :, None, :]
    xf = x.astype(jnp.float32)
    x1, x2, rest = xf[..., :half], xf[..., half:ROPE_DIM], xf[..., ROPE_DIM:]
    out = jnp.concatenate([x1 * cos - x2 * sin, x2 * cos + x1 * sin, rest], axis=-1)
    return out.astype(x.dtype)


def gated_delta_rule(q, k, v, g, beta):
    bsz, t_len, h, dk = q.shape
    dv = v.shape[-1]
    c = GDN_CHUNK
    n = t_len // c

    def to_chunks(a):
        return a.astype(jnp.float32).reshape(bsz, n, c, h, -1).transpose(0, 3, 1, 2, 4)

    q, k, v = to_chunks(q), to_chunks(k), to_chunks(v)
    g = g.astype(jnp.float32).reshape(bsz, n, c, h).transpose(0, 3, 1, 2)
    beta = beta.astype(jnp.float32).reshape(bsz, n, c, h).transpose(0, 3, 1, 2)
    g_cum = jnp.cumsum(g, axis=-1)
    decay = decay_matrix(g_cum)
    strict = np.tril(np.ones((c, c), dtype=bool), -1)
    k_beta = k * beta[..., None]
    a_low = jnp.where(strict, jnp.einsum('bhncd,bhnsd->bhncs', k_beta, k) * decay, 0.0)
    rhs = jnp.concatenate([v * beta[..., None], k_beta * jnp.exp(g_cum)[..., None]], axis=-1)
    sol = lax.linalg.triangular_solve(a_low, rhs, left_side=True, lower=True, unit_diagonal=True)
    u, w = sol[..., :dv], sol[..., dv:]
    attn = jnp.einsum('bhncd,bhnsd->bhncs', q, k) * decay
    q_dec = q * jnp.exp(g_cum)[..., None]
    k_dec = k * jnp.exp(g_cum[..., -1:] - g_cum)[..., None]
    chunk_dec = jnp.exp(g_cum[..., -1])

    def step(state, inp):
        u_c, w_c, q_c, k_c, a_c, d_c = inp
        v_new = u_c - jnp.einsum('bhck,bhkv->bhcv', w_c, state)
        o_c = jnp.einsum('bhck,bhkv->bhcv', q_c, state) + jnp.einsum('bhcs,bhsv->bhcv', a_c, v_new)
        state = state * d_c[..., None, None] + jnp.einsum('bhck,bhcv->bhkv', k_c, v_new)
        return state, o_c

    xs = tuple(jnp.moveaxis(a, 2, 0) for a in (u, w, q_dec, k_dec, attn, chunk_dec))
    s0 = jnp.zeros((bsz, h, dk, dv), jnp.float32)
    _, o = lax.scan(step, s0, xs)
    return o.transpose(1, 0, 3, 2, 4).reshape(bsz, t_len, h, dv)


def gdn_branch(qkv, z, b_logit, a_logit, conv_w, a_log, dt_bias, norm_w):
    bsz, t_len, _ = qkv.shape
    qkv = jax.nn.silu(causal_conv(qkv, conv_w))
    q, k, v = jnp.split(qkv, [GDN_QK_W, 2 * GDN_QK_W], axis=-1)
    rep = GDN_V_HEADS // GDN_QK_HEADS
    q = jnp.repeat(l2_normalize(q.reshape(bsz, t_len, GDN_QK_HEADS, GDN_DK)) * (GDN_DK ** -0.5), rep, axis=2)
    k = jnp.repeat(l2_normalize(k.reshape(bsz, t_len, GDN_QK_HEADS, GDN_DK)), rep, axis=2)
    v = v.reshape(bsz, t_len, GDN_V_HEADS, GDN_DV)
    beta = jax.nn.sigmoid(b_logit.astype(jnp.float32))
    g = -jnp.exp(a_log.astype(jnp.float32)) * jax.nn.softplus(a_logit.astype(jnp.float32) + dt_bias)
    o = gated_delta_rule(q, k, v, g, beta)
    zf = z.reshape(bsz, t_len, GDN_V_HEADS, GDN_DV).astype(jnp.float32)
    o = rms_normalize(o) * norm_w * jax.nn.silu(zf)
    return o.reshape(bsz, t_len, GDN_V_W).astype(qkv.dtype)


def ssd_scan(x, dt, a, bm, cm):
    bsz, t_len, h, p = x.shape
    g, n = bm.shape[2], bm.shape[3]
    e = h // g
    l = SSD_CHUNK
    nc = t_len // l
    x = x.astype(jnp.float32)
    dt = dt.astype(jnp.float32)
    xdt = (x * dt[..., None]).reshape(bsz, nc, l, g, e, p)
    adt = (dt * a).reshape(bsz, nc, l, g, e).transpose(0, 3, 4, 1, 2)
    bm = bm.astype(jnp.float32).reshape(bsz, nc, l, g, n)
    cm = cm.astype(jnp.float32).reshape(bsz, nc, l, g, n)
    a_cs = jnp.cumsum(adt, axis=-1)
    seg = decay_matrix(a_cs)
    cb = jnp.einsum('bclgn,bcsgn->bgcls', cm, bm)
    y_diag = jnp.einsum('bgcls,bgecls,bcsgep->bclgep', cb, seg, xdt)
    decay_states = jnp.exp(a_cs[..., -1:] - a_cs)
    states = jnp.einsum('bclgn,bgecl,bclgep->bcgepn', bm, decay_states, xdt)
    chunk_cs = jnp.cumsum(jnp.pad(a_cs[..., -1], ((0, 0), (0, 0), (0, 0), (1, 0))), axis=-1)
    decay_chunk = decay_matrix(chunk_cs)
    states = jnp.concatenate([jnp.zeros_like(states[:, :1]), states], axis=1)
    prev = jnp.einsum('bgezc,bcgepn->bzgepn', decay_chunk, states)[:, :-1]
    y_off = jnp.einsum('bclgn,bcgepn,bgecl->bclgep', cm, prev, jnp.exp(a_cs))
    return (y_diag + y_off).reshape(bsz, t_len, h, p)


def ssd_branch(z, xbc, dt_logit, conv_w, conv_b, a_log, dt_bias, d_skip, norm_w):
    bsz, t_len, _ = xbc.shape
    xbc = jax.nn.silu(causal_conv(xbc, conv_w) + conv_b)
    xs, bm, cm = jnp.split(xbc, [SSD_D_INNER, SSD_D_INNER + SSD_GROUPS * SSD_STATE], axis=-1)
    x = xs.reshape(bsz, t_len, SSD_HEADS, SSD_HEADDIM)
    bm = bm.reshape(bsz, t_len, SSD_GROUPS, SSD_STATE)
    cm = cm.reshape(bsz, t_len, SSD_GROUPS, SSD_STATE)
    dt = jax.nn.softplus(dt_logit.astype(jnp.float32) + dt_bias)
    a = -jnp.exp(a_log.astype(jnp.float32))
    y = ssd_scan(x, dt, a, bm, cm) + x.astype(jnp.float32) * d_skip[:, None]
    y = y.reshape(bsz, t_len, SSD_D_INNER) * jax.nn.silu(z.astype(jnp.float32))
    y = rms_normalize(y.reshape(bsz, t_len, SSD_GROUPS, -1)) * norm_w.reshape(SSD_GROUPS, -1)
    return y.reshape(bsz, t_len, SSD_D_INNER).astype(xbc.dtype)


def hybrid_mixer(x, w_in, gdn_conv_w, gdn_a_log, gdn_dt_bias, gdn_norm_w,
                 ssd_conv_w, ssd_conv_b, ssd_a_log, ssd_dt_bias, ssd_d, ssd_norm_w, w_out):
    proj = x @ w_in
    sizes = [GDN_CONV_DIM, GDN_V_W, GDN_V_HEADS, GDN_V_HEADS, SSD_D_INNER, SSD_CONV_DIM]
    gdn_qkv, gdn_z, gdn_b, gdn_a, ssd_z, ssd_xbc, ssd_dt = jnp.split(
        proj, [int(s) for s in np.cumsum(sizes)], axis=-1)
    o_a = gdn_branch(gdn_qkv, gdn_z, gdn_b, gdn_a, gdn_conv_w, gdn_a_log, gdn_dt_bias, gdn_norm_w)
    o_b = ssd_branch(ssd_z, ssd_xbc, ssd_dt, ssd_conv_w, ssd_conv_b, ssd_a_log, ssd_dt_bias, ssd_d, ssd_norm_w)
    return jnp.concatenate([o_a, o_b], axis=-1) @ w_out


def compress_blocks(kv, idx, pos, w1, w2):
    bsz, _, g, d = kv.shape
    blocks = kv[:, idx] + pos[:, None, :]
    flat = blocks.transpose(0, 1, 3, 2, 4).reshape(bsz, idx.shape[0], g, CMP_LEN * d)
    return jax.nn.silu(flat @ w1) @ w2


def nsa_mixer(x, positions, w_in, cmp_pos, cmp_w1, cmp_w2, w_out):
    bsz, t_len, _ = x.shape
    h, g, dk, dv = NSA_HEADS, NSA_GROUPS, NSA_DK, NSA_DV
    e = h // g
    proj = x @ w_in
    sizes = [h * dk, g * dk, g * dv, g * dk, g * dv, g * dk, g * dv]
    q, k_c, v_c, k_s, v_s, k_w, v_w, gates = jnp.split(proj, [int(s) for s in np.cumsum(sizes)], axis=-1)
    q = q.reshape(bsz, t_len, h, dk)
    k_c, k_s, k_w = (a.reshape(bsz, t_len, g, dk) for a in (k_c, k_s, k_w))
    v_c, v_s, v_w = (a.reshape(bsz, t_len, g, dv) for a in (v_c, v_s, v_w))
    q_rot = partial_rope(q, positions)
    k_s = partial_rope(k_s, positions)
    k_w = partial_rope(k_w, positions)
    gates = jax.nn.sigmoid(gates.astype(jnp.float32)).reshape(bsz, t_len, h, 3)

    n_cmp = (t_len - CMP_LEN) // CMP_STRIDE + 1
    cmp_idx = np.arange(n_cmp)[:, None] * CMP_STRIDE + np.arange(CMP_LEN)[None, :]
    k_cmp = compress_blocks(k_c, cmp_idx, cmp_pos[0], cmp_w1[0], cmp_w2[0])
    v_cmp = compress_blocks(v_c, cmp_idx, cmp_pos[1], cmp_w1[1], cmp_w2[1])
    cmp_end = jnp.asarray(cmp_idx[:, -1], jnp.int32)

    n_sel = t_len // SEL_LEN
    n_top = min(SEL_TOPK, n_sel)
    c0 = np.arange(n_cmp)[:, None] * CMP_STRIDE
    s0 = np.arange(n_sel)[None, :] * SEL_LEN
    agg = np.clip(np.minimum(c0 + CMP_LEN, s0 + SEL_LEN) - np.maximum(c0, s0), 0, None) / CMP_LEN
    agg = jnp.asarray(agg, jnp.float32)
    k_blocks = k_s.reshape(bsz, n_sel, SEL_LEN, g, dk).transpose(0, 3, 1, 2, 4)
    v_blocks = v_s.reshape(bsz, n_sel, SEL_LEN, g, dv).transpose(0, 3, 1, 2, 4)
    k_win = jnp.pad(k_w, ((0, 0), (WINDOW, 0), (0, 0), (0, 0)))
    v_win = jnp.pad(v_w, ((0, 0), (WINDOW, 0), (0, 0), (0, 0)))

    n_q = t_len // Q_BLOCK

    def to_blocks(a):
        return a.reshape(bsz, n_q, Q_BLOCK, *a.shape[2:]).swapaxes(0, 1)

    xs = (to_blocks(q), to_blocks(q_rot), to_blocks(gates), jnp.arange(n_q, dtype=jnp.int32) * Q_BLOCK)
    scale = dk ** -0.5
    b_ix = jnp.arange(bsz)[:, None, None]
    g_ix = jnp.arange(g)[None, :, None]
    sel_ids = jnp.arange(n_sel)

    def query_block(inp):
        qb, qrb, gb, q0 = inp
        t = q0 + jnp.arange(Q_BLOCK)
        qg = qb.reshape(bsz, Q_BLOCK, g, e, dk)
        qrg = qrb.reshape(bsz, Q_BLOCK, g, e, dk)
        s_cmp = jnp.einsum('bqged,bcgd->bgeqc', qg, k_cmp) * scale
        p_cmp = masked_softmax(s_cmp, cmp_end[None, :] <= t[:, None])
        o_cmp = jnp.einsum('bgeqc,bcgd->bqged', p_cmp, v_cmp)
        importance = jnp.einsum('bgeqc,cj->bgqj', p_cmp, agg)
        cur = t // SEL_LEN
        causal_blk = sel_ids[None, :] <= cur[:, None]
        forced = (sel_ids[None, :] == 0) | (causal_blk & (sel_ids[None, :] > cur[:, None] - SEL_LOCAL))
        score = jnp.where(forced, FORCE_SCORE, jnp.where(causal_blk, importance, -1.0))
        _, top = lax.top_k(score, n_top)
        flat = top.reshape(bsz, g, Q_BLOCK * n_top)
        k_sel = k_blocks[b_ix, g_ix, flat].reshape(bsz, g, Q_BLOCK, n_top * SEL_LEN, dk)
        v_sel = v_blocks[b_ix, g_ix, flat].reshape(bsz, g, Q_BLOCK, n_top * SEL_LEN, dv)
        key_pos = (top[..., None] * SEL_LEN + jnp.arange(SEL_LEN)).reshape(bsz, g, Q_BLOCK, n_top * SEL_LEN)
        s_sel = jnp.einsum('bqged,bgqmd->bgeqm', qrg, k_sel) * scale
        p_sel = masked_softmax(s_sel, (key_pos <= t[:, None])[:, :, None])
        o_sel = jnp.einsum('bgeqm,bgqmd->bqged', p_sel, v_sel)
        kw = lax.dynamic_slice_in_dim(k_win, q0, WINDOW + Q_BLOCK, axis=1)
        vw = lax.dynamic_slice_in_dim(v_win, q0, WINDOW + Q_BLOCK, axis=1)
        win_pos = q0 - WINDOW + jnp.arange(WINDOW + Q_BLOCK)
        win_mask = ((win_pos[None, :] <= t[:, None]) & (win_pos[None, :] > t[:, None] - WINDOW)
                    & (win_pos[None, :] >= 0))
        s_win = jnp.einsum('bqged,bkgd->bgeqk', qrg, kw) * scale
        p_win = masked_softmax(s_win, win_mask)
        o_win = jnp.einsum('bgeqk,bkgd->bqged', p_win, vw)
        gb = gb.reshape(bsz, Q_BLOCK, g, e, 3)
        o = gb[..., 0:1] * o_cmp + gb[..., 1:2] * o_sel + gb[..., 2:3] * o_win
        return o.reshape(bsz, Q_BLOCK, h * dv)

    o = lax.map(query_block, xs)
    o = o.swapaxes(0, 1).reshape(bsz, t_len, h * dv).astype(x.dtype)
    return o @ w_out


def setup_inputs(seed: int = 0) -> dict:
    key = jax.random.key(seed)
    keys = list(jax.random.split(key, 32))
    n_hyb = (DEPTH + 1) // 2
    n_nsa = DEPTH // 2

    def normal(i, shape, s):
        return jax.random.normal(keys[i], shape, jnp.float32) * s

    def dt_bias(i, shape):
        dt = jnp.exp(jax.random.uniform(keys[i], shape, jnp.float32, math.log(1e-3), math.log(1e-1)))
        return dt + jnp.log(-jnp.expm1(-dt))

    def log_rate(i, shape):
        return jnp.log(jax.random.uniform(keys[i], shape, jnp.float32, 1.0, 16.0))

    x = normal(0, (BATCH, SEQ, D_MODEL), 1.0)
    positions = (jnp.arange(SEQ, dtype=jnp.int32)[None, :]
                 + jax.random.randint(keys[1], (BATCH, 1), 0, 4096, dtype=jnp.int32))
    return {
        'x': x,
        'positions': positions,
        'ln_g': 1.0 + normal(2, (DEPTH, 3, D_MODEL), 0.05),
        'ln_b': normal(3, (DEPTH, 3, D_MODEL), 0.02),
        'ffn_w_in': normal(4, (DEPTH, 2, D_MODEL, 2 * D_FF), D_MODEL ** -0.5),
        'ffn_w_out': normal(5, (DEPTH, 2, D_FF, D_MODEL), DEEPNORM_BETA * D_FF ** -0.5),
        'hyb_w_in': normal(6, (n_hyb, D_MODEL, HYB_IN), D_MODEL ** -0.5),
        'gdn_conv_w': normal(7, (n_hyb, CONV_K, GDN_CONV_DIM), CONV_K ** -0.5),
        'gdn_a_log': log_rate(8, (n_hyb, GDN_V_HEADS)),
        'gdn_dt_bias': dt_bias(9, (n_hyb, GDN_V_HEADS)),
        'gdn_norm_w': 1.0 + normal(10, (n_hyb, GDN_DV), 0.05),
        'ssd_conv_w': normal(11, (n_hyb, CONV_K, SSD_CONV_DIM), CONV_K ** -0.5),
        'ssd_conv_b': normal(12, (n_hyb, SSD_CONV_DIM), 0.02),
        'ssd_a_log': log_rate(13, (n_hyb, SSD_HEADS)),
        'ssd_dt_bias': dt_bias(14, (n_hyb, SSD_HEADS)),
        'ssd_d': 1.0 + normal(15, (n_hyb, SSD_HEADS), 0.05),
        'ssd_norm_w': 1.0 + normal(16, (n_hyb, SSD_D_INNER), 0.05),
        'hyb_w_out': normal(17, (n_hyb, HYB_OUT, D_MODEL), DEEPNORM_BETA * HYB_OUT ** -0.5),
        'nsa_w_in': normal(18, (n_nsa, D_MODEL, NSA_IN), D_MODEL ** -0.5),
        'nsa_cmp_pos': normal(19, (n_nsa, 2, CMP_LEN, NSA_DK), 0.1),
        'nsa_cmp_w1': normal(20, (n_nsa, 2, CMP_LEN * NSA_DK, CMP_HIDDEN), (CMP_LEN * NSA_DK) ** -0.5),
        'nsa_cmp_w2': normal(21, (n_nsa, 2, CMP_HIDDEN, NSA_DK), CMP_HIDDEN ** -0.5),
        'nsa_w_out': normal(22, (n_nsa, NSA_HEADS * NSA_DV, D_MODEL), DEEPNORM_BETA * (NSA_HEADS * NSA_DV) ** -0.5),
    }


def reference(x, positions, ln_g, ln_b, ffn_w_in, ffn_w_out, hyb_w_in, gdn_conv_w, gdn_a_log,
              gdn_dt_bias, gdn_norm_w, ssd_conv_w, ssd_conv_b, ssd_a_log, ssd_dt_bias, ssd_d,
              ssd_norm_w, hyb_w_out, nsa_w_in, nsa_cmp_pos, nsa_cmp_w1, nsa_cmp_w2, nsa_w_out):
    h = x
    for layer in range(DEPTH):
        i = layer // 2
        h = layer_norm(DEEPNORM_ALPHA * h + 0.5 * swiglu(h, ffn_w_in[layer, 0], ffn_w_out[layer, 0]),
                       ln_g[layer, 0], ln_b[layer, 0])
        if layer % 2 == 0:
            mix = hybrid_mixer(h, hyb_w_in[i], gdn_conv_w[i], gdn_a_log[i], gdn_dt_bias[i], gdn_norm_w[i],
                               ssd_conv_w[i], ssd_conv_b[i], ssd_a_log[i], ssd_dt_bias[i], ssd_d[i],
                               ssd_norm_w[i], hyb_w_out[i])
        else:
            mix = nsa_mixer(h, positions, nsa_w_in[i], nsa_cmp_pos[i], nsa_cmp_w1[i], nsa_cmp_w2[i], nsa_w_out[i])
        h = layer_norm(DEEPNORM_ALPHA * h + mix, ln_g[layer, 1], ln_b[layer, 1])
        h = layer_norm(DEEPNORM_ALPHA * h + 0.5 * swiglu(h, ffn_w_in[layer, 1], ffn_w_out[layer, 1]),
                       ln_g[layer, 2], ln_b[layer, 2])
    return h
```

```python
import functools

import numpy as np
import jax
import jax.numpy as jnp
from jax import lax
from jax.experimental import pallas as pl
from jax.experimental.pallas import tpu as pltpu

F32 = jnp.float32
BF16 = jnp.bfloat16
HIGHEST = lax.Precision.HIGHEST

D_MODEL = 1024
DEPTH = 4
D_FF = 2816
DEEPNORM_ALPHA = (2.0 * DEPTH) ** 0.25
LN_EPS = 1e-5
RMS_EPS = 1e-6
CONV_K = 4

GDN_QK_HEADS = 4
GDN_V_HEADS = 8
GDN_DK = 128
GDN_DV = 128
GDN_CHUNK = 64
GDN_QK_W = GDN_QK_HEADS * GDN_DK
GDN_V_W = GDN_V_HEADS * GDN_DV
GDN_CONV_DIM = 2 * GDN_QK_W + GDN_V_W

SSD_D_INNER = D_MODEL
SSD_HEADDIM = 64
SSD_HEADS = SSD_D_INNER // SSD_HEADDIM
SSD_GROUPS = 2
SSD_STATE = 128
SSD_CHUNK = 128
SSD_BC_W = SSD_GROUPS * SSD_STATE
SSD_CONV_DIM = SSD_D_INNER + 2 * SSD_BC_W

NSA_HEADS = 16
NSA_GROUPS = 2
NSA_DK = 64
NSA_DV = 64
ROPE_DIM = NSA_DK // 4
ROPE_THETA = 500000.0
CMP_LEN = 32
CMP_STRIDE = 16
CMP_HIDDEN = 256
SEL_LEN = 64
SEL_TOPK = 8
SEL_LOCAL = 2
FORCE_SCORE = 1e4
WINDOW = 512
Q_BLOCK = 128

LANES = 128
SUBLANES = 8
VMEM_LIMIT_BYTES = 48 * 1024 * 1024

HYB_SMALL_OFF = GDN_CONV_DIM + GDN_V_W + SSD_D_INNER + SSD_CONV_DIM
HYB_PAD = HYB_SMALL_OFF + LANES
NSA_GATE_OFF = NSA_HEADS * NSA_DK + 3 * NSA_GROUPS * (NSA_DK + NSA_DV)
NSA_PAD = NSA_GATE_OFF + LANES

NEG_BIG = -1e30


def _cparams(*sem):
    return pltpu.CompilerParams(dimension_semantics=sem, vmem_limit_bytes=VMEM_LIMIT_BYTES)


def _silu(v):
    return v * jax.nn.sigmoid(v)


def _softplus(v):
    return jnp.maximum(v, 0.0) + jnp.log1p(jnp.exp(-jnp.abs(v)))


def _layer_norm(y, g, b):
    mu = jnp.mean(y, axis=-1, keepdims=True)
    d = y - mu
    var = jnp.mean(d * d, axis=-1, keepdims=True)
    return d * lax.rsqrt(var + LN_EPS) * g + b


def _dot(a, b):
    return jnp.dot(a, b, preferred_element_type=F32)


def _dot_exact(a, b):
    return jnp.dot(a, b, preferred_element_type=F32, precision=HIGHEST)


def _dot_nt(a, b):
    return lax.dot_general(a, b, (((1,), (1,)), ((), ())), preferred_element_type=F32)


def _ffn_ln_kernel(x_ref, wg_ref, wu_ref, wo_ref, g_ref, b_ref, o_ref, acc_ref):
    j = pl.program_id(1)

    @pl.when(j == 0)
    def _():
        acc_ref[...] = jnp.zeros_like(acc_ref)

    xb = x_ref[...].astype(BF16)
    gate = _dot(xb, wg_ref[...])
    up = _dot(xb, wu_ref[...])
    act = (_silu(gate) * up).astype(BF16)
    acc_ref[...] += _dot(act, wo_ref[...])

    @pl.when(j == pl.num_programs(1) - 1)
    def _():
        y = DEEPNORM_ALPHA * x_ref[...] + 0.5 * acc_ref[...]
        o_ref[...] = _layer_norm(y, g_ref[...], b_ref[...])


def _ffn_ln(h, w_in, w_out, g, b, *, tm, tf):
    m = h.shape[0]
    nf = D_FF // tf
    return pl.pallas_call(
        _ffn_ln_kernel,
        grid=(m // tm, nf),
        in_specs=[
            pl.BlockSpec((tm, D_MODEL), lambda i, j: (i, 0)),
            pl.BlockSpec((D_MODEL, tf), lambda i, j: (0, j)),
            pl.BlockSpec((D_MODEL, tf), lambda i, j: (0, j + nf)),
            pl.BlockSpec((tf, D_MODEL), lambda i, j: (j, 0)),
            pl.BlockSpec((1, D_MODEL), lambda i, j: (0, 0)),
            pl.BlockSpec((1, D_MODEL), lambda i, j: (0, 0)),
        ],
        out_specs=pl.BlockSpec((tm, D_MODEL), lambda i, j: (i, 0)),
        out_shape=jax.ShapeDtypeStruct((m, D_MODEL), F32),
        scratch_shapes=[pltpu.VMEM((tm, D_MODEL), F32)],
        compiler_params=_cparams("parallel", "arbitrary"),
        name="ffn_ln",
    )(h, w_in, w_in, w_out, g, b)


def _proj_kernel(x_ref, w_ref, o_ref):
    o_ref[...] = _dot(x_ref[...].astype(BF16), w_ref[...])


def _proj(h, w, *, tm, tn):
    m, k = h.shape
    n = w.shape[1]
    return pl.pallas_call(
        _proj_kernel,
        grid=(m // tm, n // tn),
        in_specs=[
            pl.BlockSpec((tm, k), lambda i, j: (i, 0)),
            pl.BlockSpec((k, tn), lambda i, j: (0, j)),
        ],
        out_specs=pl.BlockSpec((tm, tn), lambda i, j: (i, j)),
        out_shape=jax.ShapeDtypeStruct((m, n), F32),
        compiler_params=_cparams("parallel", "arbitrary"),
        name="in_proj",
    )(h, w)


def _out_ln_kernel(*refs, n_parts):
    a_refs = refs[:n_parts]
    w_refs = refs[n_parts:2 * n_parts]
    h_ref, g_ref, b_ref, o_ref = refs[2 * n_parts:]
    mix = _dot(a_refs[0][...], w_refs[0][...])
    for a_ref, w_ref in zip(a_refs[1:], w_refs[1:]):
        mix = mix + _dot(a_ref[...], w_ref[...])
    y = DEEPNORM_ALPHA * h_ref[...] + mix
    o_ref[...] = _layer_norm(y, g_ref[...], b_ref[...])


def _out_ln(parts, weights, h, g, b, *, tm):
    m = h.shape[0]
    n_parts = len(parts)
    in_specs = [pl.BlockSpec((tm, a.shape[1]), lambda i: (i, 0)) for a in parts]
    in_specs += [pl.BlockSpec(w.shape, lambda i: (0, 0)) for w in weights]
    in_specs += [
        pl.BlockSpec((tm, D_MODEL), lambda i: (i, 0)),
        pl.BlockSpec((1, D_MODEL), lambda i: (0, 0)),
        pl.BlockSpec((1, D_MODEL), lambda i: (0, 0)),
    ]
    return pl.pallas_call(
        functools.partial(_out_ln_kernel, n_parts=n_parts),
        grid=(m // tm,),
        in_specs=in_specs,
        out_specs=pl.BlockSpec((tm, D_MODEL), lambda i: (i, 0)),
        out_shape=jax.ShapeDtypeStruct((m, D_MODEL), F32),
        compiler_params=_cparams("parallel"),
        name="out_proj_ln",
    )(*parts, *weights, h, g, b)


def _causal_conv(cbuf, cw_ref, rows):
    acc = cw_ref[0:1, :] * cbuf[pl.ds(SUBLANES - CONV_K + 1, rows), :]
    for i in range(1, CONV_K):
        acc = acc + cw_ref[i:i + 1, :] * cbuf[pl.ds(SUBLANES - CONV_K + 1 + i, rows), :]
    return acc


def _ssd_kernel(z_ref, x_ref, bm_ref, cm_ref, sm_ref, cw_ref, cb_ref, dtb_ref, alog_ref,
                dskip_ref, nw_ref, eh_ref, o_ref, cbuf, state):
    L = SSD_CHUNK
    W = SSD_D_INNER
    GW = W // SSD_GROUPS
    N = SSD_STATE

    @pl.when(pl.program_id(1) == 0)
    def _():
        cbuf[0:SUBLANES, :] = jnp.zeros((SUBLANES, SSD_CONV_DIM), F32)
        state[...] = jnp.zeros_like(state)

    cbuf[SUBLANES:SUBLANES + L, 0:W] = x_ref[...]
    cbuf[SUBLANES:SUBLANES + L, W:W + SSD_BC_W] = bm_ref[...]
    cbuf[SUBLANES:SUBLANES + L, W + SSD_BC_W:SSD_CONV_DIM] = cm_ref[...]
    act = _silu(_causal_conv(cbuf, cw_ref, L) + cb_ref[...])
    cbuf[0:SUBLANES, :] = cbuf[L:L + SUBLANES, :]
    xs = act[:, 0:W]
    bm = act[:, W:W + SSD_BC_W].astype(BF16)
    cm = act[:, W + SSD_BC_W:SSD_CONV_DIM].astype(BF16)

    lane = lax.broadcasted_iota(jnp.int32, (1, LANES), 1)
    dt_lanes = (lane >= 16) & (lane < 16 + SSD_HEADS)
    dt_full = jnp.where(dt_lanes, _softplus(sm_ref[...] + dtb_ref[...]), 0.0)
    adt_full = dt_full * (-jnp.exp(alog_ref[...]))
    r_i = lax.broadcasted_iota(jnp.int32, (L, L), 0)
    c_i = lax.broadcasted_iota(jnp.int32, (L, L), 1)
    tril = r_i >= c_i
    acs_full = _dot_exact(tril.astype(F32), adt_full)
    acs_t = acs_full.T
    eh = eh_ref[...]
    dt_exp = _dot_exact(dt_full, eh)
    acs_exp = _dot_exact(acs_full, eh)
    a_last = acs_exp[L - 1:L, :]
    xdt = xs * dt_exp
    xdec = (xdt * jnp.exp(a_last - acs_exp)).astype(BF16)
    e_acs = jnp.exp(acs_exp)
    e_last = jnp.exp(a_last)
    lane_w = lax.broadcasted_iota(jnp.int32, (1, W), 1)
    lo_half = (lane_w & (LANES - 1)) < SSD_HEADDIM
    xdt_b = xdt.astype(BF16)
    xdt_lo = jnp.where(lo_half, xdt_b, jnp.zeros_like(xdt_b))
    xdt_hi = jnp.where(lo_half, jnp.zeros_like(xdt_b), xdt_b)

    heads_per_group = SSD_HEADS // SSD_GROUPS
    y_parts = []
    for g in range(SSD_GROUPS):
        bg = bm[:, g * N:(g + 1) * N]
        cg = cm[:, g * N:(g + 1) * N]
        cb = _dot_nt(cg, bg)
        s_g = state[g]
        y_off = _dot(cg, s_g.astype(BF16)) * e_acs[:, g * GW:(g + 1) * GW]
        for pr in range(heads_per_group // 2):
            y_pair = None
            for var in range(2):
                h = g * heads_per_group + 2 * pr + var
                col = acs_full[:, 16 + h:17 + h]
                row = acs_t[16 + h:17 + h, :]
                seg = jnp.exp(jnp.where(tril, col - row, -jnp.inf))
                mh = (cb * seg).astype(BF16)
                src = xdt_lo if var == 0 else xdt_hi
                lanes0 = (g * heads_per_group + 2 * pr) * SSD_HEADDIM
                d = _dot(mh, src[:, lanes0:lanes0 + LANES])
                y_pair = d if y_pair is None else y_pair + d
            off = pr * LANES
            y_parts.append(y_pair + y_off[:, off:off + LANES])
        upd = lax.dot_general(bg, xdec[:, g * GW:(g + 1) * GW], (((0,), (0,)), ((), ())),
                              preferred_element_type=F32)
        state[g] = s_g * e_last[:, g * GW:(g + 1) * GW] + upd
    y = jnp.concatenate(y_parts, axis=1) + xs * dskip_ref[...]
    y = y * _silu(z_ref[...])
    outs = []
    for g in range(SSD_GROUPS):
        yg = y[:, g * GW:(g + 1) * GW]
        outs.append(yg * lax.rsqrt(jnp.mean(yg * yg, axis=-1, keepdims=True) + RMS_EPS))
    o_ref[...] = (jnp.concatenate(outs, axis=1) * nw_ref[...]).astype(o_ref.dtype)


def _ssd(proj, bsz, t_len, cw, cb, dtb_row, alog_row, dskip_row, nw_row, eh):
    L = SSD_CHUNK
    nc = t_len // L
    W = SSD_D_INNER
    z_blk = (GDN_CONV_DIM + GDN_V_W) // W
    x_blk = (GDN_CONV_DIM + GDN_V_W + W) // W
    b_blk = (GDN_CONV_DIM + GDN_V_W + 2 * W) // SSD_BC_W
    sm_blk = HYB_SMALL_OFF // LANES
    row = lambda b, c: b * nc + c
    const = lambda b, c: (0, 0)
    return pl.pallas_call(
        _ssd_kernel,
        grid=(bsz, nc),
        in_specs=[
            pl.BlockSpec((L, W), lambda b, c: (row(b, c), z_blk)),
            pl.BlockSpec((L, W), lambda b, c: (row(b, c), x_blk)),
            pl.BlockSpec((L, SSD_BC_W), lambda b, c: (row(b, c), b_blk)),
            pl.BlockSpec((L, SSD_BC_W), lambda b, c: (row(b, c), b_blk + 1)),
            pl.BlockSpec((L, LANES), lambda b, c: (row(b, c), sm_blk)),
            pl.BlockSpec((CONV_K, SSD_CONV_DIM), const),
            pl.BlockSpec((1, SSD_CONV_DIM), const),
            pl.BlockSpec((1, LANES), const),
            pl.BlockSpec((1, LANES), const),
            pl.BlockSpec((1, W), const),
            pl.BlockSpec((1, W), const),
            pl.BlockSpec((LANES, W), const),
        ],
        out_specs=pl.BlockSpec((L, W), lambda b, c: (row(b, c), 0)),
        out_shape=jax.ShapeDtypeStruct((bsz * t_len, W), BF16),
        scratch_shapes=[
            pltpu.VMEM((L + SUBLANES, SSD_CONV_DIM), F32),
            pltpu.VMEM((SSD_GROUPS, SSD_STATE, W // SSD_GROUPS), F32),
        ],
        compiler_params=_cparams("parallel", "arbitrary"),
        name="ssd",
    )(proj, proj, proj, proj, proj, cw, cb, dtb_row, alog_row, dskip_row, nw_row, eh)


def _unit_lower_inverse(a, r_i, c_i):
    eye = (r_i == c_i).astype(F32)
    x = jnp.where((r_i >> 3) == (c_i >> 3), -a, 0.0)
    x2 = _dot_exact(x, x)
    x4 = _dot_exact(x2, x2)
    t = eye + x
    t = t + _dot_exact(t, x2)
    t = t + _dot_exact(t, x4)
    for sh in (3, 4, 5):
        same_pair = (r_i >> (sh + 1)) == (c_i >> (sh + 1))
        lower_left = (((r_i >> sh) & 1) == 1) & (((c_i >> sh) & 1) == 0)
        ms = jnp.where(same_pair & lower_left, a, 0.0)
        t = t - _dot_exact(t, _dot_exact(ms, t))
    return t


def _gdn_kernel(q_ref, k_ref, v_ref, z_ref, sm_ref, cw_ref, dtb_ref, alog_ref, nw_ref,
                eb_ref, ea_ref, o_ref, cbuf, state, *, tc):
    C = GDN_CHUNK
    DK = GDN_DK
    DV = GDN_DV

    @pl.when(pl.program_id(1) == 0)
    def _():
        cbuf[0:SUBLANES, :] = jnp.zeros((SUBLANES, GDN_CONV_DIM), F32)
        state[...] = jnp.zeros_like(state)

    cbuf[SUBLANES:SUBLANES + tc, 0:GDN_QK_W] = q_ref[...]
    cbuf[SUBLANES:SUBLANES + tc, GDN_QK_W:2 * GDN_QK_W] = k_ref[...]
    cbuf[SUBLANES:SUBLANES + tc, 2 * GDN_QK_W:GDN_CONV_DIM] = v_ref[...]
    act = _silu(_causal_conv(cbuf, cw_ref, tc))
    cbuf[0:SUBLANES, :] = cbuf[tc:tc + SUBLANES, :]

    lane = lax.broadcasted_iota(jnp.int32, (1, LANES), 1)
    sm = sm_ref[...]
    beta_full = jnp.where(lane < GDN_V_HEADS, jax.nn.sigmoid(sm), 0.0)
    g_lanes = (lane >= GDN_V_HEADS) & (lane < 2 * GDN_V_HEADS)
    g_full = jnp.where(g_lanes, -jnp.exp(alog_ref[...]) * _softplus(sm + dtb_ref[...]), 0.0)
    rt = lax.broadcasted_iota(jnp.int32, (tc, tc), 0)
    ct = lax.broadcasted_iota(jnp.int32, (tc, tc), 1)
    same_chunk_tril = ((rt >> 6) == (ct >> 6)) & (rt >= ct)
    gc_full = _dot_exact(same_chunk_tril.astype(F32), g_full)
    gc_t = gc_full.T
    b_exp = _dot_exact(beta_full, eb_ref[...])
    g_exp = _dot_exact(gc_full, ea_ref[...])
    eg_exp = jnp.exp(g_exp)

    r_i = lax.broadcasted_iota(jnp.int32, (C, C), 0)
    c_i = lax.broadcasted_iota(jnp.int32, (C, C), 1)
    tril = r_i >= c_i
    strict = r_i > c_i
    nw = nw_ref[...]
    rep = GDN_V_HEADS // GDN_QK_HEADS

    for hq in range(GDN_QK_HEADS):
        qh = act[:, hq * DK:(hq + 1) * DK]
        kh = act[:, GDN_QK_W + hq * DK:GDN_QK_W + (hq + 1) * DK]
        qn = qh * lax.rsqrt(jnp.sum(qh * qh, axis=-1, keepdims=True) + 1e-6) * (DK ** -0.5)
        kn = kh * lax.rsqrt(jnp.sum(kh * kh, axis=-1, keepdims=True) + 1e-6)
        s_heads = [state[hq * rep + j] for j in range(rep)]
        for ci in range(tc // C):
            r0 = ci * C
            q_c = qn[r0:r0 + C]
            k_c = kn[r0:r0 + C]
            k_b = k_c.astype(BF16)
            kk = _dot_nt(k_b, k_b)
            qk = _dot_nt(q_c.astype(BF16), k_b)
            k_t = k_c.T.astype(BF16)
            for j in range(rep):
                hv = hq * rep + j
                l0 = hv * DV
                gcol = g_exp[r0:r0 + C, l0:l0 + C]
                grow = gc_t[GDN_V_HEADS + hv:GDN_V_HEADS + hv + 1, r0:r0 + C]
                dmat = jnp.exp(jnp.where(tril, gcol - grow, -jnp.inf))
                bcol = b_exp[r0:r0 + C, l0:l0 + C]
                a_low = jnp.where(strict, kk * bcol * dmat, 0.0)
                t_inv = _unit_lower_inverse(a_low, r_i, c_i)
                beta_c = b_exp[r0:r0 + C, l0:l0 + DV]
                eg_c = eg_exp[r0:r0 + C, l0:l0 + DV]
                v_c = act[r0:r0 + C, 2 * GDN_QK_W + l0:2 * GDN_QK_W + l0 + DV]
                u = _dot_exact(t_inv, v_c * beta_c)
                w = _dot_exact(t_inv, k_c * (beta_c * eg_c))
                s_b = s_heads[j].astype(BF16)
                v_new = u - _dot(w.astype(BF16), s_b)
                o_c = _dot((q_c * eg_c).astype(BF16), s_b) + _dot((qk * dmat).astype(BF16), v_new.astype(BF16))
                g_last = g_exp[r0 + C - 1:r0 + C, l0:l0 + DV]
                v_dec = (v_new * jnp.exp(g_last - g_exp[r0:r0 + C, l0:l0 + DV])).astype(BF16)
                s_heads[j] = s_heads[j] * jnp.exp(g_last) + _dot(k_t, v_dec)
                o_n = o_c * lax.rsqrt(jnp.mean(o_c * o_c, axis=-1, keepdims=True) + RMS_EPS)
                o_ref[r0:r0 + C, l0:l0 + DV] = (o_n * nw * _silu(z_ref[r0:r0 + C, l0:l0 + DV])).astype(o_ref.dtype)
        for j in range(rep):
            state[hq * rep + j] = s_heads[j]


def _gdn(proj, bsz, t_len, cw, dtb_row, alog_row, nw_row, eb, ea, *, tc):
    nt = t_len // tc
    row = lambda b, t: b * nt + t
    const = lambda b, t: (0, 0)
    sm_blk = HYB_SMALL_OFF // LANES
    return pl.pallas_call(
        functools.partial(_gdn_kernel, tc=tc),
        grid=(bsz, nt),
        in_specs=[
            pl.BlockSpec((tc, GDN_QK_W), lambda b, t: (row(b, t), 0)),
            pl.BlockSpec((tc, GDN_QK_W), lambda b, t: (row(b, t), 1)),
            pl.BlockSpec((tc, GDN_V_W), lambda b, t: (row(b, t), 1)),
            pl.BlockSpec((tc, GDN_V_W), lambda b, t: (row(b, t), 2)),
            pl.BlockSpec((tc, LANES), lambda b, t: (row(b, t), sm_blk)),
            pl.BlockSpec((CONV_K, GDN_CONV_DIM), const),
            pl.BlockSpec((1, LANES), const),
            pl.BlockSpec((1, LANES), const),
            pl.BlockSpec((1, GDN_DV), const),
            pl.BlockSpec((LANES, GDN_V_W), const),
            pl.BlockSpec((LANES, GDN_V_W), const),
        ],
        out_specs=pl.BlockSpec((tc, GDN_V_W), lambda b, t: (row(b, t), 0)),
        out_shape=jax.ShapeDtypeStruct((bsz * t_len, GDN_V_W), BF16),
        scratch_shapes=[
            pltpu.VMEM((tc + SUBLANES, GDN_CONV_DIM), F32),
            pltpu.VMEM((GDN_V_HEADS, GDN_DK, GDN_DV), F32),
        ],
        compiler_params=_cparams("parallel", "arbitrary"),
        name="gdn",
    )(proj, proj, proj, proj, proj, cw, dtb_row, alog_row, nw_row, eb, ea)


def _group_variants(x):
    lane = lax.broadcasted_iota(jnp.int32, (1, LANES), 1)
    lo = lane < (LANES // 2)
    xr = pltpu.roll(x, LANES // 2, axis=1)
    zero = jnp.zeros_like(x)
    return jnp.concatenate([jnp.where(lo, x, zero), jnp.where(lo, zero, xr),
                            jnp.where(lo, xr, zero), jnp.where(lo, zero, x)], axis=1)


def _nsa_prep_kernel(q_ref, ks_ref, vs_ref, kw_ref, vw_ref, pos_ref, freq_ref, s1_ref, s2_ref,
                     qc_o, qr_o, ksz_o, vsz_o, kwz_o, vwz_o):
    scale = NSA_DK ** -0.5
    ang = pos_ref[...].astype(F32) * freq_ref[...]
    cos = jnp.cos(ang)
    sin = jnp.sin(ang)
    sin_up = sin * s1_ref[...]
    sin_dn = sin * s2_ref[...]
    half = ROPE_DIM // 2

    def rope(x):
        return x * cos + pltpu.roll(x, half, axis=1) * sin_up + pltpu.roll(x, LANES - half, axis=1) * sin_dn

    for p in range(NSA_HEADS * NSA_DK // LANES):
        x = q_ref[:, p * LANES:(p + 1) * LANES]
        qc_o[:, p * LANES:(p + 1) * LANES] = (x * scale).astype(qc_o.dtype)
        qr_o[:, p * LANES:(p + 1) * LANES] = (rope(x) * scale).astype(qr_o.dtype)
    ksz_o[...] = _group_variants(rope(ks_ref[...])).astype(ksz_o.dtype)
    kwz_o[...] = _group_variants(rope(kw_ref[...])).astype(kwz_o.dtype)
    vsz_o[...] = _group_variants(vs_ref[...]).astype(vsz_o.dtype)
    vwz_o[...] = _group_variants(vw_ref[...]).astype(vwz_o.dtype)


def _nsa_prep(proj, pos_col, freq_row, s1_row, s2_row, *, tm):
    m = proj.shape[0]
    qw = NSA_HEADS * NSA_DK
    kv0 = qw // LANES
    const = lambda i: (0, 0)
    zw = 4 * LANES
    return pl.pallas_call(
        _nsa_prep_kernel,
        grid=(m // tm,),
        in_specs=[
            pl.BlockSpec((tm, qw), lambda i: (i, 0)),
            pl.BlockSpec((tm, LANES), lambda i: (i, kv0 + 2)),
            pl.BlockSpec((tm, LANES), lambda i: (i, kv0 + 3)),
            pl.BlockSpec((tm, LANES), lambda i: (i, kv0 + 4)),
            pl.BlockSpec((tm, LANES), lambda i: (i, kv0 + 5)),
            pl.BlockSpec((tm, 1), lambda i: (i, 0)),
            pl.BlockSpec((1, LANES), const),
            pl.BlockSpec((1, LANES), const),
            pl.BlockSpec((1, LANES), const),
        ],
        out_specs=[
            pl.BlockSpec((tm, qw), lambda i: (i, 0)),
            pl.BlockSpec((tm, qw), lambda i: (i, 0)),
            pl.BlockSpec((tm, zw), lambda i: (i, 0)),
            pl.BlockSpec((tm, zw), lambda i: (i, 0)),
            pl.BlockSpec((tm, zw), lambda i: (i, 0)),
            pl.BlockSpec((tm, zw), lambda i: (i, 0)),
        ],
        out_shape=[
            jax.ShapeDtypeStruct((m, qw), BF16),
            jax.ShapeDtypeStruct((m, qw), BF16),
            jax.ShapeDtypeStruct((m, zw), BF16),
            jax.ShapeDtypeStruct((m, zw), BF16),
            jax.ShapeDtypeStruct((m, zw), BF16),
            jax.ShapeDtypeStruct((m, zw), BF16),
        ],
        compiler_params=_cparams("parallel"),
        name="nsa_prep",
    )(proj, proj, proj, proj, proj, pos_col, freq_row, s1_row, s2_row)


def _nsa_compress_kernel(k16_ref, v16_ref, pos_ref, w1a_ref, w1b_ref, w2_ref, kcz_o, vcz_o):
    nr = k16_ref.shape[0]
    for idx, (x_ref, o_ref) in enumerate(((k16_ref, kcz_o), (v16_ref, vcz_o))):
        x = x_ref[...]
        h_a = _dot((x + pos_ref[idx, 0:1, :]).astype(BF16), w1a_ref[idx])
        h_b = _dot((x + pos_ref[idx, 1:2, :]).astype(BF16), w1b_ref[idx])
        hid = h_a + pltpu.roll(h_b, nr - 1, axis=0)
        out = _dot(_silu(hid).astype(BF16), w2_ref[idx])
        o_ref[...] = _group_variants(out).astype(o_ref.dtype)


def _nsa_compress(k16, v16, pos_ab, w1a, w1b, w2):
    bsz, nr, width = k16.shape
    hid = NSA_GROUPS * CMP_HIDDEN
    zw = 4 * LANES
    c3 = lambda b: (0, 0, 0)
    return pl.pallas_call(
        _nsa_compress_kernel,
        grid=(bsz,),
        in_specs=[
            pl.BlockSpec((None, nr, width), lambda b: (b, 0, 0)),
            pl.BlockSpec((None, nr, width), lambda b: (b, 0, 0)),
            pl.BlockSpec((2, 2, width), c3),
            pl.BlockSpec((2, width, hid), c3),
            pl.BlockSpec((2, width, hid), c3),
            pl.BlockSpec((2, hid, LANES), c3),
        ],
        out_specs=[
            pl.BlockSpec((None, nr, zw), lambda b: (b, 0, 0)),
            pl.BlockSpec((None, nr, zw), lambda b: (b, 0, 0)),
        ],
        out_shape=[
            jax.ShapeDtypeStruct((bsz, nr, zw), BF16),
            jax.ShapeDtypeStruct((bsz, nr, zw), BF16),
        ],
        compiler_params=_cparams("parallel"),
        name="nsa_compress",
    )(k16, v16, pos_ab, w1a, w1b, w2)


def _flash_branch(q_stack, kz_ref, vz_ref, g, lo, hi, bias_fn, m_ref, l_ref, acc_ref, kb_size):
    pairs = q_stack.shape[0] // Q_BLOCK
    m_ref[...] = jnp.full(m_ref.shape, NEG_BIG, F32)
    l_ref[...] = jnp.zeros(l_ref.shape, F32)
    acc_ref[...] = jnp.zeros(acc_ref.shape, F32)
    lane = lax.broadcasted_iota(jnp.int32, (1, LANES), 1)
    lo_half = lane < (LANES // 2)
    reps = kb_size // LANES

    def body(kb, carry):
        ks = pl.multiple_of(kb * kb_size, kb_size)
        bias = bias_fn(ks)
        alphas = []
        pv = None
        for var in range(2):
            c0 = (2 * g + var) * LANES
            k_blk = kz_ref[pl.ds(ks, kb_size), c0:c0 + LANES]
            v_blk = vz_ref[pl.ds(ks, kb_size), c0:c0 + LANES]
            s = _dot_nt(q_stack, k_blk)
            s = (s.reshape(pairs, Q_BLOCK, kb_size) + bias[None]).reshape(pairs * Q_BLOCK, kb_size)
            m_prev = m_ref[var]
            m_new = jnp.maximum(m_prev, jnp.max(s, axis=-1, keepdims=True))
            alpha = jnp.exp(m_prev - m_new)
            p = jnp.exp(s - jnp.concatenate([m_new] * reps, axis=1))
            l_ref[var] = alpha * l_ref[var] + jnp.sum(p, axis=-1, keepdims=True)
            m_ref[var] = m_new
            alphas.append(alpha)
            d = _dot(p.astype(BF16), v_blk)
            pv = d if pv is None else pv + d
        acc_ref[...] = acc_ref[...] * jnp.where(lo_half, alphas[0], alphas[1]) + pv
        return carry

    lax.fori_loop(lo, hi, body, 0)
    return acc_ref[...] / jnp.where(lo_half, l_ref[0], l_ref[1])


def _nsa_attn_kernel(qc_ref, qr_ref, gate_ref, kcz_ref, vcz_ref, ksz_ref, vsz_ref, kwz_ref, vwz_ref,
                     agg_ref, esel_ref, eg_ref, o_ref, bias_ref, m_ref, l_ref, acc_ref,
                     *, t_len, kb_size):
    n_cmp_rows = t_len // CMP_STRIDE
    n_cmp = (t_len - CMP_LEN) // CMP_STRIDE + 1
    n_sel = t_len // SEL_LEN
    n_top = min(SEL_TOPK, n_sel)
    pairs = NSA_HEADS // NSA_GROUPS // 2
    q0 = pl.program_id(1) * Q_BLOCK
    t_col = q0 + lax.broadcasted_iota(jnp.int32, (Q_BLOCK, 1), 0)
    t_stack = q0 + (lax.broadcasted_iota(jnp.int32, (pairs * Q_BLOCK, 1), 0) & (Q_BLOCK - 1))
    lane = lax.broadcasted_iota(jnp.int32, (1, LANES), 1)
    c_idx = lax.broadcasted_iota(jnp.int32, (1, n_cmp_rows), 1)
    cmp_mask = ((c_idx * CMP_STRIDE + (CMP_LEN - 1)) <= t_stack) & (c_idx < n_cmp)
    key_pos = lax.broadcasted_iota(jnp.int32, (1, t_len), 1)
    gate = jax.nn.sigmoid(gate_ref[...])
    hi = (q0 + (Q_BLOCK - 1)) // kb_size + 1
    win_lo = jnp.maximum(q0 - (WINDOW - 1), 0) // kb_size

    def win_bias(ks):
        kp = ks + lax.broadcasted_iota(jnp.int32, (1, kb_size), 1)
        return jnp.where((kp <= t_col) & (kp > t_col - WINDOW), 0.0, NEG_BIG)

    def sel_bias(ks):
        return bias_ref[:, pl.ds(ks, kb_size)]

    for g in range(NSA_GROUPS):
        lanes_g = g * pairs * LANES
        qc_stack = jnp.concatenate(
            [qc_ref[:, lanes_g + p * LANES:lanes_g + (p + 1) * LANES] for p in range(pairs)], axis=0)
        qr_stack = jnp.concatenate(
            [qr_ref[:, lanes_g + p * LANES:lanes_g + (p + 1) * LANES] for p in range(pairs)], axis=0)

        o_cmp = None
        p_sum = None
        for var in range(2):
            c0 = (2 * g + var) * LANES
            s = jnp.where(cmp_mask, _dot_nt(qc_stack, kcz_ref[:, c0:c0 + LANES]), -jnp.inf)
            mx = jnp.max(s, axis=-1, keepdims=True)
            mx = jnp.where(mx > -jnp.inf, mx, 0.0)
            e = jnp.exp(s - mx)
            p = e / jnp.maximum(jnp.sum(e, axis=-1, keepdims=True), 1e-30)
            d = _dot(p.astype(BF16), vcz_ref[:, c0:c0 + LANES])
            o_cmp = d if o_cmp is None else o_cmp + d
            for pr in range(pairs):
                blk = p[pr * Q_BLOCK:(pr + 1) * Q_BLOCK]
                p_sum = blk if p_sum is None else p_sum + blk
        importance = _dot_exact(p_sum, agg_ref[...])

        cur = t_col >> 6
        causal_blk = lane <= cur
        forced = (lane == 0) | (causal_blk & (lane > cur - SEL_LOCAL))
        score = jnp.where(forced, FORCE_SCORE, jnp.where(causal_blk, importance, -1.0))
        score = jnp.where(lane < n_sel, score, -2.0)
        rank = jnp.zeros((Q_BLOCK, LANES), F32)
        for jj in range(n_sel):
            col = score[:, jj:jj + 1]
            beats = (col > score) | ((col == score) & (lane > jj))
            rank = rank + jnp.where(beats, 1.0, 0.0)
        sel = jnp.where((rank < n_top) & (lane < n_sel), 1.0, 0.0)
        sel_keys = _dot(sel.astype(BF16), esel_ref[...])
        bias_ref[...] = jnp.where((sel_keys > 0.5) & (key_pos <= t_col), 0.0, NEG_BIG)
        o_sel = _flash_branch(qr_stack, ksz_ref, vsz_ref, g, 0, hi, sel_bias,
                              m_ref, l_ref, acc_ref, kb_size)

        o_win = _flash_branch(qr_stack, kwz_ref, vwz_ref, g, win_lo, hi, win_bias,
                              m_ref, l_ref, acc_ref, kb_size)

        for pr in range(pairs):
            c0 = lanes_g + pr * LANES
            rows = slice(pr * Q_BLOCK, (pr + 1) * Q_BLOCK)
            g_cmp = _dot_exact(gate, eg_ref[0, :, c0:c0 + LANES])
            g_sel = _dot_exact(gate, eg_ref[1, :, c0:c0 + LANES])
            g_win = _dot_exact(gate, eg_ref[2, :, c0:c0 + LANES])
            o_ref[:, c0:c0 + LANES] = (g_cmp * o_cmp[rows] + g_sel * o_sel[rows]
                                       + g_win * o_win[rows]).astype(o_ref.dtype)


def _nsa_attn(qc, qr, proj, kcz, vcz, ksz, vsz, kwz, vwz, agg, esel, eg, bsz, t_len, *, kb_size):
    nq = t_len // Q_BLOCK
    qw = NSA_HEADS * NSA_DK
    zw = 4 * LANES
    nr = t_len // CMP_STRIDE
    pairs = NSA_HEADS // NSA_GROUPS // 2
    row = lambda b, i: (b * nq + i, 0)
    per_b = lambda b, i: (b, 0, 0)
    return pl.pallas_call(
        functools.partial(_nsa_attn_kernel, t_len=t_len, kb_size=kb_size),
        grid=(bsz, nq),
        in_specs=[
            pl.BlockSpec((Q_BLOCK, qw), row),
            pl.BlockSpec((Q_BLOCK, qw), row),
            pl.BlockSpec((Q_BLOCK, LANES), lambda b, i: (b * nq + i, NSA_GATE_OFF // LANES)),
            pl.BlockSpec((None, nr, zw), per_b),
            pl.BlockSpec((None, nr, zw), per_b),
            pl.BlockSpec((None, t_len, zw), per_b),
            pl.BlockSpec((None, t_len, zw), per_b),
            pl.BlockSpec((None, t_len, zw), per_b),
            pl.BlockSpec((None, t_len, zw), per_b),
            pl.BlockSpec(agg.shape, lambda b, i: (0, 0)),
            pl.BlockSpec(esel.shape, lambda b, i: (0, 0)),
            pl.BlockSpec(eg.shape, lambda b, i: (0, 0, 0)),
        ],
        out_specs=pl.BlockSpec((Q_BLOCK, qw), row),
        out_shape=jax.ShapeDtypeStruct((bsz * t_len, qw), BF16),
        scratch_shapes=[
            pltpu.VMEM((Q_BLOCK, t_len), F32),
            pltpu.VMEM((2, pairs * Q_BLOCK, LANES), F32),
            pltpu.VMEM((2, pairs * Q_BLOCK, LANES), F32),
            pltpu.VMEM((pairs * Q_BLOCK, LANES), F32),
        ],
        compiler_params=_cparams("parallel", "arbitrary"),
        name="nsa_attn",
    )(qc, qr, proj, kcz, vcz, ksz, vsz, kwz, vwz, agg, esel, eg)


def _expand_rows(first_row, n_heads, width):
    e = np.zeros((LANES, n_heads * width), np.float32)
    for h in range(n_heads):
        e[first_row + h, h * width:(h + 1) * width] = 1.0
    return e


def _nsa_constants(t_len):
    n_cmp = (t_len - CMP_LEN) // CMP_STRIDE + 1
    n_sel = t_len // SEL_LEN
    nr = t_len // CMP_STRIDE
    c0 = np.arange(n_cmp)[:, None] * CMP_STRIDE
    s0 = np.arange(n_sel)[None, :] * SEL_LEN
    overlap = np.clip(np.minimum(c0 + CMP_LEN, s0 + SEL_LEN) - np.maximum(c0, s0), 0, None) / CMP_LEN
    agg = np.zeros((nr, LANES), np.float32)
    agg[:n_cmp, :n_sel] = overlap
    esel = np.zeros((LANES, t_len), np.float32)
    esel[np.arange(t_len) // SEL_LEN, np.arange(t_len)] = 1.0
    eg = np.zeros((3, LANES, NSA_HEADS * NSA_DV), np.float32)
    for h in range(NSA_HEADS):
        for j in range(3):
            eg[j, h * 3 + j, h * NSA_DV:(h + 1) * NSA_DV] = 1.0
    half = ROPE_DIM // 2
    inv_freq = (ROPE_THETA ** (-np.arange(half) / half)).astype(np.float32)
    d = np.arange(LANES) % NSA_DK
    freq = np.where(d < ROPE_DIM, inv_freq[d % half], 0.0).astype(np.float32)[None]
    s1 = ((d >= half) & (d < ROPE_DIM)).astype(np.float32)[None]
    s2 = -(d < half).astype(np.float32)[None]
    return agg, esel, eg, freq, s1, s2


def _pad_row(vec, offset):
    return jnp.zeros((1, LANES), F32).at[0, offset:offset + vec.shape[0]].set(vec.astype(F32))


def _hybrid_layer(h, bsz, t_len, w_in, gdn_conv_w, gdn_a_log, gdn_dt_bias, gdn_norm_w, ssd_conv_w,
                  ssd_conv_b, ssd_a_log, ssd_dt_bias, ssd_d, ssd_norm_w, w_out, ln_g, ln_b, cfg):
    b_off = GDN_CONV_DIM + GDN_V_W
    z_off = b_off + 2 * GDN_V_HEADS
    dt_off = z_off + SSD_D_INNER + SSD_CONV_DIM
    w_pad = jnp.concatenate(
        [w_in[:, :b_off], w_in[:, z_off:dt_off], w_in[:, b_off:z_off], w_in[:, dt_off:],
         jnp.zeros((D_MODEL, HYB_PAD - w_in.shape[1]), w_in.dtype)], axis=1).astype(BF16)
    proj = _proj(h, w_pad, tm=cfg["proj_tm"], tn=cfg["hyb_tn"])
    o_a = _gdn(proj, bsz, t_len, gdn_conv_w, _pad_row(gdn_dt_bias, GDN_V_HEADS),
               _pad_row(gdn_a_log, GDN_V_HEADS), gdn_norm_w[None].astype(F32),
               jnp.asarray(_expand_rows(0, GDN_V_HEADS, GDN_DV)),
               jnp.asarray(_expand_rows(GDN_V_HEADS, GDN_V_HEADS, GDN_DV)), tc=cfg["gdn_tc"])
    o_b = _ssd(proj, bsz, t_len, ssd_conv_w, ssd_conv_b[None], _pad_row(ssd_dt_bias, 16),
               _pad_row(ssd_a_log, 16), jnp.repeat(ssd_d, SSD_HEADDIM)[None], ssd_norm_w[None],
               jnp.asarray(_expand_rows(16, SSD_HEADS, SSD_HEADDIM)))
    w_out_b = w_out.astype(BF16)
    return _out_ln([o_a, o_b], [w_out_b[:GDN_V_W], w_out_b[GDN_V_W:]], h, ln_g, ln_b, tm=cfg["out_tm"])


def _nsa_layer(h, pos_col, bsz, t_len, w_in, cmp_pos, cmp_w1, cmp_w2, w_out, ln_g, ln_b, cfg):
    agg, esel, eg, freq, s1, s2 = _nsa_constants(t_len)
    w_pad = jnp.concatenate(
        [w_in, jnp.zeros((D_MODEL, NSA_PAD - w_in.shape[1]), w_in.dtype)], axis=1).astype(BF16)
    proj = _proj(h, w_pad, tm=cfg["proj_tm"], tn=cfg["nsa_tn"])
    qc, qr, ksz, vsz, kwz, vwz = _nsa_prep(proj, pos_col, jnp.asarray(freq), jnp.asarray(s1),
                                           jnp.asarray(s2), tm=cfg["prep_tm"])
    qw = NSA_HEADS * NSA_DK
    nr = t_len // CMP_STRIDE
    width = CMP_STRIDE * LANES
    k16 = proj[:, qw:qw + LANES].reshape(bsz, nr, width)
    v16 = proj[:, qw + LANES:qw + 2 * LANES].reshape(bsz, nr, width)
    pos2 = jnp.broadcast_to(cmp_pos[:, :, None, :], (2, CMP_LEN, NSA_GROUPS, NSA_DK)).reshape(2, 2, width)
    w1 = cmp_w1.reshape(2, 2, CMP_STRIDE, NSA_DK, CMP_HIDDEN)
    eye_g = jnp.eye(NSA_GROUPS, dtype=w1.dtype)
    w1x = jnp.einsum("ksjdh,ge->ksjgdeh", w1, eye_g).reshape(
        2, 2, width, NSA_GROUPS * CMP_HIDDEN).astype(BF16)
    w2x = jnp.einsum("khd,ge->kghed", cmp_w2, eye_g).reshape(
        2, NSA_GROUPS * CMP_HIDDEN, NSA_GROUPS * NSA_DK).astype(BF16)
    kcz, vcz = _nsa_compress(k16, v16, pos2, w1x[:, 0], w1x[:, 1], w2x)
    o = _nsa_attn(qc, qr, proj, kcz, vcz,
                  ksz.reshape(bsz, t_len, -1), vsz.reshape(bsz, t_len, -1),
                  kwz.reshape(bsz, t_len, -1), vwz.reshape(bsz, t_len, -1),
                  jnp.asarray(agg), jnp.asarray(esel).astype(BF16), jnp.asarray(eg),
                  bsz, t_len, kb_size=cfg["attn_kb"])
    return _out_ln([o], [w_out.astype(BF16)], h, ln_g, ln_b, tm=cfg["out_tm"])


def _config(m, t_len):
    return dict(ffn_tm=min(512, m), ffn_tf=1408, proj_tm=min(512, m), hyb_tn=1152, nsa_tn=640,
                out_tm=min(512, m), gdn_tc=min(128, t_len), prep_tm=min(256, m),
                attn_kb=min(256, t_len))


def kernel(x, positions, ln_g, ln_b, ffn_w_in, ffn_w_out, hyb_w_in, gdn_conv_w, gdn_a_log, gdn_dt_bias, gdn_norm_w, ssd_conv_w, ssd_conv_b, ssd_a_log, ssd_dt_bias, ssd_d, ssd_norm_w, hyb_w_out, nsa_w_in, nsa_cmp_pos, nsa_cmp_w1, nsa_cmp_w2, nsa_w_out):
    bsz, t_len, _ = x.shape
    m = bsz * t_len
    cfg = _config(m, t_len)
    h = x.reshape(m, D_MODEL)
    pos_col = positions.reshape(m, 1)
    w_in_b = ffn_w_in.astype(BF16)
    w_out_b = ffn_w_out.astype(BF16)
    ln_g = ln_g[:, :, None, :]
    ln_b = ln_b[:, :, None, :]
    for layer in range(DEPTH):
        i = layer // 2
        h = _ffn_ln(h, w_in_b[layer, 0], w_out_b[layer, 0], ln_g[layer, 0], ln_b[layer, 0],
                    tm=cfg["ffn_tm"], tf=cfg["ffn_tf"])
        if layer % 2 == 0:
            h = _hybrid_layer(h, bsz, t_len, hyb_w_in[i], gdn_conv_w[i], gdn_a_log[i], gdn_dt_bias[i],
                              gdn_norm_w[i], ssd_conv_w[i], ssd_conv_b[i], ssd_a_log[i], ssd_dt_bias[i],
                              ssd_d[i], ssd_norm_w[i], hyb_w_out[i], ln_g[layer, 1], ln_b[layer, 1], cfg)
        else:
            h = _nsa_layer(h, pos_col, bsz, t_len, nsa_w_in[i], nsa_cmp_pos[i], nsa_cmp_w1[i],
                           nsa_cmp_w2[i], nsa_w_out[i], ln_g[layer, 1], ln_b[layer, 1], cfg)
        h = _ffn_ln(h, w_in_b[layer, 1], w_out_b[layer, 1], ln_g[layer, 2], ln_b[layer, 2],
                    tm=cfg["ffn_tm"], tf=cfg["ffn_tf"])
    return h.reshape(bsz, t_len, D_MODEL)
```

```python
import functools

import numpy as np
import jax
import jax.numpy as jnp
from jax import lax
from jax.experimental import pallas as pl
from jax.experimental.pallas import tpu as pltpu

F32 = jnp.float32
BF16 = jnp.bfloat16

D_MODEL = 1024
DEPTH = 4
D_FF = 2816
DEEPNORM_ALPHA = (2.0 * DEPTH) ** 0.25
LN_EPS = 1e-5
RMS_EPS = 1e-6
CONV_K = 4

GDN_QK_HEADS = 4
GDN_V_HEADS = 8
GDN_DK = 128
GDN_DV = 128
GDN_CHUNK = 64
GDN_QK_W = GDN_QK_HEADS * GDN_DK
GDN_V_W = GDN_V_HEADS * GDN_DV
GDN_CONV_DIM = 2 * GDN_QK_W + GDN_V_W

SSD_D_INNER = D_MODEL
SSD_HEADDIM = 64
SSD_HEADS = SSD_D_INNER // SSD_HEADDIM
SSD_GROUPS = 2
SSD_STATE = 128
SSD_CHUNK = 128
SSD_BC_W = SSD_GROUPS * SSD_STATE
SSD_CONV_DIM = SSD_D_INNER + 2 * SSD_BC_W

NSA_HEADS = 16
NSA_GROUPS = 2
NSA_DK = 64
NSA_DV = 64
ROPE_DIM = NSA_DK // 4
ROPE_THETA = 500000.0
CMP_LEN = 32
CMP_STRIDE = 16
CMP_HIDDEN = 256
SEL_LEN = 64
SEL_TOPK = 8
SEL_LOCAL = 2
FORCE_SCORE = 1e4
WINDOW = 512
Q_BLOCK = 128

LANES = 128
SUBLANES = 8
VMEM_LIMIT_BYTES = 48 * 1024 * 1024

HYB_SMALL_OFF = GDN_CONV_DIM + GDN_V_W + SSD_D_INNER + SSD_CONV_DIM
HYB_PAD = HYB_SMALL_OFF + LANES
NSA_GATE_OFF = NSA_HEADS * NSA_DK + 3 * NSA_GROUPS * (NSA_DK + NSA_DV)
NSA_PAD = NSA_GATE_OFF + LANES

NEG_BIG = -1e30
LOG2E = 1.4426950408889634
VT_ROWS = LANES + 16


def _cparams(*sem):
    return pltpu.CompilerParams(dimension_semantics=sem, vmem_limit_bytes=VMEM_LIMIT_BYTES)


def _silu(v):
    return v * jax.nn.sigmoid(v)


def _softplus(v):
    return jnp.maximum(v, 0.0) + jnp.log1p(jnp.exp(-jnp.abs(v)))


def _layer_norm(y, g, b):
    mu = jnp.mean(y, axis=-1, keepdims=True)
    d = y - mu
    var = jnp.mean(d * d, axis=-1, keepdims=True)
    return d * lax.rsqrt(var + LN_EPS) * g + b


def _dot(a, b):
    return jnp.dot(a, b, preferred_element_type=F32)


def _split3(x):
    hi = x.astype(BF16)
    r1 = x - hi.astype(F32)
    mid = r1.astype(BF16)
    lo = (r1 - mid.astype(F32)).astype(BF16)
    return hi, mid, lo


def _dot_sel_rhs(x, sel):
    sel = sel.astype(BF16)
    hi, mid, lo = _split3(x)
    return _dot(hi, sel) + _dot(mid, sel) + _dot(lo, sel)


def _dot_sel_lhs(sel, x):
    sel = sel.astype(BF16)
    hi, mid, lo = _split3(x)
    return _dot(sel, hi) + _dot(sel, mid) + _dot(sel, lo)


def _dot_nt(a, b):
    return lax.dot_general(a, b, (((1,), (1,)), ((), ())), preferred_element_type=F32)


def _ffn_ln_kernel(x_ref, wg_ref, wu_ref, wo_ref, g_ref, b_ref, o_ref, acc_ref):
    j = pl.program_id(1)

    @pl.when(j == 0)
    def _():
        acc_ref[...] = jnp.zeros_like(acc_ref)

    xb = x_ref[...].astype(BF16)
    gate = _dot(xb, wg_ref[...])
    up = _dot(xb, wu_ref[...])
    act = (_silu(gate) * up).astype(BF16)
    acc_ref[...] += _dot(act, wo_ref[...])

    @pl.when(j == pl.num_programs(1) - 1)
    def _():
        y = DEEPNORM_ALPHA * x_ref[...] + 0.5 * acc_ref[...]
        o_ref[...] = _layer_norm(y, g_ref[...], b_ref[...])


def _ffn_ln(h, w_in, w_out, g, b, *, tm, tf):
    m = h.shape[0]
    nf = D_FF // tf
    return pl.pallas_call(
        _ffn_ln_kernel,
        grid=(m // tm, nf),
        in_specs=[
            pl.BlockSpec((tm, D_MODEL), lambda i, j: (i, 0)),
            pl.BlockSpec((D_MODEL, tf), lambda i, j: (0, j)),
            pl.BlockSpec((D_MODEL, tf), lambda i, j: (0, j + nf)),
            pl.BlockSpec((tf, D_MODEL), lambda i, j: (j, 0)),
            pl.BlockSpec((1, D_MODEL), lambda i, j: (0, 0)),
            pl.BlockSpec((1, D_MODEL), lambda i, j: (0, 0)),
        ],
        out_specs=pl.BlockSpec((tm, D_MODEL), lambda i, j: (i, 0)),
        out_shape=jax.ShapeDtypeStruct((m, D_MODEL), F32),
        scratch_shapes=[pltpu.VMEM((tm, D_MODEL), F32)],
        compiler_params=_cparams("parallel", "arbitrary"),
        name="ffn_ln",
    )(h, w_in, w_in, w_out, g, b)


def _proj_kernel(x_ref, w_ref, o_ref):
    o_ref[...] = _dot(x_ref[...].astype(BF16), w_ref[...])


def _proj(h, w, *, tm, tn):
    m, k = h.shape
    n = w.shape[1]
    return pl.pallas_call(
        _proj_kernel,
        grid=(m // tm, n // tn),
        in_specs=[
            pl.BlockSpec((tm, k), lambda i, j: (i, 0)),
            pl.BlockSpec((k, tn), lambda i, j: (0, j)),
        ],
        out_specs=pl.BlockSpec((tm, tn), lambda i, j: (i, j)),
        out_shape=jax.ShapeDtypeStruct((m, n), F32),
        compiler_params=_cparams("parallel", "arbitrary"),
        name="in_proj",
    )(h, w)


def _out_ln_kernel(*refs, n_parts):
    a_refs = refs[:n_parts]
    w_refs = refs[n_parts:2 * n_parts]
    h_ref, g_ref, b_ref, o_ref = refs[2 * n_parts:]
    mix = _dot(a_refs[0][...], w_refs[0][...])
    for a_ref, w_ref in zip(a_refs[1:], w_refs[1:]):
        mix = mix + _dot(a_ref[...], w_ref[...])
    y = DEEPNORM_ALPHA * h_ref[...] + mix
    o_ref[...] = _layer_norm(y, g_ref[...], b_ref[...])


def _out_ln(parts, weights, h, g, b, *, tm):
    m = h.shape[0]
    n_parts = len(parts)
    in_specs = [pl.BlockSpec((tm, a.shape[1]), lambda i: (i, 0)) for a in parts]
    in_specs += [pl.BlockSpec(w.shape, lambda i: (0, 0)) for w in weights]
    in_specs += [
        pl.BlockSpec((tm, D_MODEL), lambda i: (i, 0)),
        pl.BlockSpec((1, D_MODEL), lambda i: (0, 0)),
        pl.BlockSpec((1, D_MODEL), lambda i: (0, 0)),
    ]
    return pl.pallas_call(
        functools.partial(_out_ln_kernel, n_parts=n_parts),
        grid=(m // tm,),
        in_specs=in_specs,
        out_specs=pl.BlockSpec((tm, D_MODEL), lambda i: (i, 0)),
        out_shape=jax.ShapeDtypeStruct((m, D_MODEL), F32),
        compiler_params=_cparams("parallel"),
        name="out_proj_ln",
    )(*parts, *weights, h, g, b)


def _causal_conv(cbuf, cw_ref, rows):
    acc = cw_ref[0:1, :] * cbuf[pl.ds(SUBLANES - CONV_K + 1, rows), :]
    for i in range(1, CONV_K):
        acc = acc + cw_ref[i:i + 1, :] * cbuf[pl.ds(SUBLANES - CONV_K + 1 + i, rows), :]
    return acc


def _ssd_kernel(z_ref, x_ref, bm_ref, cm_ref, sm_ref, cw_ref, cb_ref, dtb_ref, alog_ref,
                dskip_ref, nw_ref, eh_ref, o_ref, cbuf, state):
    L = SSD_CHUNK
    W = SSD_D_INNER
    GW = W // SSD_GROUPS
    N = SSD_STATE

    @pl.when(pl.program_id(1) == 0)
    def _():
        cbuf[0:SUBLANES, :] = jnp.zeros((SUBLANES, SSD_CONV_DIM), F32)
        state[...] = jnp.zeros_like(state)

    cbuf[SUBLANES:SUBLANES + L, 0:W] = x_ref[...]
    cbuf[SUBLANES:SUBLANES + L, W:W + SSD_BC_W] = bm_ref[...]
    cbuf[SUBLANES:SUBLANES + L, W + SSD_BC_W:SSD_CONV_DIM] = cm_ref[...]
    act = _silu(_causal_conv(cbuf, cw_ref, L) + cb_ref[...])
    cbuf[0:SUBLANES, :] = cbuf[L:L + SUBLANES, :]
    xs = act[:, 0:W]
    bm = act[:, W:W + SSD_BC_W].astype(BF16)
    cm = act[:, W + SSD_BC_W:SSD_CONV_DIM].astype(BF16)

    lane = lax.broadcasted_iota(jnp.int32, (1, LANES), 1)
    dt_lanes = (lane >= 16) & (lane < 16 + SSD_HEADS)
    dt_full = jnp.where(dt_lanes, _softplus(sm_ref[...] + dtb_ref[...]), 0.0)
    adt_full = dt_full * (-jnp.exp(alog_ref[...]))
    r_i = lax.broadcasted_iota(jnp.int32, (L, L), 0)
    c_i = lax.broadcasted_iota(jnp.int32, (L, L), 1)
    tril = r_i >= c_i
    acs_full = _dot_sel_lhs(jnp.where(tril, 1.0, 0.0), adt_full)
    acs_t = acs_full.T
    eh = eh_ref[...]
    dt_exp = _dot_sel_rhs(dt_full, eh)
    acs_exp = _dot_sel_rhs(acs_full, eh)
    a_last = acs_exp[L - 1:L, :]
    xdt = xs * dt_exp
    xdec = (xdt * jnp.exp(a_last - acs_exp)).astype(BF16)
    e_acs = jnp.exp(acs_exp)
    e_last = jnp.exp(a_last)
    lane_w = lax.broadcasted_iota(jnp.int32, (1, W), 1)
    lo_half = (lane_w & (LANES - 1)) < SSD_HEADDIM
    xdt_b = xdt.astype(BF16)
    xdt_lo = jnp.where(lo_half, xdt_b, jnp.zeros_like(xdt_b))
    xdt_hi = jnp.where(lo_half, jnp.zeros_like(xdt_b), xdt_b)

    heads_per_group = SSD_HEADS // SSD_GROUPS
    y_parts = []
    for g in range(SSD_GROUPS):
        bg = bm[:, g * N:(g + 1) * N]
        cg = cm[:, g * N:(g + 1) * N]
        cb = _dot_nt(cg, bg)
        s_g = state[g]
        y_off = _dot(cg, s_g.astype(BF16)) * e_acs[:, g * GW:(g + 1) * GW]
        for pr in range(heads_per_group // 2):
            y_pair = None
            for var in range(2):
                h = g * heads_per_group + 2 * pr + var
                col = acs_full[:, 16 + h:17 + h]
                row = acs_t[16 + h:17 + h, :]
                seg = jnp.exp(jnp.where(tril, col - row, -jnp.inf))
                mh = (cb * seg).astype(BF16)
                src = xdt_lo if var == 0 else xdt_hi
                lanes0 = (g * heads_per_group + 2 * pr) * SSD_HEADDIM
                d = _dot(mh, src[:, lanes0:lanes0 + LANES])
                y_pair = d if y_pair is None else y_pair + d
            off = pr * LANES
            y_parts.append(y_pair + y_off[:, off:off + LANES])
        upd = lax.dot_general(bg, xdec[:, g * GW:(g + 1) * GW], (((0,), (0,)), ((), ())),
                              preferred_element_type=F32)
        state[g] = s_g * e_last[:, g * GW:(g + 1) * GW] + upd
    y = jnp.concatenate(y_parts, axis=1) + xs * dskip_ref[...]
    y = y * _silu(z_ref[...])
    outs = []
    for g in range(SSD_GROUPS):
        yg = y[:, g * GW:(g + 1) * GW]
        outs.append(yg * lax.rsqrt(jnp.mean(yg * yg, axis=-1, keepdims=True) + RMS_EPS))
    o_ref[...] = (jnp.concatenate(outs, axis=1) * nw_ref[...]).astype(o_ref.dtype)


def _ssd(proj, bsz, t_len, cw, cb, dtb_row, alog_row, dskip_row, nw_row, eh):
    L = SSD_CHUNK
    nc = t_len // L
    W = SSD_D_INNER
    z_blk = (GDN_CONV_DIM + GDN_V_W) // W
    x_blk = (GDN_CONV_DIM + GDN_V_W + W) // W
    b_blk = (GDN_CONV_DIM + GDN_V_W + 2 * W) // SSD_BC_W
    sm_blk = HYB_SMALL_OFF // LANES
    row = lambda b, c: b * nc + c
    const = lambda b, c: (0, 0)
    return pl.pallas_call(
        _ssd_kernel,
        grid=(bsz, nc),
        in_specs=[
            pl.BlockSpec((L, W), lambda b, c: (row(b, c), z_blk)),
            pl.BlockSpec((L, W), lambda b, c: (row(b, c), x_blk)),
            pl.BlockSpec((L, SSD_BC_W), lambda b, c: (row(b, c), b_blk)),
            pl.BlockSpec((L, SSD_BC_W), lambda b, c: (row(b, c), b_blk + 1)),
            pl.BlockSpec((L, LANES), lambda b, c: (row(b, c), sm_blk)),
            pl.BlockSpec((CONV_K, SSD_CONV_DIM), const),
            pl.BlockSpec((1, SSD_CONV_DIM), const),
            pl.BlockSpec((1, LANES), const),
            pl.BlockSpec((1, LANES), const),
            pl.BlockSpec((1, W), const),
            pl.BlockSpec((1, W), const),
            pl.BlockSpec((LANES, W), const),
        ],
        out_specs=pl.BlockSpec((L, W), lambda b, c: (row(b, c), 0)),
        out_shape=jax.ShapeDtypeStruct((bsz * t_len, W), BF16),
        scratch_shapes=[
            pltpu.VMEM((L + SUBLANES, SSD_CONV_DIM), F32),
            pltpu.VMEM((SSD_GROUPS, SSD_STATE, W // SSD_GROUPS), F32),
        ],
        compiler_params=_cparams("parallel", "arbitrary"),
        name="ssd",
    )(proj, proj, proj, proj, proj, cw, cb, dtb_row, alog_row, dskip_row, nw_row, eh)


def _unit_lower_inverse_minus_eye(a, r_i, c_i):
    n = range(len(a))
    diag8 = (r_i >> 3) == (c_i >> 3)
    x = [jnp.where(diag8, -a[i], 0.0) for i in n]
    x_b = [x[i].astype(BF16) for i in n]
    x2 = [_dot(x_b[i], x_b[i]) for i in n]
    x2_b = [x2[i].astype(BF16) for i in n]
    x4 = [_dot(x2_b[i], x2_b[i]) for i in n]
    e = [x[i] + x2[i] + _dot(x_b[i], x2_b[i]) for i in n]
    e = [e[i] + x4[i] + _dot(e[i].astype(BF16), x4[i].astype(BF16)) for i in n]
    for sh in (3, 4, 5):
        same_pair = (r_i >> (sh + 1)) == (c_i >> (sh + 1))
        lower_left = (((r_i >> sh) & 1) == 1) & (((c_i >> sh) & 1) == 0)
        ms = [jnp.where(same_pair & lower_left, a[i], 0.0) for i in n]
        y = [ms[i] + _dot(ms[i].astype(BF16), e[i].astype(BF16)) for i in n]
        e = [e[i] - y[i] - _dot(e[i].astype(BF16), y[i].astype(BF16)) for i in n]
    return e


def _stack_heads(x, r0, l0):
    return jnp.concatenate([x[r0:r0 + GDN_CHUNK, l0:l0 + GDN_DV],
                            x[r0:r0 + GDN_CHUNK, l0 + GDN_DV:l0 + 2 * GDN_DV]], axis=0)


def _own_head_block(x):
    return jnp.concatenate([x[0:GDN_CHUNK, 0:GDN_DV], x[GDN_CHUNK:, GDN_DV:]], axis=0)


def _gdn_kernel(q_ref, k_ref, v_ref, z_ref, sm_ref, cw_ref, dtb_ref, alog_ref, nw_ref,
                eb_ref, ea_ref, o_ref, cbuf, state, *, tc):
    C = GDN_CHUNK
    DK = GDN_DK
    DV = GDN_DV
    P = 2 * C

    @pl.when(pl.program_id(1) == 0)
    def _():
        cbuf[0:SUBLANES, :] = jnp.zeros((SUBLANES, GDN_CONV_DIM), F32)
        state[...] = jnp.zeros_like(state)

    cbuf[SUBLANES:SUBLANES + tc, 0:GDN_QK_W] = q_ref[...]
    cbuf[SUBLANES:SUBLANES + tc, GDN_QK_W:2 * GDN_QK_W] = k_ref[...]
    cbuf[SUBLANES:SUBLANES + tc, 2 * GDN_QK_W:GDN_CONV_DIM] = v_ref[...]
    act = _silu(_causal_conv(cbuf, cw_ref, tc))
    cbuf[0:SUBLANES, :] = cbuf[tc:tc + SUBLANES, :]

    lane = lax.broadcasted_iota(jnp.int32, (1, LANES), 1)
    sm = sm_ref[...]
    beta_full = jnp.where(lane < GDN_V_HEADS, jax.nn.sigmoid(sm), 0.0)
    g_lanes = (lane >= GDN_V_HEADS) & (lane < 2 * GDN_V_HEADS)
    g_full = jnp.where(g_lanes, -jnp.exp(alog_ref[...]) * _softplus(sm + dtb_ref[...]), 0.0)
    rt = lax.broadcasted_iota(jnp.int32, (tc, tc), 0)
    ct = lax.broadcasted_iota(jnp.int32, (tc, tc), 1)
    same_chunk_tril = ((rt >> 6) == (ct >> 6)) & (rt >= ct)
    gc_full = _dot_sel_lhs(jnp.where(same_chunk_tril, 1.0, 0.0), g_full)
    b_exp = _dot_sel_rhs(beta_full, eb_ref[...])
    g_exp = _dot_sel_rhs(gc_full, ea_ref[...])
    eg_exp = jnp.exp(g_exp)

    r_i = lax.broadcasted_iota(jnp.int32, (P, P), 0)
    c_i = lax.broadcasted_iota(jnp.int32, (P, P), 1)
    same_head = (r_i >> 6) == (c_i >> 6)
    tril = same_head & (r_i >= c_i)
    strict = same_head & (r_i > c_i)
    head0_rows = lax.broadcasted_iota(jnp.int32, (P, 1), 0) < C
    nw = nw_ref[...]

    n_chunks = tc // C
    units = [(hq, ci) for ci in range(n_chunks) for hq in range(GDN_QK_HEADS)]
    qn, kn = [], []
    for hq in range(GDN_QK_HEADS):
        qh = act[:, hq * DK:(hq + 1) * DK]
        kh = act[:, GDN_QK_W + hq * DK:GDN_QK_W + (hq + 1) * DK]
        qn.append(qh * lax.rsqrt(jnp.sum(qh * qh, axis=-1, keepdims=True) + 1e-6) * (DK ** -0.5))
        kn.append(kh * lax.rsqrt(jnp.sum(kh * kh, axis=-1, keepdims=True) + 1e-6))
    pre = {}
    a_low = []
    for hq, ci in units:
        r0 = ci * C
        l0 = hq * 2 * DV
        q_c = qn[hq][r0:r0 + C]
        k_c = kn[hq][r0:r0 + C]
        q2 = jnp.concatenate([q_c, q_c], axis=0)
        k2 = jnp.concatenate([k_c, k_c], axis=0)
        k2_b = k2.astype(BF16)
        kk = _dot_nt(k2_b, k2_b)
        qk = _dot_nt(q2.astype(BF16), k2_b)
        gcol = _stack_heads(g_exp, r0, l0)
        bcol = _stack_heads(b_exp, r0, l0)
        egc = _stack_heads(eg_exp, r0, l0)
        dmat = jnp.exp(jnp.where(tril, gcol - gcol.T, -jnp.inf))
        a_low.append(jnp.where(strict, kk * bcol * dmat, 0.0))
        g_last = jnp.where(head0_rows, gcol[C - 1:C], gcol[P - 1:P])
        pre[hq, ci] = dict(
            vb=_stack_heads(act, r0, 2 * GDN_QK_W + l0) * bcol,
            kb=k2 * (bcol * egc),
            qd=(q2 * egc).astype(BF16),
            attn=(qk * dmat).astype(BF16),
            k_t=k_c.T.astype(BF16),
            v_scale=jnp.exp(g_last - gcol),
            decay_cat=jnp.concatenate([egc[C - 1:C], egc[P - 1:P]], axis=1))
    e_all = _unit_lower_inverse_minus_eye(a_low, r_i, c_i)
    for (hq, ci), e in zip(units, e_all):
        d = pre[hq, ci]
        e_b = e.astype(BF16)
        d["u"] = d["vb"] + _dot(e_b, d["vb"].astype(BF16))
        d["w"] = (d["kb"] + _dot(e_b, d["kb"].astype(BF16))).astype(BF16)

    s_cat = [state[hq] for hq in range(GDN_QK_HEADS)]
    for ci in range(n_chunks):
        r0 = ci * C
        for hq in range(GDN_QK_HEADS):
            d = pre[hq, ci]
            l0 = hq * 2 * DV
            s_b = s_cat[hq].astype(BF16)
            v_new = d["u"] - _own_head_block(_dot(d["w"], s_b))
            o2 = _own_head_block(_dot(d["qd"], s_b)) + _dot(d["attn"], v_new.astype(BF16))
            v_dec = (v_new * d["v_scale"]).astype(BF16)
            v_dec_cat = jnp.concatenate([v_dec[0:C], v_dec[C:]], axis=1)
            s_cat[hq] = s_cat[hq] * d["decay_cat"] + _dot(d["k_t"], v_dec_cat)
            o_n = o2 * lax.rsqrt(jnp.mean(o2 * o2, axis=-1, keepdims=True) + RMS_EPS)
            res = (o_n * nw * _silu(_stack_heads(z_ref, r0, l0))).astype(o_ref.dtype)
            o_ref[r0:r0 + C, l0:l0 + DV] = res[0:C]
            o_ref[r0:r0 + C, l0 + DV:l0 + 2 * DV] = res[C:]
    for hq in range(GDN_QK_HEADS):
        state[hq] = s_cat[hq]


def _gdn(proj, bsz, t_len, cw, dtb_row, alog_row, nw_row, eb, ea, *, tc):
    nt = t_len // tc
    row = lambda b, t: b * nt + t
    const = lambda b, t: (0, 0)
    sm_blk = HYB_SMALL_OFF // LANES
    return pl.pallas_call(
        functools.partial(_gdn_kernel, tc=tc),
        grid=(bsz, nt),
        in_specs=[
            pl.BlockSpec((tc, GDN_QK_W), lambda b, t: (row(b, t), 0)),
            pl.BlockSpec((tc, GDN_QK_W), lambda b, t: (row(b, t), 1)),
            pl.BlockSpec((tc, GDN_V_W), lambda b, t: (row(b, t), 1)),
            pl.BlockSpec((tc, GDN_V_W), lambda b, t: (row(b, t), 2)),
            pl.BlockSpec((tc, LANES), lambda b, t: (row(b, t), sm_blk)),
            pl.BlockSpec((CONV_K, GDN_CONV_DIM), const),
            pl.BlockSpec((1, LANES), const),
            pl.BlockSpec((1, LANES), const),
            pl.BlockSpec((1, GDN_DV), const),
            pl.BlockSpec((LANES, GDN_V_W), const),
            pl.BlockSpec((LANES, GDN_V_W), const),
        ],
        out_specs=pl.BlockSpec((tc, GDN_V_W), lambda b, t: (row(b, t), 0)),
        out_shape=jax.ShapeDtypeStruct((bsz * t_len, GDN_V_W), BF16),
        scratch_shapes=[
            pltpu.VMEM((tc + SUBLANES, GDN_CONV_DIM), F32),
            pltpu.VMEM((GDN_QK_HEADS, GDN_DK, 2 * GDN_DV), F32),
        ],
        compiler_params=_cparams("parallel", "arbitrary"),
        name="gdn",
    )(proj, proj, proj, proj, proj, cw, dtb_row, alog_row, nw_row, eb, ea)


def _group_variants(x):
    lane = lax.broadcasted_iota(jnp.int32, (1, LANES), 1)
    lo = lane < (LANES // 2)
    xr = pltpu.roll(x, LANES // 2, axis=1)
    zero = jnp.zeros_like(x)
    return jnp.concatenate([jnp.where(lo, x, zero), jnp.where(lo, zero, xr),
                            jnp.where(lo, xr, zero), jnp.where(lo, zero, x)], axis=1)


def _values_transposed(v):
    zt = _group_variants(v).T
    extra_r = lax.broadcasted_iota(jnp.int32, (VT_ROWS - LANES, v.shape[0]), 0)
    parts = []
    for var in range(2 * NSA_GROUPS):
        parts.append(zt[var * LANES:(var + 1) * LANES])
        parts.append(jnp.where(extra_r == (var % 2), 1.0, 0.0))
    return jnp.concatenate(parts, axis=0)


def _nsa_prep_kernel(q_ref, ks_ref, vs_ref, kw_ref, vw_ref, pos_ref, freq_ref, s1_ref, s2_ref,
                     qc_o, qr_o, ksz_o, vsz_o, kwz_o, vwz_o):
    scale = NSA_DK ** -0.5 * LOG2E
    ang = pos_ref[...].astype(F32) * freq_ref[...]
    cos = jnp.cos(ang)
    sin = jnp.sin(ang)
    sin_up = sin * s1_ref[...]
    sin_dn = sin * s2_ref[...]
    half = ROPE_DIM // 2

    def rope(x):
        return x * cos + pltpu.roll(x, half, axis=1) * sin_up + pltpu.roll(x, LANES - half, axis=1) * sin_dn

    for p in range(NSA_HEADS * NSA_DK // LANES):
        x = q_ref[:, p * LANES:(p + 1) * LANES]
        qc_o[:, p * LANES:(p + 1) * LANES] = (x * scale).astype(qc_o.dtype)
        qr_o[:, p * LANES:(p + 1) * LANES] = (rope(x) * scale).astype(qr_o.dtype)
    ksz_o[...] = _group_variants(rope(ks_ref[...])).astype(ksz_o.dtype)
    kwz_o[...] = _group_variants(rope(kw_ref[...])).astype(kwz_o.dtype)
    vsz_o[...] = _values_transposed(vs_ref[...]).astype(vsz_o.dtype)
    vwz_o[...] = _values_transposed(vw_ref[...]).astype(vwz_o.dtype)


def _nsa_prep(proj, pos_col, freq_row, s1_row, s2_row, bsz, t_len, *, tm):
    m = proj.shape[0]
    tiles_per_seq = t_len // tm
    seq_t = lambda i: (i // tiles_per_seq, 0, i % tiles_per_seq)
    qw = NSA_HEADS * NSA_DK
    kv0 = qw // LANES
    const = lambda i: (0, 0)
    zw = 4 * LANES
    return pl.pallas_call(
        _nsa_prep_kernel,
        grid=(m // tm,),
        in_specs=[
            pl.BlockSpec((tm, qw), lambda i: (i, 0)),
            pl.BlockSpec((tm, LANES), lambda i: (i, kv0 + 2)),
            pl.BlockSpec((tm, LANES), lambda i: (i, kv0 + 3)),
            pl.BlockSpec((tm, LANES), lambda i: (i, kv0 + 4)),
            pl.BlockSpec((tm, LANES), lambda i: (i, kv0 + 5)),
            pl.BlockSpec((tm, 1), lambda i: (i, 0)),
            pl.BlockSpec((1, LANES), const),
            pl.BlockSpec((1, LANES), const),
            pl.BlockSpec((1, LANES), const),
        ],
        out_specs=[
            pl.BlockSpec((tm, qw), lambda i: (i, 0)),
            pl.BlockSpec((tm, qw), lambda i: (i, 0)),
            pl.BlockSpec((tm, zw), lambda i: (i, 0)),
            pl.BlockSpec((None, 4 * VT_ROWS, tm), seq_t),
            pl.BlockSpec((tm, zw), lambda i: (i, 0)),
            pl.BlockSpec((None, 4 * VT_ROWS, tm), seq_t),
        ],
        out_shape=[
            jax.ShapeDtypeStruct((m, qw), BF16),
            jax.ShapeDtypeStruct((m, qw), BF16),
            jax.ShapeDtypeStruct((m, zw), BF16),
            jax.ShapeDtypeStruct((bsz, 4 * VT_ROWS, t_len), BF16),
            jax.ShapeDtypeStruct((m, zw), BF16),
            jax.ShapeDtypeStruct((bsz, 4 * VT_ROWS, t_len), BF16),
        ],
        compiler_params=_cparams("parallel"),
        name="nsa_prep",
    )(proj, proj, proj, proj, proj, pos_col, freq_row, s1_row, s2_row)


def _nsa_compress_kernel(k16_ref, v16_ref, pos_ref, w1a_ref, w1b_ref, w2_ref, kcz_o, vcz_o):
    nr = k16_ref.shape[0]
    for idx, (x_ref, o_ref) in enumerate(((k16_ref, kcz_o), (v16_ref, vcz_o))):
        x = x_ref[...]
        h_a = _dot((x + pos_ref[idx, 0:1, :]).astype(BF16), w1a_ref[idx])
        h_b = _dot((x + pos_ref[idx, 1:2, :]).astype(BF16), w1b_ref[idx])
        hid = h_a + pltpu.roll(h_b, nr - 1, axis=0)
        out = _dot(_silu(hid).astype(BF16), w2_ref[idx])
        z = _group_variants(out)
        o_ref[...] = (z.T if idx == 1 else z).astype(o_ref.dtype)


def _nsa_compress(k16, v16, pos_ab, w1a, w1b, w2):
    bsz, nr, width = k16.shape
    hid = NSA_GROUPS * CMP_HIDDEN
    zw = 4 * LANES
    c3 = lambda b: (0, 0, 0)
    return pl.pallas_call(
        _nsa_compress_kernel,
        grid=(bsz,),
        in_specs=[
            pl.BlockSpec((None, nr, width), lambda b: (b, 0, 0)),
            pl.BlockSpec((None, nr, width), lambda b: (b, 0, 0)),
            pl.BlockSpec((2, 2, width), c3),
            pl.BlockSpec((2, width, hid), c3),
            pl.BlockSpec((2, width, hid), c3),
            pl.BlockSpec((2, hid, LANES), c3),
        ],
        out_specs=[
            pl.BlockSpec((None, nr, zw), lambda b: (b, 0, 0)),
            pl.BlockSpec((None, zw, nr), lambda b: (b, 0, 0)),
        ],
        out_shape=[
            jax.ShapeDtypeStruct((bsz, nr, zw), BF16),
            jax.ShapeDtypeStruct((bsz, zw, nr), BF16),
        ],
        compiler_params=_cparams("parallel"),
        name="nsa_compress",
    )(k16, v16, pos_ab, w1a, w1b, w2)


def _flash_branch(q_stacks, kz_ref, vzt_ref, lo, hi, bias_fns, kb_size):
    r_acc = lax.broadcasted_iota(jnp.int32, (VT_ROWS, 1), 0)
    even_rows = (r_acc < LANES // 2) | (r_acc == LANES)
    half_w = 2 * Q_BLOCK
    n_half = q_stacks[0].shape[0] // half_w
    units = [(g, h) for g in range(NSA_GROUPS) for h in range(n_half)]
    q_unit = [q_stacks[g][h * half_w:(h + 1) * half_w] for g, h in units]

    def body(kb, carry):
        m_in, acc = carry
        ks = pl.multiple_of(kb * kb_size, kb_size)
        k_both, vt_both, bias = [], [], []
        for g in range(NSA_GROUPS):
            k0 = 2 * g * LANES
            v0 = 2 * g * VT_ROWS
            k_both.append(jnp.concatenate([kz_ref[pl.ds(ks, kb_size), k0:k0 + LANES],
                                           kz_ref[pl.ds(ks, kb_size), k0 + LANES:k0 + 2 * LANES]], axis=0))
            vt_both.append(jnp.concatenate([vzt_ref[v0:v0 + VT_ROWS, pl.ds(ks, kb_size)],
                                            vzt_ref[v0 + VT_ROWS:v0 + 2 * VT_ROWS, pl.ds(ks, kb_size)]],
                                           axis=1))
            b = bias_fns[g](ks)
            bias.append(jnp.concatenate([b, b], axis=1))
        s = [_dot_nt(k_both[g], q_unit[u]) for u, (g, _) in enumerate(units)]
        m_out, alpha_rows, p_cat = [], [], []
        for u, (g, _) in enumerate(units):
            p_parts, alphas, m_news = [], [], []
            for var in range(2):
                s_v = s[u][var * kb_size:(var + 1) * kb_size] + bias[g]
                m_prev = m_in[u][var]
                m_new = jnp.maximum(m_prev, jnp.max(s_v, axis=0, keepdims=True))
                alphas.append(jnp.exp2(m_prev - m_new))
                p_parts.append(jnp.exp2(s_v - m_new).astype(BF16))
                m_news.append(m_new)
            m_out.append(tuple(m_news))
            alpha_rows.append(jnp.where(even_rows, alphas[0], alphas[1]))
            p_cat.append(jnp.concatenate(p_parts, axis=0))
        pv = [_dot(vt_both[g], p_cat[u]) for u, (g, _) in enumerate(units)]
        acc_out = tuple(acc[u] * alpha_rows[u] + pv[u] for u in range(len(units)))
        return tuple(m_out), acc_out

    m_init = tuple((jnp.full((1, half_w), NEG_BIG, F32),) * 2 for _ in units)
    acc_init = tuple(jnp.zeros((VT_ROWS, half_w), F32) for _ in units)
    _, acc = lax.fori_loop(lo, hi, body, (m_init, acc_init))
    outs = []
    for g in range(NSA_GROUPS):
        a = jnp.concatenate(acc[g * n_half:(g + 1) * n_half], axis=1)
        inv_even = 1.0 / a[LANES:LANES + 1, :]
        inv_odd = 1.0 / a[LANES + 1:LANES + 2, :]
        outs.append(a[0:LANES, :] * jnp.where(r_acc[0:LANES] < LANES // 2, inv_even, inv_odd))
    return outs


def _nsa_attn_kernel(qc_ref, qr_ref, gate_ref, kcz_ref, vczt_ref, ksz_ref, vszt_ref, kwz_ref, vwzt_ref,
                     aggt_ref, eselt_ref, o_ref, bias_ref,
                     *, t_len, kb_size):
    n_cmp_rows = t_len // CMP_STRIDE
    n_cmp = (t_len - CMP_LEN) // CMP_STRIDE + 1
    n_sel = t_len // SEL_LEN
    n_top = min(SEL_TOPK, n_sel)
    pairs = NSA_HEADS // NSA_GROUPS // 2
    q0 = pl.program_id(1) * Q_BLOCK
    t_row = q0 + lax.broadcasted_iota(jnp.int32, (1, Q_BLOCK), 1)
    t_row_stack = q0 + (lax.broadcasted_iota(jnp.int32, (1, pairs * Q_BLOCK), 1) & (Q_BLOCK - 1))
    blk = lax.broadcasted_iota(jnp.int32, (n_sel, 1), 0)
    c_idx = lax.broadcasted_iota(jnp.int32, (n_cmp_rows, 1), 0)
    cmp_mask = ((c_idx * CMP_STRIDE + (CMP_LEN - 1)) <= t_row_stack) & (c_idx < n_cmp)
    key_pos = lax.broadcasted_iota(jnp.int32, (t_len, 1), 0)
    gate_t = jax.nn.sigmoid(gate_ref[...]).T
    top_half = lax.broadcasted_iota(jnp.int32, (LANES, 1), 0) < (LANES // 2)
    hi = (q0 + (Q_BLOCK - 1)) // kb_size + 1
    win_lo = jnp.maximum(q0 - (WINDOW - 1), 0) // kb_size

    def win_bias(ks):
        kp = ks + lax.broadcasted_iota(jnp.int32, (kb_size, 1), 0)
        return jnp.where((kp <= t_row) & (kp > t_row - WINDOW), 0.0, NEG_BIG)

    def sel_bias(g):
        return lambda ks: bias_ref[g, pl.ds(ks, kb_size), :]

    qr_stacks, o_cmps = [], []
    for g in range(NSA_GROUPS):
        lanes_g = g * pairs * LANES
        qc_stack = jnp.concatenate(
            [qc_ref[:, lanes_g + p * LANES:lanes_g + (p + 1) * LANES] for p in range(pairs)], axis=0)
        qr_stacks.append(jnp.concatenate(
            [qr_ref[:, lanes_g + p * LANES:lanes_g + (p + 1) * LANES] for p in range(pairs)], axis=0))

        o_cmp = None
        p_sum = None
        for var in range(2):
            c0 = (2 * g + var) * LANES
            s = jnp.where(cmp_mask, _dot_nt(kcz_ref[:, c0:c0 + LANES], qc_stack), -jnp.inf)
            mx = jnp.max(s, axis=0, keepdims=True)
            mx = jnp.where(mx > -jnp.inf, mx, 0.0)
            e = jnp.exp2(s - mx)
            p = e * (1.0 / jnp.maximum(jnp.sum(e, axis=0, keepdims=True), 1e-30))
            d = _dot(vczt_ref[c0:c0 + LANES, :], p.astype(BF16))
            o_cmp = d if o_cmp is None else o_cmp + d
            for pr in range(pairs):
                part = p[:, pr * Q_BLOCK:(pr + 1) * Q_BLOCK]
                p_sum = part if p_sum is None else p_sum + part
        importance = _dot_sel_lhs(aggt_ref[...], p_sum)

        cur = t_row >> 6
        causal_blk = blk <= cur
        forced = (blk == 0) | (causal_blk & (blk > cur - SEL_LOCAL))
        score = jnp.where(forced, FORCE_SCORE, jnp.where(causal_blk, importance, -1.0))
        rank = jnp.zeros((n_sel, Q_BLOCK), F32)
        for jj in range(n_sel):
            row = score[jj:jj + 1, :]
            beats = (row > score) | ((row == score) & (blk > jj))
            rank = rank + jnp.where(beats, 1.0, 0.0)
        sel = jnp.where(rank < n_top, 1.0, 0.0)
        sel = jnp.concatenate([sel, jnp.zeros((LANES - n_sel, Q_BLOCK), F32)], axis=0)
        sel_keys = _dot(eselt_ref[...], sel.astype(BF16))
        bias_ref[g] = jnp.where((sel_keys > 0.5) & (key_pos <= t_row), 0.0, NEG_BIG)
        o_cmps.append(o_cmp)

    o_sels = _flash_branch(qr_stacks, ksz_ref, vszt_ref, 0, hi,
                           [sel_bias(g) for g in range(NSA_GROUPS)], kb_size)
    o_wins = _flash_branch(qr_stacks, kwz_ref, vwzt_ref, win_lo, hi, [win_bias] * NSA_GROUPS, kb_size)

    for g in range(NSA_GROUPS):
        lanes_g = g * pairs * LANES
        for pr in range(pairs):
            h_even = (g * pairs + pr) * 2
            cols = slice(pr * Q_BLOCK, (pr + 1) * Q_BLOCK)
            mixed = None
            for j, o_branch in enumerate((o_cmps[g], o_sels[g], o_wins[g])):
                r_e = h_even * 3 + j
                r_o = r_e + 3
                gate_rows = jnp.where(top_half, gate_t[r_e:r_e + 1, :], gate_t[r_o:r_o + 1, :])
                term = gate_rows * o_branch[:, cols]
                mixed = term if mixed is None else mixed + term
            c0 = lanes_g + pr * LANES
            o_ref[:, c0:c0 + LANES] = mixed.T.astype(o_ref.dtype)


def _nsa_attn(qc, qr, proj, kcz, vczt, ksz, vszt, kwz, vwzt, aggt, eselt, bsz, t_len, *, kb_size):
    nq = t_len // Q_BLOCK
    qw = NSA_HEADS * NSA_DK
    zw = 4 * LANES
    nr = t_len // CMP_STRIDE
    pairs = NSA_HEADS // NSA_GROUPS // 2
    row = lambda b, i: (b * nq + i, 0)
    per_b = lambda b, i: (b, 0, 0)
    return pl.pallas_call(
        functools.partial(_nsa_attn_kernel, t_len=t_len, kb_size=kb_size),
        grid=(bsz, nq),
        in_specs=[
            pl.BlockSpec((Q_BLOCK, qw), row),
            pl.BlockSpec((Q_BLOCK, qw), row),
            pl.BlockSpec((Q_BLOCK, LANES), lambda b, i: (b * nq + i, NSA_GATE_OFF // LANES)),
            pl.BlockSpec((None, nr, zw), per_b),
            pl.BlockSpec((None, zw, nr), per_b),
            pl.BlockSpec((None, t_len, zw), per_b),
            pl.BlockSpec((None, 4 * VT_ROWS, t_len), per_b),
            pl.BlockSpec((None, t_len, zw), per_b),
            pl.BlockSpec((None, 4 * VT_ROWS, t_len), per_b),
            pl.BlockSpec(aggt.shape, lambda b, i: (0, 0)),
            pl.BlockSpec(eselt.shape, lambda b, i: (0, 0)),
        ],
        out_specs=pl.BlockSpec((Q_BLOCK, qw), row),
        out_shape=jax.ShapeDtypeStruct((bsz * t_len, qw), BF16),
        scratch_shapes=[
            pltpu.VMEM((NSA_GROUPS, t_len, Q_BLOCK), F32),
        ],
        compiler_params=_cparams("parallel", "arbitrary"),
        name="nsa_attn",
    )(qc, qr, proj, kcz, vczt, ksz, vszt, kwz, vwzt, aggt, eselt)


def _expand_rows(first_row, n_heads, width):
    e = np.zeros((LANES, n_heads * width), np.float32)
    for h in range(n_heads):
        e[first_row + h, h * width:(h + 1) * width] = 1.0
    return e


def _nsa_constants(t_len):
    n_cmp = (t_len - CMP_LEN) // CMP_STRIDE + 1
    n_sel = t_len // SEL_LEN
    nr = t_len // CMP_STRIDE
    c0 = np.arange(n_cmp)[:, None] * CMP_STRIDE
    s0 = np.arange(n_sel)[None, :] * SEL_LEN
    overlap = np.clip(np.minimum(c0 + CMP_LEN, s0 + SEL_LEN) - np.maximum(c0, s0), 0, None) / CMP_LEN
    aggt = np.zeros((n_sel, nr), np.float32)
    aggt[:, :n_cmp] = overlap.T
    eselt = np.zeros((t_len, LANES), np.float32)
    eselt[np.arange(t_len), np.arange(t_len) // SEL_LEN] = 1.0
    half = ROPE_DIM // 2
    inv_freq = (ROPE_THETA ** (-np.arange(half) / half)).astype(np.float32)
    d = np.arange(LANES) % NSA_DK
    freq = np.where(d < ROPE_DIM, inv_freq[d % half], 0.0).astype(np.float32)[None]
    s1 = ((d >= half) & (d < ROPE_DIM)).astype(np.float32)[None]
    s2 = -(d < half).astype(np.float32)[None]
    return aggt, eselt, freq, s1, s2


def _pad_row(vec, offset):
    return jnp.zeros((1, LANES), F32).at[0, offset:offset + vec.shape[0]].set(vec.astype(F32))


def _hybrid_layer(h, bsz, t_len, w_in, gdn_conv_w, gdn_a_log, gdn_dt_bias, gdn_norm_w, ssd_conv_w,
                  ssd_conv_b, ssd_a_log, ssd_dt_bias, ssd_d, ssd_norm_w, w_out, ln_g, ln_b, cfg):
    b_off = GDN_CONV_DIM + GDN_V_W
    z_off = b_off + 2 * GDN_V_HEADS
    dt_off = z_off + SSD_D_INNER + SSD_CONV_DIM
    w_pad = jnp.concatenate(
        [w_in[:, :b_off], w_in[:, z_off:dt_off], w_in[:, b_off:z_off], w_in[:, dt_off:],
         jnp.zeros((D_MODEL, HYB_PAD - w_in.shape[1]), w_in.dtype)], axis=1).astype(BF16)
    proj = _proj(h, w_pad, tm=cfg["proj_tm"], tn=cfg["hyb_tn"])
    o_a = _gdn(proj, bsz, t_len, gdn_conv_w, _pad_row(gdn_dt_bias, GDN_V_HEADS),
               _pad_row(gdn_a_log, GDN_V_HEADS), gdn_norm_w[None].astype(F32),
               jnp.asarray(_expand_rows(0, GDN_V_HEADS, GDN_DV)).astype(BF16),
               jnp.asarray(_expand_rows(GDN_V_HEADS, GDN_V_HEADS, GDN_DV)).astype(BF16), tc=cfg["gdn_tc"])
    o_b = _ssd(proj, bsz, t_len, ssd_conv_w, ssd_conv_b[None], _pad_row(ssd_dt_bias, 16),
               _pad_row(ssd_a_log, 16), jnp.repeat(ssd_d, SSD_HEADDIM)[None], ssd_norm_w[None],
               jnp.asarray(_expand_rows(16, SSD_HEADS, SSD_HEADDIM)).astype(BF16))
    w_out_b = w_out.astype(BF16)
    return _out_ln([o_a, o_b], [w_out_b[:GDN_V_W], w_out_b[GDN_V_W:]], h, ln_g, ln_b, tm=cfg["out_tm"])


def _nsa_layer(h, pos_col, bsz, t_len, w_in, cmp_pos, cmp_w1, cmp_w2, w_out, ln_g, ln_b, cfg):
    aggt, eselt, freq, s1, s2 = _nsa_constants(t_len)
    w_pad = jnp.concatenate(
        [w_in, jnp.zeros((D_MODEL, NSA_PAD - w_in.shape[1]), w_in.dtype)], axis=1).astype(BF16)
    proj = _proj(h, w_pad, tm=cfg["proj_tm"], tn=cfg["nsa_tn"])
    qc, qr, ksz, vszt, kwz, vwzt = _nsa_prep(proj, pos_col, jnp.asarray(freq), jnp.asarray(s1),
                                             jnp.asarray(s2), bsz, t_len, tm=cfg["prep_tm"])
    qw = NSA_HEADS * NSA_DK
    nr = t_len // CMP_STRIDE
    width = CMP_STRIDE * LANES
    k16 = proj[:, qw:qw + LANES].reshape(bsz, nr, width)
    v16 = proj[:, qw + LANES:qw + 2 * LANES].reshape(bsz, nr, width)
    pos2 = jnp.broadcast_to(cmp_pos[:, :, None, :], (2, CMP_LEN, NSA_GROUPS, NSA_DK)).reshape(2, 2, width)
    w1 = cmp_w1.reshape(2, 2, CMP_STRIDE, NSA_DK, CMP_HIDDEN)
    eye_g = jnp.eye(NSA_GROUPS, dtype=w1.dtype)
    w1x = jnp.einsum("ksjdh,ge->ksjgdeh", w1, eye_g).reshape(
        2, 2, width, NSA_GROUPS * CMP_HIDDEN).astype(BF16)
    w2x = jnp.einsum("khd,ge->kghed", cmp_w2, eye_g).reshape(
        2, NSA_GROUPS * CMP_HIDDEN, NSA_GROUPS * NSA_DK).astype(BF16)
    kcz, vczt = _nsa_compress(k16, v16, pos2, w1x[:, 0], w1x[:, 1], w2x)
    o = _nsa_attn(qc, qr, proj, kcz, vczt, ksz.reshape(bsz, t_len, -1), vszt,
                  kwz.reshape(bsz, t_len, -1), vwzt,
                  jnp.asarray(aggt).astype(BF16), jnp.asarray(eselt).astype(BF16),
                  bsz, t_len, kb_size=cfg["attn_kb"])
    return _out_ln([o], [w_out.astype(BF16)], h, ln_g, ln_b, tm=cfg["out_tm"])


def _config(m, t_len):
    return dict(ffn_tm=min(512, m), ffn_tf=1408, proj_tm=min(512, m), hyb_tn=1152, nsa_tn=640,
                out_tm=min(512, m), gdn_tc=min(256, t_len), prep_tm=min(256, m),
                attn_kb=min(256, t_len))


def kernel(x, positions, ln_g, ln_b, ffn_w_in, ffn_w_out, hyb_w_in, gdn_conv_w, gdn_a_log, gdn_dt_bias, gdn_norm_w, ssd_conv_w, ssd_conv_b, ssd_a_log, ssd_dt_bias, ssd_d, ssd_norm_w, hyb_w_out, nsa_w_in, nsa_cmp_pos, nsa_cmp_w1, nsa_cmp_w2, nsa_w_out):
    bsz, t_len, _ = x.shape
    m = bsz * t_len
    cfg = _config(m, t_len)
    h = x.reshape(m, D_MODEL)
    pos_col = positions.reshape(m, 1)
    w_in_b = ffn_w_in.astype(BF16)
    w_out_b = ffn_w_out.astype(BF16)
    ln_g = ln_g[:, :, None, :]
    ln_b = ln_b[:, :, None, :]
    for layer in range(DEPTH):
        i = layer // 2
        h = _ffn_ln(h, w_in_b[layer, 0], w_out_b[layer, 0], ln_g[layer, 0], ln_b[layer, 0],
                    tm=cfg["ffn_tm"], tf=cfg["ffn_tf"])
        if layer % 2 == 0:
            h = _hybrid_layer(h, bsz, t_len, hyb_w_in[i], gdn_conv_w[i], gdn_a_log[i], gdn_dt_bias[i],
                              gdn_norm_w[i], ssd_conv_w[i], ssd_conv_b[i], ssd_a_log[i], ssd_dt_bias[i],
                              ssd_d[i], ssd_norm_w[i], hyb_w_out[i], ln_g[layer, 1], ln_b[layer, 1], cfg)
        else:
            h = _nsa_layer(h, pos_col, bsz, t_len, nsa_w_in[i], nsa_cmp_pos[i], nsa_cmp_w1[i],
                           nsa_cmp_w2[i], nsa_w_out[i], ln_g[layer, 1], ln_b[layer, 1], cfg)
        h = _ffn_ln(h, w_in_b[layer, 1], w_out_b[layer, 1], ln_g[layer, 2], ln_b[layer, 2],
                    tm=cfg["ffn_tm"], tf=cfg["ffn_tf"])
    return h.reshape(bsz, t_len, D_MODEL)
```

```python
import functools

import numpy as np
import jax
import jax.numpy as jnp
from jax import lax
from jax.experimental import pallas as pl
from jax.experimental.pallas import tpu as pltpu

F32 = jnp.float32
BF16 = jnp.bfloat16

D_MODEL = 1024
DEPTH = 4
D_FF = 2816
DEEPNORM_ALPHA = (2.0 * DEPTH) ** 0.25
LN_EPS = 1e-5
RMS_EPS = 1e-6
CONV_K = 4

GDN_QK_HEADS = 4
GDN_V_HEADS = 8
GDN_DK = 128
GDN_DV = 128
GDN_CHUNK = 64
GDN_QK_W = GDN_QK_HEADS * GDN_DK
GDN_V_W = GDN_V_HEADS * GDN_DV
GDN_CONV_DIM = 2 * GDN_QK_W + GDN_V_W

SSD_D_INNER = D_MODEL
SSD_HEADDIM = 64
SSD_HEADS = SSD_D_INNER // SSD_HEADDIM
SSD_GROUPS = 2
SSD_STATE = 128
SSD_CHUNK = 128
SSD_BC_W = SSD_GROUPS * SSD_STATE
SSD_CONV_DIM = SSD_D_INNER + 2 * SSD_BC_W

NSA_HEADS = 16
NSA_GROUPS = 2
NSA_DK = 64
NSA_DV = 64
ROPE_DIM = NSA_DK // 4
ROPE_THETA = 500000.0
CMP_LEN = 32
CMP_STRIDE = 16
CMP_HIDDEN = 256
SEL_LEN = 64
SEL_TOPK = 8
SEL_LOCAL = 2
FORCE_SCORE = 1e4
WINDOW = 512
Q_BLOCK = 128

LANES = 128
SUBLANES = 8
VMEM_LIMIT_BYTES = 48 * 1024 * 1024

HYB_SMALL_OFF = GDN_CONV_DIM + GDN_V_W + SSD_D_INNER + SSD_CONV_DIM
HYB_PAD = HYB_SMALL_OFF + LANES
NSA_GATE_OFF = NSA_HEADS * NSA_DK + 3 * NSA_GROUPS * (NSA_DK + NSA_DV)
NSA_PAD = NSA_GATE_OFF + LANES

NEG_BIG = -1e30
LOG2E = 1.4426950408889634
VT_ROWS = LANES + 16


def _cparams(*sem):
    return pltpu.CompilerParams(dimension_semantics=sem, vmem_limit_bytes=VMEM_LIMIT_BYTES)


def _silu(v):
    return v * jax.nn.sigmoid(v)


def _softplus(v):
    return jnp.maximum(v, 0.0) + jnp.log1p(jnp.exp(-jnp.abs(v)))


def _layer_norm(y, g, b):
    mu = jnp.mean(y, axis=-1, keepdims=True)
    d = y - mu
    var = jnp.mean(d * d, axis=-1, keepdims=True)
    return d * lax.rsqrt(var + LN_EPS) * g + b


def _dot(a, b):
    return jnp.dot(a, b, preferred_element_type=F32)


def _split3(x):
    hi = x.astype(BF16)
    r1 = x - hi.astype(F32)
    mid = r1.astype(BF16)
    lo = (r1 - mid.astype(F32)).astype(BF16)
    return hi, mid, lo


def _dot_sel_rhs(x, sel):
    sel = sel.astype(BF16)
    hi, mid, lo = _split3(x)
    return _dot(hi, sel) + _dot(mid, sel) + _dot(lo, sel)


def _dot_sel_lhs(sel, x):
    sel = sel.astype(BF16)
    hi, mid, lo = _split3(x)
    return _dot(sel, hi) + _dot(sel, mid) + _dot(sel, lo)


def _dot_nt(a, b):
    return lax.dot_general(a, b, (((1,), (1,)), ((), ())), preferred_element_type=F32)


def _ffn_ln_kernel(x_ref, wi_ref, wo_ref, g_ref, b_ref, o_ref, *, tf):
    x = x_ref[...]
    xb = x.astype(BF16)
    acts = []
    for c in range(D_FF // tf):
        gate = _dot(xb, wi_ref[:, c * tf:(c + 1) * tf])
        up = _dot(xb, wi_ref[:, D_FF + c * tf:D_FF + (c + 1) * tf])
        acts.append((_silu(gate) * up).astype(BF16))
    mix = _dot(jnp.concatenate(acts, axis=1), wo_ref[...])
    y = DEEPNORM_ALPHA * x + 0.5 * mix
    o_ref[...] = _layer_norm(y, g_ref[...], b_ref[...])


def _resident(shape):
    return pl.BlockSpec(shape, lambda *_: (0,) * len(shape), pipeline_mode=pl.Buffered(1))


def _ffn_ln(h, w_in, w_out, g, b, *, tm, tf):
    m = h.shape[0]
    return pl.pallas_call(
        functools.partial(_ffn_ln_kernel, tf=tf),
        grid=(m // tm,),
        in_specs=[
            pl.BlockSpec((tm, D_MODEL), lambda i: (i, 0)),
            _resident(w_in.shape),
            _resident(w_out.shape),
            _resident((1, D_MODEL)),
            _resident((1, D_MODEL)),
        ],
        out_specs=pl.BlockSpec((tm, D_MODEL), lambda i: (i, 0)),
        out_shape=jax.ShapeDtypeStruct((m, D_MODEL), F32),
        compiler_params=_cparams("parallel"),
        name="ffn_ln",
    )(h, w_in, w_out, g, b)


def _proj_kernel(x_ref, w_ref, o_ref, *, tn):
    xb = x_ref[...].astype(BF16)
    for c in range(w_ref.shape[1] // tn):
        o_ref[:, c * tn:(c + 1) * tn] = _dot(xb, w_ref[:, c * tn:(c + 1) * tn])


def _proj(h, w, *, tm, tn):
    m, k = h.shape
    n = w.shape[1]
    return pl.pallas_call(
        functools.partial(_proj_kernel, tn=tn),
        grid=(m // tm,),
        in_specs=[
            pl.BlockSpec((tm, k), lambda i: (i, 0)),
            _resident(w.shape),
        ],
        out_specs=pl.BlockSpec((tm, n), lambda i: (i, 0)),
        out_shape=jax.ShapeDtypeStruct((m, n), F32),
        compiler_params=_cparams("parallel"),
        name="in_proj",
    )(h, w)


def _out_ln_kernel(*refs, n_parts):
    a_refs = refs[:n_parts]
    w_refs = refs[n_parts:2 * n_parts]
    h_ref, g_ref, b_ref, o_ref = refs[2 * n_parts:]
    mix = _dot(a_refs[0][...], w_refs[0][...])
    for a_ref, w_ref in zip(a_refs[1:], w_refs[1:]):
        mix = mix + _dot(a_ref[...], w_ref[...])
    y = DEEPNORM_ALPHA * h_ref[...] + mix
    o_ref[...] = _layer_norm(y, g_ref[...], b_ref[...])


def _out_ln(parts, weights, h, g, b, *, tm):
    m = h.shape[0]
    n_parts = len(parts)
    in_specs = [pl.BlockSpec((tm, a.shape[1]), lambda i: (i, 0)) for a in parts]
    in_specs += [pl.BlockSpec(w.shape, lambda i: (0, 0)) for w in weights]
    in_specs += [
        pl.BlockSpec((tm, D_MODEL), lambda i: (i, 0)),
        pl.BlockSpec((1, D_MODEL), lambda i: (0, 0)),
        pl.BlockSpec((1, D_MODEL), lambda i: (0, 0)),
    ]
    return pl.pallas_call(
        functools.partial(_out_ln_kernel, n_parts=n_parts),
        grid=(m // tm,),
        in_specs=in_specs,
        out_specs=pl.BlockSpec((tm, D_MODEL), lambda i: (i, 0)),
        out_shape=jax.ShapeDtypeStruct((m, D_MODEL), F32),
        compiler_params=_cparams("parallel"),
        name="out_proj_ln",
    )(*parts, *weights, h, g, b)


def _causal_conv(cbuf, cw_ref, rows):
    acc = cw_ref[0:1, :] * cbuf[pl.ds(SUBLANES - CONV_K + 1, rows), :]
    for i in range(1, CONV_K):
        acc = acc + cw_ref[i:i + 1, :] * cbuf[pl.ds(SUBLANES - CONV_K + 1 + i, rows), :]
    return acc


def _ssd_kernel(z_ref, x_ref, bm_ref, cm_ref, sm_ref, cw_ref, cb_ref, dtb_ref, alog_ref,
                dskip_ref, nw_ref, eh_ref, o_ref, cbuf, state):
    L = SSD_CHUNK
    W = SSD_D_INNER
    GW = W // SSD_GROUPS
    N = SSD_STATE

    @pl.when(pl.program_id(1) == 0)
    def _():
        cbuf[0:SUBLANES, :] = jnp.zeros((SUBLANES, SSD_CONV_DIM), F32)
        state[...] = jnp.zeros_like(state)

    cbuf[SUBLANES:SUBLANES + L, 0:W] = x_ref[...]
    cbuf[SUBLANES:SUBLANES + L, W:W + SSD_BC_W] = bm_ref[...]
    cbuf[SUBLANES:SUBLANES + L, W + SSD_BC_W:SSD_CONV_DIM] = cm_ref[...]
    act = _silu(_causal_conv(cbuf, cw_ref, L) + cb_ref[...])
    cbuf[0:SUBLANES, :] = cbuf[L:L + SUBLANES, :]
    xs = act[:, 0:W]
    bm = act[:, W:W + SSD_BC_W].astype(BF16)
    cm = act[:, W + SSD_BC_W:SSD_CONV_DIM].astype(BF16)

    lane = lax.broadcasted_iota(jnp.int32, (1, LANES), 1)
    dt_lanes = (lane >= 16) & (lane < 16 + SSD_HEADS)
    dt_full = jnp.where(dt_lanes, _softplus(sm_ref[...] + dtb_ref[...]), 0.0)
    adt_full = dt_full * (-jnp.exp(alog_ref[...]))
    r_i = lax.broadcasted_iota(jnp.int32, (L, L), 0)
    c_i = lax.broadcasted_iota(jnp.int32, (L, L), 1)
    tril = r_i >= c_i
    acs_full = _dot_sel_lhs(jnp.where(tril, 1.0, 0.0), adt_full)
    acs_t = acs_full.T
    eh = eh_ref[...]
    dt_exp = _dot_sel_rhs(dt_full, eh)
    acs_exp = _dot_sel_rhs(acs_full, eh)
    a_last = acs_exp[L - 1:L, :]
    xdt = xs * dt_exp
    xdec = (xdt * jnp.exp(a_last - acs_exp)).astype(BF16)
    e_acs = jnp.exp(acs_exp)
    e_last = jnp.exp(a_last)
    lane_w = lax.broadcasted_iota(jnp.int32, (1, W), 1)
    lo_half = (lane_w & (LANES - 1)) < SSD_HEADDIM
    xdt_b = xdt.astype(BF16)
    xdt_lo = jnp.where(lo_half, xdt_b, jnp.zeros_like(xdt_b))
    xdt_hi = jnp.where(lo_half, jnp.zeros_like(xdt_b), xdt_b)

    heads_per_group = SSD_HEADS // SSD_GROUPS
    y_parts = []
    for g in range(SSD_GROUPS):
        bg = bm[:, g * N:(g + 1) * N]
        cg = cm[:, g * N:(g + 1) * N]
        cb = _dot_nt(cg, bg)
        s_g = state[g]
        y_off = _dot(cg, s_g.astype(BF16)) * e_acs[:, g * GW:(g + 1) * GW]
        for pr in range(heads_per_group // 2):
            y_pair = None
            for var in range(2):
                h = g * heads_per_group + 2 * pr + var
                col = acs_full[:, 16 + h:17 + h]
                row = acs_t[16 + h:17 + h, :]
                seg = jnp.exp(jnp.where(tril, col - row, -jnp.inf))
                mh = (cb * seg).astype(BF16)
                src = xdt_lo if var == 0 else xdt_hi
                lanes0 = (g * heads_per_group + 2 * pr) * SSD_HEADDIM
                d = _dot(mh, src[:, lanes0:lanes0 + LANES])
                y_pair = d if y_pair is None else y_pair + d
            off = pr * LANES
            y_parts.append(y_pair + y_off[:, off:off + LANES])
        upd = lax.dot_general(bg, xdec[:, g * GW:(g + 1) * GW], (((0,), (0,)), ((), ())),
                              preferred_element_type=F32)
        state[g] = s_g * e_last[:, g * GW:(g + 1) * GW] + upd
    y = jnp.concatenate(y_parts, axis=1) + xs * dskip_ref[...]
    y = y * _silu(z_ref[...])
    outs = []
    for g in range(SSD_GROUPS):
        yg = y[:, g * GW:(g + 1) * GW]
        outs.append(yg * lax.rsqrt(jnp.mean(yg * yg, axis=-1, keepdims=True) + RMS_EPS))
    o_ref[...] = (jnp.concatenate(outs, axis=1) * nw_ref[...]).astype(o_ref.dtype)


def _ssd(proj, bsz, t_len, cw, cb, dtb_row, alog_row, dskip_row, nw_row, eh):
    L = SSD_CHUNK
    nc = t_len // L
    W = SSD_D_INNER
    z_blk = (GDN_CONV_DIM + GDN_V_W) // W
    x_blk = (GDN_CONV_DIM + GDN_V_W + W) // W
    b_blk = (GDN_CONV_DIM + GDN_V_W + 2 * W) // SSD_BC_W
    sm_blk = HYB_SMALL_OFF // LANES
    row = lambda b, c: b * nc + c
    const = lambda b, c: (0, 0)
    return pl.pallas_call(
        _ssd_kernel,
        grid=(bsz, nc),
        in_specs=[
            pl.BlockSpec((L, W), lambda b, c: (row(b, c), z_blk)),
            pl.BlockSpec((L, W), lambda b, c: (row(b, c), x_blk)),
            pl.BlockSpec((L, SSD_BC_W), lambda b, c: (row(b, c), b_blk)),
            pl.BlockSpec((L, SSD_BC_W), lambda b, c: (row(b, c), b_blk + 1)),
            pl.BlockSpec((L, LANES), lambda b, c: (row(b, c), sm_blk)),
            pl.BlockSpec((CONV_K, SSD_CONV_DIM), const),
            pl.BlockSpec((1, SSD_CONV_DIM), const),
            pl.BlockSpec((1, LANES), const),
            pl.BlockSpec((1, LANES), const),
            pl.BlockSpec((1, W), const),
            pl.BlockSpec((1, W), const),
            pl.BlockSpec((LANES, W), const),
        ],
        out_specs=pl.BlockSpec((L, W), lambda b, c: (row(b, c), 0)),
        out_shape=jax.ShapeDtypeStruct((bsz * t_len, W), BF16),
        scratch_shapes=[
            pltpu.VMEM((L + SUBLANES, SSD_CONV_DIM), F32),
            pltpu.VMEM((SSD_GROUPS, SSD_STATE, W // SSD_GROUPS), F32),
        ],
        compiler_params=_cparams("parallel", "arbitrary"),
        name="ssd",
    )(proj, proj, proj, proj, proj, cw, cb, dtb_row, alog_row, dskip_row, nw_row, eh)


def _unit_lower_inverse_minus_eye(a, r_i, c_i):
    n = range(len(a))
    diag8 = (r_i >> 3) == (c_i >> 3)
    x = [jnp.where(diag8, -a[i], 0.0) for i in n]
    x_b = [x[i].astype(BF16) for i in n]
    x2 = [_dot(x_b[i], x_b[i]) for i in n]
    x2_b = [x2[i].astype(BF16) for i in n]
    x4 = [_dot(x2_b[i], x2_b[i]) for i in n]
    e = [x[i] + x2[i] + _dot(x_b[i], x2_b[i]) for i in n]
    e = [e[i] + x4[i] + _dot(e[i].astype(BF16), x4[i].astype(BF16)) for i in n]
    for sh in (3, 4, 5):
        same_pair = (r_i >> (sh + 1)) == (c_i >> (sh + 1))
        lower_left = (((r_i >> sh) & 1) == 1) & (((c_i >> sh) & 1) == 0)
        ms = [jnp.where(same_pair & lower_left, a[i], 0.0) for i in n]
        y = [ms[i] + _dot(ms[i].astype(BF16), e[i].astype(BF16)) for i in n]
        e = [e[i] - y[i] - _dot(e[i].astype(BF16), y[i].astype(BF16)) for i in n]
    return e


def _stack_heads(x, r0, l0):
    return jnp.concatenate([x[r0:r0 + GDN_CHUNK, l0:l0 + GDN_DV],
                            x[r0:r0 + GDN_CHUNK, l0 + GDN_DV:l0 + 2 * GDN_DV]], axis=0)


def _own_head_block(x):
    return jnp.concatenate([x[0:GDN_CHUNK, 0:GDN_DV], x[GDN_CHUNK:, GDN_DV:]], axis=0)


def _gdn_kernel(q_ref, k_ref, v_ref, z_ref, sm_ref, cw_ref, dtb_ref, alog_ref, nw_ref,
                eb_ref, ea_ref, o_ref, cbuf, state, *, tc):
    C = GDN_CHUNK
    DK = GDN_DK
    DV = GDN_DV
    P = 2 * C

    @pl.when(pl.program_id(1) == 0)
    def _():
        cbuf[0:SUBLANES, :] = jnp.zeros((SUBLANES, GDN_CONV_DIM), F32)
        state[...] = jnp.zeros_like(state)

    cbuf[SUBLANES:SUBLANES + tc, 0:GDN_QK_W] = q_ref[...]
    cbuf[SUBLANES:SUBLANES + tc, GDN_QK_W:2 * GDN_QK_W] = k_ref[...]
    cbuf[SUBLANES:SUBLANES + tc, 2 * GDN_QK_W:GDN_CONV_DIM] = v_ref[...]
    act = _silu(_causal_conv(cbuf, cw_ref, tc))
    cbuf[0:SUBLANES, :] = cbuf[tc:tc + SUBLANES, :]

    lane = lax.broadcasted_iota(jnp.int32, (1, LANES), 1)
    sm = sm_ref[...]
    beta_full = jnp.where(lane < GDN_V_HEADS, jax.nn.sigmoid(sm), 0.0)
    g_lanes = (lane >= GDN_V_HEADS) & (lane < 2 * GDN_V_HEADS)
    g_full = jnp.where(g_lanes, -jnp.exp(alog_ref[...]) * _softplus(sm + dtb_ref[...]), 0.0)
    rt = lax.broadcasted_iota(jnp.int32, (tc, tc), 0)
    ct = lax.broadcasted_iota(jnp.int32, (tc, tc), 1)
    same_chunk_tril = ((rt >> 6) == (ct >> 6)) & (rt >= ct)
    gc_full = _dot_sel_lhs(jnp.where(same_chunk_tril, 1.0, 0.0), g_full)
    b_exp = _dot_sel_rhs(beta_full, eb_ref[...])
    g_exp = _dot_sel_rhs(gc_full, ea_ref[...])
    eg_exp = jnp.exp(g_exp)

    r_i = lax.broadcasted_iota(jnp.int32, (P, P), 0)
    c_i = lax.broadcasted_iota(jnp.int32, (P, P), 1)
    same_head = (r_i >> 6) == (c_i >> 6)
    tril = same_head & (r_i >= c_i)
    strict = same_head & (r_i > c_i)
    head0_rows = lax.broadcasted_iota(jnp.int32, (P, 1), 0) < C
    nw = nw_ref[...]

    n_chunks = tc // C
    units = [(hq, ci) for ci in range(n_chunks) for hq in range(GDN_QK_HEADS)]
    qn, kn = [], []
    for hq in range(GDN_QK_HEADS):
        qh = act[:, hq * DK:(hq + 1) * DK]
        kh = act[:, GDN_QK_W + hq * DK:GDN_QK_W + (hq + 1) * DK]
        qn.append(qh * lax.rsqrt(jnp.sum(qh * qh, axis=-1, keepdims=True) + 1e-6) * (DK ** -0.5))
        kn.append(kh * lax.rsqrt(jnp.sum(kh * kh, axis=-1, keepdims=True) + 1e-6))
    pre = {}
    a_low = []
    for hq, ci in units:
        r0 = ci * C
        l0 = hq * 2 * DV
        q_c = qn[hq][r0:r0 + C]
        k_c = kn[hq][r0:r0 + C]
        q2 = jnp.concatenate([q_c, q_c], axis=0)
        k2 = jnp.concatenate([k_c, k_c], axis=0)
        k2_b = k2.astype(BF16)
        kk = _dot_nt(k2_b, k2_b)
        qk = _dot_nt(q2.astype(BF16), k2_b)
        gcol = _stack_heads(g_exp, r0, l0)
        bcol = _stack_heads(b_exp, r0, l0)
        egc = _stack_heads(eg_exp, r0, l0)
        dmat = jnp.exp(jnp.where(tril, gcol - gcol.T, -jnp.inf))
        a_low.append(jnp.where(strict, kk * bcol * dmat, 0.0))
        g_last = jnp.where(head0_rows, gcol[C - 1:C], gcol[P - 1:P])
        pre[hq, ci] = dict(
            vb=_stack_heads(act, r0, 2 * GDN_QK_W + l0) * bcol,
            kb=k2 * (bcol * egc),
            qd=(q2 * egc).astype(BF16),
            attn=(qk * dmat).astype(BF16),
            k_t=k_c.T.astype(BF16),
            v_scale=jnp.exp(g_last - gcol),
            decay_cat=jnp.concatenate([egc[C - 1:C], egc[P - 1:P]], axis=1))
    e_all = _unit_lower_inverse_minus_eye(a_low, r_i, c_i)
    for (hq, ci), e in zip(units, e_all):
        d = pre[hq, ci]
        e_b = e.astype(BF16)
        d["u"] = d["vb"] + _dot(e_b, d["vb"].astype(BF16))
        d["w"] = (d["kb"] + _dot(e_b, d["kb"].astype(BF16))).astype(BF16)

    s_cat = [state[hq] for hq in range(GDN_QK_HEADS)]
    for ci in range(n_chunks):
        r0 = ci * C
        for hq in range(GDN_QK_HEADS):
            d = pre[hq, ci]
            l0 = hq * 2 * DV
            s_b = s_cat[hq].astype(BF16)
            v_new = d["u"] - _own_head_block(_dot(d["w"], s_b))
            o2 = _own_head_block(_dot(d["qd"], s_b)) + _dot(d["attn"], v_new.astype(BF16))
            v_dec = (v_new * d["v_scale"]).astype(BF16)
            v_dec_cat = jnp.concatenate([v_dec[0:C], v_dec[C:]], axis=1)
            s_cat[hq] = s_cat[hq] * d["decay_cat"] + _dot(d["k_t"], v_dec_cat)
            o_n = o2 * lax.rsqrt(jnp.mean(o2 * o2, axis=-1, keepdims=True) + RMS_EPS)
            res = (o_n * nw * _silu(_stack_heads(z_ref, r0, l0))).astype(o_ref.dtype)
            o_ref[r0:r0 + C, l0:l0 + DV] = res[0:C]
            o_ref[r0:r0 + C, l0 + DV:l0 + 2 * DV] = res[C:]
    for hq in range(GDN_QK_HEADS):
        state[hq] = s_cat[hq]


def _gdn(proj, bsz, t_len, cw, dtb_row, alog_row, nw_row, eb, ea, *, tc):
    nt = t_len // tc
    row = lambda b, t: b * nt + t
    const = lambda b, t: (0, 0)
    sm_blk = HYB_SMALL_OFF // LANES
    return pl.pallas_call(
        functools.partial(_gdn_kernel, tc=tc),
        grid=(bsz, nt),
        in_specs=[
            pl.BlockSpec((tc, GDN_QK_W), lambda b, t: (row(b, t), 0)),
            pl.BlockSpec((tc, GDN_QK_W), lambda b, t: (row(b, t), 1)),
            pl.BlockSpec((tc, GDN_V_W), lambda b, t: (row(b, t), 1)),
            pl.BlockSpec((tc, GDN_V_W), lambda b, t: (row(b, t), 2)),
            pl.BlockSpec((tc, LANES), lambda b, t: (row(b, t), sm_blk)),
            pl.BlockSpec((CONV_K, GDN_CONV_DIM), const),
            pl.BlockSpec((1, LANES), const),
            pl.BlockSpec((1, LANES), const),
            pl.BlockSpec((1, GDN_DV), const),
            pl.BlockSpec((LANES, GDN_V_W), const),
            pl.BlockSpec((LANES, GDN_V_W), const),
        ],
        out_specs=pl.BlockSpec((tc, GDN_V_W), lambda b, t: (row(b, t), 0)),
        out_shape=jax.ShapeDtypeStruct((bsz * t_len, GDN_V_W), BF16),
        scratch_shapes=[
            pltpu.VMEM((tc + SUBLANES, GDN_CONV_DIM), F32),
            pltpu.VMEM((GDN_QK_HEADS, GDN_DK, 2 * GDN_DV), F32),
        ],
        compiler_params=_cparams("parallel", "arbitrary"),
        name="gdn",
    )(proj, proj, proj, proj, proj, cw, dtb_row, alog_row, nw_row, eb, ea)


def _group_variants(x):
    lane = lax.broadcasted_iota(jnp.int32, (1, LANES), 1)
    lo = lane < (LANES // 2)
    xr = pltpu.roll(x, LANES // 2, axis=1)
    zero = jnp.zeros_like(x)
    return jnp.concatenate([jnp.where(lo, x, zero), jnp.where(lo, zero, xr),
                            jnp.where(lo, xr, zero), jnp.where(lo, zero, x)], axis=1)


def _values_transposed(v):
    zt = _group_variants(v).T
    extra_r = lax.broadcasted_iota(jnp.int32, (VT_ROWS - LANES, v.shape[0]), 0)
    parts = []
    for var in range(2 * NSA_GROUPS):
        parts.append(zt[var * LANES:(var + 1) * LANES])
        parts.append(jnp.where(extra_r == (var % 2), 1.0, 0.0))
    return jnp.concatenate(parts, axis=0)


def _nsa_prep_kernel(q_ref, ks_ref, vs_ref, kw_ref, vw_ref, pos_ref, freq_ref, s1_ref, s2_ref,
                     qc_o, qr_o, ksz_o, vsz_o, kwz_o, vwz_o):
    scale = NSA_DK ** -0.5 * LOG2E
    ang = pos_ref[...].astype(F32) * freq_ref[...]
    cos = jnp.cos(ang)
    sin = jnp.sin(ang)
    sin_up = sin * s1_ref[...]
    sin_dn = sin * s2_ref[...]
    half = ROPE_DIM // 2

    def rope(x):
        return x * cos + pltpu.roll(x, half, axis=1) * sin_up + pltpu.roll(x, LANES - half, axis=1) * sin_dn

    for p in range(NSA_HEADS * NSA_DK // LANES):
        x = q_ref[:, p * LANES:(p + 1) * LANES]
        qc_o[:, p * LANES:(p + 1) * LANES] = (x * scale).astype(qc_o.dtype)
        qr_o[:, p * LANES:(p + 1) * LANES] = (rope(x) * scale).astype(qr_o.dtype)
    ksz_o[...] = _group_variants(rope(ks_ref[...])).astype(ksz_o.dtype)
    kwz_o[...] = _group_variants(rope(kw_ref[...])).astype(kwz_o.dtype)
    vsz_o[...] = _values_transposed(vs_ref[...]).astype(vsz_o.dtype)
    vwz_o[...] = _values_transposed(vw_ref[...]).astype(vwz_o.dtype)


def _nsa_prep(proj, pos_col, freq_row, s1_row, s2_row, bsz, t_len, *, tm):
    m = proj.shape[0]
    tiles_per_seq = t_len // tm
    seq_t = lambda i: (i // tiles_per_seq, 0, i % tiles_per_seq)
    qw = NSA_HEADS * NSA_DK
    kv0 = qw // LANES
    const = lambda i: (0, 0)
    zw = 4 * LANES
    return pl.pallas_call(
        _nsa_prep_kernel,
        grid=(m // tm,),
        in_specs=[
            pl.BlockSpec((tm, qw), lambda i: (i, 0)),
            pl.BlockSpec((tm, LANES), lambda i: (i, kv0 + 2)),
            pl.BlockSpec((tm, LANES), lambda i: (i, kv0 + 3)),
            pl.BlockSpec((tm, LANES), lambda i: (i, kv0 + 4)),
            pl.BlockSpec((tm, LANES), lambda i: (i, kv0 + 5)),
            pl.BlockSpec((tm, 1), lambda i: (i, 0)),
            pl.BlockSpec((1, LANES), const),
            pl.BlockSpec((1, LANES), const),
            pl.BlockSpec((1, LANES), const),
        ],
        out_specs=[
            pl.BlockSpec((tm, qw), lambda i: (i, 0)),
            pl.BlockSpec((tm, qw), lambda i: (i, 0)),
            pl.BlockSpec((tm, zw), lambda i: (i, 0)),
            pl.BlockSpec((None, 4 * VT_ROWS, tm), seq_t),
            pl.BlockSpec((tm, zw), lambda i: (i, 0)),
            pl.BlockSpec((None, 4 * VT_ROWS, tm), seq_t),
        ],
        out_shape=[
            jax.ShapeDtypeStruct((m, qw), BF16),
            jax.ShapeDtypeStruct((m, qw), BF16),
            jax.ShapeDtypeStruct((m, zw), BF16),
            jax.ShapeDtypeStruct((bsz, 4 * VT_ROWS, t_len), BF16),
            jax.ShapeDtypeStruct((m, zw), BF16),
            jax.ShapeDtypeStruct((bsz, 4 * VT_ROWS, t_len), BF16),
        ],
        compiler_params=_cparams("parallel"),
        name="nsa_prep",
    )(proj, proj, proj, proj, proj, pos_col, freq_row, s1_row, s2_row)


def _nsa_compress_kernel(k16_ref, v16_ref, pos_ref, w1a_ref, w1b_ref, w2_ref, kcz_o, vcz_o):
    nr = k16_ref.shape[0]
    for idx, (x_ref, o_ref) in enumerate(((k16_ref, kcz_o), (v16_ref, vcz_o))):
        x = x_ref[...]
        h_a = _dot((x + pos_ref[idx, 0:1, :]).astype(BF16), w1a_ref[idx])
        h_b = _dot((x + pos_ref[idx, 1:2, :]).astype(BF16), w1b_ref[idx])
        hid = h_a + pltpu.roll(h_b, nr - 1, axis=0)
        out = _dot(_silu(hid).astype(BF16), w2_ref[idx])
        z = _group_variants(out)
        o_ref[...] = (z.T if idx == 1 else z).astype(o_ref.dtype)


def _nsa_compress(k16, v16, pos_ab, w1a, w1b, w2):
    bsz, nr, width = k16.shape
    hid = NSA_GROUPS * CMP_HIDDEN
    zw = 4 * LANES
    c3 = lambda b: (0, 0, 0)
    return pl.pallas_call(
        _nsa_compress_kernel,
        grid=(bsz,),
        in_specs=[
            pl.BlockSpec((None, nr, width), lambda b: (b, 0, 0)),
            pl.BlockSpec((None, nr, width), lambda b: (b, 0, 0)),
            pl.BlockSpec((2, 2, width), c3),
            pl.BlockSpec((2, width, hid), c3),
            pl.BlockSpec((2, width, hid), c3),
            pl.BlockSpec((2, hid, LANES), c3),
        ],
        out_specs=[
            pl.BlockSpec((None, nr, zw), lambda b: (b, 0, 0)),
            pl.BlockSpec((None, zw, nr), lambda b: (b, 0, 0)),
        ],
        out_shape=[
            jax.ShapeDtypeStruct((bsz, nr, zw), BF16),
            jax.ShapeDtypeStruct((bsz, zw, nr), BF16),
        ],
        compiler_params=_cparams("parallel"),
        name="nsa_compress",
    )(k16, v16, pos_ab, w1a, w1b, w2)


def _flash_branch(q_stacks, kz_ref, vzt_ref, lo, hi, bias_fns, kb_size):
    r_acc = lax.broadcasted_iota(jnp.int32, (VT_ROWS, 1), 0)
    even_rows = (r_acc < LANES // 2) | (r_acc == LANES)
    half_w = 2 * Q_BLOCK
    n_half = q_stacks[0].shape[0] // half_w
    units = [(g, h) for g in range(NSA_GROUPS) for h in range(n_half)]
    q_unit = [q_stacks[g][h * half_w:(h + 1) * half_w] for g, h in units]

    def body(kb, carry):
        m_in, acc = carry
        ks = pl.multiple_of(kb * kb_size, kb_size)
        k_both, vt_both, bias = [], [], []
        for g in range(NSA_GROUPS):
            k0 = 2 * g * LANES
            v0 = 2 * g * VT_ROWS
            k_both.append(jnp.concatenate([kz_ref[pl.ds(ks, kb_size), k0:k0 + LANES],
                                           kz_ref[pl.ds(ks, kb_size), k0 + LANES:k0 + 2 * LANES]], axis=0))
            vt_both.append(jnp.concatenate([vzt_ref[v0:v0 + VT_ROWS, pl.ds(ks, kb_size)],
                                            vzt_ref[v0 + VT_ROWS:v0 + 2 * VT_ROWS, pl.ds(ks, kb_size)]],
                                           axis=1))
            b = bias_fns[g](ks)
            bias.append(jnp.concatenate([b, b], axis=1))
        n_u = len(units)
        s, m_out, alpha_rows, p_cat, pv = {}, {}, {}, {}, {}

        def softmax_stage(u):
            g = units[u][0]
            p_parts, alphas, m_news = [], [], []
            for var in range(2):
                s_v = s[u][var * kb_size:(var + 1) * kb_size] + bias[g]
                m_prev = m_in[u][var]
                m_new = jnp.maximum(m_prev, jnp.max(s_v, axis=0, keepdims=True))
                alphas.append(jnp.exp2(m_prev - m_new))
                p_parts.append(jnp.exp2(s_v - m_new).astype(BF16))
                m_news.append(m_new)
            m_out[u] = tuple(m_news)
            alpha_rows[u] = jnp.where(even_rows, alphas[0], alphas[1])
            p_cat[u] = jnp.concatenate(p_parts, axis=0)

        for u in range(n_u):
            s[u] = _dot_nt(k_both[units[u][0]], q_unit[u])
        for u in range(n_u):
            softmax_stage(u)
        for u in range(n_u):
            pv[u] = _dot(vt_both[units[u][0]], p_cat[u])
        acc_out = tuple(acc[u] * alpha_rows[u] + pv[u] for u in range(n_u))
        return tuple(m_out[u] for u in range(n_u)), acc_out

    m_init = tuple((jnp.full((1, half_w), NEG_BIG, F32),) * 2 for _ in units)
    acc_init = tuple(jnp.zeros((VT_ROWS, half_w), F32) for _ in units)
    _, acc = lax.fori_loop(lo, hi, body, (m_init, acc_init))
    outs = []
    for g in range(NSA_GROUPS):
        a = jnp.concatenate(acc[g * n_half:(g + 1) * n_half], axis=1)
        inv_even = 1.0 / a[LANES:LANES + 1, :]
        inv_odd = 1.0 / a[LANES + 1:LANES + 2, :]
        outs.append(a[0:LANES, :] * jnp.where(r_acc[0:LANES] < LANES // 2, inv_even, inv_odd))
    return outs


def _nsa_attn_kernel(qc_ref, qr_ref, gate_ref, kcz_ref, vczt_ref, ksz_ref, vszt_ref, kwz_ref, vwzt_ref,
                     aggt_ref, eselt_ref, o_ref, bias_ref,
                     *, t_len, kb_size):
    n_cmp_rows = t_len // CMP_STRIDE
    n_cmp = (t_len - CMP_LEN) // CMP_STRIDE + 1
    n_sel = t_len // SEL_LEN
    n_top = min(SEL_TOPK, n_sel)
    pairs = NSA_HEADS // NSA_GROUPS // 2
    q0 = pl.program_id(1) * Q_BLOCK
    t_row = q0 + lax.broadcasted_iota(jnp.int32, (1, Q_BLOCK), 1)
    t_row_stack = q0 + (lax.broadcasted_iota(jnp.int32, (1, pairs * Q_BLOCK), 1) & (Q_BLOCK - 1))
    blk = lax.broadcasted_iota(jnp.int32, (n_sel, 1), 0)
    c_idx = lax.broadcasted_iota(jnp.int32, (n_cmp_rows, 1), 0)
    cmp_mask = ((c_idx * CMP_STRIDE + (CMP_LEN - 1)) <= t_row_stack) & (c_idx < n_cmp)
    key_pos = lax.broadcasted_iota(jnp.int32, (t_len, 1), 0)
    gate_t = jax.nn.sigmoid(gate_ref[...]).T
    top_half = lax.broadcasted_iota(jnp.int32, (LANES, 1), 0) < (LANES // 2)
    hi = (q0 + (Q_BLOCK - 1)) // kb_size + 1
    win_lo = jnp.maximum(q0 - (WINDOW - 1), 0) // kb_size

    def win_bias(ks):
        kp = ks + lax.broadcasted_iota(jnp.int32, (kb_size, 1), 0)
        return jnp.where((kp <= t_row) & (kp > t_row - WINDOW), 0.0, NEG_BIG)

    def sel_bias(g):
        return lambda ks: bias_ref[g, pl.ds(ks, kb_size), :]

    qr_stacks, o_cmps = [], []
    for g in range(NSA_GROUPS):
        lanes_g = g * pairs * LANES
        qc_stack = jnp.concatenate(
            [qc_ref[:, lanes_g + p * LANES:lanes_g + (p + 1) * LANES] for p in range(pairs)], axis=0)
        qr_stacks.append(jnp.concatenate(
            [qr_ref[:, lanes_g + p * LANES:lanes_g + (p + 1) * LANES] for p in range(pairs)], axis=0))

        o_cmp = None
        p_sum = None
        for var in range(2):
            c0 = (2 * g + var) * LANES
            s = jnp.where(cmp_mask, _dot_nt(kcz_ref[:, c0:c0 + LANES], qc_stack), -jnp.inf)
            mx = jnp.max(s, axis=0, keepdims=True)
            mx = jnp.where(mx > -jnp.inf, mx, 0.0)
            e = jnp.exp2(s - mx)
            p = e * (1.0 / jnp.maximum(jnp.sum(e, axis=0, keepdims=True), 1e-30))
            d = _dot(vczt_ref[c0:c0 + LANES, :], p.astype(BF16))
            o_cmp = d if o_cmp is None else o_cmp + d
            for pr in range(pairs):
                part = p[:, pr * Q_BLOCK:(pr + 1) * Q_BLOCK]
                p_sum = part if p_sum is None else p_sum + part
        importance = _dot_sel_lhs(aggt_ref[...], p_sum)

        cur = t_row >> 6
        causal_blk = blk <= cur
        forced = (blk == 0) | (causal_blk & (blk > cur - SEL_LOCAL))
        score = jnp.where(forced, FORCE_SCORE, jnp.where(causal_blk, importance, -1.0))
        rank = jnp.zeros((n_sel, Q_BLOCK), F32)
        for jj in range(n_sel):
            row = score[jj:jj + 1, :]
            beats = (row > score) | ((row == score) & (blk > jj))
            rank = rank + jnp.where(beats, 1.0, 0.0)
        sel = jnp.where(rank < n_top, 1.0, 0.0)
        sel = jnp.concatenate([sel, jnp.zeros((LANES - n_sel, Q_BLOCK), F32)], axis=0)
        sel_keys = _dot(eselt_ref[...], sel.astype(BF16))
        bias_ref[g] = jnp.where((sel_keys > 0.5) & (key_pos <= t_row), 0.0, NEG_BIG)
        o_cmps.append(o_cmp)

    o_sels = _flash_branch(qr_stacks, ksz_ref, vszt_ref, 0, hi,
                           [sel_bias(g) for g in range(NSA_GROUPS)], kb_size)
    o_wins = _flash_branch(qr_stacks, kwz_ref, vwzt_ref, win_lo, hi, [win_bias] * NSA_GROUPS, kb_size)

    for g in range(NSA_GROUPS):
        lanes_g = g * pairs * LANES
        for pr in range(pairs):
            h_even = (g * pairs + pr) * 2
            cols = slice(pr * Q_BLOCK, (pr + 1) * Q_BLOCK)
            mixed = None
            for j, o_branch in enumerate((o_cmps[g], o_sels[g], o_wins[g])):
                r_e = h_even * 3 + j
                r_o = r_e + 3
                gate_rows = jnp.where(top_half, gate_t[r_e:r_e + 1, :], gate_t[r_o:r_o + 1, :])
                term = gate_rows * o_branch[:, cols]
                mixed = term if mixed is None else mixed + term
            c0 = lanes_g + pr * LANES
            o_ref[:, c0:c0 + LANES] = mixed.T.astype(o_ref.dtype)


def _nsa_attn(qc, qr, proj, kcz, vczt, ksz, vszt, kwz, vwzt, aggt, eselt, bsz, t_len, *, kb_size):
    nq = t_len // Q_BLOCK
    qw = NSA_HEADS * NSA_DK
    zw = 4 * LANES
    nr = t_len // CMP_STRIDE
    pairs = NSA_HEADS // NSA_GROUPS // 2
    row = lambda b, i: (b * nq + i, 0)
    per_b = lambda b, i: (b, 0, 0)
    return pl.pallas_call(
        functools.partial(_nsa_attn_kernel, t_len=t_len, kb_size=kb_size),
        grid=(bsz, nq),
        in_specs=[
            pl.BlockSpec((Q_BLOCK, qw), row),
            pl.BlockSpec((Q_BLOCK, qw), row),
            pl.BlockSpec((Q_BLOCK, LANES), lambda b, i: (b * nq + i, NSA_GATE_OFF // LANES)),
            pl.BlockSpec((None, nr, zw), per_b),
            pl.BlockSpec((None, zw, nr), per_b),
            pl.BlockSpec((None, t_len, zw), per_b),
            pl.BlockSpec((None, 4 * VT_ROWS, t_len), per_b),
            pl.BlockSpec((None, t_len, zw), per_b),
            pl.BlockSpec((None, 4 * VT_ROWS, t_len), per_b),
            pl.BlockSpec(aggt.shape, lambda b, i: (0, 0)),
            pl.BlockSpec(eselt.shape, lambda b, i: (0, 0)),
        ],
        out_specs=pl.BlockSpec((Q_BLOCK, qw), row),
        out_shape=jax.ShapeDtypeStruct((bsz * t_len, qw), BF16),
        scratch_shapes=[
            pltpu.VMEM((NSA_GROUPS, t_len, Q_BLOCK), F32),
        ],
        compiler_params=_cparams("parallel", "arbitrary"),
        name="nsa_attn",
    )(qc, qr, proj, kcz, vczt, ksz, vszt, kwz, vwzt, aggt, eselt)


def _expand_rows(first_row, n_heads, width):
    e = np.zeros((LANES, n_heads * width), np.float32)
    for h in range(n_heads):
        e[first_row + h, h * width:(h + 1) * width] = 1.0
    return e


def _nsa_constants(t_len):
    n_cmp = (t_len - CMP_LEN) // CMP_STRIDE + 1
    n_sel = t_len // SEL_LEN
    nr = t_len // CMP_STRIDE
    c0 = np.arange(n_cmp)[:, None] * CMP_STRIDE
    s0 = np.arange(n_sel)[None, :] * SEL_LEN
    overlap = np.clip(np.minimum(c0 + CMP_LEN, s0 + SEL_LEN) - np.maximum(c0, s0), 0, None) / CMP_LEN
    aggt = np.zeros((n_sel, nr), np.float32)
    aggt[:, :n_cmp] = overlap.T
    eselt = np.zeros((t_len, LANES), np.float32)
    eselt[np.arange(t_len), np.arange(t_len) // SEL_LEN] = 1.0
    half = ROPE_DIM // 2
    inv_freq = (ROPE_THETA ** (-np.arange(half) / half)).astype(np.float32)
    d = np.arange(LANES) % NSA_DK
    freq = np.where(d < ROPE_DIM, inv_freq[d % half], 0.0).astype(np.float32)[None]
    s1 = ((d >= half) & (d < ROPE_DIM)).astype(np.float32)[None]
    s2 = -(d < half).astype(np.float32)[None]
    return aggt, eselt, freq, s1, s2


def _pad_row(vec, offset):
    return jnp.zeros((1, LANES), F32).at[0, offset:offset + vec.shape[0]].set(vec.astype(F32))


def _hybrid_layer(h, bsz, t_len, w_in, gdn_conv_w, gdn_a_log, gdn_dt_bias, gdn_norm_w, ssd_conv_w,
                  ssd_conv_b, ssd_a_log, ssd_dt_bias, ssd_d, ssd_norm_w, w_out, ln_g, ln_b, cfg):
    b_off = GDN_CONV_DIM + GDN_V_W
    z_off = b_off + 2 * GDN_V_HEADS
    dt_off = z_off + SSD_D_INNER + SSD_CONV_DIM
    w_pad = jnp.concatenate(
        [w_in[:, :b_off], w_in[:, z_off:dt_off], w_in[:, b_off:z_off], w_in[:, dt_off:],
         jnp.zeros((D_MODEL, HYB_PAD - w_in.shape[1]), w_in.dtype)], axis=1).astype(BF16)
    proj = _proj(h, w_pad, tm=cfg["proj_tm"], tn=cfg["hyb_tn"])
    o_a = _gdn(proj, bsz, t_len, gdn_conv_w, _pad_row(gdn_dt_bias, GDN_V_HEADS),
               _pad_row(gdn_a_log, GDN_V_HEADS), gdn_norm_w[None].astype(F32),
               jnp.asarray(_expand_rows(0, GDN_V_HEADS, GDN_DV)).astype(BF16),
               jnp.asarray(_expand_rows(GDN_V_HEADS, GDN_V_HEADS, GDN_DV)).astype(BF16), tc=cfg["gdn_tc"])
    o_b = _ssd(proj, bsz, t_len, ssd_conv_w, ssd_conv_b[None], _pad_row(ssd_dt_bias, 16),
               _pad_row(ssd_a_log, 16), jnp.repeat(ssd_d, SSD_HEADDIM)[None], ssd_norm_w[None],
               jnp.asarray(_expand_rows(16, SSD_HEADS, SSD_HEADDIM)).astype(BF16))
    w_out_b = w_out.astype(BF16)
    return _out_ln([o_a, o_b], [w_out_b[:GDN_V_W], w_out_b[GDN_V_W:]], h, ln_g, ln_b, tm=cfg["out_tm"])


def _nsa_layer(h, pos_col, bsz, t_len, w_in, cmp_pos, cmp_w1, cmp_w2, w_out, ln_g, ln_b, cfg):
    aggt, eselt, freq, s1, s2 = _nsa_constants(t_len)
    w_pad = jnp.concatenate(
        [w_in, jnp.zeros((D_MODEL, NSA_PAD - w_in.shape[1]), w_in.dtype)], axis=1).astype(BF16)
    proj = _proj(h, w_pad, tm=cfg["proj_tm"], tn=cfg["nsa_tn"])
    qc, qr, ksz, vszt, kwz, vwzt = _nsa_prep(proj, pos_col, jnp.asarray(freq), jnp.asarray(s1),
                                             jnp.asarray(s2), bsz, t_len, tm=cfg["prep_tm"])
    qw = NSA_HEADS * NSA_DK
    nr = t_len // CMP_STRIDE
    width = CMP_STRIDE * LANES
    k16 = proj[:, qw:qw + LANES].reshape(bsz, nr, width)
    v16 = proj[:, qw + LANES:qw + 2 * LANES].reshape(bsz, nr, width)
    pos2 = jnp.broadcast_to(cmp_pos[:, :, None, :], (2, CMP_LEN, NSA_GROUPS, NSA_DK)).reshape(2, 2, width)
    w1 = cmp_w1.reshape(2, 2, CMP_STRIDE, NSA_DK, CMP_HIDDEN)
    eye_g = jnp.eye(NSA_GROUPS, dtype=w1.dtype)
    w1x = jnp.einsum("ksjdh,ge->ksjgdeh", w1, eye_g).reshape(
        2, 2, width, NSA_GROUPS * CMP_HIDDEN).astype(BF16)
    w2x = jnp.einsum("khd,ge->kghed", cmp_w2, eye_g).reshape(
        2, NSA_GROUPS * CMP_HIDDEN, NSA_GROUPS * NSA_DK).astype(BF16)
    kcz, vczt = _nsa_compress(k16, v16, pos2, w1x[:, 0], w1x[:, 1], w2x)
    o = _nsa_attn(qc, qr, proj, kcz, vczt, ksz.reshape(bsz, t_len, -1), vszt,
                  kwz.reshape(bsz, t_len, -1), vwzt,
                  jnp.asarray(aggt).astype(BF16), jnp.asarray(eselt).astype(BF16),
                  bsz, t_len, kb_size=cfg["attn_kb"])
    return _out_ln([o], [w_out.astype(BF16)], h, ln_g, ln_b, tm=cfg["out_tm"])


def _config(m, t_len):
    return dict(ffn_tm=min(512, m), ffn_tf=256, proj_tm=min(512, m), hyb_tn=HYB_PAD, nsa_tn=NSA_PAD,
                out_tm=min(512, m), gdn_tc=min(256, t_len), prep_tm=min(256, m),
                attn_kb=min(256, t_len))


def kernel(x, positions, ln_g, ln_b, ffn_w_in, ffn_w_out, hyb_w_in, gdn_conv_w, gdn_a_log, gdn_dt_bias, gdn_norm_w, ssd_conv_w, ssd_conv_b, ssd_a_log, ssd_dt_bias, ssd_d, ssd_norm_w, hyb_w_out, nsa_w_in, nsa_cmp_pos, nsa_cmp_w1, nsa_cmp_w2, nsa_w_out):
    bsz, t_len, _ = x.shape
    m = bsz * t_len
    cfg = _config(m, t_len)
    h = x.reshape(m, D_MODEL)
    pos_col = positions.reshape(m, 1)
    w_in_b = ffn_w_in.astype(BF16)
    w_out_b = ffn_w_out.astype(BF16)
    ln_g = ln_g[:, :, None, :]
    ln_b = ln_b[:, :, None, :]
    for layer in range(DEPTH):
        i = layer // 2
        h = _ffn_ln(h, w_in_b[layer, 0], w_out_b[layer, 0], ln_g[layer, 0], ln_b[layer, 0],
                    tm=cfg["ffn_tm"], tf=cfg["ffn_tf"])
        if layer % 2 == 0:
            h = _hybrid_layer(h, bsz, t_len, hyb_w_in[i], gdn_conv_w[i], gdn_a_log[i], gdn_dt_bias[i],
                              gdn_norm_w[i], ssd_conv_w[i], ssd_conv_b[i], ssd_a_log[i], ssd_dt_bias[i],
                              ssd_d[i], ssd_norm_w[i], hyb_w_out[i], ln_g[layer, 1], ln_b[layer, 1], cfg)
        else:
            h = _nsa_layer(h, pos_col, bsz, t_len, nsa_w_in[i], nsa_cmp_pos[i], nsa_cmp_w1[i],
                           nsa_cmp_w2[i], nsa_w_out[i], ln_g[layer, 1], ln_b[layer, 1], cfg)
        h = _ffn_ln(h, w_in_b[layer, 1], w_out_b[layer, 1], ln_g[layer, 2], ln_b[layer, 2],
                    tm=cfg["ffn_tm"], tf=cfg["ffn_tf"])
    return h.reshape(bsz, t_len, D_MODEL)
```

```python
import functools

import numpy as np
import jax
import jax.numpy as jnp
from jax import lax
from jax.experimental import pallas as pl
from jax.experimental.pallas import tpu as pltpu

F32 = jnp.float32
BF16 = jnp.bfloat16

D_MODEL = 1024
DEPTH = 4
D_FF = 2816
DEEPNORM_ALPHA = (2.0 * DEPTH) ** 0.25
LN_EPS = 1e-5
RMS_EPS = 1e-6
CONV_K = 4

GDN_QK_HEADS = 4
GDN_V_HEADS = 8
GDN_DK = 128
GDN_DV = 128
GDN_CHUNK = 64
GDN_QK_W = GDN_QK_HEADS * GDN_DK
GDN_V_W = GDN_V_HEADS * GDN_DV
GDN_CONV_DIM = 2 * GDN_QK_W + GDN_V_W

SSD_D_INNER = D_MODEL
SSD_HEADDIM = 64
SSD_HEADS = SSD_D_INNER // SSD_HEADDIM
SSD_GROUPS = 2
SSD_STATE = 128
SSD_CHUNK = 128
SSD_BC_W = SSD_GROUPS * SSD_STATE
SSD_CONV_DIM = SSD_D_INNER + 2 * SSD_BC_W

NSA_HEADS = 16
NSA_GROUPS = 2
NSA_DK = 64
NSA_DV = 64
ROPE_DIM = NSA_DK // 4
ROPE_THETA = 500000.0
CMP_LEN = 32
CMP_STRIDE = 16
CMP_HIDDEN = 256
SEL_LEN = 64
SEL_TOPK = 8
SEL_LOCAL = 2
FORCE_SCORE = 1e4
WINDOW = 512
Q_BLOCK = 128

LANES = 128
SUBLANES = 8
VMEM_LIMIT_BYTES = 48 * 1024 * 1024

HYB_SMALL_OFF = GDN_CONV_DIM + GDN_V_W + SSD_D_INNER + SSD_CONV_DIM
HYB_PAD = HYB_SMALL_OFF + LANES
NSA_GATE_OFF = NSA_HEADS * NSA_DK + 3 * NSA_GROUPS * (NSA_DK + NSA_DV)
NSA_PAD = NSA_GATE_OFF + LANES

NEG_BIG = -1e30
LOG2E = 1.4426950408889634
DENOM_FLOOR = 2.0 ** -100
VT_ROWS = LANES + 16


def _cparams(*sem):
    return pltpu.CompilerParams(dimension_semantics=sem, vmem_limit_bytes=VMEM_LIMIT_BYTES)


def _silu(v):
    return v * jax.nn.sigmoid(v)


def _softplus(v):
    return jnp.maximum(v, 0.0) + jnp.log1p(jnp.exp(-jnp.abs(v)))


def _layer_norm(y, g, b):
    mu = jnp.mean(y, axis=-1, keepdims=True)
    d = y - mu
    var = jnp.mean(d * d, axis=-1, keepdims=True)
    return d * lax.rsqrt(var + LN_EPS) * g + b


def _dot(a, b):
    return jnp.dot(a, b, preferred_element_type=F32)


def _split3(x):
    hi = x.astype(BF16)
    r1 = x - hi.astype(F32)
    mid = r1.astype(BF16)
    lo = (r1 - mid.astype(F32)).astype(BF16)
    return hi, mid, lo


def _dot_sel_rhs(x, sel):
    sel = sel.astype(BF16)
    hi, mid, lo = _split3(x)
    return _dot(hi, sel) + _dot(mid, sel) + _dot(lo, sel)


def _dot_sel_lhs(sel, x):
    sel = sel.astype(BF16)
    hi, mid, lo = _split3(x)
    return _dot(sel, hi) + _dot(sel, mid) + _dot(sel, lo)


def _dot_nt(a, b):
    return lax.dot_general(a, b, (((1,), (1,)), ((), ())), preferred_element_type=F32)


def _ffn_ln_kernel(x_ref, wi_ref, wo_ref, g_ref, b_ref, o_ref, *, tf):
    x = x_ref[...]
    xb = x.astype(BF16)
    acts = []
    for c in range(D_FF // tf):
        gate = _dot(xb, wi_ref[:, c * tf:(c + 1) * tf])
        up = _dot(xb, wi_ref[:, D_FF + c * tf:D_FF + (c + 1) * tf])
        acts.append((_silu(gate) * up).astype(BF16))
    mix = _dot(jnp.concatenate(acts, axis=1), wo_ref[...])
    y = DEEPNORM_ALPHA * x + 0.5 * mix
    o_ref[...] = _layer_norm(y, g_ref[...], b_ref[...])


def _resident(shape):
    return pl.BlockSpec(shape, lambda *_: (0,) * len(shape), pipeline_mode=pl.Buffered(1))


def _ffn_ln(h, w_in, w_out, g, b, layer, which, *, tm, tf):
    m = h.shape[0]
    pick = lambda i: (layer, which, 0, 0)
    return pl.pallas_call(
        functools.partial(_ffn_ln_kernel, tf=tf),
        grid=(m // tm,),
        in_specs=[
            pl.BlockSpec((tm, D_MODEL), lambda i: (i, 0)),
            pl.BlockSpec((None, None) + w_in.shape[2:], pick, pipeline_mode=pl.Buffered(1)),
            pl.BlockSpec((None, None) + w_out.shape[2:], pick, pipeline_mode=pl.Buffered(1)),
            _resident((1, D_MODEL)),
            _resident((1, D_MODEL)),
        ],
        out_specs=pl.BlockSpec((tm, D_MODEL), lambda i: (i, 0)),
        out_shape=jax.ShapeDtypeStruct((m, D_MODEL), F32),
        compiler_params=_cparams("parallel"),
        name="ffn_ln",
    )(h, w_in, w_out, g, b)


def _proj_kernel(x_ref, w_ref, o_ref, *, tn):
    xb = x_ref[...].astype(BF16)
    for c in range(w_ref.shape[1] // tn):
        o_ref[:, c * tn:(c + 1) * tn] = _dot(xb, w_ref[:, c * tn:(c + 1) * tn])


def _proj(h, w, *, tm, tn):
    m, k = h.shape
    n = w.shape[1]
    return pl.pallas_call(
        functools.partial(_proj_kernel, tn=tn),
        grid=(m // tm,),
        in_specs=[
            pl.BlockSpec((tm, k), lambda i: (i, 0)),
            _resident(w.shape),
        ],
        out_specs=pl.BlockSpec((tm, n), lambda i: (i, 0)),
        out_shape=jax.ShapeDtypeStruct((m, n), F32),
        compiler_params=_cparams("parallel"),
        name="in_proj",
    )(h, w)


def _out_ln_kernel(*refs, n_parts):
    a_refs = refs[:n_parts]
    w_refs = refs[n_parts:2 * n_parts]
    h_ref, g_ref, b_ref, o_ref = refs[2 * n_parts:]
    mix = _dot(a_refs[0][...], w_refs[0][...])
    for a_ref, w_ref in zip(a_refs[1:], w_refs[1:]):
        mix = mix + _dot(a_ref[...], w_ref[...])
    y = DEEPNORM_ALPHA * h_ref[...] + mix
    o_ref[...] = _layer_norm(y, g_ref[...], b_ref[...])


def _out_ln(parts, weights, h, g, b, *, tm):
    m = h.shape[0]
    n_parts = len(parts)
    in_specs = [pl.BlockSpec((tm, a.shape[1]), lambda i: (i, 0)) for a in parts]
    in_specs += [pl.BlockSpec(w.shape, lambda i: (0, 0)) for w in weights]
    in_specs += [
        pl.BlockSpec((tm, D_MODEL), lambda i: (i, 0)),
        pl.BlockSpec((1, D_MODEL), lambda i: (0, 0)),
        pl.BlockSpec((1, D_MODEL), lambda i: (0, 0)),
    ]
    return pl.pallas_call(
        functools.partial(_out_ln_kernel, n_parts=n_parts),
        grid=(m // tm,),
        in_specs=in_specs,
        out_specs=pl.BlockSpec((tm, D_MODEL), lambda i: (i, 0)),
        out_shape=jax.ShapeDtypeStruct((m, D_MODEL), F32),
        compiler_params=_cparams("parallel"),
        name="out_proj_ln",
    )(*parts, *weights, h, g, b)


def _causal_conv(cbuf, cw_ref, rows):
    acc = cw_ref[0:1, :] * cbuf[pl.ds(SUBLANES - CONV_K + 1, rows), :]
    for i in range(1, CONV_K):
        acc = acc + cw_ref[i:i + 1, :] * cbuf[pl.ds(SUBLANES - CONV_K + 1 + i, rows), :]
    return acc


def _ssd_kernel(z_ref, x_ref, bm_ref, cm_ref, sm_ref, cw_ref, cb_ref, dtb_ref, alog_ref,
                dskip_ref, nw_ref, eh_ref, o_ref, cbuf, state):
    L = SSD_CHUNK
    W = SSD_D_INNER
    GW = W // SSD_GROUPS
    N = SSD_STATE

    @pl.when(pl.program_id(1) == 0)
    def _():
        cbuf[0:SUBLANES, :] = jnp.zeros((SUBLANES, SSD_CONV_DIM), F32)
        state[...] = jnp.zeros_like(state)

    cbuf[SUBLANES:SUBLANES + L, 0:W] = x_ref[...]
    cbuf[SUBLANES:SUBLANES + L, W:W + SSD_BC_W] = bm_ref[...]
    cbuf[SUBLANES:SUBLANES + L, W + SSD_BC_W:SSD_CONV_DIM] = cm_ref[...]
    act = _silu(_causal_conv(cbuf, cw_ref, L) + cb_ref[...])
    cbuf[0:SUBLANES, :] = cbuf[L:L + SUBLANES, :]
    xs = act[:, 0:W]
    bm = act[:, W:W + SSD_BC_W].astype(BF16)
    cm = act[:, W + SSD_BC_W:SSD_CONV_DIM].astype(BF16)

    lane = lax.broadcasted_iota(jnp.int32, (1, LANES), 1)
    dt_lanes = (lane >= 16) & (lane < 16 + SSD_HEADS)
    dt_full = jnp.where(dt_lanes, _softplus(sm_ref[...] + dtb_ref[...]), 0.0)
    adt_full = dt_full * (-jnp.exp(alog_ref[...]))
    r_i = lax.broadcasted_iota(jnp.int32, (L, L), 0)
    c_i = lax.broadcasted_iota(jnp.int32, (L, L), 1)
    tril = r_i >= c_i
    acs_full = _dot_sel_lhs(jnp.where(tril, 1.0, 0.0), adt_full)
    acs_t = acs_full.T
    eh = eh_ref[...]
    dt_exp = _dot_sel_rhs(dt_full, eh)
    acs_exp = _dot_sel_rhs(acs_full, eh)
    a_last = acs_exp[L - 1:L, :]
    xdt = xs * dt_exp
    xdec = (xdt * jnp.exp(a_last - acs_exp)).astype(BF16)
    e_acs = jnp.exp(acs_exp)
    e_last = jnp.exp(a_last)
    lane_w = lax.broadcasted_iota(jnp.int32, (1, W), 1)
    lo_half = (lane_w & (LANES - 1)) < SSD_HEADDIM
    xdt_b = xdt.astype(BF16)
    xdt_lo = jnp.where(lo_half, xdt_b, jnp.zeros_like(xdt_b))
    xdt_hi = jnp.where(lo_half, jnp.zeros_like(xdt_b), xdt_b)

    heads_per_group = SSD_HEADS // SSD_GROUPS
    y_parts = []
    for g in range(SSD_GROUPS):
        bg = bm[:, g * N:(g + 1) * N]
        cg = cm[:, g * N:(g + 1) * N]
        cb = _dot_nt(cg, bg)
        s_g = state[g]
        y_off = _dot(cg, s_g.astype(BF16)) * e_acs[:, g * GW:(g + 1) * GW]
        for pr in range(heads_per_group // 2):
            y_pair = None
            for var in range(2):
                h = g * heads_per_group + 2 * pr + var
                col = acs_full[:, 16 + h:17 + h]
                row = acs_t[16 + h:17 + h, :]
                seg = jnp.exp(jnp.where(tril, col - row, -jnp.inf))
                mh = (cb * seg).astype(BF16)
                src = xdt_lo if var == 0 else xdt_hi
                lanes0 = (g * heads_per_group + 2 * pr) * SSD_HEADDIM
                d = _dot(mh, src[:, lanes0:lanes0 + LANES])
                y_pair = d if y_pair is None else y_pair + d
            off = pr * LANES
            y_parts.append(y_pair + y_off[:, off:off + LANES])
        upd = lax.dot_general(bg, xdec[:, g * GW:(g + 1) * GW], (((0,), (0,)), ((), ())),
                              preferred_element_type=F32)
        state[g] = s_g * e_last[:, g * GW:(g + 1) * GW] + upd
    y = jnp.concatenate(y_parts, axis=1) + xs * dskip_ref[...]
    y = y * _silu(z_ref[...])
    outs = []
    for g in range(SSD_GROUPS):
        yg = y[:, g * GW:(g + 1) * GW]
        outs.append(yg * lax.rsqrt(jnp.mean(yg * yg, axis=-1, keepdims=True) + RMS_EPS))
    o_ref[...] = (jnp.concatenate(outs, axis=1) * nw_ref[...]).astype(o_ref.dtype)


def _ssd(proj, bsz, t_len, cw, cb, dtb_row, alog_row, dskip_row, nw_row, eh):
    L = SSD_CHUNK
    nc = t_len // L
    W = SSD_D_INNER
    z_blk = (GDN_CONV_DIM + GDN_V_W) // W
    x_blk = (GDN_CONV_DIM + GDN_V_W + W) // W
    b_blk = (GDN_CONV_DIM + GDN_V_W + 2 * W) // SSD_BC_W
    sm_blk = HYB_SMALL_OFF // LANES
    row = lambda b, c: b * nc + c
    const = lambda b, c: (0, 0)
    return pl.pallas_call(
        _ssd_kernel,
        grid=(bsz, nc),
        in_specs=[
            pl.BlockSpec((L, W), lambda b, c: (row(b, c), z_blk)),
            pl.BlockSpec((L, W), lambda b, c: (row(b, c), x_blk)),
            pl.BlockSpec((L, SSD_BC_W), lambda b, c: (row(b, c), b_blk)),
            pl.BlockSpec((L, SSD_BC_W), lambda b, c: (row(b, c), b_blk + 1)),
            pl.BlockSpec((L, LANES), lambda b, c: (row(b, c), sm_blk)),
            pl.BlockSpec((CONV_K, SSD_CONV_DIM), const),
            pl.BlockSpec((1, SSD_CONV_DIM), const),
            pl.BlockSpec((1, LANES), const),
            pl.BlockSpec((1, LANES), const),
            pl.BlockSpec((1, W), const),
            pl.BlockSpec((1, W), const),
            pl.BlockSpec((LANES, W), const),
        ],
        out_specs=pl.BlockSpec((L, W), lambda b, c: (row(b, c), 0)),
        out_shape=jax.ShapeDtypeStruct((bsz * t_len, W), BF16),
        scratch_shapes=[
            pltpu.VMEM((L + SUBLANES, SSD_CONV_DIM), F32),
            pltpu.VMEM((SSD_GROUPS, SSD_STATE, W // SSD_GROUPS), F32),
        ],
        compiler_params=_cparams("parallel", "arbitrary"),
        name="ssd",
    )(proj, proj, proj, proj, proj, cw, cb, dtb_row, alog_row, dskip_row, nw_row, eh)


def _unit_lower_inverse_minus_eye(a, r_i, c_i):
    n = range(len(a))
    diag8 = (r_i >> 3) == (c_i >> 3)
    x = [jnp.where(diag8, -a[i], 0.0) for i in n]
    x_b = [x[i].astype(BF16) for i in n]
    x2 = [_dot(x_b[i], x_b[i]) for i in n]
    x2_b = [x2[i].astype(BF16) for i in n]
    x4 = [_dot(x2_b[i], x2_b[i]) for i in n]
    e = [x[i] + x2[i] + _dot(x_b[i], x2_b[i]) for i in n]
    e = [e[i] + x4[i] + _dot(e[i].astype(BF16), x4[i].astype(BF16)) for i in n]
    for sh in (3, 4, 5):
        same_pair = (r_i >> (sh + 1)) == (c_i >> (sh + 1))
        lower_left = (((r_i >> sh) & 1) == 1) & (((c_i >> sh) & 1) == 0)
        ms = [jnp.where(same_pair & lower_left, a[i], 0.0) for i in n]
        y = [ms[i] + _dot(ms[i].astype(BF16), e[i].astype(BF16)) for i in n]
        e = [e[i] - y[i] - _dot(e[i].astype(BF16), y[i].astype(BF16)) for i in n]
    return e


def _stack_heads(x, r0, l0):
    return jnp.concatenate([x[r0:r0 + GDN_CHUNK, l0:l0 + GDN_DV],
                            x[r0:r0 + GDN_CHUNK, l0 + GDN_DV:l0 + 2 * GDN_DV]], axis=0)


def _own_head_block(x):
    return jnp.concatenate([x[0:GDN_CHUNK, 0:GDN_DV], x[GDN_CHUNK:, GDN_DV:]], axis=0)


def _gdn_kernel(q_ref, k_ref, v_ref, z_ref, sm_ref, cw_ref, dtb_ref, alog_ref, nw_ref,
                eb_ref, ea_ref, o_ref, cbuf, state, *, tc):
    C = GDN_CHUNK
    DK = GDN_DK
    DV = GDN_DV
    P = 2 * C

    @pl.when(pl.program_id(1) == 0)
    def _():
        cbuf[0:SUBLANES, :] = jnp.zeros((SUBLANES, GDN_CONV_DIM), F32)
        state[...] = jnp.zeros_like(state)

    cbuf[SUBLANES:SUBLANES + tc, 0:GDN_QK_W] = q_ref[...]
    cbuf[SUBLANES:SUBLANES + tc, GDN_QK_W:2 * GDN_QK_W] = k_ref[...]
    cbuf[SUBLANES:SUBLANES + tc, 2 * GDN_QK_W:GDN_CONV_DIM] = v_ref[...]
    act = _silu(_causal_conv(cbuf, cw_ref, tc))
    cbuf[0:SUBLANES, :] = cbuf[tc:tc + SUBLANES, :]

    lane = lax.broadcasted_iota(jnp.int32, (1, LANES), 1)
    sm = sm_ref[...]
    beta_full = jnp.where(lane < GDN_V_HEADS, jax.nn.sigmoid(sm), 0.0)
    g_lanes = (lane >= GDN_V_HEADS) & (lane < 2 * GDN_V_HEADS)
    g_full = jnp.where(g_lanes, -jnp.exp(alog_ref[...]) * _softplus(sm + dtb_ref[...]), 0.0)
    rt = lax.broadcasted_iota(jnp.int32, (tc, tc), 0)
    ct = lax.broadcasted_iota(jnp.int32, (tc, tc), 1)
    same_chunk_tril = ((rt >> 6) == (ct >> 6)) & (rt >= ct)
    gc_full = _dot_sel_lhs(jnp.where(same_chunk_tril, 1.0, 0.0), g_full)
    b_exp = _dot_sel_rhs(beta_full, eb_ref[...])
    g_exp = _dot_sel_rhs(gc_full, ea_ref[...])
    eg_exp = jnp.exp(g_exp)

    r_i = lax.broadcasted_iota(jnp.int32, (P, P), 0)
    c_i = lax.broadcasted_iota(jnp.int32, (P, P), 1)
    same_head = (r_i >> 6) == (c_i >> 6)
    tril = same_head & (r_i >= c_i)
    strict = same_head & (r_i > c_i)
    head0_rows = lax.broadcasted_iota(jnp.int32, (P, 1), 0) < C
    nw = nw_ref[...]

    n_chunks = tc // C
    units = [(hq, ci) for ci in range(n_chunks) for hq in range(GDN_QK_HEADS)]
    qn, kn = [], []
    for hq in range(GDN_QK_HEADS):
        qh = act[:, hq * DK:(hq + 1) * DK]
        kh = act[:, GDN_QK_W + hq * DK:GDN_QK_W + (hq + 1) * DK]
        qn.append(qh * lax.rsqrt(jnp.sum(qh * qh, axis=-1, keepdims=True) + 1e-6) * (DK ** -0.5))
        kn.append(kh * lax.rsqrt(jnp.sum(kh * kh, axis=-1, keepdims=True) + 1e-6))
    pre = {}
    a_low = []
    for hq, ci in units:
        r0 = ci * C
        l0 = hq * 2 * DV
        q_c = qn[hq][r0:r0 + C]
        k_c = kn[hq][r0:r0 + C]
        q2 = jnp.concatenate([q_c, q_c], axis=0)
        k2 = jnp.concatenate([k_c, k_c], axis=0)
        k2_b = k2.astype(BF16)
        kk = _dot_nt(k2_b, k2_b)
        qk = _dot_nt(q2.astype(BF16), k2_b)
        gcol = _stack_heads(g_exp, r0, l0)
        bcol = _stack_heads(b_exp, r0, l0)
        egc = _stack_heads(eg_exp, r0, l0)
        dmat = jnp.exp(jnp.where(tril, gcol - gcol.T, -jnp.inf))
        a_low.append(jnp.where(strict, kk * bcol * dmat, 0.0))
        g_last = jnp.where(head0_rows, gcol[C - 1:C], gcol[P - 1:P])
        pre[hq, ci] = dict(
            vb=_stack_heads(act, r0, 2 * GDN_QK_W + l0) * bcol,
            kb=k2 * (bcol * egc),
            qd=(q2 * egc).astype(BF16),
            attn=(qk * dmat).astype(BF16),
            k_t=k_c.T.astype(BF16),
            v_scale=jnp.exp(g_last - gcol),
            decay_cat=jnp.concatenate([egc[C - 1:C], egc[P - 1:P]], axis=1))
    e_all = _unit_lower_inverse_minus_eye(a_low, r_i, c_i)
    for (hq, ci), e in zip(units, e_all):
        d = pre[hq, ci]
        e_b = e.astype(BF16)
        d["u"] = d["vb"] + _dot(e_b, d["vb"].astype(BF16))
        d["w"] = (d["kb"] + _dot(e_b, d["kb"].astype(BF16))).astype(BF16)

    s_cat = [state[hq] for hq in range(GDN_QK_HEADS)]
    for ci in range(n_chunks):
        r0 = ci * C
        for hq in range(GDN_QK_HEADS):
            d = pre[hq, ci]
            l0 = hq * 2 * DV
            s_b = s_cat[hq].astype(BF16)
            v_new = d["u"] - _own_head_block(_dot(d["w"], s_b))
            o2 = _own_head_block(_dot(d["qd"], s_b)) + _dot(d["attn"], v_new.astype(BF16))
            v_dec = (v_new * d["v_scale"]).astype(BF16)
            v_dec_cat = jnp.concatenate([v_dec[0:C], v_dec[C:]], axis=1)
            s_cat[hq] = s_cat[hq] * d["decay_cat"] + _dot(d["k_t"], v_dec_cat)
            o_n = o2 * lax.rsqrt(jnp.mean(o2 * o2, axis=-1, keepdims=True) + RMS_EPS)
            res = (o_n * nw * _silu(_stack_heads(z_ref, r0, l0))).astype(o_ref.dtype)
            o_ref[r0:r0 + C, l0:l0 + DV] = res[0:C]
            o_ref[r0:r0 + C, l0 + DV:l0 + 2 * DV] = res[C:]
    for hq in range(GDN_QK_HEADS):
        state[hq] = s_cat[hq]


def _gdn(proj, bsz, t_len, cw, dtb_row, alog_row, nw_row, eb, ea, *, tc):
    nt = t_len // tc
    row = lambda b, t: b * nt + t
    const = lambda b, t: (0, 0)
    sm_blk = HYB_SMALL_OFF // LANES
    return pl.pallas_call(
        functools.partial(_gdn_kernel, tc=tc),
        grid=(bsz, nt),
        in_specs=[
            pl.BlockSpec((tc, GDN_QK_W), lambda b, t: (row(b, t), 0)),
            pl.BlockSpec((tc, GDN_QK_W), lambda b, t: (row(b, t), 1)),
            pl.BlockSpec((tc, GDN_V_W), lambda b, t: (row(b, t), 1)),
            pl.BlockSpec((tc, GDN_V_W), lambda b, t: (row(b, t), 2)),
            pl.BlockSpec((tc, LANES), lambda b, t: (row(b, t), sm_blk)),
            pl.BlockSpec((CONV_K, GDN_CONV_DIM), const),
            pl.BlockSpec((1, LANES), const),
            pl.BlockSpec((1, LANES), const),
            pl.BlockSpec((1, GDN_DV), const),
            pl.BlockSpec((LANES, GDN_V_W), const),
            pl.BlockSpec((LANES, GDN_V_W), const),
        ],
        out_specs=pl.BlockSpec((tc, GDN_V_W), lambda b, t: (row(b, t), 0)),
        out_shape=jax.ShapeDtypeStruct((bsz * t_len, GDN_V_W), BF16),
        scratch_shapes=[
            pltpu.VMEM((tc + SUBLANES, GDN_CONV_DIM), F32),
            pltpu.VMEM((GDN_QK_HEADS, GDN_DK, 2 * GDN_DV), F32),
        ],
        compiler_params=_cparams("parallel", "arbitrary"),
        name="gdn",
    )(proj, proj, proj, proj, proj, cw, dtb_row, alog_row, nw_row, eb, ea)


def _group_variants(x):
    lane = lax.broadcasted_iota(jnp.int32, (1, LANES), 1)
    lo = lane < (LANES // 2)
    xr = pltpu.roll(x, LANES // 2, axis=1)
    zero = jnp.zeros_like(x)
    return jnp.concatenate([jnp.where(lo, x, zero), jnp.where(lo, zero, xr),
                            jnp.where(lo, xr, zero), jnp.where(lo, zero, x)], axis=1)


def _values_transposed(v):
    zt = _group_variants(v).T
    extra_r = lax.broadcasted_iota(jnp.int32, (VT_ROWS - LANES, v.shape[0]), 0)
    parts = []
    for var in range(2 * NSA_GROUPS):
        parts.append(zt[var * LANES:(var + 1) * LANES])
        parts.append(jnp.where(extra_r == (var % 2), 1.0, 0.0))
    return jnp.concatenate(parts, axis=0)


def _nsa_prep_kernel(q_ref, ks_ref, vs_ref, kw_ref, vw_ref, pos_ref, freq_ref, s1_ref, s2_ref,
                     qc_o, qr_o, ksz_o, vsz_o, kwz_o, vwz_o):
    scale = NSA_DK ** -0.5 * LOG2E
    ang = pos_ref[...].astype(F32) * freq_ref[...]
    cos = jnp.cos(ang)
    sin = jnp.sin(ang)
    sin_up = sin * s1_ref[...]
    sin_dn = sin * s2_ref[...]
    half = ROPE_DIM // 2

    def rope(x):
        return x * cos + pltpu.roll(x, half, axis=1) * sin_up + pltpu.roll(x, LANES - half, axis=1) * sin_dn

    for p in range(NSA_HEADS * NSA_DK // LANES):
        x = q_ref[:, p * LANES:(p + 1) * LANES]
        qc_o[:, p * LANES:(p + 1) * LANES] = (x * scale).astype(qc_o.dtype)
        qr_o[:, p * LANES:(p + 1) * LANES] = (rope(x) * scale).astype(qr_o.dtype)
    ksz_o[...] = _group_variants(rope(ks_ref[...])).astype(ksz_o.dtype)
    kwz_o[...] = _group_variants(rope(kw_ref[...])).astype(kwz_o.dtype)
    vsz_o[...] = _values_transposed(vs_ref[...]).astype(vsz_o.dtype)
    vwz_o[...] = _values_transposed(vw_ref[...]).astype(vwz_o.dtype)


def _nsa_prep(proj, pos_col, freq_row, s1_row, s2_row, bsz, t_len, *, tm):
    m = proj.shape[0]
    tiles_per_seq = t_len // tm
    seq_t = lambda i: (i // tiles_per_seq, 0, i % tiles_per_seq)
    qw = NSA_HEADS * NSA_DK
    kv0 = qw // LANES
    const = lambda i: (0, 0)
    zw = 4 * LANES
    return pl.pallas_call(
        _nsa_prep_kernel,
        grid=(m // tm,),
        in_specs=[
            pl.BlockSpec((tm, qw), lambda i: (i, 0)),
            pl.BlockSpec((tm, LANES), lambda i: (i, kv0 + 2)),
            pl.BlockSpec((tm, LANES), lambda i: (i, kv0 + 3)),
            pl.BlockSpec((tm, LANES), lambda i: (i, kv0 + 4)),
            pl.BlockSpec((tm, LANES), lambda i: (i, kv0 + 5)),
            pl.BlockSpec((tm, 1), lambda i: (i, 0)),
            pl.BlockSpec((1, LANES), const),
            pl.BlockSpec((1, LANES), const),
            pl.BlockSpec((1, LANES), const),
        ],
        out_specs=[
            pl.BlockSpec((tm, qw), lambda i: (i, 0)),
            pl.BlockSpec((tm, qw), lambda i: (i, 0)),
            pl.BlockSpec((tm, zw), lambda i: (i, 0)),
            pl.BlockSpec((None, 4 * VT_ROWS, tm), seq_t),
            pl.BlockSpec((tm, zw), lambda i: (i, 0)),
            pl.BlockSpec((None, 4 * VT_ROWS, tm), seq_t),
        ],
        out_shape=[
            jax.ShapeDtypeStruct((m, qw), BF16),
            jax.ShapeDtypeStruct((m, qw), BF16),
            jax.ShapeDtypeStruct((m, zw), BF16),
            jax.ShapeDtypeStruct((bsz, 4 * VT_ROWS, t_len), BF16),
            jax.ShapeDtypeStruct((m, zw), BF16),
            jax.ShapeDtypeStruct((bsz, 4 * VT_ROWS, t_len), BF16),
        ],
        compiler_params=_cparams("parallel"),
        name="nsa_prep",
    )(proj, proj, proj, proj, proj, pos_col, freq_row, s1_row, s2_row)


def _nsa_compress_kernel(k16_ref, v16_ref, pos_ref, w1a_ref, w1b_ref, w2_ref, kcz_o, vcz_o):
    nr = k16_ref.shape[0]
    for idx, (x_ref, o_ref) in enumerate(((k16_ref, kcz_o), (v16_ref, vcz_o))):
        x = x_ref[...]
        h_a = _dot((x + pos_ref[idx, 0:1, :]).astype(BF16), w1a_ref[idx])
        h_b = _dot((x + pos_ref[idx, 1:2, :]).astype(BF16), w1b_ref[idx])
        hid = h_a + pltpu.roll(h_b, nr - 1, axis=0)
        out = _dot(_silu(hid).astype(BF16), w2_ref[idx])
        z = _group_variants(out)
        o_ref[...] = (z.T if idx == 1 else z).astype(o_ref.dtype)


def _nsa_compress(k16, v16, pos_ab, w1a, w1b, w2):
    bsz, nr, width = k16.shape
    hid = NSA_GROUPS * CMP_HIDDEN
    zw = 4 * LANES
    c3 = lambda b: (0, 0, 0)
    return pl.pallas_call(
        _nsa_compress_kernel,
        grid=(bsz,),
        in_specs=[
            pl.BlockSpec((None, nr, width), lambda b: (b, 0, 0)),
            pl.BlockSpec((None, nr, width), lambda b: (b, 0, 0)),
            pl.BlockSpec((2, 2, width), c3),
            pl.BlockSpec((2, width, hid), c3),
            pl.BlockSpec((2, width, hid), c3),
            pl.BlockSpec((2, hid, LANES), c3),
        ],
        out_specs=[
            pl.BlockSpec((None, nr, zw), lambda b: (b, 0, 0)),
            pl.BlockSpec((None, zw, nr), lambda b: (b, 0, 0)),
        ],
        out_shape=[
            jax.ShapeDtypeStruct((bsz, nr, zw), BF16),
            jax.ShapeDtypeStruct((bsz, zw, nr), BF16),
        ],
        compiler_params=_cparams("parallel"),
        name="nsa_compress",
    )(k16, v16, pos_ab, w1a, w1b, w2)


def _flash_branch(q_stacks, kz_ref, vzt_ref, lo, hi, bias_fns, kb_size):
    r_acc = lax.broadcasted_iota(jnp.int32, (VT_ROWS, 1), 0)
    even_rows = (r_acc < LANES // 2) | (r_acc == LANES)
    half_w = 2 * Q_BLOCK
    n_half = q_stacks[0].shape[0] // half_w
    units = [(g, h) for g in range(NSA_GROUPS) for h in range(n_half)]
    q_unit = [q_stacks[g][h * half_w:(h + 1) * half_w] for g, h in units]

    def body(kb, carry):
        m_in, acc = carry
        ks = pl.multiple_of(kb * kb_size, kb_size)
        k_both, vt_both, bias = [], [], []
        for g in range(NSA_GROUPS):
            k0 = 2 * g * LANES
            v0 = 2 * g * VT_ROWS
            k_both.append(jnp.concatenate([kz_ref[pl.ds(ks, kb_size), k0:k0 + LANES],
                                           kz_ref[pl.ds(ks, kb_size), k0 + LANES:k0 + 2 * LANES]], axis=0))
            vt_both.append(jnp.concatenate([vzt_ref[v0:v0 + VT_ROWS, pl.ds(ks, kb_size)],
                                            vzt_ref[v0 + VT_ROWS:v0 + 2 * VT_ROWS, pl.ds(ks, kb_size)]],
                                           axis=1))
            b = bias_fns[g](ks)
            bias.append(jnp.concatenate([b, b], axis=1))
        n_u = len(units)
        s, m_out, alpha_rows, p_cat, pv = {}, {}, {}, {}, {}

        def softmax_stage(u):
            g = units[u][0]
            p_parts, alphas, m_news = [], [], []
            for var in range(2):
                s_v = s[u][var * kb_size:(var + 1) * kb_size] + bias[g]
                m_prev = m_in[u][var]
                m_new = jnp.maximum(m_prev, jnp.max(s_v, axis=0, keepdims=True))
                alphas.append(jnp.exp2(m_prev - m_new))
                p_parts.append(jnp.exp2(s_v - m_new).astype(BF16))
                m_news.append(m_new)
            m_out[u] = tuple(m_news)
            alpha_rows[u] = jnp.where(even_rows, alphas[0], alphas[1])
            p_cat[u] = jnp.concatenate(p_parts, axis=0)

        for u in range(n_u):
            s[u] = _dot_nt(k_both[units[u][0]], q_unit[u])
        for u in range(n_u):
            softmax_stage(u)
        for u in range(n_u):
            pv[u] = _dot(vt_both[units[u][0]], p_cat[u])
        acc_out = tuple(acc[u] * alpha_rows[u] + pv[u] for u in range(n_u))
        return tuple(m_out[u] for u in range(n_u)), acc_out

    m_init = tuple((jnp.full((1, half_w), NEG_BIG, F32),) * 2 for _ in units)
    acc_init = tuple(jnp.zeros((VT_ROWS, half_w), F32) for _ in units)
    _, acc = lax.fori_loop(lo, hi, body, (m_init, acc_init))
    outs = []
    for g in range(NSA_GROUPS):
        a = jnp.concatenate(acc[g * n_half:(g + 1) * n_half], axis=1)
        inv_even = 1.0 / a[LANES:LANES + 1, :]
        inv_odd = 1.0 / a[LANES + 1:LANES + 2, :]
        outs.append(a[0:LANES, :] * jnp.where(r_acc[0:LANES] < LANES // 2, inv_even, inv_odd))
    return outs


def _flash_branch_bounded(q_exts, kz_ref, kext_ref, vzt_ref, lo, hi, bias_fns, last_bias_fns, kb_size):
    r_acc = lax.broadcasted_iota(jnp.int32, (VT_ROWS, 1), 0)
    half_w = 2 * Q_BLOCK
    n_half = q_exts[0].shape[0] // half_w
    units = [(g, h) for g in range(NSA_GROUPS) for h in range(n_half)]
    q_unit = [q_exts[g][h * half_w:(h + 1) * half_w] for g, h in units]
    n_u = len(units)

    def step(kb, acc, fns):
        ks = pl.multiple_of(kb * kb_size, kb_size)
        k_ext = kext_ref[pl.ds(ks, kb_size), :]
        k_both, vt_both, bias = [], [], []
        for g in range(NSA_GROUPS):
            k0 = 2 * g * LANES
            v0 = 2 * g * VT_ROWS
            k_both.append(jnp.concatenate(
                [jnp.concatenate([kz_ref[pl.ds(ks, kb_size), k0:k0 + LANES], k_ext], axis=1),
                 jnp.concatenate([kz_ref[pl.ds(ks, kb_size), k0 + LANES:k0 + 2 * LANES], k_ext], axis=1)],
                axis=0))
            vt_both.append(jnp.concatenate([vzt_ref[v0:v0 + VT_ROWS, pl.ds(ks, kb_size)],
                                            vzt_ref[v0 + VT_ROWS:v0 + 2 * VT_ROWS, pl.ds(ks, kb_size)]],
                                           axis=1))
            if fns is None:
                bias.append(None)
            else:
                b = fns[g](ks)
                b = jnp.concatenate([b, b], axis=1)
                bias.append(jnp.concatenate([b, b], axis=0))
        s = [_dot_nt(k_both[units[u][0]], q_unit[u]) for u in range(n_u)]
        p = [jnp.exp2(s[u] if bias[units[u][0]] is None else s[u] + bias[units[u][0]]).astype(BF16)
             for u in range(n_u)]
        pv = [_dot(vt_both[units[u][0]], p[u]) for u in range(n_u)]
        return tuple(acc[u] + pv[u] for u in range(n_u))

    acc = tuple(jnp.zeros((VT_ROWS, half_w), F32) for _ in units)
    if last_bias_fns is None:
        acc = lax.fori_loop(lo, hi, lambda kb, a: step(kb, a, bias_fns), acc)
    else:
        acc = lax.fori_loop(lo, hi - 1, lambda kb, a: step(kb, a, bias_fns), acc)
        acc = step(hi - 1, acc, last_bias_fns)
    outs = []
    l_min = None
    for g in range(NSA_GROUPS):
        a = jnp.concatenate(acc[g * n_half:(g + 1) * n_half], axis=1)
        l_even = a[LANES:LANES + 1, :]
        l_odd = a[LANES + 1:LANES + 2, :]
        outs.append(a[0:LANES, :] * jnp.where(r_acc[0:LANES] < LANES // 2, 1.0 / l_even, 1.0 / l_odd))
        l_g = jnp.min(jnp.minimum(l_even, l_odd))
        l_min = l_g if l_min is None else jnp.minimum(l_min, l_g)
    return outs, l_min


def _nsa_attn_kernel(qc_ref, qr_ref, gate_ref, kcz_ref, vczt_ref, ksz_ref, vszt_ref, kwz_ref, vwzt_ref,
                     aggt_ref, eselt_ref, o_ref, bias_ref, kmax_ref,
                     *, t_len, kb_size):
    n_cmp_rows = t_len // CMP_STRIDE
    n_cmp = (t_len - CMP_LEN) // CMP_STRIDE + 1
    n_sel = t_len // SEL_LEN
    n_top = min(SEL_TOPK, n_sel)
    pairs = NSA_HEADS // NSA_GROUPS // 2
    q0 = pl.program_id(1) * Q_BLOCK
    t_row = q0 + lax.broadcasted_iota(jnp.int32, (1, Q_BLOCK), 1)
    t_row_stack = q0 + (lax.broadcasted_iota(jnp.int32, (1, pairs * Q_BLOCK), 1) & (Q_BLOCK - 1))
    blk = lax.broadcasted_iota(jnp.int32, (n_sel, 1), 0)
    c_idx = lax.broadcasted_iota(jnp.int32, (n_cmp_rows, 1), 0)
    cmp_mask = ((c_idx * CMP_STRIDE + (CMP_LEN - 1)) <= t_row_stack) & (c_idx < n_cmp)
    key_pos = lax.broadcasted_iota(jnp.int32, (t_len, 1), 0)
    gate_t = jax.nn.sigmoid(gate_ref[...]).T
    top_half = lax.broadcasted_iota(jnp.int32, (LANES, 1), 0) < (LANES // 2)
    hi = (q0 + (Q_BLOCK - 1)) // kb_size + 1
    win_lo = jnp.maximum(q0 - (WINDOW - 1), 0) // kb_size

    def win_bias(ks):
        kp = ks + lax.broadcasted_iota(jnp.int32, (kb_size, 1), 0)
        return jnp.where((kp <= t_row) & (kp > t_row - WINDOW), 0.0, NEG_BIG)

    def sel_bias(g):
        return lambda ks: bias_ref[g, pl.ds(ks, kb_size), :]

    @pl.when(pl.program_id(1) == 0)
    def _():
        for br, k_ref in enumerate((ksz_ref, kwz_ref)):
            for g in range(NSA_GROUPS):
                k = k_ref[:, 2 * g * LANES:(2 * g + 1) * LANES].astype(F32)
                n2 = jnp.max(jnp.sum(k * k, axis=1, keepdims=True), axis=0, keepdims=True)
                kmax_ref[br * NSA_GROUPS + g] = jnp.broadcast_to(jnp.sqrt(n2), (SUBLANES, LANES))

    qr_stacks, o_cmps, sels = [], [], []
    for g in range(NSA_GROUPS):
        lanes_g = g * pairs * LANES
        qc_stack = jnp.concatenate(
            [qc_ref[:, lanes_g + p * LANES:lanes_g + (p + 1) * LANES] for p in range(pairs)], axis=0)
        qr_stacks.append(jnp.concatenate(
            [qr_ref[:, lanes_g + p * LANES:lanes_g + (p + 1) * LANES] for p in range(pairs)], axis=0))

        o_cmp = None
        p_sum = None
        for var in range(2):
            c0 = (2 * g + var) * LANES
            s = jnp.where(cmp_mask, _dot_nt(kcz_ref[:, c0:c0 + LANES], qc_stack), -jnp.inf)
            mx = jnp.max(s, axis=0, keepdims=True)
            mx = jnp.where(mx > -jnp.inf, mx, 0.0)
            e = jnp.exp2(s - mx)
            p = e * (1.0 / jnp.maximum(jnp.sum(e, axis=0, keepdims=True), 1e-30))
            d = _dot(vczt_ref[c0:c0 + LANES, :], p.astype(BF16))
            o_cmp = d if o_cmp is None else o_cmp + d
            for pr in range(pairs):
                part = p[:, pr * Q_BLOCK:(pr + 1) * Q_BLOCK]
                p_sum = part if p_sum is None else p_sum + part
        importance = _dot_sel_lhs(aggt_ref[...], p_sum)

        cur = t_row >> 6
        causal_blk = blk <= cur
        forced = (blk == 0) | (causal_blk & (blk > cur - SEL_LOCAL))
        score = jnp.where(forced, FORCE_SCORE, jnp.where(causal_blk, importance, -1.0))
        rank = jnp.zeros((n_sel, Q_BLOCK), F32)
        for jj in range(n_sel):
            row = score[jj:jj + 1, :]
            beats = (row > score) | ((row == score) & (blk > jj))
            rank = rank + jnp.where(beats, 1.0, 0.0)
        sel = jnp.where(rank < n_top, 1.0, 0.0)
        sels.append(jnp.concatenate([sel, jnp.zeros((LANES - n_sel, Q_BLOCK), F32)], axis=0))
        o_cmps.append(o_cmp)

    lane = lax.broadcasted_iota(jnp.int32, (1, LANES), 1)
    blk_pad = lax.broadcasted_iota(jnp.int32, (LANES, 1), 0)
    q_ext_sel, q_ext_win = [], []
    for g in range(NSA_GROUPS):
        x = qr_stacks[g].astype(F32)
        sq = x * x
        tot = jnp.sum(sq, axis=1, keepdims=True)
        even = jnp.sum(jnp.where(lane < LANES // 2, sq, 0.0), axis=1, keepdims=True)
        q_norm = jnp.sqrt(jnp.maximum(even, tot - even))
        sel_c = jnp.where(blk_pad * SEL_LEN <= t_row, sels[g], 0.0).T
        sel_b = jnp.where(lane < n_sel, (sel_c - 1.0) * (-NEG_BIG), 0.0)
        sel_b = jnp.concatenate([sel_b] * pairs, axis=0)
        ext_s = jnp.where(lane == n_sel, -q_norm * kmax_ref[g][0:1, 0:1], sel_b)
        ext_w = jnp.where(lane == n_sel, -q_norm * kmax_ref[NSA_GROUPS + g][0:1, 0:1], 0.0)
        q_ext_sel.append(jnp.concatenate([qr_stacks[g], ext_s.astype(BF16)], axis=1))
        q_ext_win.append(jnp.concatenate([qr_stacks[g], ext_w.astype(BF16)], axis=1))

    def diag_bias(ks):
        kp = ks + lax.broadcasted_iota(jnp.int32, (kb_size, 1), 0)
        return jnp.where(kp <= t_row, 0.0, NEG_BIG)

    fast_sel, l_sel = _flash_branch_bounded(q_ext_sel, ksz_ref, eselt_ref, vszt_ref, 0, hi, None,
                                            [diag_bias] * NSA_GROUPS, kb_size)
    fast_win, l_win = _flash_branch_bounded(q_ext_win, kwz_ref, eselt_ref, vwzt_ref, win_lo, hi,
                                            [win_bias] * NSA_GROUPS, None, kb_size)

    def robust():
        for g in range(NSA_GROUPS):
            sel_keys = _dot(eselt_ref[...], sels[g].astype(BF16))
            bias_ref[g] = jnp.where((sel_keys > 0.5) & (key_pos <= t_row), 0.0, NEG_BIG)
        o_s = _flash_branch(qr_stacks, ksz_ref, vszt_ref, 0, hi,
                            [sel_bias(g) for g in range(NSA_GROUPS)], kb_size)
        o_w = _flash_branch(qr_stacks, kwz_ref, vwzt_ref, win_lo, hi, [win_bias] * NSA_GROUPS, kb_size)
        return tuple(o_s) + tuple(o_w)

    o_all = lax.cond(jnp.minimum(l_sel, l_win) > DENOM_FLOOR,
                     lambda: tuple(fast_sel) + tuple(fast_win), robust)
    o_sels, o_wins = o_all[:NSA_GROUPS], o_all[NSA_GROUPS:]

    for g in range(NSA_GROUPS):
        lanes_g = g * pairs * LANES
        for pr in range(pairs):
            h_even = (g * pairs + pr) * 2
            cols = slice(pr * Q_BLOCK, (pr + 1) * Q_BLOCK)
            mixed = None
            for j, o_branch in enumerate((o_cmps[g], o_sels[g], o_wins[g])):
                r_e = h_even * 3 + j
                r_o = r_e + 3
                gate_rows = jnp.where(top_half, gate_t[r_e:r_e + 1, :], gate_t[r_o:r_o + 1, :])
                term = gate_rows * o_branch[:, cols]
                mixed = term if mixed is None else mixed + term
            c0 = lanes_g + pr * LANES
            o_ref[:, c0:c0 + LANES] = mixed.T.astype(o_ref.dtype)


def _nsa_attn(qc, qr, proj, kcz, vczt, ksz, vszt, kwz, vwzt, aggt, eselt, bsz, t_len, *, kb_size):
    nq = t_len // Q_BLOCK
    qw = NSA_HEADS * NSA_DK
    zw = 4 * LANES
    nr = t_len // CMP_STRIDE
    pairs = NSA_HEADS // NSA_GROUPS // 2
    row = lambda b, i: (b * nq + i, 0)
    per_b = lambda b, i: (b, 0, 0)
    return pl.pallas_call(
        functools.partial(_nsa_attn_kernel, t_len=t_len, kb_size=kb_size),
        grid=(bsz, nq),
        in_specs=[
            pl.BlockSpec((Q_BLOCK, qw), row),
            pl.BlockSpec((Q_BLOCK, qw), row),
            pl.BlockSpec((Q_BLOCK, LANES), lambda b, i: (b * nq + i, NSA_GATE_OFF // LANES)),
            pl.BlockSpec((None, nr, zw), per_b),
            pl.BlockSpec((None, zw, nr), per_b),
            pl.BlockSpec((None, t_len, zw), per_b),
            pl.BlockSpec((None, 4 * VT_ROWS, t_len), per_b),
            pl.BlockSpec((None, t_len, zw), per_b),
            pl.BlockSpec((None, 4 * VT_ROWS, t_len), per_b),
            pl.BlockSpec(aggt.shape, lambda b, i: (0, 0)),
            pl.BlockSpec(eselt.shape, lambda b, i: (0, 0)),
        ],
        out_specs=pl.BlockSpec((Q_BLOCK, qw), row),
        out_shape=jax.ShapeDtypeStruct((bsz * t_len, qw), BF16),
        scratch_shapes=[
            pltpu.VMEM((NSA_GROUPS, t_len, Q_BLOCK), F32),
            pltpu.VMEM((2 * NSA_GROUPS, SUBLANES, LANES), F32),
        ],
        compiler_params=_cparams("parallel", "arbitrary"),
        name="nsa_attn",
    )(qc, qr, proj, kcz, vczt, ksz, vszt, kwz, vwzt, aggt, eselt)


def _expand_rows(first_row, n_heads, width):
    e = np.zeros((LANES, n_heads * width), np.float32)
    for h in range(n_heads):
        e[first_row + h, h * width:(h + 1) * width] = 1.0
    return e


def _nsa_constants(t_len):
    n_cmp = (t_len - CMP_LEN) // CMP_STRIDE + 1
    n_sel = t_len // SEL_LEN
    nr = t_len // CMP_STRIDE
    c0 = np.arange(n_cmp)[:, None] * CMP_STRIDE
    s0 = np.arange(n_sel)[None, :] * SEL_LEN
    overlap = np.clip(np.minimum(c0 + CMP_LEN, s0 + SEL_LEN) - np.maximum(c0, s0), 0, None) / CMP_LEN
    aggt = np.zeros((n_sel, nr), np.float32)
    aggt[:, :n_cmp] = overlap.T
    eselt = np.zeros((t_len, LANES), np.float32)
    eselt[np.arange(t_len), np.arange(t_len) // SEL_LEN] = 1.0
    eselt[:, n_sel] = 1.0
    half = ROPE_DIM // 2
    inv_freq = (ROPE_THETA ** (-np.arange(half) / half)).astype(np.float32)
    d = np.arange(LANES) % NSA_DK
    freq = np.where(d < ROPE_DIM, inv_freq[d % half], 0.0).astype(np.float32)[None]
    s1 = ((d >= half) & (d < ROPE_DIM)).astype(np.float32)[None]
    s2 = -(d < half).astype(np.float32)[None]
    return aggt, eselt, freq, s1, s2


def _pad_row(vec, offset):
    return jnp.zeros((1, LANES), F32).at[0, offset:offset + vec.shape[0]].set(vec.astype(F32))


def _hybrid_layer(h, bsz, t_len, w_in, gdn_conv_w, gdn_a_log, gdn_dt_bias, gdn_norm_w, ssd_conv_w,
                  ssd_conv_b, ssd_a_log, ssd_dt_bias, ssd_d, ssd_norm_w, w_out, ln_g, ln_b, cfg):
    b_off = GDN_CONV_DIM + GDN_V_W
    z_off = b_off + 2 * GDN_V_HEADS
    dt_off = z_off + SSD_D_INNER + SSD_CONV_DIM
    w_pad = jnp.concatenate(
        [w_in[:, :b_off], w_in[:, z_off:dt_off], w_in[:, b_off:z_off], w_in[:, dt_off:],
         jnp.zeros((D_MODEL, HYB_PAD - w_in.shape[1]), w_in.dtype)], axis=1).astype(BF16)
    proj = _proj(h, w_pad, tm=cfg["proj_tm"], tn=cfg["hyb_tn"])
    o_a = _gdn(proj, bsz, t_len, gdn_conv_w, _pad_row(gdn_dt_bias, GDN_V_HEADS),
               _pad_row(gdn_a_log, GDN_V_HEADS), gdn_norm_w[None].astype(F32),
               jnp.asarray(_expand_rows(0, GDN_V_HEADS, GDN_DV)).astype(BF16),
               jnp.asarray(_expand_rows(GDN_V_HEADS, GDN_V_HEADS, GDN_DV)).astype(BF16), tc=cfg["gdn_tc"])
    o_b = _ssd(proj, bsz, t_len, ssd_conv_w, ssd_conv_b[None], _pad_row(ssd_dt_bias, 16),
               _pad_row(ssd_a_log, 16), jnp.repeat(ssd_d, SSD_HEADDIM)[None], ssd_norm_w[None],
               jnp.asarray(_expand_rows(16, SSD_HEADS, SSD_HEADDIM)).astype(BF16))
    w_out_b = w_out.astype(BF16)
    return _out_ln([o_a, o_b], [w_out_b[:GDN_V_W], w_out_b[GDN_V_W:]], h, ln_g, ln_b, tm=cfg["out_tm"])


def _nsa_layer(h, pos_col, bsz, t_len, w_in, cmp_pos, cmp_w1, cmp_w2, w_out, ln_g, ln_b, cfg):
    aggt, eselt, freq, s1, s2 = _nsa_constants(t_len)
    w_pad = jnp.concatenate(
        [w_in, jnp.zeros((D_MODEL, NSA_PAD - w_in.shape[1]), w_in.dtype)], axis=1).astype(BF16)
    proj = _proj(h, w_pad, tm=cfg["proj_tm"], tn=cfg["nsa_tn"])
    qc, qr, ksz, vszt, kwz, vwzt = _nsa_prep(proj, pos_col, jnp.asarray(freq), jnp.asarray(s1),
                                             jnp.asarray(s2), bsz, t_len, tm=cfg["prep_tm"])
    qw = NSA_HEADS * NSA_DK
    nr = t_len // CMP_STRIDE
    width = CMP_STRIDE * LANES
    k16 = proj[:, qw:qw + LANES].reshape(bsz, nr, width)
    v16 = proj[:, qw + LANES:qw + 2 * LANES].reshape(bsz, nr, width)
    pos2 = jnp.broadcast_to(cmp_pos[:, :, None, :], (2, CMP_LEN, NSA_GROUPS, NSA_DK)).reshape(2, 2, width)
    w1 = cmp_w1.reshape(2, 2, CMP_STRIDE, NSA_DK, CMP_HIDDEN)
    eye_g = jnp.eye(NSA_GROUPS, dtype=w1.dtype)
    w1x = jnp.einsum("ksjdh,ge->ksjgdeh", w1, eye_g).reshape(
        2, 2, width, NSA_GROUPS * CMP_HIDDEN).astype(BF16)
    w2x = jnp.einsum("khd,ge->kghed", cmp_w2, eye_g).reshape(
        2, NSA_GROUPS * CMP_HIDDEN, NSA_GROUPS * NSA_DK).astype(BF16)
    kcz, vczt = _nsa_compress(k16, v16, pos2, w1x[:, 0], w1x[:, 1], w2x)
    o = _nsa_attn(qc, qr, proj, kcz, vczt, ksz.reshape(bsz, t_len, -1), vszt,
                  kwz.reshape(bsz, t_len, -1), vwzt,
                  jnp.asarray(aggt).astype(BF16), jnp.asarray(eselt).astype(BF16),
                  bsz, t_len, kb_size=cfg["attn_kb"])
    return _out_ln([o], [w_out.astype(BF16)], h, ln_g, ln_b, tm=cfg["out_tm"])


def _config(m, t_len):
    return dict(ffn_tm=min(512, m), ffn_tf=256, proj_tm=min(512, m), hyb_tn=HYB_PAD, nsa_tn=NSA_PAD,
                out_tm=min(512, m), gdn_tc=min(256, t_len), prep_tm=min(256, m),
                attn_kb=min(256, t_len))


def kernel(x, positions, ln_g, ln_b, ffn_w_in, ffn_w_out, hyb_w_in, gdn_conv_w, gdn_a_log, gdn_dt_bias, gdn_norm_w, ssd_conv_w, ssd_conv_b, ssd_a_log, ssd_dt_bias, ssd_d, ssd_norm_w, hyb_w_out, nsa_w_in, nsa_cmp_pos, nsa_cmp_w1, nsa_cmp_w2, nsa_w_out):
    bsz, t_len, _ = x.shape
    m = bsz * t_len
    cfg = _config(m, t_len)
    h = x.reshape(m, D_MODEL)
    pos_col = positions.reshape(m, 1)
    w_in_b = ffn_w_in.astype(BF16)
    w_out_b = ffn_w_out.astype(BF16)
    ln_g = ln_g[:, :, None, :]
    ln_b = ln_b[:, :, None, :]
    for layer in range(DEPTH):
        i = layer // 2
        h = _ffn_ln(h, w_in_b, w_out_b, ln_g[layer, 0], ln_b[layer, 0], layer, 0,
                    tm=cfg["ffn_tm"], tf=cfg["ffn_tf"])
        if layer % 2 == 0:
            h = _hybrid_layer(h, bsz, t_len, hyb_w_in[i], gdn_conv_w[i], gdn_a_log[i], gdn_dt_bias[i],
                              gdn_norm_w[i], ssd_conv_w[i], ssd_conv_b[i], ssd_a_log[i], ssd_dt_bias[i],
                              ssd_d[i], ssd_norm_w[i], hyb_w_out[i], ln_g[layer, 1], ln_b[layer, 1], cfg)
        else:
            h = _nsa_layer(h, pos_col, bsz, t_len, nsa_w_in[i], nsa_cmp_pos[i], nsa_cmp_w1[i],
                           nsa_cmp_w2[i], nsa_w_out[i], ln_g[layer, 1], ln_b[layer, 1], cfg)
        h = _ffn_ln(h, w_in_b, w_out_b, ln_g[layer, 2], ln_b[layer, 2], layer, 1,
                    tm=cfg["ffn_tm"], tf=cfg["ffn_tf"])
    return h.reshape(bsz, t_len, D_MODEL)
```

```python
import functools

import numpy as np
import jax
import jax.numpy as jnp
from jax import lax
from jax.experimental import pallas as pl
from jax.experimental.pallas import tpu as pltpu

F32 = jnp.float32
BF16 = jnp.bfloat16

D_MODEL = 1024
DEPTH = 4
D_FF = 2816
DEEPNORM_ALPHA = (2.0 * DEPTH) ** 0.25
LN_EPS = 1e-5
RMS_EPS = 1e-6
CONV_K = 4

GDN_QK_HEADS = 4
GDN_V_HEADS = 8
GDN_DK = 128
GDN_DV = 128
GDN_CHUNK = 64
GDN_QK_W = GDN_QK_HEADS * GDN_DK
GDN_V_W = GDN_V_HEADS * GDN_DV
GDN_CONV_DIM = 2 * GDN_QK_W + GDN_V_W

SSD_D_INNER = D_MODEL
SSD_HEADDIM = 64
SSD_HEADS = SSD_D_INNER // SSD_HEADDIM
SSD_GROUPS = 2
SSD_STATE = 128
SSD_CHUNK = 128
SSD_BC_W = SSD_GROUPS * SSD_STATE
SSD_CONV_DIM = SSD_D_INNER + 2 * SSD_BC_W

NSA_HEADS = 16
NSA_GROUPS = 2
NSA_DK = 64
NSA_DV = 64
ROPE_DIM = NSA_DK // 4
ROPE_THETA = 500000.0
CMP_LEN = 32
CMP_STRIDE = 16
CMP_HIDDEN = 256
SEL_LEN = 64
SEL_TOPK = 8
SEL_LOCAL = 2
FORCE_SCORE = 1e4
WINDOW = 512
Q_BLOCK = 128

LANES = 128
SUBLANES = 8
VMEM_LIMIT_BYTES = 48 * 1024 * 1024

HYB_SMALL_OFF = GDN_CONV_DIM + GDN_V_W + SSD_D_INNER + SSD_CONV_DIM
HYB_PAD = HYB_SMALL_OFF + LANES
NSA_GATE_OFF = NSA_HEADS * NSA_DK + 3 * NSA_GROUPS * (NSA_DK + NSA_DV)
NSA_PAD = NSA_GATE_OFF + LANES

NEG_BIG = -1e30
LOG2E = 1.4426950408889634
DENOM_FLOOR = 2.0 ** -100
VT_ROWS = LANES + 16


def _cparams(*sem):
    return pltpu.CompilerParams(dimension_semantics=sem, vmem_limit_bytes=VMEM_LIMIT_BYTES)


def _silu(v):
    return v * jax.nn.sigmoid(v)


def _softplus(v):
    return jnp.maximum(v, 0.0) + jnp.log1p(jnp.exp(-jnp.abs(v)))


def _layer_norm(y, g, b):
    mu = jnp.mean(y, axis=-1, keepdims=True)
    d = y - mu
    var = jnp.mean(d * d, axis=-1, keepdims=True)
    return d * lax.rsqrt(var + LN_EPS) * g + b


def _dot(a, b):
    return jnp.dot(a, b, preferred_element_type=F32)


def _split3(x):
    hi = x.astype(BF16)
    r1 = x - hi.astype(F32)
    mid = r1.astype(BF16)
    lo = (r1 - mid.astype(F32)).astype(BF16)
    return hi, mid, lo


def _dot_sel_rhs(x, sel):
    sel = sel.astype(BF16)
    hi, mid, lo = _split3(x)
    return _dot(hi, sel) + _dot(mid, sel) + _dot(lo, sel)


def _dot_sel_lhs(sel, x):
    sel = sel.astype(BF16)
    hi, mid, lo = _split3(x)
    return _dot(sel, hi) + _dot(sel, mid) + _dot(sel, lo)


def _dot_nt(a, b):
    return lax.dot_general(a, b, (((1,), (1,)), ((), ())), preferred_element_type=F32)


def _ffn_ln_kernel(x_ref, wi_ref, wo_ref, g_ref, b_ref, o_ref, *, tf):
    x = x_ref[...]
    xb = x.astype(BF16)
    acts = []
    for c in range(D_FF // tf):
        gate = _dot(xb, wi_ref[:, c * tf:(c + 1) * tf])
        up = _dot(xb, wi_ref[:, D_FF + c * tf:D_FF + (c + 1) * tf])
        acts.append((_silu(gate) * up).astype(BF16))
    mix = _dot(jnp.concatenate(acts, axis=1), wo_ref[...])
    y = DEEPNORM_ALPHA * x + 0.5 * mix
    o_ref[...] = _layer_norm(y, g_ref[...], b_ref[...])


def _resident(shape):
    return pl.BlockSpec(shape, lambda *_: (0,) * len(shape), pipeline_mode=pl.Buffered(1))


def _ffn_ln(h, w_in, w_out, g, b, layer, which, *, tm, tf):
    m = h.shape[0]
    pick = lambda i: (layer, which, 0, 0)
    return pl.pallas_call(
        functools.partial(_ffn_ln_kernel, tf=tf),
        grid=(m // tm,),
        in_specs=[
            pl.BlockSpec((tm, D_MODEL), lambda i: (i, 0)),
            pl.BlockSpec((None, None) + w_in.shape[2:], pick, pipeline_mode=pl.Buffered(1)),
            pl.BlockSpec((None, None) + w_out.shape[2:], pick, pipeline_mode=pl.Buffered(1)),
            _resident((1, D_MODEL)),
            _resident((1, D_MODEL)),
        ],
        out_specs=pl.BlockSpec((tm, D_MODEL), lambda i: (i, 0)),
        out_shape=jax.ShapeDtypeStruct((m, D_MODEL), F32),
        compiler_params=_cparams("parallel"),
        name="ffn_ln",
    )(h, w_in, w_out, g, b)


def _proj_kernel(x_ref, w_ref, o_ref, *, tn):
    xb = x_ref[...].astype(BF16)
    for c in range(w_ref.shape[1] // tn):
        o_ref[:, c * tn:(c + 1) * tn] = _dot(xb, w_ref[:, c * tn:(c + 1) * tn])


def _proj(h, w, *, tm, tn):
    m, k = h.shape
    n = w.shape[1]
    return pl.pallas_call(
        functools.partial(_proj_kernel, tn=tn),
        grid=(m // tm,),
        in_specs=[
            pl.BlockSpec((tm, k), lambda i: (i, 0)),
            _resident(w.shape),
        ],
        out_specs=pl.BlockSpec((tm, n), lambda i: (i, 0)),
        out_shape=jax.ShapeDtypeStruct((m, n), F32),
        compiler_params=_cparams("parallel"),
        name="in_proj",
    )(h, w)


def _out_ln_kernel(*refs, n_parts):
    a_refs = refs[:n_parts]
    w_refs = refs[n_parts:2 * n_parts]
    h_ref, g_ref, b_ref, o_ref = refs[2 * n_parts:]
    mix = _dot(a_refs[0][...], w_refs[0][...])
    for a_ref, w_ref in zip(a_refs[1:], w_refs[1:]):
        mix = mix + _dot(a_ref[...], w_ref[...])
    y = DEEPNORM_ALPHA * h_ref[...] + mix
    o_ref[...] = _layer_norm(y, g_ref[...], b_ref[...])


def _out_ln(parts, weights, h, g, b, *, tm):
    m = h.shape[0]
    n_parts = len(parts)
    in_specs = [pl.BlockSpec((tm, a.shape[1]), lambda i: (i, 0)) for a in parts]
    in_specs += [pl.BlockSpec(w.shape, lambda i: (0, 0)) for w in weights]
    in_specs += [
        pl.BlockSpec((tm, D_MODEL), lambda i: (i, 0)),
        pl.BlockSpec((1, D_MODEL), lambda i: (0, 0)),
        pl.BlockSpec((1, D_MODEL), lambda i: (0, 0)),
    ]
    return pl.pallas_call(
        functools.partial(_out_ln_kernel, n_parts=n_parts),
        grid=(m // tm,),
        in_specs=in_specs,
        out_specs=pl.BlockSpec((tm, D_MODEL), lambda i: (i, 0)),
        out_shape=jax.ShapeDtypeStruct((m, D_MODEL), F32),
        compiler_params=_cparams("parallel"),
        name="out_proj_ln",
    )(*parts, *weights, h, g, b)


def _causal_conv(cbuf, cw_ref, rows):
    acc = cw_ref[0:1, :] * cbuf[pl.ds(SUBLANES - CONV_K + 1, rows), :]
    for i in range(1, CONV_K):
        acc = acc + cw_ref[i:i + 1, :] * cbuf[pl.ds(SUBLANES - CONV_K + 1 + i, rows), :]
    return acc


def _ssd_kernel(z_ref, x_ref, bm_ref, cm_ref, sm_ref, cw_ref, cb_ref, dtb_ref, alog_ref,
                dskip_ref, nw_ref, eh_ref, o_ref, cbuf, state):
    L = SSD_CHUNK
    W = SSD_D_INNER
    GW = W // SSD_GROUPS
    N = SSD_STATE

    @pl.when(pl.program_id(1) == 0)
    def _():
        cbuf[0:SUBLANES, :] = jnp.zeros((SUBLANES, SSD_CONV_DIM), F32)
        state[...] = jnp.zeros_like(state)

    cbuf[SUBLANES:SUBLANES + L, 0:W] = x_ref[...]
    cbuf[SUBLANES:SUBLANES + L, W:W + SSD_BC_W] = bm_ref[...]
    cbuf[SUBLANES:SUBLANES + L, W + SSD_BC_W:SSD_CONV_DIM] = cm_ref[...]
    act = _silu(_causal_conv(cbuf, cw_ref, L) + cb_ref[...])
    cbuf[0:SUBLANES, :] = cbuf[L:L + SUBLANES, :]
    xs = act[:, 0:W]
    bm = act[:, W:W + SSD_BC_W].astype(BF16)
    cm = act[:, W + SSD_BC_W:SSD_CONV_DIM].astype(BF16)

    lane = lax.broadcasted_iota(jnp.int32, (1, LANES), 1)
    dt_lanes = (lane >= 16) & (lane < 16 + SSD_HEADS)
    dt_full = jnp.where(dt_lanes, _softplus(sm_ref[...] + dtb_ref[...]), 0.0)
    adt_full = dt_full * (-jnp.exp(alog_ref[...]))
    r_i = lax.broadcasted_iota(jnp.int32, (L, L), 0)
    c_i = lax.broadcasted_iota(jnp.int32, (L, L), 1)
    tril = r_i >= c_i
    acs_full = _dot_sel_lhs(jnp.where(tril, 1.0, 0.0), adt_full)
    acs_t = acs_full.T
    eh = eh_ref[...]
    dt_exp = _dot_sel_rhs(dt_full, eh)
    acs_exp = _dot_sel_rhs(acs_full, eh)
    a_last = acs_exp[L - 1:L, :]
    xdt = xs * dt_exp
    xdec = (xdt * jnp.exp(a_last - acs_exp)).astype(BF16)
    e_acs = jnp.exp(acs_exp)
    e_last = jnp.exp(a_last)
    lane_w = lax.broadcasted_iota(jnp.int32, (1, W), 1)
    lo_half = (lane_w & (LANES - 1)) < SSD_HEADDIM
    xdt_b = xdt.astype(BF16)
    xdt_lo = jnp.where(lo_half, xdt_b, jnp.zeros_like(xdt_b))
    xdt_hi = jnp.where(lo_half, jnp.zeros_like(xdt_b), xdt_b)

    heads_per_group = SSD_HEADS // SSD_GROUPS
    y_parts = []
    for g in range(SSD_GROUPS):
        bg = bm[:, g * N:(g + 1) * N]
        cg = cm[:, g * N:(g + 1) * N]
        cb = _dot_nt(cg, bg)
        s_g = state[g]
        y_off = _dot(cg, s_g.astype(BF16)) * e_acs[:, g * GW:(g + 1) * GW]
        for pr in range(heads_per_group // 2):
            y_pair = None
            for var in range(2):
                h = g * heads_per_group + 2 * pr + var
                col = acs_full[:, 16 + h:17 + h]
                row = acs_t[16 + h:17 + h, :]
                seg = jnp.exp(jnp.where(tril, col - row, -jnp.inf))
                mh = (cb * seg).astype(BF16)
                src = xdt_lo if var == 0 else xdt_hi
                lanes0 = (g * heads_per_group + 2 * pr) * SSD_HEADDIM
                d = _dot(mh, src[:, lanes0:lanes0 + LANES])
                y_pair = d if y_pair is None else y_pair + d
            off = pr * LANES
            y_parts.append(y_pair + y_off[:, off:off + LANES])
        upd = lax.dot_general(bg, xdec[:, g * GW:(g + 1) * GW], (((0,), (0,)), ((), ())),
                              preferred_element_type=F32)
        state[g] = s_g * e_last[:, g * GW:(g + 1) * GW] + upd
    y = jnp.concatenate(y_parts, axis=1) + xs * dskip_ref[...]
    y = y * _silu(z_ref[...])
    outs = []
    for g in range(SSD_GROUPS):
        yg = y[:, g * GW:(g + 1) * GW]
        outs.append(yg * lax.rsqrt(jnp.mean(yg * yg, axis=-1, keepdims=True) + RMS_EPS))
    o_ref[...] = (jnp.concatenate(outs, axis=1) * nw_ref[...]).astype(o_ref.dtype)


def _ssd(proj, bsz, t_len, cw, cb, dtb_row, alog_row, dskip_row, nw_row, eh):
    L = SSD_CHUNK
    nc = t_len // L
    W = SSD_D_INNER
    z_blk = (GDN_CONV_DIM + GDN_V_W) // W
    x_blk = (GDN_CONV_DIM + GDN_V_W + W) // W
    b_blk = (GDN_CONV_DIM + GDN_V_W + 2 * W) // SSD_BC_W
    sm_blk = HYB_SMALL_OFF // LANES
    row = lambda b, c: b * nc + c
    const = lambda b, c: (0, 0)
    return pl.pallas_call(
        _ssd_kernel,
        grid=(bsz, nc),
        in_specs=[
            pl.BlockSpec((L, W), lambda b, c: (row(b, c), z_blk)),
            pl.BlockSpec((L, W), lambda b, c: (row(b, c), x_blk)),
            pl.BlockSpec((L, SSD_BC_W), lambda b, c: (row(b, c), b_blk)),
            pl.BlockSpec((L, SSD_BC_W), lambda b, c: (row(b, c), b_blk + 1)),
            pl.BlockSpec((L, LANES), lambda b, c: (row(b, c), sm_blk)),
            pl.BlockSpec((CONV_K, SSD_CONV_DIM), const),
            pl.BlockSpec((1, SSD_CONV_DIM), const),
            pl.BlockSpec((1, LANES), const),
            pl.BlockSpec((1, LANES), const),
            pl.BlockSpec((1, W), const),
            pl.BlockSpec((1, W), const),
            pl.BlockSpec((LANES, W), const),
        ],
        out_specs=pl.BlockSpec((L, W), lambda b, c: (row(b, c), 0)),
        out_shape=jax.ShapeDtypeStruct((bsz * t_len, W), BF16),
        scratch_shapes=[
            pltpu.VMEM((L + SUBLANES, SSD_CONV_DIM), F32),
            pltpu.VMEM((SSD_GROUPS, SSD_STATE, W // SSD_GROUPS), F32),
        ],
        compiler_params=_cparams("parallel", "arbitrary"),
        name="ssd",
    )(proj, proj, proj, proj, proj, cw, cb, dtb_row, alog_row, dskip_row, nw_row, eh)


def _unit_lower_inverse_minus_eye(a, r_i, c_i):
    n = range(len(a))
    diag8 = (r_i >> 3) == (c_i >> 3)
    x = [jnp.where(diag8, -a[i], 0.0) for i in n]
    x_b = [x[i].astype(BF16) for i in n]
    x2 = [_dot(x_b[i], x_b[i]) for i in n]
    x2_b = [x2[i].astype(BF16) for i in n]
    x4 = [_dot(x2_b[i], x2_b[i]) for i in n]
    e = [x[i] + x2[i] + _dot(x_b[i], x2_b[i]) for i in n]
    e = [e[i] + x4[i] + _dot(e[i].astype(BF16), x4[i].astype(BF16)) for i in n]
    for sh in (3, 4, 5):
        same_pair = (r_i >> (sh + 1)) == (c_i >> (sh + 1))
        lower_left = (((r_i >> sh) & 1) == 1) & (((c_i >> sh) & 1) == 0)
        ms = [jnp.where(same_pair & lower_left, a[i], 0.0) for i in n]
        y = [ms[i] + _dot(ms[i].astype(BF16), e[i].astype(BF16)) for i in n]
        e = [e[i] - y[i] - _dot(e[i].astype(BF16), y[i].astype(BF16)) for i in n]
    return e


def _stack_heads(x, r0, l0):
    return jnp.concatenate([x[r0:r0 + GDN_CHUNK, l0:l0 + GDN_DV],
                            x[r0:r0 + GDN_CHUNK, l0 + GDN_DV:l0 + 2 * GDN_DV]], axis=0)


def _own_head_block(x):
    return jnp.concatenate([x[0:GDN_CHUNK, 0:GDN_DV], x[GDN_CHUNK:, GDN_DV:]], axis=0)


def _gdn_kernel(q_ref, k_ref, v_ref, z_ref, sm_ref, cw_ref, dtb_ref, alog_ref, nw_ref,
                eb_ref, ea_ref, o_ref, cbuf, state, *, tc):
    C = GDN_CHUNK
    DK = GDN_DK
    DV = GDN_DV
    P = 2 * C

    @pl.when(pl.program_id(1) == 0)
    def _():
        cbuf[0:SUBLANES, :] = jnp.zeros((SUBLANES, GDN_CONV_DIM), F32)
        state[...] = jnp.zeros_like(state)

    cbuf[SUBLANES:SUBLANES + tc, 0:GDN_QK_W] = q_ref[...]
    cbuf[SUBLANES:SUBLANES + tc, GDN_QK_W:2 * GDN_QK_W] = k_ref[...]
    cbuf[SUBLANES:SUBLANES + tc, 2 * GDN_QK_W:GDN_CONV_DIM] = v_ref[...]
    act = _silu(_causal_conv(cbuf, cw_ref, tc))
    cbuf[0:SUBLANES, :] = cbuf[tc:tc + SUBLANES, :]

    lane = lax.broadcasted_iota(jnp.int32, (1, LANES), 1)
    sm = sm_ref[...]
    beta_full = jnp.where(lane < GDN_V_HEADS, jax.nn.sigmoid(sm), 0.0)
    g_lanes = (lane >= GDN_V_HEADS) & (lane < 2 * GDN_V_HEADS)
    g_full = jnp.where(g_lanes, -jnp.exp(alog_ref[...]) * _softplus(sm + dtb_ref[...]), 0.0)
    rt = lax.broadcasted_iota(jnp.int32, (tc, tc), 0)
    ct = lax.broadcasted_iota(jnp.int32, (tc, tc), 1)
    same_chunk_tril = ((rt >> 6) == (ct >> 6)) & (rt >= ct)
    gc_full = _dot_sel_lhs(jnp.where(same_chunk_tril, 1.0, 0.0), g_full)
    b_exp = _dot_sel_rhs(beta_full, eb_ref[...])
    g_exp = _dot_sel_rhs(gc_full, ea_ref[...])
    eg_exp = jnp.exp(g_exp)

    r_i = lax.broadcasted_iota(jnp.int32, (P, P), 0)
    c_i = lax.broadcasted_iota(jnp.int32, (P, P), 1)
    same_head = (r_i >> 6) == (c_i >> 6)
    tril = same_head & (r_i >= c_i)
    strict = same_head & (r_i > c_i)
    head0_rows = lax.broadcasted_iota(jnp.int32, (P, 1), 0) < C
    nw = nw_ref[...]

    n_chunks = tc // C
    units = [(hq, ci) for ci in range(n_chunks) for hq in range(GDN_QK_HEADS)]
    qn, kn = [], []
    for hq in range(GDN_QK_HEADS):
        qh = act[:, hq * DK:(hq + 1) * DK]
        kh = act[:, GDN_QK_W + hq * DK:GDN_QK_W + (hq + 1) * DK]
        qn.append(qh * lax.rsqrt(jnp.sum(qh * qh, axis=-1, keepdims=True) + 1e-6) * (DK ** -0.5))
        kn.append(kh * lax.rsqrt(jnp.sum(kh * kh, axis=-1, keepdims=True) + 1e-6))
    pre = {}
    a_low = []
    for hq, ci in units:
        r0 = ci * C
        l0 = hq * 2 * DV
        q_c = qn[hq][r0:r0 + C]
        k_c = kn[hq][r0:r0 + C]
        q2 = jnp.concatenate([q_c, q_c], axis=0)
        k2 = jnp.concatenate([k_c, k_c], axis=0)
        k2_b = k2.astype(BF16)
        kk = _dot_nt(k2_b, k2_b)
        qk = _dot_nt(q2.astype(BF16), k2_b)
        gcol = _stack_heads(g_exp, r0, l0)
        bcol = _stack_heads(b_exp, r0, l0)
        egc = _stack_heads(eg_exp, r0, l0)
        dmat = jnp.exp(jnp.where(tril, gcol - gcol.T, -jnp.inf))
        a_low.append(jnp.where(strict, kk * bcol * dmat, 0.0))
        g_last = jnp.where(head0_rows, gcol[C - 1:C], gcol[P - 1:P])
        pre[hq, ci] = dict(
            vb=_stack_heads(act, r0, 2 * GDN_QK_W + l0) * bcol,
            kb=k2 * (bcol * egc),
            qd=(q2 * egc).astype(BF16),
            attn=(qk * dmat).astype(BF16),
            k_t=k_c.T.astype(BF16),
            v_scale=jnp.exp(g_last - gcol),
            decay_cat=jnp.concatenate([egc[C - 1:C], egc[P - 1:P]], axis=1))
    e_all = _unit_lower_inverse_minus_eye(a_low, r_i, c_i)
    for (hq, ci), e in zip(units, e_all):
        d = pre[hq, ci]
        e_b = e.astype(BF16)
        d["u"] = d["vb"] + _dot(e_b, d["vb"].astype(BF16))
        d["w"] = (d["kb"] + _dot(e_b, d["kb"].astype(BF16))).astype(BF16)

    s_cat = [state[hq] for hq in range(GDN_QK_HEADS)]
    for ci in range(n_chunks):
        r0 = ci * C
        for hq in range(GDN_QK_HEADS):
            d = pre[hq, ci]
            l0 = hq * 2 * DV
            s_b = s_cat[hq].astype(BF16)
            v_new = d["u"] - _own_head_block(_dot(d["w"], s_b))
            o2 = _own_head_block(_dot(d["qd"], s_b)) + _dot(d["attn"], v_new.astype(BF16))
            v_dec = (v_new * d["v_scale"]).astype(BF16)
            v_dec_cat = jnp.concatenate([v_dec[0:C], v_dec[C:]], axis=1)
            s_cat[hq] = s_cat[hq] * d["decay_cat"] + _dot(d["k_t"], v_dec_cat)
            o_n = o2 * lax.rsqrt(jnp.mean(o2 * o2, axis=-1, keepdims=True) + RMS_EPS)
            res = (o_n * nw * _silu(_stack_heads(z_ref, r0, l0))).astype(o_ref.dtype)
            o_ref[r0:r0 + C, l0:l0 + DV] = res[0:C]
            o_ref[r0:r0 + C, l0 + DV:l0 + 2 * DV] = res[C:]
    for hq in range(GDN_QK_HEADS):
        state[hq] = s_cat[hq]


def _gdn(proj, bsz, t_len, cw, dtb_row, alog_row, nw_row, eb, ea, *, tc):
    nt = t_len // tc
    row = lambda b, t: b * nt + t
    const = lambda b, t: (0, 0)
    sm_blk = HYB_SMALL_OFF // LANES
    return pl.pallas_call(
        functools.partial(_gdn_kernel, tc=tc),
        grid=(bsz, nt),
        in_specs=[
            pl.BlockSpec((tc, GDN_QK_W), lambda b, t: (row(b, t), 0)),
            pl.BlockSpec((tc, GDN_QK_W), lambda b, t: (row(b, t), 1)),
            pl.BlockSpec((tc, GDN_V_W), lambda b, t: (row(b, t), 1)),
            pl.BlockSpec((tc, GDN_V_W), lambda b, t: (row(b, t), 2)),
            pl.BlockSpec((tc, LANES), lambda b, t: (row(b, t), sm_blk)),
            pl.BlockSpec((CONV_K, GDN_CONV_DIM), const),
            pl.BlockSpec((1, LANES), const),
            pl.BlockSpec((1, LANES), const),
            pl.BlockSpec((1, GDN_DV), const),
            pl.BlockSpec((LANES, GDN_V_W), const),
            pl.BlockSpec((LANES, GDN_V_W), const),
        ],
        out_specs=pl.BlockSpec((tc, GDN_V_W), lambda b, t: (row(b, t), 0)),
        out_shape=jax.ShapeDtypeStruct((bsz * t_len, GDN_V_W), BF16),
        scratch_shapes=[
            pltpu.VMEM((tc + SUBLANES, GDN_CONV_DIM), F32),
            pltpu.VMEM((GDN_QK_HEADS, GDN_DK, 2 * GDN_DV), F32),
        ],
        compiler_params=_cparams("parallel", "arbitrary"),
        name="gdn",
    )(proj, proj, proj, proj, proj, cw, dtb_row, alog_row, nw_row, eb, ea)


def _group_variants(x):
    lane = lax.broadcasted_iota(jnp.int32, (1, LANES), 1)
    lo = lane < (LANES // 2)
    xr = pltpu.roll(x, LANES // 2, axis=1)
    zero = jnp.zeros_like(x)
    return jnp.concatenate([jnp.where(lo, x, zero), jnp.where(lo, zero, xr),
                            jnp.where(lo, xr, zero), jnp.where(lo, zero, x)], axis=1)


def _values_transposed(v):
    zt = _group_variants(v).T
    extra_r = lax.broadcasted_iota(jnp.int32, (VT_ROWS - LANES, v.shape[0]), 0)
    parts = []
    for var in range(2 * NSA_GROUPS):
        parts.append(zt[var * LANES:(var + 1) * LANES])
        parts.append(jnp.where(extra_r == (var % 2), 1.0, 0.0))
    return jnp.concatenate(parts, axis=0)


def _nsa_prep_kernel(q_ref, ks_ref, vs_ref, kw_ref, vw_ref, pos_ref, freq_ref, sel_ref, one_ref,
                     qc_o, qr_o, ksz_o, vsz_o, kwz_o, vwz_o):
    scale = NSA_DK ** -0.5 * LOG2E
    ang = freq_ref[...] * pos_ref[...].astype(F32)
    half = ROPE_DIM // 2
    tm = ang.shape[1]
    cs = jnp.concatenate([jnp.cos(ang), jnp.sin(ang), jnp.zeros((LANES - 2 * half, tm), F32)], axis=0)
    spread = _dot_sel_rhs(cs.T, sel_ref[...])
    cos = spread[:, 0:LANES] + one_ref[...]
    sin_up = spread[:, LANES:2 * LANES]
    sin_dn = spread[:, 2 * LANES:3 * LANES]

    def rope(x):
        return x * cos + pltpu.roll(x, half, axis=1) * sin_up + pltpu.roll(x, LANES - half, axis=1) * sin_dn

    for p in range(NSA_HEADS * NSA_DK // LANES):
        x = q_ref[:, p * LANES:(p + 1) * LANES]
        qc_o[:, p * LANES:(p + 1) * LANES] = (x * scale).astype(qc_o.dtype)
        qr_o[:, p * LANES:(p + 1) * LANES] = (rope(x) * scale).astype(qr_o.dtype)
    ksz_o[...] = _group_variants(rope(ks_ref[...])).astype(ksz_o.dtype)
    kwz_o[...] = _group_variants(rope(kw_ref[...])).astype(kwz_o.dtype)
    vsz_o[...] = _values_transposed(vs_ref[...]).astype(vsz_o.dtype)
    vwz_o[...] = _values_transposed(vw_ref[...]).astype(vwz_o.dtype)


def _nsa_prep(proj, pos_row, freq_col, rope_sel, one_row, bsz, t_len, *, tm):
    m = proj.shape[0]
    tiles_per_seq = t_len // tm
    seq_t = lambda i: (i // tiles_per_seq, 0, i % tiles_per_seq)
    qw = NSA_HEADS * NSA_DK
    kv0 = qw // LANES
    const = lambda i: (0, 0)
    zw = 4 * LANES
    return pl.pallas_call(
        _nsa_prep_kernel,
        grid=(m // tm,),
        in_specs=[
            pl.BlockSpec((tm, qw), lambda i: (i, 0)),
            pl.BlockSpec((tm, LANES), lambda i: (i, kv0 + 2)),
            pl.BlockSpec((tm, LANES), lambda i: (i, kv0 + 3)),
            pl.BlockSpec((tm, LANES), lambda i: (i, kv0 + 4)),
            pl.BlockSpec((tm, LANES), lambda i: (i, kv0 + 5)),
            pl.BlockSpec((1, tm), lambda i: (0, i)),
            pl.BlockSpec(freq_col.shape, const),
            pl.BlockSpec(rope_sel.shape, const),
            pl.BlockSpec((1, LANES), const),
        ],
        out_specs=[
            pl.BlockSpec((tm, qw), lambda i: (i, 0)),
            pl.BlockSpec((tm, qw), lambda i: (i, 0)),
            pl.BlockSpec((tm, zw), lambda i: (i, 0)),
            pl.BlockSpec((None, 4 * VT_ROWS, tm), seq_t),
            pl.BlockSpec((tm, zw), lambda i: (i, 0)),
            pl.BlockSpec((None, 4 * VT_ROWS, tm), seq_t),
        ],
        out_shape=[
            jax.ShapeDtypeStruct((m, qw), BF16),
            jax.ShapeDtypeStruct((m, qw), BF16),
            jax.ShapeDtypeStruct((m, zw), BF16),
            jax.ShapeDtypeStruct((bsz, 4 * VT_ROWS, t_len), BF16),
            jax.ShapeDtypeStruct((m, zw), BF16),
            jax.ShapeDtypeStruct((bsz, 4 * VT_ROWS, t_len), BF16),
        ],
        compiler_params=_cparams("parallel"),
        name="nsa_prep",
    )(proj, proj, proj, proj, proj, pos_row, freq_col, rope_sel, one_row)


def _nsa_compress_kernel(k16_ref, v16_ref, pos_ref, w1a_ref, w1b_ref, w2_ref, kcz_o, vcz_o):
    nr = k16_ref.shape[0]
    for idx, (x_ref, o_ref) in enumerate(((k16_ref, kcz_o), (v16_ref, vcz_o))):
        x = x_ref[...]
        h_a = _dot((x + pos_ref[idx, 0:1, :]).astype(BF16), w1a_ref[idx])
        h_b = _dot((x + pos_ref[idx, 1:2, :]).astype(BF16), w1b_ref[idx])
        hid = h_a + pltpu.roll(h_b, nr - 1, axis=0)
        out = _dot(_silu(hid).astype(BF16), w2_ref[idx])
        z = _group_variants(out)
        o_ref[...] = (z.T if idx == 1 else z).astype(o_ref.dtype)


def _nsa_compress(k16, v16, pos_ab, w1a, w1b, w2):
    bsz, nr, width = k16.shape
    hid = NSA_GROUPS * CMP_HIDDEN
    zw = 4 * LANES
    c3 = lambda b: (0, 0, 0)
    return pl.pallas_call(
        _nsa_compress_kernel,
        grid=(bsz,),
        in_specs=[
            pl.BlockSpec((None, nr, width), lambda b: (b, 0, 0)),
            pl.BlockSpec((None, nr, width), lambda b: (b, 0, 0)),
            pl.BlockSpec((2, 2, width), c3),
            pl.BlockSpec((2, width, hid), c3),
            pl.BlockSpec((2, width, hid), c3),
            pl.BlockSpec((2, hid, LANES), c3),
        ],
        out_specs=[
            pl.BlockSpec((None, nr, zw), lambda b: (b, 0, 0)),
            pl.BlockSpec((None, zw, nr), lambda b: (b, 0, 0)),
        ],
        out_shape=[
            jax.ShapeDtypeStruct((bsz, nr, zw), BF16),
            jax.ShapeDtypeStruct((bsz, zw, nr), BF16),
        ],
        compiler_params=_cparams("parallel"),
        name="nsa_compress",
    )(k16, v16, pos_ab, w1a, w1b, w2)


def _flash_branch(q_stacks, kz_ref, vzt_ref, lo, hi, bias_fns, kb_size):
    r_acc = lax.broadcasted_iota(jnp.int32, (VT_ROWS, 1), 0)
    even_rows = (r_acc < LANES // 2) | (r_acc == LANES)
    half_w = 2 * Q_BLOCK
    n_half = q_stacks[0].shape[0] // half_w
    units = [(g, h) for g in range(NSA_GROUPS) for h in range(n_half)]
    q_unit = [q_stacks[g][h * half_w:(h + 1) * half_w] for g, h in units]

    def body(kb, carry):
        m_in, acc = carry
        ks = pl.multiple_of(kb * kb_size, kb_size)
        k_both, vt_both, bias = [], [], []
        for g in range(NSA_GROUPS):
            k0 = 2 * g * LANES
            v0 = 2 * g * VT_ROWS
            k_both.append(jnp.concatenate([kz_ref[pl.ds(ks, kb_size), k0:k0 + LANES],
                                           kz_ref[pl.ds(ks, kb_size), k0 + LANES:k0 + 2 * LANES]], axis=0))
            vt_both.append(jnp.concatenate([vzt_ref[v0:v0 + VT_ROWS, pl.ds(ks, kb_size)],
                                            vzt_ref[v0 + VT_ROWS:v0 + 2 * VT_ROWS, pl.ds(ks, kb_size)]],
                                           axis=1))
            b = bias_fns[g](ks)
            bias.append(jnp.concatenate([b, b], axis=1))
        n_u = len(units)
        s, m_out, alpha_rows, p_cat, pv = {}, {}, {}, {}, {}

        def softmax_stage(u):
            g = units[u][0]
            p_parts, alphas, m_news = [], [], []
            for var in range(2):
                s_v = s[u][var * kb_size:(var + 1) * kb_size] + bias[g]
                m_prev = m_in[u][var]
                m_new = jnp.maximum(m_prev, jnp.max(s_v, axis=0, keepdims=True))
                alphas.append(jnp.exp2(m_prev - m_new))
                p_parts.append(jnp.exp2(s_v - m_new).astype(BF16))
                m_news.append(m_new)
            m_out[u] = tuple(m_news)
            alpha_rows[u] = jnp.where(even_rows, alphas[0], alphas[1])
            p_cat[u] = jnp.concatenate(p_parts, axis=0)

        for u in range(n_u):
            s[u] = _dot_nt(k_both[units[u][0]], q_unit[u])
        for u in range(n_u):
            softmax_stage(u)
        for u in range(n_u):
            pv[u] = _dot(vt_both[units[u][0]], p_cat[u])
        acc_out = tuple(acc[u] * alpha_rows[u] + pv[u] for u in range(n_u))
        return tuple(m_out[u] for u in range(n_u)), acc_out

    m_init = tuple((jnp.full((1, half_w), NEG_BIG, F32),) * 2 for _ in units)
    acc_init = tuple(jnp.zeros((VT_ROWS, half_w), F32) for _ in units)
    _, acc = lax.fori_loop(lo, hi, body, (m_init, acc_init))
    outs = []
    for g in range(NSA_GROUPS):
        a = jnp.concatenate(acc[g * n_half:(g + 1) * n_half], axis=1)
        inv_even = 1.0 / a[LANES:LANES + 1, :]
        inv_odd = 1.0 / a[LANES + 1:LANES + 2, :]
        outs.append(a[0:LANES, :] * jnp.where(r_acc[0:LANES] < LANES // 2, inv_even, inv_odd))
    return outs


def _flash_branch_bounded(q_exts, kz_ref, kext_ref, vzt_ref, lo, hi, bias_fns, last_bias_fns, kb_size):
    r_acc = lax.broadcasted_iota(jnp.int32, (VT_ROWS, 1), 0)
    half_w = 2 * Q_BLOCK
    n_half = q_exts[0].shape[0] // half_w
    units = [(g, h) for g in range(NSA_GROUPS) for h in range(n_half)]
    q_unit = [q_exts[g][h * half_w:(h + 1) * half_w] for g, h in units]
    n_u = len(units)

    def step(kbs, acc, fns):
        work = []
        for kb in kbs:
            ks = pl.multiple_of(jnp.maximum(kb, 0) * kb_size, kb_size)
            k_ext = kext_ref[pl.ds(ks, kb_size), :]
            for g in range(NSA_GROUPS):
                k0 = 2 * g * LANES
                v0 = 2 * g * VT_ROWS
                k_both = jnp.concatenate(
                    [jnp.concatenate([kz_ref[pl.ds(ks, kb_size), k0:k0 + LANES], k_ext], axis=1),
                     jnp.concatenate([kz_ref[pl.ds(ks, kb_size), k0 + LANES:k0 + 2 * LANES], k_ext],
                                     axis=1)], axis=0)
                vt_both = jnp.concatenate([vzt_ref[v0:v0 + VT_ROWS, pl.ds(ks, kb_size)],
                                           vzt_ref[v0 + VT_ROWS:v0 + 2 * VT_ROWS, pl.ds(ks, kb_size)]],
                                          axis=1)
                bias = None
                if fns is not None:
                    b = fns[g](kb, ks)
                    b = jnp.concatenate([b, b], axis=1)
                    bias = jnp.concatenate([b, b], axis=0)
                work += [(u, k_both, vt_both, bias) for u in range(n_u) if units[u][0] == g]
        s = [_dot_nt(k, q_unit[u]) for u, k, _, _ in work]
        p = [jnp.exp2(s_i if w[3] is None else s_i + w[3]).astype(BF16) for s_i, w in zip(s, work)]
        pv = [_dot(w[2], p_i) for p_i, w in zip(p, work)]
        out = list(acc)
        for (u, _, _, _), pv_i in zip(work, pv):
            out[u] = out[u] + pv_i
        return tuple(out)

    acc = tuple(jnp.zeros((VT_ROWS, half_w), F32) for _ in units)
    if isinstance(lo, (list, tuple)):
        acc = step(lo, acc, bias_fns)
    elif last_bias_fns is None:
        acc = lax.fori_loop(lo, hi, lambda kb, a: step([kb], a, bias_fns), acc)
    else:
        acc = lax.fori_loop(lo, hi - 1, lambda kb, a: step([kb], a, bias_fns), acc)
        acc = step([hi - 1], acc, last_bias_fns)
    outs = []
    l_min = None
    for g in range(NSA_GROUPS):
        a = jnp.concatenate(acc[g * n_half:(g + 1) * n_half], axis=1)
        l_even = a[LANES:LANES + 1, :]
        l_odd = a[LANES + 1:LANES + 2, :]
        outs.append(a[0:LANES, :] * jnp.where(r_acc[0:LANES] < LANES // 2, 1.0 / l_even, 1.0 / l_odd))
        l_g = jnp.min(jnp.minimum(l_even, l_odd))
        l_min = l_g if l_min is None else jnp.minimum(l_min, l_g)
    return outs, l_min


def _nsa_attn_kernel(qc_ref, qr_ref, gate_ref, kcz_ref, vczt_ref, ksz_ref, vszt_ref, kwz_ref, vwzt_ref,
                     aggt_ref, eselt_ref, o_ref, bias_ref, kmax_ref,
                     *, t_len, kb_size):
    n_cmp_rows = t_len // CMP_STRIDE
    n_cmp = (t_len - CMP_LEN) // CMP_STRIDE + 1
    n_sel = t_len // SEL_LEN
    n_top = min(SEL_TOPK, n_sel)
    pairs = NSA_HEADS // NSA_GROUPS // 2
    q0 = pl.program_id(1) * Q_BLOCK
    t_row = q0 + lax.broadcasted_iota(jnp.int32, (1, Q_BLOCK), 1)
    t_row_stack = q0 + (lax.broadcasted_iota(jnp.int32, (1, pairs * Q_BLOCK), 1) & (Q_BLOCK - 1))
    blk = lax.broadcasted_iota(jnp.int32, (n_sel, 1), 0)
    c_idx = lax.broadcasted_iota(jnp.int32, (n_cmp_rows, 1), 0)
    cmp_mask = ((c_idx * CMP_STRIDE + (CMP_LEN - 1)) <= t_row_stack) & (c_idx < n_cmp)
    key_pos = lax.broadcasted_iota(jnp.int32, (t_len, 1), 0)
    gate_t = jax.nn.sigmoid(gate_ref[...]).T
    top_half = lax.broadcasted_iota(jnp.int32, (LANES, 1), 0) < (LANES // 2)
    hi = (q0 + (Q_BLOCK - 1)) // kb_size + 1
    win_lo = jnp.maximum(q0 - (WINDOW - 1), 0) // kb_size

    def win_bias(ks):
        kp = ks + lax.broadcasted_iota(jnp.int32, (kb_size, 1), 0)
        return jnp.where((kp <= t_row) & (kp > t_row - WINDOW), 0.0, NEG_BIG)

    def sel_bias(g):
        return lambda ks: bias_ref[g, pl.ds(ks, kb_size), :]

    @pl.when(pl.program_id(1) == 0)
    def _():
        for br, k_ref in enumerate((ksz_ref, kwz_ref)):
            for g in range(NSA_GROUPS):
                k = k_ref[:, 2 * g * LANES:(2 * g + 1) * LANES].astype(F32)
                n2 = jnp.max(jnp.sum(k * k, axis=1, keepdims=True), axis=0, keepdims=True)
                kmax_ref[br * NSA_GROUPS + g] = jnp.broadcast_to(jnp.sqrt(n2), (SUBLANES, LANES))

    qr_stacks, o_cmps, sels = [], [], []
    for g in range(NSA_GROUPS):
        lanes_g = g * pairs * LANES
        qc_stack = jnp.concatenate(
            [qc_ref[:, lanes_g + p * LANES:lanes_g + (p + 1) * LANES] for p in range(pairs)], axis=0)
        qr_stacks.append(jnp.concatenate(
            [qr_ref[:, lanes_g + p * LANES:lanes_g + (p + 1) * LANES] for p in range(pairs)], axis=0))

        o_cmp = None
        p_sum = None
        for var in range(2):
            c0 = (2 * g + var) * LANES
            s = jnp.where(cmp_mask, _dot_nt(kcz_ref[:, c0:c0 + LANES], qc_stack), -jnp.inf)
            mx = jnp.max(s, axis=0, keepdims=True)
            mx = jnp.where(mx > -jnp.inf, mx, 0.0)
            e = jnp.exp2(s - mx)
            p = e * (1.0 / jnp.maximum(jnp.sum(e, axis=0, keepdims=True), 1e-30))
            d = _dot(vczt_ref[c0:c0 + LANES, :], p.astype(BF16))
            o_cmp = d if o_cmp is None else o_cmp + d
            for pr in range(pairs):
                part = p[:, pr * Q_BLOCK:(pr + 1) * Q_BLOCK]
                p_sum = part if p_sum is None else p_sum + part
        importance = _dot_sel_lhs(aggt_ref[...], p_sum)

        cur = t_row >> 6
        causal_blk = blk <= cur
        forced = (blk == 0) | (causal_blk & (blk > cur - SEL_LOCAL))
        score = jnp.where(forced, FORCE_SCORE, jnp.where(causal_blk, importance, -1.0))
        rank = jnp.zeros((n_sel, Q_BLOCK), F32)
        for jj in range(n_sel):
            row = score[jj:jj + 1, :]
            beats = (row > score) | ((row == score) & (blk > jj))
            rank = rank + jnp.where(beats, 1.0, 0.0)
        sel = jnp.where(rank < n_top, 1.0, 0.0)
        sels.append(jnp.concatenate([sel, jnp.zeros((LANES - n_sel, Q_BLOCK), F32)], axis=0))
        o_cmps.append(o_cmp)

    lane = lax.broadcasted_iota(jnp.int32, (1, LANES), 1)
    blk_pad = lax.broadcasted_iota(jnp.int32, (LANES, 1), 0)
    q_ext_sel, q_ext_win = [], []
    for g in range(NSA_GROUPS):
        x = qr_stacks[g].astype(F32)
        sq = x * x
        tot = jnp.sum(sq, axis=1, keepdims=True)
        even = jnp.sum(jnp.where(lane < LANES // 2, sq, 0.0), axis=1, keepdims=True)
        q_norm = jnp.sqrt(jnp.maximum(even, tot - even))
        sel_c = jnp.where(blk_pad * SEL_LEN <= t_row, sels[g], 0.0).T
        sel_b = jnp.where(lane < n_sel, (sel_c - 1.0) * (-NEG_BIG), 0.0)
        sel_b = jnp.concatenate([sel_b] * pairs, axis=0)
        ext_s = jnp.where(lane == n_sel, -q_norm * kmax_ref[g][0:1, 0:1], sel_b)
        ext_w = jnp.where(lane == n_sel, -q_norm * kmax_ref[NSA_GROUPS + g][0:1, 0:1], 0.0)
        q_ext_sel.append(jnp.concatenate([qr_stacks[g], ext_s.astype(BF16)], axis=1))
        q_ext_win.append(jnp.concatenate([qr_stacks[g], ext_w.astype(BF16)], axis=1))

    sel_kb = min(2 * kb_size, t_len)

    def diag_bias(kb, ks):
        kp = ks + lax.broadcasted_iota(jnp.int32, (sel_kb, 1), 0)
        return jnp.where(kp <= t_row, 0.0, NEG_BIG)

    n_win = WINDOW // kb_size + 1
    win_blocks = [hi - n_win + j for j in range(n_win)]

    def win_bias_fast(kb, ks):
        return jnp.where(kb >= 0, win_bias(ks), NEG_BIG)

    fast_sel, l_sel = _flash_branch_bounded(q_ext_sel, ksz_ref, eselt_ref, vszt_ref, 0,
                                            (q0 + (Q_BLOCK - 1)) // sel_kb + 1, None,
                                            [diag_bias] * NSA_GROUPS, sel_kb)
    fast_win, l_win = _flash_branch_bounded(q_ext_win, kwz_ref, eselt_ref, vwzt_ref, win_blocks, None,
                                            [win_bias_fast] * NSA_GROUPS, None, kb_size)

    def robust():
        for g in range(NSA_GROUPS):
            sel_keys = _dot(eselt_ref[...], sels[g].astype(BF16))
            bias_ref[g] = jnp.where((sel_keys > 0.5) & (key_pos <= t_row), 0.0, NEG_BIG)
        o_s = _flash_branch(qr_stacks, ksz_ref, vszt_ref, 0, hi,
                            [sel_bias(g) for g in range(NSA_GROUPS)], kb_size)
        o_w = _flash_branch(qr_stacks, kwz_ref, vwzt_ref, win_lo, hi, [win_bias] * NSA_GROUPS, kb_size)
        return tuple(o_s) + tuple(o_w)

    o_all = lax.cond(jnp.minimum(l_sel, l_win) > DENOM_FLOOR,
                     lambda: tuple(fast_sel) + tuple(fast_win), robust)
    o_sels, o_wins = o_all[:NSA_GROUPS], o_all[NSA_GROUPS:]

    for g in range(NSA_GROUPS):
        lanes_g = g * pairs * LANES
        for pr in range(pairs):
            h_even = (g * pairs + pr) * 2
            cols = slice(pr * Q_BLOCK, (pr + 1) * Q_BLOCK)
            mixed = None
            for j, o_branch in enumerate((o_cmps[g], o_sels[g], o_wins[g])):
                r_e = h_even * 3 + j
                r_o = r_e + 3
                gate_rows = jnp.where(top_half, gate_t[r_e:r_e + 1, :], gate_t[r_o:r_o + 1, :])
                term = gate_rows * o_branch[:, cols]
                mixed = term if mixed is None else mixed + term
            c0 = lanes_g + pr * LANES
            o_ref[:, c0:c0 + LANES] = mixed.T.astype(o_ref.dtype)


def _nsa_attn(qc, qr, proj, kcz, vczt, ksz, vszt, kwz, vwzt, aggt, eselt, bsz, t_len, *, kb_size):
    nq = t_len // Q_BLOCK
    qw = NSA_HEADS * NSA_DK
    zw = 4 * LANES
    nr = t_len // CMP_STRIDE
    pairs = NSA_HEADS // NSA_GROUPS // 2
    row = lambda b, i: (b * nq + i, 0)
    per_b = lambda b, i: (b, 0, 0)
    return pl.pallas_call(
        functools.partial(_nsa_attn_kernel, t_len=t_len, kb_size=kb_size),
        grid=(bsz, nq),
        in_specs=[
            pl.BlockSpec((Q_BLOCK, qw), row),
            pl.BlockSpec((Q_BLOCK, qw), row),
            pl.BlockSpec((Q_BLOCK, LANES), lambda b, i: (b * nq + i, NSA_GATE_OFF // LANES)),
            pl.BlockSpec((None, nr, zw), per_b),
            pl.BlockSpec((None, zw, nr), per_b),
            pl.BlockSpec((None, t_len, zw), per_b),
            pl.BlockSpec((None, 4 * VT_ROWS, t_len), per_b),
            pl.BlockSpec((None, t_len, zw), per_b),
            pl.BlockSpec((None, 4 * VT_ROWS, t_len), per_b),
            pl.BlockSpec(aggt.shape, lambda b, i: (0, 0)),
            pl.BlockSpec(eselt.shape, lambda b, i: (0, 0)),
        ],
        out_specs=pl.BlockSpec((Q_BLOCK, qw), row),
        out_shape=jax.ShapeDtypeStruct((bsz * t_len, qw), BF16),
        scratch_shapes=[
            pltpu.VMEM((NSA_GROUPS, t_len, Q_BLOCK), F32),
            pltpu.VMEM((2 * NSA_GROUPS, SUBLANES, LANES), F32),
        ],
        compiler_params=_cparams("parallel", "arbitrary"),
        name="nsa_attn",
    )(qc, qr, proj, kcz, vczt, ksz, vszt, kwz, vwzt, aggt, eselt)


def _expand_rows(first_row, n_heads, width):
    e = np.zeros((LANES, n_heads * width), np.float32)
    for h in range(n_heads):
        e[first_row + h, h * width:(h + 1) * width] = 1.0
    return e


def _nsa_constants(t_len):
    n_cmp = (t_len - CMP_LEN) // CMP_STRIDE + 1
    n_sel = t_len // SEL_LEN
    nr = t_len // CMP_STRIDE
    c0 = np.arange(n_cmp)[:, None] * CMP_STRIDE
    s0 = np.arange(n_sel)[None, :] * SEL_LEN
    overlap = np.clip(np.minimum(c0 + CMP_LEN, s0 + SEL_LEN) - np.maximum(c0, s0), 0, None) / CMP_LEN
    aggt = np.zeros((n_sel, nr), np.float32)
    aggt[:, :n_cmp] = overlap.T
    eselt = np.zeros((t_len, LANES), np.float32)
    eselt[np.arange(t_len), np.arange(t_len) // SEL_LEN] = 1.0
    eselt[:, n_sel] = 1.0
    half = ROPE_DIM // 2
    inv_freq = (ROPE_THETA ** (-np.arange(half) / half)).astype(np.float32)
    d = np.arange(LANES) % NSA_DK
    rope_sel = np.zeros((LANES, 3 * LANES), np.float32)
    for lane in range(LANES):
        if d[lane] < ROPE_DIM:
            f = d[lane] % half
            rope_sel[f, lane] = 1.0
            if d[lane] >= half:
                rope_sel[half + f, LANES + lane] = 1.0
            else:
                rope_sel[half + f, 2 * LANES + lane] = -1.0
    one_row = (d >= ROPE_DIM).astype(np.float32)[None]
    return aggt, eselt, inv_freq[:, None], rope_sel, one_row


def _pad_row(vec, offset):
    return jnp.zeros((1, LANES), F32).at[0, offset:offset + vec.shape[0]].set(vec.astype(F32))


def _hybrid_layer(h, bsz, t_len, w_in, gdn_conv_w, gdn_a_log, gdn_dt_bias, gdn_norm_w, ssd_conv_w,
                  ssd_conv_b, ssd_a_log, ssd_dt_bias, ssd_d, ssd_norm_w, w_out, ln_g, ln_b, cfg):
    b_off = GDN_CONV_DIM + GDN_V_W
    z_off = b_off + 2 * GDN_V_HEADS
    dt_off = z_off + SSD_D_INNER + SSD_CONV_DIM
    w_pad = jnp.concatenate(
        [w_in[:, :b_off], w_in[:, z_off:dt_off], w_in[:, b_off:z_off], w_in[:, dt_off:],
         jnp.zeros((D_MODEL, HYB_PAD - w_in.shape[1]), w_in.dtype)], axis=1).astype(BF16)
    proj = _proj(h, w_pad, tm=cfg["proj_tm"], tn=cfg["hyb_tn"])
    o_a = _gdn(proj, bsz, t_len, gdn_conv_w, _pad_row(gdn_dt_bias, GDN_V_HEADS),
               _pad_row(gdn_a_log, GDN_V_HEADS), gdn_norm_w[None].astype(F32),
               jnp.asarray(_expand_rows(0, GDN_V_HEADS, GDN_DV)).astype(BF16),
               jnp.asarray(_expand_rows(GDN_V_HEADS, GDN_V_HEADS, GDN_DV)).astype(BF16), tc=cfg["gdn_tc"])
    o_b = _ssd(proj, bsz, t_len, ssd_conv_w, ssd_conv_b[None], _pad_row(ssd_dt_bias, 16),
               _pad_row(ssd_a_log, 16), jnp.repeat(ssd_d, SSD_HEADDIM)[None], ssd_norm_w[None],
               jnp.asarray(_expand_rows(16, SSD_HEADS, SSD_HEADDIM)).astype(BF16))
    w_out_b = w_out.astype(BF16)
    return _out_ln([o_a, o_b], [w_out_b[:GDN_V_W], w_out_b[GDN_V_W:]], h, ln_g, ln_b, tm=cfg["out_tm"])


def _nsa_layer(h, pos_row, bsz, t_len, w_in, cmp_pos, cmp_w1, cmp_w2, w_out, ln_g, ln_b, cfg):
    aggt, eselt, freq_col, rope_sel, one_row = _nsa_constants(t_len)
    w_pad = jnp.concatenate(
        [w_in, jnp.zeros((D_MODEL, NSA_PAD - w_in.shape[1]), w_in.dtype)], axis=1).astype(BF16)
    proj = _proj(h, w_pad, tm=cfg["proj_tm"], tn=cfg["nsa_tn"])
    qc, qr, ksz, vszt, kwz, vwzt = _nsa_prep(proj, pos_row, jnp.asarray(freq_col),
                                             jnp.asarray(rope_sel).astype(BF16), jnp.asarray(one_row),
                                             bsz, t_len, tm=cfg["prep_tm"])
    qw = NSA_HEADS * NSA_DK
    nr = t_len // CMP_STRIDE
    width = CMP_STRIDE * LANES
    k16 = proj[:, qw:qw + LANES].reshape(bsz, nr, width)
    v16 = proj[:, qw + LANES:qw + 2 * LANES].reshape(bsz, nr, width)
    pos2 = jnp.broadcast_to(cmp_pos[:, :, None, :], (2, CMP_LEN, NSA_GROUPS, NSA_DK)).reshape(2, 2, width)
    w1 = cmp_w1.reshape(2, 2, CMP_STRIDE, NSA_DK, CMP_HIDDEN)
    eye_g = jnp.eye(NSA_GROUPS, dtype=w1.dtype)
    w1x = jnp.einsum("ksjdh,ge->ksjgdeh", w1, eye_g).reshape(
        2, 2, width, NSA_GROUPS * CMP_HIDDEN).astype(BF16)
    w2x = jnp.einsum("khd,ge->kghed", cmp_w2, eye_g).reshape(
        2, NSA_GROUPS * CMP_HIDDEN, NSA_GROUPS * NSA_DK).astype(BF16)
    kcz, vczt = _nsa_compress(k16, v16, pos2, w1x[:, 0], w1x[:, 1], w2x)
    o = _nsa_attn(qc, qr, proj, kcz, vczt, ksz.reshape(bsz, t_len, -1), vszt,
                  kwz.reshape(bsz, t_len, -1), vwzt,
                  jnp.asarray(aggt).astype(BF16), jnp.asarray(eselt).astype(BF16),
                  bsz, t_len, kb_size=cfg["attn_kb"])
    return _out_ln([o], [w_out.astype(BF16)], h, ln_g, ln_b, tm=cfg["out_tm"])


def _config(m, t_len):
    return dict(ffn_tm=min(512, m), ffn_tf=256, proj_tm=min(512, m), hyb_tn=HYB_PAD, nsa_tn=NSA_PAD,
                out_tm=min(512, m), gdn_tc=min(256, t_len), prep_tm=min(256, m),
                attn_kb=min(256, t_len))


def kernel(x, positions, ln_g, ln_b, ffn_w_in, ffn_w_out, hyb_w_in, gdn_conv_w, gdn_a_log, gdn_dt_bias, gdn_norm_w, ssd_conv_w, ssd_conv_b, ssd_a_log, ssd_dt_bias, ssd_d, ssd_norm_w, hyb_w_out, nsa_w_in, nsa_cmp_pos, nsa_cmp_w1, nsa_cmp_w2, nsa_w_out):
    bsz, t_len, _ = x.shape
    m = bsz * t_len
    cfg = _config(m, t_len)
    h = x.reshape(m, D_MODEL)
    pos_row = positions.reshape(1, m)
    w_in_b = ffn_w_in.astype(BF16)
    w_out_b = ffn_w_out.astype(BF16)
    ln_g = ln_g[:, :, None, :]
    ln_b = ln_b[:, :, None, :]
    for layer in range(DEPTH):
        i = layer // 2
        h = _ffn_ln(h, w_in_b, w_out_b, ln_g[layer, 0], ln_b[layer, 0], layer, 0,
                    tm=cfg["ffn_tm"], tf=cfg["ffn_tf"])
        if layer % 2 == 0:
            h = _hybrid_layer(h, bsz, t_len, hyb_w_in[i], gdn_conv_w[i], gdn_a_log[i], gdn_dt_bias[i],
                              gdn_norm_w[i], ssd_conv_w[i], ssd_conv_b[i], ssd_a_log[i], ssd_dt_bias[i],
                              ssd_d[i], ssd_norm_w[i], hyb_w_out[i], ln_g[layer, 1], ln_b[layer, 1], cfg)
        else:
            h = _nsa_layer(h, pos_row, bsz, t_len, nsa_w_in[i], nsa_cmp_pos[i], nsa_cmp_w1[i],
                           nsa_cmp_w2[i], nsa_w_out[i], ln_g[layer, 1], ln_b[layer, 1], cfg)
        h = _ffn_ln(h, w_in_b, w_out_b, ln_g[layer, 2], ln_b[layer, 2], layer, 1,
                    tm=cfg["ffn_tm"], tf=cfg["ffn_tf"])
    return h.reshape(bsz, t_len, D_MODEL)
```

```python
import functools

import numpy as np
import jax
import jax.numpy as jnp
from jax import lax
from jax.experimental import pallas as pl
from jax.experimental.pallas import tpu as pltpu

F32 = jnp.float32
BF16 = jnp.bfloat16

D_MODEL = 1024
DEPTH = 4
D_FF = 2816
DEEPNORM_ALPHA = (2.0 * DEPTH) ** 0.25
LN_EPS = 1e-5
RMS_EPS = 1e-6
CONV_K = 4

GDN_QK_HEADS = 4
GDN_V_HEADS = 8
GDN_DK = 128
GDN_DV = 128
GDN_CHUNK = 64
GDN_QK_W = GDN_QK_HEADS * GDN_DK
GDN_V_W = GDN_V_HEADS * GDN_DV
GDN_CONV_DIM = 2 * GDN_QK_W + GDN_V_W

SSD_D_INNER = D_MODEL
SSD_HEADDIM = 64
SSD_HEADS = SSD_D_INNER // SSD_HEADDIM
SSD_GROUPS = 2
SSD_STATE = 128
SSD_CHUNK = 128
SSD_BC_W = SSD_GROUPS * SSD_STATE
SSD_CONV_DIM = SSD_D_INNER + 2 * SSD_BC_W

NSA_HEADS = 16
NSA_GROUPS = 2
NSA_DK = 64
NSA_DV = 64
ROPE_DIM = NSA_DK // 4
ROPE_THETA = 500000.0
CMP_LEN = 32
CMP_STRIDE = 16
CMP_HIDDEN = 256
SEL_LEN = 64
SEL_TOPK = 8
SEL_LOCAL = 2
FORCE_SCORE = 1e4
WINDOW = 512
Q_BLOCK = 128

LANES = 128
SUBLANES = 8
VMEM_LIMIT_BYTES = 48 * 1024 * 1024

HYB_SMALL_OFF = GDN_CONV_DIM + GDN_V_W + SSD_D_INNER + SSD_CONV_DIM
HYB_PAD = HYB_SMALL_OFF + LANES
NSA_GATE_OFF = NSA_HEADS * NSA_DK + 3 * NSA_GROUPS * (NSA_DK + NSA_DV)
NSA_PAD = NSA_GATE_OFF + LANES

NEG_BIG = -1e30
LOG2E = 1.4426950408889634
DENOM_FLOOR = 2.0 ** -100
VT_ROWS = LANES + 16


def _cparams(*sem):
    return pltpu.CompilerParams(dimension_semantics=sem, vmem_limit_bytes=VMEM_LIMIT_BYTES)


def _sigmoid(v):
    return 0.5 + 0.5 * jnp.tanh(0.5 * v)


def _silu(v):
    h = 0.5 * v
    return h + h * jnp.tanh(h)


def _softplus(v):
    return jnp.maximum(v, 0.0) + jnp.log1p(jnp.exp(-jnp.abs(v)))


def _layer_norm(y, g, b):
    mu = jnp.mean(y, axis=-1, keepdims=True)
    d = y - mu
    var = jnp.mean(d * d, axis=-1, keepdims=True)
    return d * lax.rsqrt(var + LN_EPS) * g + b


def _dot(a, b):
    return jnp.dot(a, b, preferred_element_type=F32)


def _split3(x):
    hi = x.astype(BF16)
    r1 = x - hi.astype(F32)
    mid = r1.astype(BF16)
    lo = (r1 - mid.astype(F32)).astype(BF16)
    return hi, mid, lo


def _dot_sel_rhs(x, sel):
    sel = sel.astype(BF16)
    hi, mid, lo = _split3(x)
    return _dot(hi, sel) + _dot(mid, sel) + _dot(lo, sel)


def _dot_sel_lhs(sel, x):
    sel = sel.astype(BF16)
    hi, mid, lo = _split3(x)
    return _dot(sel, hi) + _dot(sel, mid) + _dot(sel, lo)


def _dot_nt(a, b):
    return lax.dot_general(a, b, (((1,), (1,)), ((), ())), preferred_element_type=F32)


def _ffn_ln_kernel(x_ref, wi_ref, wo_ref, g_ref, b_ref, o_ref, *, tf):
    x = x_ref[...]
    xb = x.astype(BF16)
    acts = []
    for c in range(D_FF // tf):
        gate = _dot(xb, wi_ref[:, c * tf:(c + 1) * tf])
        up = _dot(xb, wi_ref[:, D_FF + c * tf:D_FF + (c + 1) * tf])
        acts.append((_silu(gate) * up).astype(BF16))
    mix = _dot(jnp.concatenate(acts, axis=1), wo_ref[...])
    y = DEEPNORM_ALPHA * x + 0.5 * mix
    o_ref[...] = _layer_norm(y, g_ref[...], b_ref[...])


def _resident(shape):
    return pl.BlockSpec(shape, lambda *_: (0,) * len(shape), pipeline_mode=pl.Buffered(1))


def _ffn_ln(h, w_in, w_out, g, b, layer, which, *, tm, tf):
    m = h.shape[0]
    pick = lambda i: (layer, which, 0, 0)
    return pl.pallas_call(
        functools.partial(_ffn_ln_kernel, tf=tf),
        grid=(m // tm,),
        in_specs=[
            pl.BlockSpec((tm, D_MODEL), lambda i: (i, 0)),
            pl.BlockSpec((None, None) + w_in.shape[2:], pick, pipeline_mode=pl.Buffered(1)),
            pl.BlockSpec((None, None) + w_out.shape[2:], pick, pipeline_mode=pl.Buffered(1)),
            _resident((1, D_MODEL)),
            _resident((1, D_MODEL)),
        ],
        out_specs=pl.BlockSpec((tm, D_MODEL), lambda i: (i, 0)),
        out_shape=jax.ShapeDtypeStruct((m, D_MODEL), F32),
        compiler_params=_cparams("parallel"),
        name="ffn_ln",
    )(h, w_in, w_out, g, b)


def _proj_kernel(x_ref, w_ref, o_ref, *, tn):
    xb = x_ref[...].astype(BF16)
    for c in range(w_ref.shape[1] // tn):
        o_ref[:, c * tn:(c + 1) * tn] = _dot(xb, w_ref[:, c * tn:(c + 1) * tn])


def _proj(h, w, *, tm, tn):
    m, k = h.shape
    n = w.shape[1]
    return pl.pallas_call(
        functools.partial(_proj_kernel, tn=tn),
        grid=(m // tm,),
        in_specs=[
            pl.BlockSpec((tm, k), lambda i: (i, 0)),
            _resident(w.shape),
        ],
        out_specs=pl.BlockSpec((tm, n), lambda i: (i, 0)),
        out_shape=jax.ShapeDtypeStruct((m, n), F32),
        compiler_params=_cparams("parallel"),
        name="in_proj",
    )(h, w)


def _proj_conv_kernel(x_ref, w_ref, cw_ref, cb_ref, o_ref, cbuf, *, chunks, tiles_per_seq):
    tm = x_ref.shape[0]

    @pl.when(pl.program_id(0) % tiles_per_seq == 0)
    def _():
        cbuf[0:SUBLANES, :] = jnp.zeros((SUBLANES, cbuf.shape[1]), F32)

    xb = x_ref[...].astype(BF16)

    def finish(chunk, y):
        col, width, ccol = chunk
        if ccol is not None:
            cbuf[SUBLANES:SUBLANES + tm, ccol:ccol + width] = y
            acc = cb_ref[:, ccol:ccol + width]
            for i in range(CONV_K):
                acc = acc + (cw_ref[i:i + 1, ccol:ccol + width]
                             * cbuf[pl.ds(SUBLANES - CONV_K + 1 + i, tm), ccol:ccol + width])
            cbuf[0:SUBLANES, ccol:ccol + width] = cbuf[tm:tm + SUBLANES, ccol:ccol + width]
            y = _silu(acc)
        o_ref[:, col:col + width] = y

    pending = None
    for chunk in chunks:
        y = _dot(xb, w_ref[:, chunk[0]:chunk[0] + chunk[1]])
        if pending is not None:
            finish(*pending)
        pending = (chunk, y)
    finish(*pending)


def _proj_conv(h, w, cw, cb, t_len, *, tm, tn):
    m, k = h.shape
    n = w.shape[1]
    conv_w = cw.shape[1]
    ssd_lo = GDN_CONV_DIM + GDN_V_W + SSD_D_INNER
    chunks = []
    for col in range(0, n, tn):
        width = min(tn, n - col)
        if col + width <= GDN_CONV_DIM:
            ccol = col
        elif ssd_lo <= col and col + width <= ssd_lo + SSD_CONV_DIM:
            ccol = col - ssd_lo + GDN_CONV_DIM
        else:
            ccol = None
        chunks.append((col, width, ccol))
    return pl.pallas_call(
        functools.partial(_proj_conv_kernel, chunks=tuple(chunks), tiles_per_seq=t_len // tm),
        grid=(m // tm,),
        in_specs=[
            pl.BlockSpec((tm, k), lambda i: (i, 0)),
            _resident(w.shape),
            _resident(cw.shape),
            _resident(cb.shape),
        ],
        out_specs=pl.BlockSpec((tm, n), lambda i: (i, 0)),
        out_shape=jax.ShapeDtypeStruct((m, n), F32),
        scratch_shapes=[pltpu.VMEM((tm + SUBLANES, conv_w), F32)],
        compiler_params=_cparams("arbitrary"),
        name="in_proj_conv",
    )(h, w, cw, cb)


def _out_ln_kernel(*refs, n_parts):
    a_refs = refs[:n_parts]
    w_refs = refs[n_parts:2 * n_parts]
    h_ref, g_ref, b_ref, o_ref = refs[2 * n_parts:]
    mix = _dot(a_refs[0][...], w_refs[0][...])
    for a_ref, w_ref in zip(a_refs[1:], w_refs[1:]):
        mix = mix + _dot(a_ref[...], w_ref[...])
    y = DEEPNORM_ALPHA * h_ref[...] + mix
    o_ref[...] = _layer_norm(y, g_ref[...], b_ref[...])


def _out_ln(parts, weights, h, g, b, *, tm):
    m = h.shape[0]
    n_parts = len(parts)
    in_specs = [pl.BlockSpec((tm, a.shape[1]), lambda i: (i, 0)) for a in parts]
    in_specs += [pl.BlockSpec(w.shape, lambda i: (0, 0)) for w in weights]
    in_specs += [
        pl.BlockSpec((tm, D_MODEL), lambda i: (i, 0)),
        pl.BlockSpec((1, D_MODEL), lambda i: (0, 0)),
        pl.BlockSpec((1, D_MODEL), lambda i: (0, 0)),
    ]
    return pl.pallas_call(
        functools.partial(_out_ln_kernel, n_parts=n_parts),
        grid=(m // tm,),
        in_specs=in_specs,
        out_specs=pl.BlockSpec((tm, D_MODEL), lambda i: (i, 0)),
        out_shape=jax.ShapeDtypeStruct((m, D_MODEL), F32),
        compiler_params=_cparams("parallel"),
        name="out_proj_ln",
    )(*parts, *weights, h, g, b)


def _ssd_kernel(z_ref, x_ref, bm_ref, cm_ref, sm_ref, dtb_ref, alog_ref,
                dskip_ref, nw_ref, eh_ref, o_ref, state):
    L = SSD_CHUNK
    W = SSD_D_INNER
    GW = W // SSD_GROUPS
    N = SSD_STATE

    @pl.when(pl.program_id(1) == 0)
    def _():
        state[...] = jnp.zeros_like(state)

    xs = x_ref[...]
    bm = bm_ref[...].astype(BF16)
    cm = cm_ref[...].astype(BF16)

    lane = lax.broadcasted_iota(jnp.int32, (1, LANES), 1)
    dt_lanes = (lane >= 16) & (lane < 16 + SSD_HEADS)
    dt_full = jnp.where(dt_lanes, _softplus(sm_ref[...] + dtb_ref[...]), 0.0)
    adt_full = dt_full * (-jnp.exp(alog_ref[...]))
    r_i = lax.broadcasted_iota(jnp.int32, (L, L), 0)
    c_i = lax.broadcasted_iota(jnp.int32, (L, L), 1)
    tril = r_i >= c_i
    acs_full = _dot_sel_lhs(jnp.where(tril, 1.0, 0.0), adt_full)
    acs_t = acs_full.T
    eh = eh_ref[...]
    dt_exp = _dot_sel_rhs(dt_full, eh)
    acs_exp = _dot_sel_rhs(acs_full, eh)
    a_last = acs_exp[L - 1:L, :]
    xdt = xs * dt_exp
    xdec = (xdt * jnp.exp(a_last - acs_exp)).astype(BF16)
    e_acs = jnp.exp(acs_exp)
    e_last = jnp.exp(a_last)
    lane_w = lax.broadcasted_iota(jnp.int32, (1, W), 1)
    lo_half = (lane_w & (LANES - 1)) < SSD_HEADDIM
    xdt_b = xdt.astype(BF16)
    xdt_lo = jnp.where(lo_half, xdt_b, jnp.zeros_like(xdt_b))
    xdt_hi = jnp.where(lo_half, jnp.zeros_like(xdt_b), xdt_b)

    heads_per_group = SSD_HEADS // SSD_GROUPS
    y_parts = []
    for g in range(SSD_GROUPS):
        bg = bm[:, g * N:(g + 1) * N]
        cg = cm[:, g * N:(g + 1) * N]
        cb = _dot_nt(cg, bg)
        s_g = state[g]
        y_off = _dot(cg, s_g.astype(BF16)) * e_acs[:, g * GW:(g + 1) * GW]
        for pr in range(heads_per_group // 2):
            y_pair = None
            for var in range(2):
                h = g * heads_per_group + 2 * pr + var
                col = acs_full[:, 16 + h:17 + h]
                row = acs_t[16 + h:17 + h, :]
                seg = jnp.exp(jnp.where(tril, col - row, -jnp.inf))
                mh = (cb * seg).astype(BF16)
                src = xdt_lo if var == 0 else xdt_hi
                lanes0 = (g * heads_per_group + 2 * pr) * SSD_HEADDIM
                d = _dot(mh, src[:, lanes0:lanes0 + LANES])
                y_pair = d if y_pair is None else y_pair + d
            off = pr * LANES
            y_parts.append(y_pair + y_off[:, off:off + LANES])
        upd = lax.dot_general(bg, xdec[:, g * GW:(g + 1) * GW], (((0,), (0,)), ((), ())),
                              preferred_element_type=F32)
        state[g] = s_g * e_last[:, g * GW:(g + 1) * GW] + upd
    y = jnp.concatenate(y_parts, axis=1) + xs * dskip_ref[...]
    y = y * _silu(z_ref[...])
    outs = []
    for g in range(SSD_GROUPS):
        yg = y[:, g * GW:(g + 1) * GW]
        outs.append(yg * lax.rsqrt(jnp.mean(yg * yg, axis=-1, keepdims=True) + RMS_EPS))
    o_ref[...] = (jnp.concatenate(outs, axis=1) * nw_ref[...]).astype(o_ref.dtype)


def _ssd(proj, bsz, t_len, dtb_row, alog_row, dskip_row, nw_row, eh):
    L = SSD_CHUNK
    nc = t_len // L
    W = SSD_D_INNER
    z_blk = (GDN_CONV_DIM + GDN_V_W) // W
    x_blk = (GDN_CONV_DIM + GDN_V_W + W) // W
    b_blk = (GDN_CONV_DIM + GDN_V_W + 2 * W) // SSD_BC_W
    sm_blk = HYB_SMALL_OFF // LANES
    row = lambda b, c: b * nc + c
    const = lambda b, c: (0, 0)
    return pl.pallas_call(
        _ssd_kernel,
        grid=(bsz, nc),
        in_specs=[
            pl.BlockSpec((L, W), lambda b, c: (row(b, c), z_blk)),
            pl.BlockSpec((L, W), lambda b, c: (row(b, c), x_blk)),
            pl.BlockSpec((L, SSD_BC_W), lambda b, c: (row(b, c), b_blk)),
            pl.BlockSpec((L, SSD_BC_W), lambda b, c: (row(b, c), b_blk + 1)),
            pl.BlockSpec((L, LANES), lambda b, c: (row(b, c), sm_blk)),
            pl.BlockSpec((1, LANES), const),
            pl.BlockSpec((1, LANES), const),
            pl.BlockSpec((1, W), const),
            pl.BlockSpec((1, W), const),
            pl.BlockSpec((LANES, W), const),
        ],
        out_specs=pl.BlockSpec((L, W), lambda b, c: (row(b, c), 0)),
        out_shape=jax.ShapeDtypeStruct((bsz * t_len, W), BF16),
        scratch_shapes=[
            pltpu.VMEM((SSD_GROUPS, SSD_STATE, W // SSD_GROUPS), F32),
        ],
        compiler_params=_cparams("parallel", "arbitrary"),
        name="ssd",
    )(proj, proj, proj, proj, proj, dtb_row, alog_row, dskip_row, nw_row, eh)


def _unit_lower_inverse_minus_eye(a, r_i, c_i):
    n = range(len(a))
    diag8 = (r_i >> 3) == (c_i >> 3)
    x = [jnp.where(diag8, -a[i], 0.0) for i in n]
    x_b = [x[i].astype(BF16) for i in n]
    x2 = [_dot(x_b[i], x_b[i]) for i in n]
    x2_b = [x2[i].astype(BF16) for i in n]
    x4 = [_dot(x2_b[i], x2_b[i]) for i in n]
    e = [x[i] + x2[i] + _dot(x_b[i], x2_b[i]) for i in n]
    e = [e[i] + x4[i] + _dot(e[i].astype(BF16), x4[i].astype(BF16)) for i in n]
    for sh in (3, 4, 5):
        same_pair = (r_i >> (sh + 1)) == (c_i >> (sh + 1))
        lower_left = (((r_i >> sh) & 1) == 1) & (((c_i >> sh) & 1) == 0)
        ms = [jnp.where(same_pair & lower_left, a[i], 0.0) for i in n]
        y = [ms[i] + _dot(ms[i].astype(BF16), e[i].astype(BF16)) for i in n]
        e = [e[i] - y[i] - _dot(e[i].astype(BF16), y[i].astype(BF16)) for i in n]
    return e


def _stack_heads(x, r0, l0):
    return jnp.concatenate([x[r0:r0 + GDN_CHUNK, l0:l0 + GDN_DV],
                            x[r0:r0 + GDN_CHUNK, l0 + GDN_DV:l0 + 2 * GDN_DV]], axis=0)


def _own_head_block(x):
    return jnp.concatenate([x[0:GDN_CHUNK, 0:GDN_DV], x[GDN_CHUNK:, GDN_DV:]], axis=0)


def _gdn_kernel(q_ref, k_ref, v_ref, z_ref, sm_ref, dtb_ref, alog_ref, nw_ref,
                eb_ref, ea_ref, o_ref, state, *, tc):
    C = GDN_CHUNK
    DK = GDN_DK
    DV = GDN_DV
    P = 2 * C

    @pl.when(pl.program_id(1) == 0)
    def _():
        state[...] = jnp.zeros_like(state)

    lane = lax.broadcasted_iota(jnp.int32, (1, LANES), 1)
    sm = sm_ref[...]
    beta_full = jnp.where(lane < GDN_V_HEADS, _sigmoid(sm), 0.0)
    g_lanes = (lane >= GDN_V_HEADS) & (lane < 2 * GDN_V_HEADS)
    g_full = jnp.where(g_lanes, -jnp.exp(alog_ref[...]) * _softplus(sm + dtb_ref[...]), 0.0)
    rt = lax.broadcasted_iota(jnp.int32, (tc, tc), 0)
    ct = lax.broadcasted_iota(jnp.int32, (tc, tc), 1)
    same_chunk_tril = ((rt >> 6) == (ct >> 6)) & (rt >= ct)
    gc_full = _dot_sel_lhs(jnp.where(same_chunk_tril, 1.0, 0.0), g_full)
    b_exp = _dot_sel_rhs(beta_full, eb_ref[...])
    g_exp = _dot_sel_rhs(gc_full, ea_ref[...])
    eg_exp = jnp.exp(g_exp)

    r_i = lax.broadcasted_iota(jnp.int32, (P, P), 0)
    c_i = lax.broadcasted_iota(jnp.int32, (P, P), 1)
    same_head = (r_i >> 6) == (c_i >> 6)
    tril = same_head & (r_i >= c_i)
    strict = same_head & (r_i > c_i)
    head0_rows = lax.broadcasted_iota(jnp.int32, (P, 1), 0) < C
    nw = nw_ref[...]

    n_chunks = tc // C
    units = [(hq, ci) for ci in range(n_chunks) for hq in range(GDN_QK_HEADS)]
    qn, kn = [], []
    for hq in range(GDN_QK_HEADS):
        qh = q_ref[:, hq * DK:(hq + 1) * DK]
        kh = k_ref[:, hq * DK:(hq + 1) * DK]
        qn.append(qh * lax.rsqrt(jnp.sum(qh * qh, axis=-1, keepdims=True) + 1e-6) * (DK ** -0.5))
        kn.append(kh * lax.rsqrt(jnp.sum(kh * kh, axis=-1, keepdims=True) + 1e-6))
    pre = {}
    a_low = []
    for hq, ci in units:
        r0 = ci * C
        l0 = hq * 2 * DV
        q_c = qn[hq][r0:r0 + C]
        k_c = kn[hq][r0:r0 + C]
        q2 = jnp.concatenate([q_c, q_c], axis=0)
        k2 = jnp.concatenate([k_c, k_c], axis=0)
        k2_b = k2.astype(BF16)
        kk = _dot_nt(k2_b, k2_b)
        qk = _dot_nt(q2.astype(BF16), k2_b)
        gcol = _stack_heads(g_exp, r0, l0)
        bcol = _stack_heads(b_exp, r0, l0)
        egc = _stack_heads(eg_exp, r0, l0)
        dmat = jnp.exp(jnp.where(tril, gcol - gcol.T, -jnp.inf))
        a_low.append(jnp.where(strict, kk * bcol * dmat, 0.0))
        g_last = jnp.where(head0_rows, gcol[C - 1:C], gcol[P - 1:P])
        pre[hq, ci] = dict(
            vb=_stack_heads(v_ref, r0, l0) * bcol,
            kb=k2 * (bcol * egc),
            qd=(q2 * egc).astype(BF16),
            attn=(qk * dmat).astype(BF16),
            k_t=k_c.T.astype(BF16),
            v_scale=jnp.exp(g_last - gcol),
            decay_cat=jnp.concatenate([egc[C - 1:C], egc[P - 1:P]], axis=1))
    e_all = _unit_lower_inverse_minus_eye(a_low, r_i, c_i)
    for (hq, ci), e in zip(units, e_all):
        d = pre[hq, ci]
        e_b = e.astype(BF16)
        d["u"] = d["vb"] + _dot(e_b, d["vb"].astype(BF16))
        d["w"] = (d["kb"] + _dot(e_b, d["kb"].astype(BF16))).astype(BF16)

    s_cat = [state[hq] for hq in range(GDN_QK_HEADS)]
    for ci in range(n_chunks):
        r0 = ci * C
        for hq in range(GDN_QK_HEADS):
            d = pre[hq, ci]
            l0 = hq * 2 * DV
            s_b = s_cat[hq].astype(BF16)
            v_new = d["u"] - _own_head_block(_dot(d["w"], s_b))
            o2 = _own_head_block(_dot(d["qd"], s_b)) + _dot(d["attn"], v_new.astype(BF16))
            v_dec = (v_new * d["v_scale"]).astype(BF16)
            v_dec_cat = jnp.concatenate([v_dec[0:C], v_dec[C:]], axis=1)
            s_cat[hq] = s_cat[hq] * d["decay_cat"] + _dot(d["k_t"], v_dec_cat)
            o_n = o2 * lax.rsqrt(jnp.mean(o2 * o2, axis=-1, keepdims=True) + RMS_EPS)
            res = (o_n * nw * _silu(_stack_heads(z_ref, r0, l0))).astype(o_ref.dtype)
            o_ref[r0:r0 + C, l0:l0 + DV] = res[0:C]
            o_ref[r0:r0 + C, l0 + DV:l0 + 2 * DV] = res[C:]
    for hq in range(GDN_QK_HEADS):
        state[hq] = s_cat[hq]


def _gdn(proj, bsz, t_len, dtb_row, alog_row, nw_row, eb, ea, *, tc):
    nt = t_len // tc
    row = lambda b, t: b * nt + t
    const = lambda b, t: (0, 0)
    sm_blk = HYB_SMALL_OFF // LANES
    return pl.pallas_call(
        functools.partial(_gdn_kernel, tc=tc),
        grid=(bsz, nt),
        in_specs=[
            pl.BlockSpec((tc, GDN_QK_W), lambda b, t: (row(b, t), 0)),
            pl.BlockSpec((tc, GDN_QK_W), lambda b, t: (row(b, t), 1)),
            pl.BlockSpec((tc, GDN_V_W), lambda b, t: (row(b, t), 1)),
            pl.BlockSpec((tc, GDN_V_W), lambda b, t: (row(b, t), 2)),
            pl.BlockSpec((tc, LANES), lambda b, t: (row(b, t), sm_blk)),
            pl.BlockSpec((1, LANES), const),
            pl.BlockSpec((1, LANES), const),
            pl.BlockSpec((1, GDN_DV), const),
            pl.BlockSpec((LANES, GDN_V_W), const),
            pl.BlockSpec((LANES, GDN_V_W), const),
        ],
        out_specs=pl.BlockSpec((tc, GDN_V_W), lambda b, t: (row(b, t), 0)),
        out_shape=jax.ShapeDtypeStruct((bsz * t_len, GDN_V_W), BF16),
        scratch_shapes=[
            pltpu.VMEM((GDN_QK_HEADS, GDN_DK, 2 * GDN_DV), F32),
        ],
        compiler_params=_cparams("parallel", "arbitrary"),
        name="gdn",
    )(proj, proj, proj, proj, proj, dtb_row, alog_row, nw_row, eb, ea)


def _group_variants(x):
    lane = lax.broadcasted_iota(jnp.int32, (1, LANES), 1)
    lo = lane < (LANES // 2)
    xr = pltpu.roll(x, LANES // 2, axis=1)
    zero = jnp.zeros_like(x)
    return jnp.concatenate([jnp.where(lo, x, zero), jnp.where(lo, zero, xr),
                            jnp.where(lo, xr, zero), jnp.where(lo, zero, x)], axis=1)


def _values_transposed(v):
    zt = _group_variants(v).T
    extra_r = lax.broadcasted_iota(jnp.int32, (VT_ROWS - LANES, v.shape[0]), 0)
    parts = []
    for var in range(2 * NSA_GROUPS):
        parts.append(zt[var * LANES:(var + 1) * LANES])
        parts.append(jnp.where(extra_r == (var % 2), 1.0, 0.0))
    return jnp.concatenate(parts, axis=0)


def _nsa_prep_kernel(q_ref, ks_ref, vs_ref, kw_ref, vw_ref, pos_ref, freq_ref, sel_ref, one_ref,
                     qc_o, qr_o, ksz_o, vsz_o, kwz_o, vwz_o):
    scale = NSA_DK ** -0.5 * LOG2E
    ang = freq_ref[...] * pos_ref[...].astype(F32)
    half = ROPE_DIM // 2
    tm = ang.shape[1]
    cs = jnp.concatenate([jnp.cos(ang), jnp.sin(ang), jnp.zeros((LANES - 2 * half, tm), F32)], axis=0)
    spread = _dot_sel_rhs(cs.T, sel_ref[...])
    cos = spread[:, 0:LANES] + one_ref[...]
    sin_up = spread[:, LANES:2 * LANES]
    sin_dn = spread[:, 2 * LANES:3 * LANES]

    def rope(x):
        return x * cos + pltpu.roll(x, half, axis=1) * sin_up + pltpu.roll(x, LANES - half, axis=1) * sin_dn

    for p in range(NSA_HEADS * NSA_DK // LANES):
        x = q_ref[:, p * LANES:(p + 1) * LANES]
        qc_o[:, p * LANES:(p + 1) * LANES] = (x * scale).astype(qc_o.dtype)
        qr_o[:, p * LANES:(p + 1) * LANES] = (rope(x) * scale).astype(qr_o.dtype)
    ksz_o[...] = _group_variants(rope(ks_ref[...])).astype(ksz_o.dtype)
    kwz_o[...] = _group_variants(rope(kw_ref[...])).astype(kwz_o.dtype)
    vsz_o[...] = _values_transposed(vs_ref[...]).astype(vsz_o.dtype)
    vwz_o[...] = _values_transposed(vw_ref[...]).astype(vwz_o.dtype)


def _nsa_prep(proj, pos_row, freq_col, rope_sel, one_row, bsz, t_len, *, tm):
    m = proj.shape[0]
    tiles_per_seq = t_len // tm
    seq_t = lambda i: (i // tiles_per_seq, 0, i % tiles_per_seq)
    qw = NSA_HEADS * NSA_DK
    kv0 = qw // LANES
    const = lambda i: (0, 0)
    zw = 4 * LANES
    return pl.pallas_call(
        _nsa_prep_kernel,
        grid=(m // tm,),
        in_specs=[
            pl.BlockSpec((tm, qw), lambda i: (i, 0)),
            pl.BlockSpec((tm, LANES), lambda i: (i, kv0 + 2)),
            pl.BlockSpec((tm, LANES), lambda i: (i, kv0 + 3)),
            pl.BlockSpec((tm, LANES), lambda i: (i, kv0 + 4)),
            pl.BlockSpec((tm, LANES), lambda i: (i, kv0 + 5)),
            pl.BlockSpec((1, tm), lambda i: (0, i)),
            pl.BlockSpec(freq_col.shape, const),
            pl.BlockSpec(rope_sel.shape, const),
            pl.BlockSpec((1, LANES), const),
        ],
        out_specs=[
            pl.BlockSpec((tm, qw), lambda i: (i, 0)),
            pl.BlockSpec((tm, qw), lambda i: (i, 0)),
            pl.BlockSpec((tm, zw), lambda i: (i, 0)),
            pl.BlockSpec((None, 4 * VT_ROWS, tm), seq_t),
            pl.BlockSpec((tm, zw), lambda i: (i, 0)),
            pl.BlockSpec((None, 4 * VT_ROWS, tm), seq_t),
        ],
        out_shape=[
            jax.ShapeDtypeStruct((m, qw), BF16),
            jax.ShapeDtypeStruct((m, qw), BF16),
            jax.ShapeDtypeStruct((m, zw), BF16),
            jax.ShapeDtypeStruct((bsz, 4 * VT_ROWS, t_len), BF16),
            jax.ShapeDtypeStruct((m, zw), BF16),
            jax.ShapeDtypeStruct((bsz, 4 * VT_ROWS, t_len), BF16),
        ],
        compiler_params=_cparams("parallel"),
        name="nsa_prep",
    )(proj, proj, proj, proj, proj, pos_row, freq_col, rope_sel, one_row)


def _nsa_compress_kernel(k16_ref, v16_ref, pos_ref, w1a_ref, w1b_ref, w2_ref, kcz_o, vcz_o):
    nr = k16_ref.shape[0]
    for idx, (x_ref, o_ref) in enumerate(((k16_ref, kcz_o), (v16_ref, vcz_o))):
        x = x_ref[...]
        h_a = _dot((x + pos_ref[idx, 0:1, :]).astype(BF16), w1a_ref[idx])
        h_b = _dot((x + pos_ref[idx, 1:2, :]).astype(BF16), w1b_ref[idx])
        hid = h_a + pltpu.roll(h_b, nr - 1, axis=0)
        out = _dot(_silu(hid).astype(BF16), w2_ref[idx])
        z = _group_variants(out)
        o_ref[...] = (z.T if idx == 1 else z).astype(o_ref.dtype)


def _nsa_compress(k16, v16, pos_ab, w1a, w1b, w2):
    bsz, nr, width = k16.shape
    hid = NSA_GROUPS * CMP_HIDDEN
    zw = 4 * LANES
    c3 = lambda b: (0, 0, 0)
    return pl.pallas_call(
        _nsa_compress_kernel,
        grid=(bsz,),
        in_specs=[
            pl.BlockSpec((None, nr, width), lambda b: (b, 0, 0)),
            pl.BlockSpec((None, nr, width), lambda b: (b, 0, 0)),
            pl.BlockSpec((2, 2, width), c3),
            pl.BlockSpec((2, width, hid), c3),
            pl.BlockSpec((2, width, hid), c3),
            pl.BlockSpec((2, hid, LANES), c3),
        ],
        out_specs=[
            pl.BlockSpec((None, nr, zw), lambda b: (b, 0, 0)),
            pl.BlockSpec((None, zw, nr), lambda b: (b, 0, 0)),
        ],
        out_shape=[
            jax.ShapeDtypeStruct((bsz, nr, zw), BF16),
            jax.ShapeDtypeStruct((bsz, zw, nr), BF16),
        ],
        compiler_params=_cparams("parallel"),
        name="nsa_compress",
    )(k16, v16, pos_ab, w1a, w1b, w2)


def _flash_branch(q_stacks, kz_ref, vzt_ref, lo, hi, bias_fns, kb_size):
    r_acc = lax.broadcasted_iota(jnp.int32, (VT_ROWS, 1), 0)
    even_rows = (r_acc < LANES // 2) | (r_acc == LANES)
    half_w = 2 * Q_BLOCK
    n_half = q_stacks[0].shape[0] // half_w
    units = [(g, h) for g in range(NSA_GROUPS) for h in range(n_half)]
    q_unit = [q_stacks[g][h * half_w:(h + 1) * half_w] for g, h in units]

    def body(kb, carry):
        m_in, acc = carry
        ks = pl.multiple_of(kb * kb_size, kb_size)
        k_both, vt_both, bias = [], [], []
        for g in range(NSA_GROUPS):
            k0 = 2 * g * LANES
            v0 = 2 * g * VT_ROWS
            k_both.append(jnp.concatenate([kz_ref[pl.ds(ks, kb_size), k0:k0 + LANES],
                                           kz_ref[pl.ds(ks, kb_size), k0 + LANES:k0 + 2 * LANES]], axis=0))
            vt_both.append(jnp.concatenate([vzt_ref[v0:v0 + VT_ROWS, pl.ds(ks, kb_size)],
                                            vzt_ref[v0 + VT_ROWS:v0 + 2 * VT_ROWS, pl.ds(ks, kb_size)]],
                                           axis=1))
            b = bias_fns[g](ks)
            bias.append(jnp.concatenate([b, b], axis=1))
        n_u = len(units)
        s, m_out, alpha_rows, p_cat, pv = {}, {}, {}, {}, {}

        def softmax_stage(u):
            g = units[u][0]
            p_parts, alphas, m_news = [], [], []
            for var in range(2):
                s_v = s[u][var * kb_size:(var + 1) * kb_size] + bias[g]
                m_prev = m_in[u][var]
                m_new = jnp.maximum(m_prev, jnp.max(s_v, axis=0, keepdims=True))
                alphas.append(jnp.exp2(m_prev - m_new))
                p_parts.append(jnp.exp2(s_v - m_new).astype(BF16))
                m_news.append(m_new)
            m_out[u] = tuple(m_news)
            alpha_rows[u] = jnp.where(even_rows, alphas[0], alphas[1])
            p_cat[u] = jnp.concatenate(p_parts, axis=0)

        for u in range(n_u):
            s[u] = _dot_nt(k_both[units[u][0]], q_unit[u])
        for u in range(n_u):
            softmax_stage(u)
        for u in range(n_u):
            pv[u] = _dot(vt_both[units[u][0]], p_cat[u])
        acc_out = tuple(acc[u] * alpha_rows[u] + pv[u] for u in range(n_u))
        return tuple(m_out[u] for u in range(n_u)), acc_out

    m_init = tuple((jnp.full((1, half_w), NEG_BIG, F32),) * 2 for _ in units)
    acc_init = tuple(jnp.zeros((VT_ROWS, half_w), F32) for _ in units)
    _, acc = lax.fori_loop(lo, hi, body, (m_init, acc_init))
    outs = []
    for g in range(NSA_GROUPS):
        a = jnp.concatenate(acc[g * n_half:(g + 1) * n_half], axis=1)
        inv_even = 1.0 / a[LANES:LANES + 1, :]
        inv_odd = 1.0 / a[LANES + 1:LANES + 2, :]
        outs.append(a[0:LANES, :] * jnp.where(r_acc[0:LANES] < LANES // 2, inv_even, inv_odd))
    return outs


def _flash_branch_bounded(q_exts, kz_ref, kext_ref, vzt_ref, lo, hi, bias_fns, last_bias_fns, kb_size):
    r_acc = lax.broadcasted_iota(jnp.int32, (VT_ROWS, 1), 0)
    half_w = 2 * Q_BLOCK
    n_half = q_exts[0].shape[0] // half_w
    units = [(g, h) for g in range(NSA_GROUPS) for h in range(n_half)]
    q_unit = [q_exts[g][h * half_w:(h + 1) * half_w] for g, h in units]
    n_u = len(units)

    def step(kbs, acc, fns):
        work = []
        for kb in kbs:
            ks = pl.multiple_of(jnp.maximum(kb, 0) * kb_size, kb_size)
            k_ext = kext_ref[pl.ds(ks, kb_size), :]
            for g in range(NSA_GROUPS):
                k0 = 2 * g * LANES
                v0 = 2 * g * VT_ROWS
                k_both = jnp.concatenate(
                    [jnp.concatenate([kz_ref[pl.ds(ks, kb_size), k0:k0 + LANES], k_ext], axis=1),
                     jnp.concatenate([kz_ref[pl.ds(ks, kb_size), k0 + LANES:k0 + 2 * LANES], k_ext],
                                     axis=1)], axis=0)
                vt_both = jnp.concatenate([vzt_ref[v0:v0 + VT_ROWS, pl.ds(ks, kb_size)],
                                           vzt_ref[v0 + VT_ROWS:v0 + 2 * VT_ROWS, pl.ds(ks, kb_size)]],
                                          axis=1)
                bias = None
                if fns is not None:
                    b = fns[g](kb, ks)
                    b = jnp.concatenate([b, b], axis=1)
                    bias = jnp.concatenate([b, b], axis=0)
                work += [(u, k_both, vt_both, bias) for u in range(n_u) if units[u][0] == g]
        s = [_dot_nt(k, q_unit[u]) for u, k, _, _ in work]
        p = [jnp.exp2(s_i if w[3] is None else s_i + w[3]).astype(BF16) for s_i, w in zip(s, work)]
        pv = [_dot(w[2], p_i) for p_i, w in zip(p, work)]
        out = list(acc)
        for (u, _, _, _), pv_i in zip(work, pv):
            out[u] = out[u] + pv_i
        return tuple(out)

    acc = tuple(jnp.zeros((VT_ROWS, half_w), F32) for _ in units)
    if isinstance(lo, (list, tuple)):
        acc = step(lo, acc, bias_fns)
    elif last_bias_fns is None:
        acc = lax.fori_loop(lo, hi, lambda kb, a: step([kb], a, bias_fns), acc)
    else:
        acc = lax.fori_loop(lo, hi - 1, lambda kb, a: step([kb], a, bias_fns), acc)
        acc = step([hi - 1], acc, last_bias_fns)
    outs = []
    l_min = None
    for g in range(NSA_GROUPS):
        a = jnp.concatenate(acc[g * n_half:(g + 1) * n_half], axis=1)
        l_even = a[LANES:LANES + 1, :]
        l_odd = a[LANES + 1:LANES + 2, :]
        outs.append(a[0:LANES, :] * jnp.where(r_acc[0:LANES] < LANES // 2, 1.0 / l_even, 1.0 / l_odd))
        l_g = jnp.min(jnp.minimum(l_even, l_odd))
        l_min = l_g if l_min is None else jnp.minimum(l_min, l_g)
    return outs, l_min


def _nsa_attn_kernel(qc_ref, qr_ref, gate_ref, kcz_ref, vczt_ref, ksz_ref, vszt_ref, kwz_ref, vwzt_ref,
                     aggt_ref, eselt_ref, o_ref, bias_ref, kmax_ref,
                     *, t_len, kb_size):
    n_cmp_rows = t_len // CMP_STRIDE
    n_cmp = (t_len - CMP_LEN) // CMP_STRIDE + 1
    n_sel = t_len // SEL_LEN
    n_top = min(SEL_TOPK, n_sel)
    pairs = NSA_HEADS // NSA_GROUPS // 2
    q0 = pl.program_id(1) * Q_BLOCK
    t_row = q0 + lax.broadcasted_iota(jnp.int32, (1, Q_BLOCK), 1)
    t_row_stack = q0 + (lax.broadcasted_iota(jnp.int32, (1, pairs * Q_BLOCK), 1) & (Q_BLOCK - 1))
    blk = lax.broadcasted_iota(jnp.int32, (n_sel, 1), 0)
    c_idx = lax.broadcasted_iota(jnp.int32, (n_cmp_rows, 1), 0)
    cmp_mask = ((c_idx * CMP_STRIDE + (CMP_LEN - 1)) <= t_row_stack) & (c_idx < n_cmp)
    key_pos = lax.broadcasted_iota(jnp.int32, (t_len, 1), 0)
    gate_t = _sigmoid(gate_ref[...]).T
    top_half = lax.broadcasted_iota(jnp.int32, (LANES, 1), 0) < (LANES // 2)
    hi = (q0 + (Q_BLOCK - 1)) // kb_size + 1
    win_lo = jnp.maximum(q0 - (WINDOW - 1), 0) // kb_size

    def win_bias(ks):
        kp = ks + lax.broadcasted_iota(jnp.int32, (kb_size, 1), 0)
        return jnp.where((kp <= t_row) & (kp > t_row - WINDOW), 0.0, NEG_BIG)

    def sel_bias(g):
        return lambda ks: bias_ref[g, pl.ds(ks, kb_size), :]

    @pl.when(pl.program_id(1) == 0)
    def _():
        for br, k_ref in enumerate((ksz_ref, kwz_ref)):
            for g in range(NSA_GROUPS):
                k = k_ref[:, 2 * g * LANES:(2 * g + 1) * LANES].astype(F32)
                n2 = jnp.max(jnp.sum(k * k, axis=1, keepdims=True), axis=0, keepdims=True)
                kmax_ref[br * NSA_GROUPS + g] = jnp.broadcast_to(jnp.sqrt(n2), (SUBLANES, LANES))

    qr_stacks, o_cmps, sels = [], [], []
    for g in range(NSA_GROUPS):
        lanes_g = g * pairs * LANES
        qc_stack = jnp.concatenate(
            [qc_ref[:, lanes_g + p * LANES:lanes_g + (p + 1) * LANES] for p in range(pairs)], axis=0)
        qr_stacks.append(jnp.concatenate(
            [qr_ref[:, lanes_g + p * LANES:lanes_g + (p + 1) * LANES] for p in range(pairs)], axis=0))

        o_cmp = None
        p_sum = None
        for var in range(2):
            c0 = (2 * g + var) * LANES
            s = jnp.where(cmp_mask, _dot_nt(kcz_ref[:, c0:c0 + LANES], qc_stack), -jnp.inf)
            mx = jnp.max(s, axis=0, keepdims=True)
            mx = jnp.where(mx > -jnp.inf, mx, 0.0)
            e = jnp.exp2(s - mx)
            p = e * (1.0 / jnp.maximum(jnp.sum(e, axis=0, keepdims=True), 1e-30))
            d = _dot(vczt_ref[c0:c0 + LANES, :], p.astype(BF16))
            o_cmp = d if o_cmp is None else o_cmp + d
            for pr in range(pairs):
                part = p[:, pr * Q_BLOCK:(pr + 1) * Q_BLOCK]
                p_sum = part if p_sum is None else p_sum + part
        importance = _dot_sel_lhs(aggt_ref[...], p_sum)

        cur = t_row >> 6
        causal_blk = blk <= cur
        forced = (blk == 0) | (causal_blk & (blk > cur - SEL_LOCAL))
        score = jnp.where(forced, FORCE_SCORE, jnp.where(causal_blk, importance, -1.0))
        rank = jnp.zeros((n_sel, Q_BLOCK), F32)
        for jj in range(n_sel):
            row = score[jj:jj + 1, :]
            beats = (row > score) | ((row == score) & (blk > jj))
            rank = rank + jnp.where(beats, 1.0, 0.0)
        sel = jnp.where(rank < n_top, 1.0, 0.0)
        sels.append(jnp.concatenate([sel, jnp.zeros((LANES - n_sel, Q_BLOCK), F32)], axis=0))
        o_cmps.append(o_cmp)

    lane = lax.broadcasted_iota(jnp.int32, (1, LANES), 1)
    blk_pad = lax.broadcasted_iota(jnp.int32, (LANES, 1), 0)
    q_ext_sel, q_ext_win = [], []
    for g in range(NSA_GROUPS):
        x = qr_stacks[g].astype(F32)
        sq = x * x
        tot = jnp.sum(sq, axis=1, keepdims=True)
        even = jnp.sum(jnp.where(lane < LANES // 2, sq, 0.0), axis=1, keepdims=True)
        q_norm = jnp.sqrt(jnp.maximum(even, tot - even))
        sel_c = jnp.where(blk_pad * SEL_LEN <= t_row, sels[g], 0.0).T
        sel_b = jnp.where(lane < n_sel, (sel_c - 1.0) * (-NEG_BIG), 0.0)
        sel_b = jnp.concatenate([sel_b] * pairs, axis=0)
        ext_s = jnp.where(lane == n_sel, -q_norm * kmax_ref[g][0:1, 0:1], sel_b)
        ext_w = jnp.where(lane == n_sel, -q_norm * kmax_ref[NSA_GROUPS + g][0:1, 0:1], 0.0)
        q_ext_sel.append(jnp.concatenate([qr_stacks[g], ext_s.astype(BF16)], axis=1))
        q_ext_win.append(jnp.concatenate([qr_stacks[g], ext_w.astype(BF16)], axis=1))

    sel_kb = min(2 * kb_size, t_len)

    def diag_bias(kb, ks):
        kp = ks + lax.broadcasted_iota(jnp.int32, (sel_kb, 1), 0)
        return jnp.where(kp <= t_row, 0.0, NEG_BIG)

    n_win = WINDOW // kb_size + 1
    win_blocks = [hi - n_win + j for j in range(n_win)]

    def win_bias_fast(kb, ks):
        return jnp.where(kb >= 0, win_bias(ks), NEG_BIG)

    fast_sel, l_sel = _flash_branch_bounded(q_ext_sel, ksz_ref, eselt_ref, vszt_ref, 0,
                                            (q0 + (Q_BLOCK - 1)) // sel_kb + 1, None,
                                            [diag_bias] * NSA_GROUPS, sel_kb)
    fast_win, l_win = _flash_branch_bounded(q_ext_win, kwz_ref, eselt_ref, vwzt_ref, win_blocks, None,
                                            [win_bias_fast] * NSA_GROUPS, None, kb_size)

    def robust():
        for g in range(NSA_GROUPS):
            sel_keys = _dot(eselt_ref[...], sels[g].astype(BF16))
            bias_ref[g] = jnp.where((sel_keys > 0.5) & (key_pos <= t_row), 0.0, NEG_BIG)
        o_s = _flash_branch(qr_stacks, ksz_ref, vszt_ref, 0, hi,
                            [sel_bias(g) for g in range(NSA_GROUPS)], kb_size)
        o_w = _flash_branch(qr_stacks, kwz_ref, vwzt_ref, win_lo, hi, [win_bias] * NSA_GROUPS, kb_size)
        return tuple(o_s) + tuple(o_w)

    o_all = lax.cond(jnp.minimum(l_sel, l_win) > DENOM_FLOOR,
                     lambda: tuple(fast_sel) + tuple(fast_win), robust)
    o_sels, o_wins = o_all[:NSA_GROUPS], o_all[NSA_GROUPS:]

    for g in range(NSA_GROUPS):
        lanes_g = g * pairs * LANES
        for pr in range(pairs):
            h_even = (g * pairs + pr) * 2
            cols = slice(pr * Q_BLOCK, (pr + 1) * Q_BLOCK)
            mixed = None
            for j, o_branch in enumerate((o_cmps[g], o_sels[g], o_wins[g])):
                r_e = h_even * 3 + j
                r_o = r_e + 3
                gate_rows = jnp.where(top_half, gate_t[r_e:r_e + 1, :], gate_t[r_o:r_o + 1, :])
                term = gate_rows * o_branch[:, cols]
                mixed = term if mixed is None else mixed + term
            c0 = lanes_g + pr * LANES
            o_ref[:, c0:c0 + LANES] = mixed.T.astype(o_ref.dtype)


def _nsa_attn(qc, qr, proj, kcz, vczt, ksz, vszt, kwz, vwzt, aggt, eselt, bsz, t_len, *, kb_size):
    nq = t_len // Q_BLOCK
    qw = NSA_HEADS * NSA_DK
    zw = 4 * LANES
    nr = t_len // CMP_STRIDE
    pairs = NSA_HEADS // NSA_GROUPS // 2
    row = lambda b, i: (b * nq + i, 0)
    per_b = lambda b, i: (b, 0, 0)
    return pl.pallas_call(
        functools.partial(_nsa_attn_kernel, t_len=t_len, kb_size=kb_size),
        grid=(bsz, nq),
        in_specs=[
            pl.BlockSpec((Q_BLOCK, qw), row),
            pl.BlockSpec((Q_BLOCK, qw), row),
            pl.BlockSpec((Q_BLOCK, LANES), lambda b, i: (b * nq + i, NSA_GATE_OFF // LANES)),
            pl.BlockSpec((None, nr, zw), per_b),
            pl.BlockSpec((None, zw, nr), per_b),
            pl.BlockSpec((None, t_len, zw), per_b),
            pl.BlockSpec((None, 4 * VT_ROWS, t_len), per_b),
            pl.BlockSpec((None, t_len, zw), per_b),
            pl.BlockSpec((None, 4 * VT_ROWS, t_len), per_b),
            pl.BlockSpec(aggt.shape, lambda b, i: (0, 0)),
            pl.BlockSpec(eselt.shape, lambda b, i: (0, 0)),
        ],
        out_specs=pl.BlockSpec((Q_BLOCK, qw), row),
        out_shape=jax.ShapeDtypeStruct((bsz * t_len, qw), BF16),
        scratch_shapes=[
            pltpu.VMEM((NSA_GROUPS, t_len, Q_BLOCK), F32),
            pltpu.VMEM((2 * NSA_GROUPS, SUBLANES, LANES), F32),
        ],
        compiler_params=_cparams("parallel", "arbitrary"),
        name="nsa_attn",
    )(qc, qr, proj, kcz, vczt, ksz, vszt, kwz, vwzt, aggt, eselt)


def _expand_rows(first_row, n_heads, width):
    e = np.zeros((LANES, n_heads * width), np.float32)
    for h in range(n_heads):
        e[first_row + h, h * width:(h + 1) * width] = 1.0
    return e


def _nsa_constants(t_len):
    n_cmp = (t_len - CMP_LEN) // CMP_STRIDE + 1
    n_sel = t_len // SEL_LEN
    nr = t_len // CMP_STRIDE
    c0 = np.arange(n_cmp)[:, None] * CMP_STRIDE
    s0 = np.arange(n_sel)[None, :] * SEL_LEN
    overlap = np.clip(np.minimum(c0 + CMP_LEN, s0 + SEL_LEN) - np.maximum(c0, s0), 0, None) / CMP_LEN
    aggt = np.zeros((n_sel, nr), np.float32)
    aggt[:, :n_cmp] = overlap.T
    eselt = np.zeros((t_len, LANES), np.float32)
    eselt[np.arange(t_len), np.arange(t_len) // SEL_LEN] = 1.0
    eselt[:, n_sel] = 1.0
    half = ROPE_DIM // 2
    inv_freq = (ROPE_THETA ** (-np.arange(half) / half)).astype(np.float32)
    d = np.arange(LANES) % NSA_DK
    rope_sel = np.zeros((LANES, 3 * LANES), np.float32)
    for lane in range(LANES):
        if d[lane] < ROPE_DIM:
            f = d[lane] % half
            rope_sel[f, lane] = 1.0
            if d[lane] >= half:
                rope_sel[half + f, LANES + lane] = 1.0
            else:
                rope_sel[half + f, 2 * LANES + lane] = -1.0
    one_row = (d >= ROPE_DIM).astype(np.float32)[None]
    return aggt, eselt, inv_freq[:, None], rope_sel, one_row


def _pad_row(vec, offset):
    return jnp.zeros((1, LANES), F32).at[0, offset:offset + vec.shape[0]].set(vec.astype(F32))


def _hybrid_layer(h, bsz, t_len, w_in, gdn_conv_w, gdn_a_log, gdn_dt_bias, gdn_norm_w, ssd_conv_w,
                  ssd_conv_b, ssd_a_log, ssd_dt_bias, ssd_d, ssd_norm_w, w_out, ln_g, ln_b, cfg):
    b_off = GDN_CONV_DIM + GDN_V_W
    z_off = b_off + 2 * GDN_V_HEADS
    dt_off = z_off + SSD_D_INNER + SSD_CONV_DIM
    w_pad = jnp.concatenate(
        [w_in[:, :b_off], w_in[:, z_off:dt_off], w_in[:, b_off:z_off], w_in[:, dt_off:],
         jnp.zeros((D_MODEL, HYB_PAD - w_in.shape[1]), w_in.dtype)], axis=1).astype(BF16)
    conv_w = jnp.concatenate([gdn_conv_w, ssd_conv_w], axis=1).astype(F32)
    conv_b = jnp.concatenate([jnp.zeros((GDN_CONV_DIM,), F32), ssd_conv_b.astype(F32)])[None]
    proj = _proj_conv(h, w_pad, conv_w, conv_b, t_len, tm=min(cfg["proj_tm"], t_len), tn=cfg["hyb_tn"])
    o_a = _gdn(proj, bsz, t_len, _pad_row(gdn_dt_bias, GDN_V_HEADS),
               _pad_row(gdn_a_log, GDN_V_HEADS), gdn_norm_w[None].astype(F32),
               jnp.asarray(_expand_rows(0, GDN_V_HEADS, GDN_DV)).astype(BF16),
               jnp.asarray(_expand_rows(GDN_V_HEADS, GDN_V_HEADS, GDN_DV)).astype(BF16), tc=cfg["gdn_tc"])
    o_b = _ssd(proj, bsz, t_len, _pad_row(ssd_dt_bias, 16),
               _pad_row(ssd_a_log, 16), jnp.repeat(ssd_d, SSD_HEADDIM)[None], ssd_norm_w[None],
               jnp.asarray(_expand_rows(16, SSD_HEADS, SSD_HEADDIM)).astype(BF16))
    w_out_b = w_out.astype(BF16)
    return _out_ln([o_a, o_b], [w_out_b[:GDN_V_W], w_out_b[GDN_V_W:]], h, ln_g, ln_b, tm=cfg["out_tm"])


def _nsa_layer(h, pos_row, bsz, t_len, w_in, cmp_pos, cmp_w1, cmp_w2, w_out, ln_g, ln_b, cfg):
    aggt, eselt, freq_col, rope_sel, one_row = _nsa_constants(t_len)
    w_pad = jnp.concatenate(
        [w_in, jnp.zeros((D_MODEL, NSA_PAD - w_in.shape[1]), w_in.dtype)], axis=1).astype(BF16)
    proj = _proj(h, w_pad, tm=cfg["proj_tm"], tn=cfg["nsa_tn"])
    qc, qr, ksz, vszt, kwz, vwzt = _nsa_prep(proj, pos_row, jnp.asarray(freq_col),
                                             jnp.asarray(rope_sel).astype(BF16), jnp.asarray(one_row),
                                             bsz, t_len, tm=cfg["prep_tm"])
    qw = NSA_HEADS * NSA_DK
    nr = t_len // CMP_STRIDE
    width = CMP_STRIDE * LANES
    k16 = proj[:, qw:qw + LANES].reshape(bsz, nr, width)
    v16 = proj[:, qw + LANES:qw + 2 * LANES].reshape(bsz, nr, width)
    pos2 = jnp.broadcast_to(cmp_pos[:, :, None, :], (2, CMP_LEN, NSA_GROUPS, NSA_DK)).reshape(2, 2, width)
    w1 = cmp_w1.reshape(2, 2, CMP_STRIDE, NSA_DK, CMP_HIDDEN)
    eye_g = jnp.eye(NSA_GROUPS, dtype=w1.dtype)
    w1x = jnp.einsum("ksjdh,ge->ksjgdeh", w1, eye_g).reshape(
        2, 2, width, NSA_GROUPS * CMP_HIDDEN).astype(BF16)
    w2x = jnp.einsum("khd,ge->kghed", cmp_w2, eye_g).reshape(
        2, NSA_GROUPS * CMP_HIDDEN, NSA_GROUPS * NSA_DK).astype(BF16)
    kcz, vczt = _nsa_compress(k16, v16, pos2, w1x[:, 0], w1x[:, 1], w2x)
    o = _nsa_attn(qc, qr, proj, kcz, vczt, ksz.reshape(bsz, t_len, -1), vszt,
                  kwz.reshape(bsz, t_len, -1), vwzt,
                  jnp.asarray(aggt).astype(BF16), jnp.asarray(eselt).astype(BF16),
                  bsz, t_len, kb_size=cfg["attn_kb"])
    return _out_ln([o], [w_out.astype(BF16)], h, ln_g, ln_b, tm=cfg["out_tm"])


def _config(m, t_len):
    return dict(ffn_tm=min(512, m), ffn_tf=256, proj_tm=min(512, m), hyb_tn=512, nsa_tn=NSA_PAD,
                out_tm=min(512, m), gdn_tc=min(256, t_len), prep_tm=min(256, m),
                attn_kb=min(256, t_len))


def kernel(x, positions, ln_g, ln_b, ffn_w_in, ffn_w_out, hyb_w_in, gdn_conv_w, gdn_a_log, gdn_dt_bias, gdn_norm_w, ssd_conv_w, ssd_conv_b, ssd_a_log, ssd_dt_bias, ssd_d, ssd_norm_w, hyb_w_out, nsa_w_in, nsa_cmp_pos, nsa_cmp_w1, nsa_cmp_w2, nsa_w_out):
    bsz, t_len, _ = x.shape
    m = bsz * t_len
    cfg = _config(m, t_len)
    h = x.reshape(m, D_MODEL)
    pos_row = positions.reshape(1, m)
    w_in_b = ffn_w_in.astype(BF16)
    w_out_b = ffn_w_out.astype(BF16)
    ln_g = ln_g[:, :, None, :]
    ln_b = ln_b[:, :, None, :]
    for layer in range(DEPTH):
        i = layer // 2
        h = _ffn_ln(h, w_in_b, w_out_b, ln_g[layer, 0], ln_b[layer, 0], layer, 0,
                    tm=cfg["ffn_tm"], tf=cfg["ffn_tf"])
        if layer % 2 == 0:
            h = _hybrid_layer(h, bsz, t_len, hyb_w_in[i], gdn_conv_w[i], gdn_a_log[i], gdn_dt_bias[i],
                              gdn_norm_w[i], ssd_conv_w[i], ssd_conv_b[i], ssd_a_log[i], ssd_dt_bias[i],
                              ssd_d[i], ssd_norm_w[i], hyb_w_out[i], ln_g[layer, 1], ln_b[layer, 1], cfg)
        else:
            h = _nsa_layer(h, pos_row, bsz, t_len, nsa_w_in[i], nsa_cmp_pos[i], nsa_cmp_w1[i],
                           nsa_cmp_w2[i], nsa_w_out[i], ln_g[layer, 1], ln_b[layer, 1], cfg)
        h = _ffn_ln(h, w_in_b, w_out_b, ln_g[layer, 2], ln_b[layer, 2], layer, 1,
                    tm=cfg["ffn_tm"], tf=cfg["ffn_tf"])
    return h.reshape(bsz, t_len, D_MODEL)
```

```python
import functools

import numpy as np
import jax
import jax.numpy as jnp
from jax import lax
from jax.experimental import pallas as pl
from jax.experimental.pallas import tpu as pltpu

F32 = jnp.float32
BF16 = jnp.bfloat16

D_MODEL = 1024
DEPTH = 4
D_FF = 2816
DEEPNORM_ALPHA = (2.0 * DEPTH) ** 0.25
LN_EPS = 1e-5
RMS_EPS = 1e-6
CONV_K = 4

GDN_QK_HEADS = 4
GDN_V_HEADS = 8
GDN_DK = 128
GDN_DV = 128
GDN_CHUNK = 64
GDN_QK_W = GDN_QK_HEADS * GDN_DK
GDN_V_W = GDN_V_HEADS * GDN_DV
GDN_CONV_DIM = 2 * GDN_QK_W + GDN_V_W

SSD_D_INNER = D_MODEL
SSD_HEADDIM = 64
SSD_HEADS = SSD_D_INNER // SSD_HEADDIM
SSD_GROUPS = 2
SSD_STATE = 128
SSD_CHUNK = 128
SSD_BC_W = SSD_GROUPS * SSD_STATE
SSD_CONV_DIM = SSD_D_INNER + 2 * SSD_BC_W

NSA_HEADS = 16
NSA_GROUPS = 2
NSA_DK = 64
NSA_DV = 64
ROPE_DIM = NSA_DK // 4
ROPE_THETA = 500000.0
CMP_LEN = 32
CMP_STRIDE = 16
CMP_HIDDEN = 256
SEL_LEN = 64
SEL_TOPK = 8
SEL_LOCAL = 2
FORCE_SCORE = 1e4
WINDOW = 512
Q_BLOCK = 128

LANES = 128
SUBLANES = 8
VMEM_LIMIT_BYTES = 48 * 1024 * 1024

HYB_SMALL_OFF = GDN_CONV_DIM + GDN_V_W + SSD_D_INNER + SSD_CONV_DIM
HYB_PAD = HYB_SMALL_OFF + LANES
NSA_GATE_OFF = NSA_HEADS * NSA_DK + 3 * NSA_GROUPS * (NSA_DK + NSA_DV)
NSA_PAD = NSA_GATE_OFF + LANES

NEG_BIG = -1e30
LOG2E = 1.4426950408889634
DENOM_FLOOR = 2.0 ** -100
VT_ROWS = LANES + 16


def _cparams(*sem):
    return pltpu.CompilerParams(dimension_semantics=sem, vmem_limit_bytes=VMEM_LIMIT_BYTES)


def _sigmoid(v):
    return 0.5 + 0.5 * jnp.tanh(0.5 * v)


def _silu(v):
    h = 0.5 * v
    return h + h * jnp.tanh(h)


def _softplus(v):
    return jnp.maximum(v, 0.0) + jnp.log1p(jnp.exp(-jnp.abs(v)))


def _layer_norm(y, g, b):
    mu = jnp.mean(y, axis=-1, keepdims=True)
    d = y - mu
    var = jnp.mean(d * d, axis=-1, keepdims=True)
    return d * lax.rsqrt(var + LN_EPS) * g + b


def _dot(a, b):
    return jnp.dot(a, b, preferred_element_type=F32)


def _split3(x):
    hi = x.astype(BF16)
    r1 = x - hi.astype(F32)
    mid = r1.astype(BF16)
    lo = (r1 - mid.astype(F32)).astype(BF16)
    return hi, mid, lo


def _dot_sel_rhs(x, sel):
    sel = sel.astype(BF16)
    hi, mid, lo = _split3(x)
    return _dot(hi, sel) + _dot(mid, sel) + _dot(lo, sel)


def _dot_sel_lhs(sel, x):
    sel = sel.astype(BF16)
    hi, mid, lo = _split3(x)
    return _dot(sel, hi) + _dot(sel, mid) + _dot(sel, lo)


def _dot_nt(a, b):
    return lax.dot_general(a, b, (((1,), (1,)), ((), ())), preferred_element_type=F32)


def _ffn_ln_kernel(x_ref, wi_ref, wo_ref, g_ref, b_ref, o_ref, *, tf):
    x = x_ref[...]
    xb = x.astype(BF16)
    acts = []
    for c in range(D_FF // tf):
        gate = _dot(xb, wi_ref[:, c * tf:(c + 1) * tf])
        up = _dot(xb, wi_ref[:, D_FF + c * tf:D_FF + (c + 1) * tf])
        acts.append((_silu(gate) * up).astype(BF16))
    mix = _dot(jnp.concatenate(acts, axis=1), wo_ref[...])
    y = DEEPNORM_ALPHA * x + 0.5 * mix
    o_ref[...] = _layer_norm(y, g_ref[...], b_ref[...])


def _resident(shape):
    return pl.BlockSpec(shape, lambda *_: (0,) * len(shape), pipeline_mode=pl.Buffered(1))


def _ffn_ln(h, w_in, w_out, g, b, layer, which, *, tm, tf):
    m = h.shape[0]
    pick = lambda i: (layer, which, 0, 0)
    return pl.pallas_call(
        functools.partial(_ffn_ln_kernel, tf=tf),
        grid=(m // tm,),
        in_specs=[
            pl.BlockSpec((tm, D_MODEL), lambda i: (i, 0)),
            pl.BlockSpec((None, None) + w_in.shape[2:], pick, pipeline_mode=pl.Buffered(1)),
            pl.BlockSpec((None, None) + w_out.shape[2:], pick, pipeline_mode=pl.Buffered(1)),
            _resident((1, D_MODEL)),
            _resident((1, D_MODEL)),
        ],
        out_specs=pl.BlockSpec((tm, D_MODEL), lambda i: (i, 0)),
        out_shape=jax.ShapeDtypeStruct((m, D_MODEL), F32),
        compiler_params=_cparams("parallel"),
        name="ffn_ln",
    )(h, w_in, w_out, g, b)


def _proj_kernel(x_ref, w_ref, o_ref, *, tn):
    xb = x_ref[...].astype(BF16)
    for c in range(w_ref.shape[1] // tn):
        o_ref[:, c * tn:(c + 1) * tn] = _dot(xb, w_ref[:, c * tn:(c + 1) * tn])


def _proj(h, w, *, tm, tn):
    m, k = h.shape
    n = w.shape[1]
    return pl.pallas_call(
        functools.partial(_proj_kernel, tn=tn),
        grid=(m // tm,),
        in_specs=[
            pl.BlockSpec((tm, k), lambda i: (i, 0)),
            _resident(w.shape),
        ],
        out_specs=pl.BlockSpec((tm, n), lambda i: (i, 0)),
        out_shape=jax.ShapeDtypeStruct((m, n), F32),
        compiler_params=_cparams("parallel"),
        name="in_proj",
    )(h, w)


def _proj_conv_kernel(x_ref, w_ref, cw_ref, cb_ref, o_ref, cbuf, *, chunks, tiles_per_seq):
    tm = x_ref.shape[0]

    @pl.when(pl.program_id(0) % tiles_per_seq == 0)
    def _():
        cbuf[0:SUBLANES, :] = jnp.zeros((SUBLANES, cbuf.shape[1]), F32)

    xb = x_ref[...].astype(BF16)

    def finish(chunk, y):
        col, width, ccol = chunk
        if ccol is not None:
            cbuf[SUBLANES:SUBLANES + tm, ccol:ccol + width] = y
            acc = cb_ref[:, ccol:ccol + width]
            for i in range(CONV_K):
                acc = acc + (cw_ref[i:i + 1, ccol:ccol + width]
                             * cbuf[pl.ds(SUBLANES - CONV_K + 1 + i, tm), ccol:ccol + width])
            cbuf[0:SUBLANES, ccol:ccol + width] = cbuf[tm:tm + SUBLANES, ccol:ccol + width]
            y = _silu(acc)
        o_ref[:, col:col + width] = y

    pending = None
    for chunk in chunks:
        y = _dot(xb, w_ref[:, chunk[0]:chunk[0] + chunk[1]])
        if pending is not None:
            finish(*pending)
        pending = (chunk, y)
    finish(*pending)


def _proj_conv(h, w, cw, cb, t_len, *, tm, tn):
    m, k = h.shape
    n = w.shape[1]
    conv_w = cw.shape[1]
    ssd_lo = GDN_CONV_DIM + GDN_V_W + SSD_D_INNER
    chunks = []
    for col in range(0, n, tn):
        width = min(tn, n - col)
        if col + width <= GDN_CONV_DIM:
            ccol = col
        elif ssd_lo <= col and col + width <= ssd_lo + SSD_CONV_DIM:
            ccol = col - ssd_lo + GDN_CONV_DIM
        else:
            ccol = None
        chunks.append((col, width, ccol))
    return pl.pallas_call(
        functools.partial(_proj_conv_kernel, chunks=tuple(chunks), tiles_per_seq=t_len // tm),
        grid=(m // tm,),
        in_specs=[
            pl.BlockSpec((tm, k), lambda i: (i, 0)),
            _resident(w.shape),
            _resident(cw.shape),
            _resident(cb.shape),
        ],
        out_specs=pl.BlockSpec((tm, n), lambda i: (i, 0)),
        out_shape=jax.ShapeDtypeStruct((m, n), F32),
        scratch_shapes=[pltpu.VMEM((tm + SUBLANES, conv_w), F32)],
        compiler_params=_cparams("arbitrary"),
        name="in_proj_conv",
    )(h, w, cw, cb)


def _out_ln_kernel(*refs, n_parts):
    a_refs = refs[:n_parts]
    w_refs = refs[n_parts:2 * n_parts]
    h_ref, g_ref, b_ref, o_ref = refs[2 * n_parts:]
    mix = _dot(a_refs[0][...], w_refs[0][...])
    for a_ref, w_ref in zip(a_refs[1:], w_refs[1:]):
        mix = mix + _dot(a_ref[...], w_ref[...])
    y = DEEPNORM_ALPHA * h_ref[...] + mix
    o_ref[...] = _layer_norm(y, g_ref[...], b_ref[...])


def _out_ln(parts, weights, h, g, b, *, tm):
    m = h.shape[0]
    n_parts = len(parts)
    in_specs = [pl.BlockSpec((tm, a.shape[1]), lambda i: (i, 0)) for a in parts]
    in_specs += [pl.BlockSpec(w.shape, lambda i: (0, 0)) for w in weights]
    in_specs += [
        pl.BlockSpec((tm, D_MODEL), lambda i: (i, 0)),
        pl.BlockSpec((1, D_MODEL), lambda i: (0, 0)),
        pl.BlockSpec((1, D_MODEL), lambda i: (0, 0)),
    ]
    return pl.pallas_call(
        functools.partial(_out_ln_kernel, n_parts=n_parts),
        grid=(m // tm,),
        in_specs=in_specs,
        out_specs=pl.BlockSpec((tm, D_MODEL), lambda i: (i, 0)),
        out_shape=jax.ShapeDtypeStruct((m, D_MODEL), F32),
        compiler_params=_cparams("parallel"),
        name="out_proj_ln",
    )(*parts, *weights, h, g, b)


def _ssd_kernel(z_ref, x_ref, bm_ref, cm_ref, sm_ref, dtb_ref, alog_ref,
                dskip_ref, nw_ref, eh_ref, o_ref, state, *, n_sub):
    @pl.when(pl.program_id(1) == 0)
    def _():
        state[...] = jnp.zeros_like(state)

    for sub in range(n_sub):
        _ssd_chunk(pl.ds(sub * SSD_CHUNK, SSD_CHUNK), z_ref, x_ref, bm_ref, cm_ref, sm_ref, dtb_ref,
                   alog_ref, dskip_ref, nw_ref, eh_ref, o_ref, state)


def _ssd_chunk(rows, z_ref, x_ref, bm_ref, cm_ref, sm_ref, dtb_ref, alog_ref,
               dskip_ref, nw_ref, eh_ref, o_ref, state):
    L = SSD_CHUNK
    W = SSD_D_INNER
    GW = W // SSD_GROUPS
    N = SSD_STATE

    xs = x_ref[rows, :]
    bm = bm_ref[rows, :].astype(BF16)
    cm = cm_ref[rows, :].astype(BF16)

    lane = lax.broadcasted_iota(jnp.int32, (1, LANES), 1)
    dt_lanes = (lane >= 16) & (lane < 16 + SSD_HEADS)
    dt_full = jnp.where(dt_lanes, _softplus(sm_ref[rows, :] + dtb_ref[...]), 0.0)
    adt_full = dt_full * (-jnp.exp(alog_ref[...]))
    r_i = lax.broadcasted_iota(jnp.int32, (L, L), 0)
    c_i = lax.broadcasted_iota(jnp.int32, (L, L), 1)
    tril = r_i >= c_i
    acs_full = _dot_sel_lhs(jnp.where(tril, 1.0, 0.0), adt_full)
    acs_t = acs_full.T
    eh = eh_ref[...]
    dt_exp = _dot_sel_rhs(dt_full, eh)
    acs_exp = _dot_sel_rhs(acs_full, eh)
    a_last = acs_exp[L - 1:L, :]
    xdt = xs * dt_exp
    xdec = (xdt * jnp.exp(a_last - acs_exp)).astype(BF16)
    e_acs = jnp.exp(acs_exp)
    e_last = jnp.exp(a_last)
    lane_w = lax.broadcasted_iota(jnp.int32, (1, W), 1)
    lo_half = (lane_w & (LANES - 1)) < SSD_HEADDIM
    xdt_b = xdt.astype(BF16)
    xdt_lo = jnp.where(lo_half, xdt_b, jnp.zeros_like(xdt_b))
    xdt_hi = jnp.where(lo_half, jnp.zeros_like(xdt_b), xdt_b)

    heads_per_group = SSD_HEADS // SSD_GROUPS
    y_parts = []
    for g in range(SSD_GROUPS):
        bg = bm[:, g * N:(g + 1) * N]
        cg = cm[:, g * N:(g + 1) * N]
        cb = _dot_nt(cg, bg)
        s_g = state[g]
        y_off = _dot(cg, s_g.astype(BF16)) * e_acs[:, g * GW:(g + 1) * GW]
        for pr in range(heads_per_group // 2):
            y_pair = None
            for var in range(2):
                h = g * heads_per_group + 2 * pr + var
                col = acs_full[:, 16 + h:17 + h]
                row = acs_t[16 + h:17 + h, :]
                seg = jnp.exp(jnp.where(tril, col - row, -jnp.inf))
                mh = (cb * seg).astype(BF16)
                src = xdt_lo if var == 0 else xdt_hi
                lanes0 = (g * heads_per_group + 2 * pr) * SSD_HEADDIM
                d = _dot(mh, src[:, lanes0:lanes0 + LANES])
                y_pair = d if y_pair is None else y_pair + d
            off = pr * LANES
            y_parts.append(y_pair + y_off[:, off:off + LANES])
        upd = lax.dot_general(bg, xdec[:, g * GW:(g + 1) * GW], (((0,), (0,)), ((), ())),
                              preferred_element_type=F32)
        state[g] = s_g * e_last[:, g * GW:(g + 1) * GW] + upd
    y = jnp.concatenate(y_parts, axis=1) + xs * dskip_ref[...]
    y = y * _silu(z_ref[rows, :])
    outs = []
    for g in range(SSD_GROUPS):
        yg = y[:, g * GW:(g + 1) * GW]
        outs.append(yg * lax.rsqrt(jnp.mean(yg * yg, axis=-1, keepdims=True) + RMS_EPS))
    o_ref[rows, :] = (jnp.concatenate(outs, axis=1) * nw_ref[...]).astype(o_ref.dtype)


def _ssd(proj, bsz, t_len, dtb_row, alog_row, dskip_row, nw_row, eh, *, n_sub):
    L = n_sub * SSD_CHUNK
    nc = t_len // L
    W = SSD_D_INNER
    z_blk = (GDN_CONV_DIM + GDN_V_W) // W
    x_blk = (GDN_CONV_DIM + GDN_V_W + W) // W
    b_blk = (GDN_CONV_DIM + GDN_V_W + 2 * W) // SSD_BC_W
    sm_blk = HYB_SMALL_OFF // LANES
    row = lambda b, c: b * nc + c
    const = lambda b, c: (0, 0)
    return pl.pallas_call(
        functools.partial(_ssd_kernel, n_sub=n_sub),
        grid=(bsz, nc),
        in_specs=[
            pl.BlockSpec((L, W), lambda b, c: (row(b, c), z_blk)),
            pl.BlockSpec((L, W), lambda b, c: (row(b, c), x_blk)),
            pl.BlockSpec((L, SSD_BC_W), lambda b, c: (row(b, c), b_blk)),
            pl.BlockSpec((L, SSD_BC_W), lambda b, c: (row(b, c), b_blk + 1)),
            pl.BlockSpec((L, LANES), lambda b, c: (row(b, c), sm_blk)),
            pl.BlockSpec((1, LANES), const),
            pl.BlockSpec((1, LANES), const),
            pl.BlockSpec((1, W), const),
            pl.BlockSpec((1, W), const),
            pl.BlockSpec((LANES, W), const),
        ],
        out_specs=pl.BlockSpec((L, W), lambda b, c: (row(b, c), 0)),
        out_shape=jax.ShapeDtypeStruct((bsz * t_len, W), BF16),
        scratch_shapes=[
            pltpu.VMEM((SSD_GROUPS, SSD_STATE, W // SSD_GROUPS), F32),
        ],
        compiler_params=_cparams("parallel", "arbitrary"),
        name="ssd",
    )(proj, proj, proj, proj, proj, dtb_row, alog_row, dskip_row, nw_row, eh)


def _unit_lower_inverse_minus_eye(a, r_i, c_i):
    n = range(len(a))
    diag8 = (r_i >> 3) == (c_i >> 3)
    x = [jnp.where(diag8, -a[i], 0.0) for i in n]
    x_b = [x[i].astype(BF16) for i in n]
    x2 = [_dot(x_b[i], x_b[i]) for i in n]
    x2_b = [x2[i].astype(BF16) for i in n]
    x4 = [_dot(x2_b[i], x2_b[i]) for i in n]
    e = [x[i] + x2[i] + _dot(x_b[i], x2_b[i]) for i in n]
    e = [e[i] + x4[i] + _dot(e[i].astype(BF16), x4[i].astype(BF16)) for i in n]
    for sh in (3, 4, 5):
        same_pair = (r_i >> (sh + 1)) == (c_i >> (sh + 1))
        lower_left = (((r_i >> sh) & 1) == 1) & (((c_i >> sh) & 1) == 0)
        ms = [jnp.where(same_pair & lower_left, a[i], 0.0) for i in n]
        y = [ms[i] + _dot(ms[i].astype(BF16), e[i].astype(BF16)) for i in n]
        e = [e[i] - y[i] - _dot(e[i].astype(BF16), y[i].astype(BF16)) for i in n]
    return e


def _stack_heads(x, r0, l0):
    return jnp.concatenate([x[r0:r0 + GDN_CHUNK, l0:l0 + GDN_DV],
                            x[r0:r0 + GDN_CHUNK, l0 + GDN_DV:l0 + 2 * GDN_DV]], axis=0)


def _own_head_block(x):
    return jnp.concatenate([x[0:GDN_CHUNK, 0:GDN_DV], x[GDN_CHUNK:, GDN_DV:]], axis=0)


def _gdn_kernel(q_ref, k_ref, v_ref, z_ref, sm_ref, dtb_ref, alog_ref, nw_ref,
                eb_ref, ea_ref, o_ref, state, *, tc):
    C = GDN_CHUNK
    DK = GDN_DK
    DV = GDN_DV
    P = 2 * C

    @pl.when(pl.program_id(1) == 0)
    def _():
        state[...] = jnp.zeros_like(state)

    lane = lax.broadcasted_iota(jnp.int32, (1, LANES), 1)
    sm = sm_ref[...]
    beta_full = jnp.where(lane < GDN_V_HEADS, _sigmoid(sm), 0.0)
    g_lanes = (lane >= GDN_V_HEADS) & (lane < 2 * GDN_V_HEADS)
    g_full = jnp.where(g_lanes, -jnp.exp(alog_ref[...]) * _softplus(sm + dtb_ref[...]), 0.0)
    rt = lax.broadcasted_iota(jnp.int32, (tc, tc), 0)
    ct = lax.broadcasted_iota(jnp.int32, (tc, tc), 1)
    same_chunk_tril = ((rt >> 6) == (ct >> 6)) & (rt >= ct)
    gc_full = _dot_sel_lhs(jnp.where(same_chunk_tril, 1.0, 0.0), g_full)
    b_exp = _dot_sel_rhs(beta_full, eb_ref[...])
    g_exp = _dot_sel_rhs(gc_full, ea_ref[...])
    eg_exp = jnp.exp(g_exp)

    r_i = lax.broadcasted_iota(jnp.int32, (P, P), 0)
    c_i = lax.broadcasted_iota(jnp.int32, (P, P), 1)
    same_head = (r_i >> 6) == (c_i >> 6)
    tril = same_head & (r_i >= c_i)
    strict = same_head & (r_i > c_i)
    head0_rows = lax.broadcasted_iota(jnp.int32, (P, 1), 0) < C
    nw = nw_ref[...]

    n_chunks = tc // C
    units = [(hq, ci) for ci in range(n_chunks) for hq in range(GDN_QK_HEADS)]
    qn, kn = [], []
    for hq in range(GDN_QK_HEADS):
        qh = q_ref[:, hq * DK:(hq + 1) * DK]
        kh = k_ref[:, hq * DK:(hq + 1) * DK]
        qn.append(qh * lax.rsqrt(jnp.sum(qh * qh, axis=-1, keepdims=True) + 1e-6) * (DK ** -0.5))
        kn.append(kh * lax.rsqrt(jnp.sum(kh * kh, axis=-1, keepdims=True) + 1e-6))
    pre = {}
    a_low = []
    for hq, ci in units:
        r0 = ci * C
        l0 = hq * 2 * DV
        q_c = qn[hq][r0:r0 + C]
        k_c = kn[hq][r0:r0 + C]
        q2 = jnp.concatenate([q_c, q_c], axis=0)
        k2 = jnp.concatenate([k_c, k_c], axis=0)
        k2_b = k2.astype(BF16)
        kk = _dot_nt(k2_b, k2_b)
        qk = _dot_nt(q2.astype(BF16), k2_b)
        gcol = _stack_heads(g_exp, r0, l0)
        bcol = _stack_heads(b_exp, r0, l0)
        egc = _stack_heads(eg_exp, r0, l0)
        dmat = jnp.exp(jnp.where(tril, gcol - gcol.T, -jnp.inf))
        a_low.append(jnp.where(strict, kk * bcol * dmat, 0.0))
        g_last = jnp.where(head0_rows, gcol[C - 1:C], gcol[P - 1:P])
        pre[hq, ci] = dict(
            vb=_stack_heads(v_ref, r0, l0) * bcol,
            kb=k2 * (bcol * egc),
            qd=(q2 * egc).astype(BF16),
            attn=(qk * dmat).astype(BF16),
            k_t=k_c.T.astype(BF16),
            v_scale=jnp.exp(g_last - gcol),
            decay_cat=jnp.concatenate([egc[C - 1:C], egc[P - 1:P]], axis=1))
    e_all = _unit_lower_inverse_minus_eye(a_low, r_i, c_i)
    for (hq, ci), e in zip(units, e_all):
        d = pre[hq, ci]
        e_b = e.astype(BF16)
        d["u"] = d["vb"] + _dot(e_b, d["vb"].astype(BF16))
        d["w"] = (d["kb"] + _dot(e_b, d["kb"].astype(BF16))).astype(BF16)

    s_cat = [state[hq] for hq in range(GDN_QK_HEADS)]
    for ci in range(n_chunks):
        r0 = ci * C
        for hq in range(GDN_QK_HEADS):
            d = pre[hq, ci]
            l0 = hq * 2 * DV
            s_b = s_cat[hq].astype(BF16)
            v_new = d["u"] - _own_head_block(_dot(d["w"], s_b))
            o2 = _own_head_block(_dot(d["qd"], s_b)) + _dot(d["attn"], v_new.astype(BF16))
            v_dec = (v_new * d["v_scale"]).astype(BF16)
            v_dec_cat = jnp.concatenate([v_dec[0:C], v_dec[C:]], axis=1)
            s_cat[hq] = s_cat[hq] * d["decay_cat"] + _dot(d["k_t"], v_dec_cat)
            o_n = o2 * lax.rsqrt(jnp.mean(o2 * o2, axis=-1, keepdims=True) + RMS_EPS)
            res = (o_n * nw * _silu(_stack_heads(z_ref, r0, l0))).astype(o_ref.dtype)
            o_ref[r0:r0 + C, l0:l0 + DV] = res[0:C]
            o_ref[r0:r0 + C, l0 + DV:l0 + 2 * DV] = res[C:]
    for hq in range(GDN_QK_HEADS):
        state[hq] = s_cat[hq]


def _gdn(proj, bsz, t_len, dtb_row, alog_row, nw_row, eb, ea, *, tc):
    nt = t_len // tc
    row = lambda b, t: b * nt + t
    const = lambda b, t: (0, 0)
    sm_blk = HYB_SMALL_OFF // LANES
    return pl.pallas_call(
        functools.partial(_gdn_kernel, tc=tc),
        grid=(bsz, nt),
        in_specs=[
            pl.BlockSpec((tc, GDN_QK_W), lambda b, t: (row(b, t), 0)),
            pl.BlockSpec((tc, GDN_QK_W), lambda b, t: (row(b, t), 1)),
            pl.BlockSpec((tc, GDN_V_W), lambda b, t: (row(b, t), 1)),
            pl.BlockSpec((tc, GDN_V_W), lambda b, t: (row(b, t), 2)),
            pl.BlockSpec((tc, LANES), lambda b, t: (row(b, t), sm_blk)),
            pl.BlockSpec((1, LANES), const),
            pl.BlockSpec((1, LANES), const),
            pl.BlockSpec((1, GDN_DV), const),
            pl.BlockSpec((LANES, GDN_V_W), const),
            pl.BlockSpec((LANES, GDN_V_W), const),
        ],
        out_specs=pl.BlockSpec((tc, GDN_V_W), lambda b, t: (row(b, t), 0)),
        out_shape=jax.ShapeDtypeStruct((bsz * t_len, GDN_V_W), BF16),
        scratch_shapes=[
            pltpu.VMEM((GDN_QK_HEADS, GDN_DK, 2 * GDN_DV), F32),
        ],
        compiler_params=_cparams("parallel", "arbitrary"),
        name="gdn",
    )(proj, proj, proj, proj, proj, dtb_row, alog_row, nw_row, eb, ea)


def _group_variants(x):
    lane = lax.broadcasted_iota(jnp.int32, (1, LANES), 1)
    lo = lane < (LANES // 2)
    xr = pltpu.roll(x, LANES // 2, axis=1)
    zero = jnp.zeros_like(x)
    return jnp.concatenate([jnp.where(lo, x, zero), jnp.where(lo, zero, xr),
                            jnp.where(lo, xr, zero), jnp.where(lo, zero, x)], axis=1)


def _values_transposed(v):
    zt = _group_variants(v).T
    extra_r = lax.broadcasted_iota(jnp.int32, (VT_ROWS - LANES, v.shape[0]), 0)
    parts = []
    for var in range(2 * NSA_GROUPS):
        parts.append(zt[var * LANES:(var + 1) * LANES])
        parts.append(jnp.where(extra_r == (var % 2), 1.0, 0.0))
    return jnp.concatenate(parts, axis=0)


def _nsa_prep_kernel(q_ref, ks_ref, vs_ref, kw_ref, vw_ref, pos_ref, freq_ref, sel_ref, one_ref,
                     qc_o, qr_o, ksz_o, vsz_o, kwz_o, vwz_o):
    scale = NSA_DK ** -0.5 * LOG2E
    ang = freq_ref[...] * pos_ref[...].astype(F32)
    half = ROPE_DIM // 2
    tm = ang.shape[1]
    cs = jnp.concatenate([jnp.cos(ang), jnp.sin(ang), jnp.zeros((LANES - 2 * half, tm), F32)], axis=0)
    spread = _dot_sel_rhs(cs.T, sel_ref[...])
    cos = spread[:, 0:LANES] + one_ref[...]
    sin_up = spread[:, LANES:2 * LANES]
    sin_dn = spread[:, 2 * LANES:3 * LANES]

    def rope(x):
        return x * cos + pltpu.roll(x, half, axis=1) * sin_up + pltpu.roll(x, LANES - half, axis=1) * sin_dn

    for p in range(NSA_HEADS * NSA_DK // LANES):
        x = q_ref[:, p * LANES:(p + 1) * LANES]
        qc_o[:, p * LANES:(p + 1) * LANES] = (x * scale).astype(qc_o.dtype)
        qr_o[:, p * LANES:(p + 1) * LANES] = (rope(x) * scale).astype(qr_o.dtype)
    ksz_o[...] = _group_variants(rope(ks_ref[...])).astype(ksz_o.dtype)
    kwz_o[...] = _group_variants(rope(kw_ref[...])).astype(kwz_o.dtype)
    vsz_o[...] = _values_transposed(vs_ref[...]).astype(vsz_o.dtype)
    vwz_o[...] = _values_transposed(vw_ref[...]).astype(vwz_o.dtype)


def _nsa_prep(proj, pos_row, freq_col, rope_sel, one_row, bsz, t_len, *, tm):
    m = proj.shape[0]
    tiles_per_seq = t_len // tm
    seq_t = lambda i: (i // tiles_per_seq, 0, i % tiles_per_seq)
    qw = NSA_HEADS * NSA_DK
    kv0 = qw // LANES
    const = lambda i: (0, 0)
    zw = 4 * LANES
    return pl.pallas_call(
        _nsa_prep_kernel,
        grid=(m // tm,),
        in_specs=[
            pl.BlockSpec((tm, qw), lambda i: (i, 0)),
            pl.BlockSpec((tm, LANES), lambda i: (i, kv0 + 2)),
            pl.BlockSpec((tm, LANES), lambda i: (i, kv0 + 3)),
            pl.BlockSpec((tm, LANES), lambda i: (i, kv0 + 4)),
            pl.BlockSpec((tm, LANES), lambda i: (i, kv0 + 5)),
            pl.BlockSpec((1, tm), lambda i: (0, i)),
            pl.BlockSpec(freq_col.shape, const),
            pl.BlockSpec(rope_sel.shape, const),
            pl.BlockSpec((1, LANES), const),
        ],
        out_specs=[
            pl.BlockSpec((tm, qw), lambda i: (i, 0)),
            pl.BlockSpec((tm, qw), lambda i: (i, 0)),
            pl.BlockSpec((tm, zw), lambda i: (i, 0)),
            pl.BlockSpec((None, 4 * VT_ROWS, tm), seq_t),
            pl.BlockSpec((tm, zw), lambda i: (i, 0)),
            pl.BlockSpec((None, 4 * VT_ROWS, tm), seq_t),
        ],
        out_shape=[
            jax.ShapeDtypeStruct((m, qw), BF16),
            jax.ShapeDtypeStruct((m, qw), BF16),
            jax.ShapeDtypeStruct((m, zw), BF16),
            jax.ShapeDtypeStruct((bsz, 4 * VT_ROWS, t_len), BF16),
            jax.ShapeDtypeStruct((m, zw), BF16),
            jax.ShapeDtypeStruct((bsz, 4 * VT_ROWS, t_len), BF16),
        ],
        compiler_params=_cparams("parallel"),
        name="nsa_prep",
    )(proj, proj, proj, proj, proj, pos_row, freq_col, rope_sel, one_row)


def _nsa_compress_kernel(k16_ref, v16_ref, pos_ref, w1a_ref, w1b_ref, w2_ref, kcz_o, vcz_o):
    nr = k16_ref.shape[0]
    for idx, (x_ref, o_ref) in enumerate(((k16_ref, kcz_o), (v16_ref, vcz_o))):
        x = x_ref[...]
        h_a = _dot((x + pos_ref[idx, 0:1, :]).astype(BF16), w1a_ref[idx])
        h_b = _dot((x + pos_ref[idx, 1:2, :]).astype(BF16), w1b_ref[idx])
        hid = h_a + pltpu.roll(h_b, nr - 1, axis=0)
        out = _dot(_silu(hid).astype(BF16), w2_ref[idx])
        z = _group_variants(out)
        o_ref[...] = (z.T if idx == 1 else z).astype(o_ref.dtype)


def _nsa_compress(k16, v16, pos_ab, w1a, w1b, w2):
    bsz, nr, width = k16.shape
    hid = NSA_GROUPS * CMP_HIDDEN
    zw = 4 * LANES
    c3 = lambda b: (0, 0, 0)
    return pl.pallas_call(
        _nsa_compress_kernel,
        grid=(bsz,),
        in_specs=[
            pl.BlockSpec((None, nr, width), lambda b: (b, 0, 0)),
            pl.BlockSpec((None, nr, width), lambda b: (b, 0, 0)),
            pl.BlockSpec((2, 2, width), c3),
            pl.BlockSpec((2, width, hid), c3),
            pl.BlockSpec((2, width, hid), c3),
            pl.BlockSpec((2, hid, LANES), c3),
        ],
        out_specs=[
            pl.BlockSpec((None, nr, zw), lambda b: (b, 0, 0)),
            pl.BlockSpec((None, zw, nr), lambda b: (b, 0, 0)),
        ],
        out_shape=[
            jax.ShapeDtypeStruct((bsz, nr, zw), BF16),
            jax.ShapeDtypeStruct((bsz, zw, nr), BF16),
        ],
        compiler_params=_cparams("parallel"),
        name="nsa_compress",
    )(k16, v16, pos_ab, w1a, w1b, w2)


def _flash_branch(q_stacks, kz_ref, vzt_ref, lo, hi, bias_fns, kb_size):
    r_acc = lax.broadcasted_iota(jnp.int32, (VT_ROWS, 1), 0)
    even_rows = (r_acc < LANES // 2) | (r_acc == LANES)
    half_w = 2 * Q_BLOCK
    n_half = q_stacks[0].shape[0] // half_w
    units = [(g, h) for g in range(NSA_GROUPS) for h in range(n_half)]
    q_unit = [q_stacks[g][h * half_w:(h + 1) * half_w] for g, h in units]

    def body(kb, carry):
        m_in, acc = carry
        ks = pl.multiple_of(kb * kb_size, kb_size)
        k_both, vt_both, bias = [], [], []
        for g in range(NSA_GROUPS):
            k0 = 2 * g * LANES
            v0 = 2 * g * VT_ROWS
            k_both.append(jnp.concatenate([kz_ref[pl.ds(ks, kb_size), k0:k0 + LANES],
                                           kz_ref[pl.ds(ks, kb_size), k0 + LANES:k0 + 2 * LANES]], axis=0))
            vt_both.append(jnp.concatenate([vzt_ref[v0:v0 + VT_ROWS, pl.ds(ks, kb_size)],
                                            vzt_ref[v0 + VT_ROWS:v0 + 2 * VT_ROWS, pl.ds(ks, kb_size)]],
                                           axis=1))
            b = bias_fns[g](ks)
            bias.append(jnp.concatenate([b, b], axis=1))
        n_u = len(units)
        s, m_out, alpha_rows, p_cat, pv = {}, {}, {}, {}, {}

        def softmax_stage(u):
            g = units[u][0]
            p_parts, alphas, m_news = [], [], []
            for var in range(2):
                s_v = s[u][var * kb_size:(var + 1) * kb_size] + bias[g]
                m_prev = m_in[u][var]
                m_new = jnp.maximum(m_prev, jnp.max(s_v, axis=0, keepdims=True))
                alphas.append(jnp.exp2(m_prev - m_new))
                p_parts.append(jnp.exp2(s_v - m_new).astype(BF16))
                m_news.append(m_new)
            m_out[u] = tuple(m_news)
            alpha_rows[u] = jnp.where(even_rows, alphas[0], alphas[1])
            p_cat[u] = jnp.concatenate(p_parts, axis=0)

        for u in range(n_u):
            s[u] = _dot_nt(k_both[units[u][0]], q_unit[u])
        for u in range(n_u):
            softmax_stage(u)
        for u in range(n_u):
            pv[u] = _dot(vt_both[units[u][0]], p_cat[u])
        acc_out = tuple(acc[u] * alpha_rows[u] + pv[u] for u in range(n_u))
        return tuple(m_out[u] for u in range(n_u)), acc_out

    m_init = tuple((jnp.full((1, half_w), NEG_BIG, F32),) * 2 for _ in units)
    acc_init = tuple(jnp.zeros((VT_ROWS, half_w), F32) for _ in units)
    _, acc = lax.fori_loop(lo, hi, body, (m_init, acc_init))
    outs = []
    for g in range(NSA_GROUPS):
        a = jnp.concatenate(acc[g * n_half:(g + 1) * n_half], axis=1)
        inv_even = 1.0 / a[LANES:LANES + 1, :]
        inv_odd = 1.0 / a[LANES + 1:LANES + 2, :]
        outs.append(a[0:LANES, :] * jnp.where(r_acc[0:LANES] < LANES // 2, inv_even, inv_odd))
    return outs


def _flash_branch_bounded(q_exts, kz_ref, kext_ref, vzt_ref, lo, hi, bias_fns, last_bias_fns, kb_size):
    r_acc = lax.broadcasted_iota(jnp.int32, (VT_ROWS, 1), 0)
    half_w = 2 * Q_BLOCK
    n_half = q_exts[0].shape[0] // half_w
    units = [(g, h) for g in range(NSA_GROUPS) for h in range(n_half)]
    q_unit = [q_exts[g][h * half_w:(h + 1) * half_w] for g, h in units]
    n_u = len(units)

    def step(kbs, acc, fns):
        work = []
        for kb in kbs:
            ks = pl.multiple_of(jnp.maximum(kb, 0) * kb_size, kb_size)
            k_ext = kext_ref[pl.ds(ks, kb_size), :]
            for g in range(NSA_GROUPS):
                k0 = 2 * g * LANES
                v0 = 2 * g * VT_ROWS
                k_both = jnp.concatenate(
                    [jnp.concatenate([kz_ref[pl.ds(ks, kb_size), k0:k0 + LANES], k_ext], axis=1),
                     jnp.concatenate([kz_ref[pl.ds(ks, kb_size), k0 + LANES:k0 + 2 * LANES], k_ext],
                                     axis=1)], axis=0)
                vt_both = jnp.concatenate([vzt_ref[v0:v0 + VT_ROWS, pl.ds(ks, kb_size)],
                                           vzt_ref[v0 + VT_ROWS:v0 + 2 * VT_ROWS, pl.ds(ks, kb_size)]],
                                          axis=1)
                bias = None
                if fns is not None:
                    b = fns[g](kb, ks)
                    b = jnp.concatenate([b, b], axis=1)
                    bias = jnp.concatenate([b, b], axis=0)
                work += [(u, k_both, vt_both, bias) for u in range(n_u) if units[u][0] == g]
        s = [_dot_nt(k, q_unit[u]) for u, k, _, _ in work]
        p = [jnp.exp2(s_i if w[3] is None else s_i + w[3]).astype(BF16) for s_i, w in zip(s, work)]
        pv = [_dot(w[2], p_i) for p_i, w in zip(p, work)]
        out = list(acc)
        for (u, _, _, _), pv_i in zip(work, pv):
            out[u] = out[u] + pv_i
        return tuple(out)

    acc = tuple(jnp.zeros((VT_ROWS, half_w), F32) for _ in units)
    if isinstance(lo, (list, tuple)):
        acc = step(lo, acc, bias_fns)
    elif last_bias_fns is None:
        acc = lax.fori_loop(lo, hi, lambda kb, a: step([kb], a, bias_fns), acc)
    else:
        acc = lax.fori_loop(lo, hi - 1, lambda kb, a: step([kb], a, bias_fns), acc)
        acc = step([hi - 1], acc, last_bias_fns)
    outs = []
    l_min = None
    for g in range(NSA_GROUPS):
        a = jnp.concatenate(acc[g * n_half:(g + 1) * n_half], axis=1)
        l_even = a[LANES:LANES + 1, :]
        l_odd = a[LANES + 1:LANES + 2, :]
        outs.append(a[0:LANES, :] * jnp.where(r_acc[0:LANES] < LANES // 2, 1.0 / l_even, 1.0 / l_odd))
        l_g = jnp.min(jnp.minimum(l_even, l_odd))
        l_min = l_g if l_min is None else jnp.minimum(l_min, l_g)
    return outs, l_min


def _nsa_attn_kernel(qc_ref, qr_ref, gate_ref, kcz_ref, vczt_ref, ksz_ref, vszt_ref, kwz_ref, vwzt_ref,
                     aggt_ref, eselt_ref, o_ref, bias_ref, kmax_ref,
                     *, t_len, kb_size):
    n_cmp_rows = t_len // CMP_STRIDE
    n_cmp = (t_len - CMP_LEN) // CMP_STRIDE + 1
    n_sel = t_len // SEL_LEN
    n_top = min(SEL_TOPK, n_sel)
    pairs = NSA_HEADS // NSA_GROUPS // 2
    q0 = pl.program_id(1) * Q_BLOCK
    t_row = q0 + lax.broadcasted_iota(jnp.int32, (1, Q_BLOCK), 1)
    t_row_stack = q0 + (lax.broadcasted_iota(jnp.int32, (1, pairs * Q_BLOCK), 1) & (Q_BLOCK - 1))
    blk = lax.broadcasted_iota(jnp.int32, (n_sel, 1), 0)
    c_idx = lax.broadcasted_iota(jnp.int32, (n_cmp_rows, 1), 0)
    cmp_mask = ((c_idx * CMP_STRIDE + (CMP_LEN - 1)) <= t_row_stack) & (c_idx < n_cmp)
    key_pos = lax.broadcasted_iota(jnp.int32, (t_len, 1), 0)
    gate_t = _sigmoid(gate_ref[...]).T
    top_half = lax.broadcasted_iota(jnp.int32, (LANES, 1), 0) < (LANES // 2)
    hi = (q0 + (Q_BLOCK - 1)) // kb_size + 1
    win_lo = jnp.maximum(q0 - (WINDOW - 1), 0) // kb_size

    def win_bias(ks):
        kp = ks + lax.broadcasted_iota(jnp.int32, (kb_size, 1), 0)
        return jnp.where((kp <= t_row) & (kp > t_row - WINDOW), 0.0, NEG_BIG)

    def sel_bias(g):
        return lambda ks: bias_ref[g, pl.ds(ks, kb_size), :]

    @pl.when(pl.program_id(1) == 0)
    def _():
        for br, k_ref in enumerate((ksz_ref, kwz_ref)):
            for g in range(NSA_GROUPS):
                k = k_ref[:, 2 * g * LANES:(2 * g + 1) * LANES].astype(F32)
                n2 = jnp.max(jnp.sum(k * k, axis=1, keepdims=True), axis=0, keepdims=True)
                kmax_ref[br * NSA_GROUPS + g] = jnp.broadcast_to(jnp.sqrt(n2), (SUBLANES, LANES))

    qr_stacks, o_cmps, sels = [], [], []
    for g in range(NSA_GROUPS):
        lanes_g = g * pairs * LANES
        qc_stack = jnp.concatenate(
            [qc_ref[:, lanes_g + p * LANES:lanes_g + (p + 1) * LANES] for p in range(pairs)], axis=0)
        qr_stacks.append(jnp.concatenate(
            [qr_ref[:, lanes_g + p * LANES:lanes_g + (p + 1) * LANES] for p in range(pairs)], axis=0))

        o_cmp = None
        p_sum = None
        for var in range(2):
            c0 = (2 * g + var) * LANES
            s = jnp.where(cmp_mask, _dot_nt(kcz_ref[:, c0:c0 + LANES], qc_stack), -jnp.inf)
            mx = jnp.max(s, axis=0, keepdims=True)
            mx = jnp.where(mx > -jnp.inf, mx, 0.0)
            e = jnp.exp2(s - mx)
            p = e * (1.0 / jnp.maximum(jnp.sum(e, axis=0, keepdims=True), 1e-30))
            d = _dot(vczt_ref[c0:c0 + LANES, :], p.astype(BF16))
            o_cmp = d if o_cmp is None else o_cmp + d
            for pr in range(pairs):
                part = p[:, pr * Q_BLOCK:(pr + 1) * Q_BLOCK]
                p_sum = part if p_sum is None else p_sum + part
        importance = _dot_sel_lhs(aggt_ref[...], p_sum)

        cur = t_row >> 6
        causal_blk = blk <= cur
        forced = (blk == 0) | (causal_blk & (blk > cur - SEL_LOCAL))
        score = jnp.where(forced, FORCE_SCORE, jnp.where(causal_blk, importance, -1.0))
        rank = jnp.zeros((n_sel, Q_BLOCK), F32)
        for jj in range(n_sel):
            row = score[jj:jj + 1, :]
            beats = (row > score) | ((row == score) & (blk > jj))
            rank = rank + jnp.where(beats, 1.0, 0.0)
        sel = jnp.where(rank < n_top, 1.0, 0.0)
        sels.append(jnp.concatenate([sel, jnp.zeros((LANES - n_sel, Q_BLOCK), F32)], axis=0))
        o_cmps.append(o_cmp)

    lane = lax.broadcasted_iota(jnp.int32, (1, LANES), 1)
    blk_pad = lax.broadcasted_iota(jnp.int32, (LANES, 1), 0)
    q_ext_sel, q_ext_win = [], []
    for g in range(NSA_GROUPS):
        x = qr_stacks[g].astype(F32)
        sq = x * x
        tot = jnp.sum(sq, axis=1, keepdims=True)
        even = jnp.sum(jnp.where(lane < LANES // 2, sq, 0.0), axis=1, keepdims=True)
        q_norm = jnp.sqrt(jnp.maximum(even, tot - even))
        sel_c = jnp.where(blk_pad * SEL_LEN <= t_row, sels[g], 0.0).T
        sel_b = jnp.where(lane < n_sel, (sel_c - 1.0) * (-NEG_BIG), 0.0)
        sel_b = jnp.concatenate([sel_b] * pairs, axis=0)
        ext_s = jnp.where(lane == n_sel, -q_norm * kmax_ref[g][0:1, 0:1], sel_b)
        ext_w = jnp.where(lane == n_sel, -q_norm * kmax_ref[NSA_GROUPS + g][0:1, 0:1], 0.0)
        q_ext_sel.append(jnp.concatenate([qr_stacks[g], ext_s.astype(BF16)], axis=1))
        q_ext_win.append(jnp.concatenate([qr_stacks[g], ext_w.astype(BF16)], axis=1))

    sel_kb = min(2 * kb_size, t_len)

    def diag_bias(kb, ks):
        kp = ks + lax.broadcasted_iota(jnp.int32, (sel_kb, 1), 0)
        return jnp.where(kp <= t_row, 0.0, NEG_BIG)

    n_win = WINDOW // kb_size + 1
    win_blocks = [hi - n_win + j for j in range(n_win)]

    def win_bias_fast(kb, ks):
        return jnp.where(kb >= 0, win_bias(ks), NEG_BIG)

    fast_sel, l_sel = _flash_branch_bounded(q_ext_sel, ksz_ref, eselt_ref, vszt_ref, 0,
                                            (q0 + (Q_BLOCK - 1)) // sel_kb + 1, None,
                                            [diag_bias] * NSA_GROUPS, sel_kb)
    fast_win, l_win = _flash_branch_bounded(q_ext_win, kwz_ref, eselt_ref, vwzt_ref, win_blocks, None,
                                            [win_bias_fast] * NSA_GROUPS, None, kb_size)

    def robust():
        for g in range(NSA_GROUPS):
            sel_keys = _dot(eselt_ref[...], sels[g].astype(BF16))
            bias_ref[g] = jnp.where((sel_keys > 0.5) & (key_pos <= t_row), 0.0, NEG_BIG)
        o_s = _flash_branch(qr_stacks, ksz_ref, vszt_ref, 0, hi,
                            [sel_bias(g) for g in range(NSA_GROUPS)], kb_size)
        o_w = _flash_branch(qr_stacks, kwz_ref, vwzt_ref, win_lo, hi, [win_bias] * NSA_GROUPS, kb_size)
        return tuple(o_s) + tuple(o_w)

    o_all = lax.cond(jnp.minimum(l_sel, l_win) > DENOM_FLOOR,
                     lambda: tuple(fast_sel) + tuple(fast_win), robust)
    o_sels, o_wins = o_all[:NSA_GROUPS], o_all[NSA_GROUPS:]

    for g in range(NSA_GROUPS):
        lanes_g = g * pairs * LANES
        for pr in range(pairs):
            h_even = (g * pairs + pr) * 2
            cols = slice(pr * Q_BLOCK, (pr + 1) * Q_BLOCK)
            mixed = None
            for j, o_branch in enumerate((o_cmps[g], o_sels[g], o_wins[g])):
                r_e = h_even * 3 + j
                r_o = r_e + 3
                gate_rows = jnp.where(top_half, gate_t[r_e:r_e + 1, :], gate_t[r_o:r_o + 1, :])
                term = gate_rows * o_branch[:, cols]
                mixed = term if mixed is None else mixed + term
            c0 = lanes_g + pr * LANES
            o_ref[:, c0:c0 + LANES] = mixed.T.astype(o_ref.dtype)


def _nsa_attn(qc, qr, proj, kcz, vczt, ksz, vszt, kwz, vwzt, aggt, eselt, bsz, t_len, *, kb_size):
    nq = t_len // Q_BLOCK
    qw = NSA_HEADS * NSA_DK
    zw = 4 * LANES
    nr = t_len // CMP_STRIDE
    pairs = NSA_HEADS // NSA_GROUPS // 2
    row = lambda b, i: (b * nq + i, 0)
    per_b = lambda b, i: (b, 0, 0)
    return pl.pallas_call(
        functools.partial(_nsa_attn_kernel, t_len=t_len, kb_size=kb_size),
        grid=(bsz, nq),
        in_specs=[
            pl.BlockSpec((Q_BLOCK, qw), row),
            pl.BlockSpec((Q_BLOCK, qw), row),
            pl.BlockSpec((Q_BLOCK, LANES), lambda b, i: (b * nq + i, NSA_GATE_OFF // LANES)),
            pl.BlockSpec((None, nr, zw), per_b),
            pl.BlockSpec((None, zw, nr), per_b),
            pl.BlockSpec((None, t_len, zw), per_b),
            pl.BlockSpec((None, 4 * VT_ROWS, t_len), per_b),
            pl.BlockSpec((None, t_len, zw), per_b),
            pl.BlockSpec((None, 4 * VT_ROWS, t_len), per_b),
            pl.BlockSpec(aggt.shape, lambda b, i: (0, 0)),
            pl.BlockSpec(eselt.shape, lambda b, i: (0, 0)),
        ],
        out_specs=pl.BlockSpec((Q_BLOCK, qw), row),
        out_shape=jax.ShapeDtypeStruct((bsz * t_len, qw), BF16),
        scratch_shapes=[
            pltpu.VMEM((NSA_GROUPS, t_len, Q_BLOCK), F32),
            pltpu.VMEM((2 * NSA_GROUPS, SUBLANES, LANES), F32),
        ],
        compiler_params=_cparams("parallel", "arbitrary"),
        name="nsa_attn",
    )(qc, qr, proj, kcz, vczt, ksz, vszt, kwz, vwzt, aggt, eselt)


def _expand_rows(first_row, n_heads, width):
    e = np.zeros((LANES, n_heads * width), np.float32)
    for h in range(n_heads):
        e[first_row + h, h * width:(h + 1) * width] = 1.0
    return e


def _nsa_constants(t_len):
    n_cmp = (t_len - CMP_LEN) // CMP_STRIDE + 1
    n_sel = t_len // SEL_LEN
    nr = t_len // CMP_STRIDE
    c0 = np.arange(n_cmp)[:, None] * CMP_STRIDE
    s0 = np.arange(n_sel)[None, :] * SEL_LEN
    overlap = np.clip(np.minimum(c0 + CMP_LEN, s0 + SEL_LEN) - np.maximum(c0, s0), 0, None) / CMP_LEN
    aggt = np.zeros((n_sel, nr), np.float32)
    aggt[:, :n_cmp] = overlap.T
    eselt = np.zeros((t_len, LANES), np.float32)
    eselt[np.arange(t_len), np.arange(t_len) // SEL_LEN] = 1.0
    eselt[:, n_sel] = 1.0
    half = ROPE_DIM // 2
    inv_freq = (ROPE_THETA ** (-np.arange(half) / half)).astype(np.float32)
    d = np.arange(LANES) % NSA_DK
    rope_sel = np.zeros((LANES, 3 * LANES), np.float32)
    for lane in range(LANES):
        if d[lane] < ROPE_DIM:
            f = d[lane] % half
            rope_sel[f, lane] = 1.0
            if d[lane] >= half:
                rope_sel[half + f, LANES + lane] = 1.0
            else:
                rope_sel[half + f, 2 * LANES + lane] = -1.0
    one_row = (d >= ROPE_DIM).astype(np.float32)[None]
    return aggt, eselt, inv_freq[:, None], rope_sel, one_row


def _pad_row(vec, offset):
    return jnp.zeros((1, LANES), F32).at[0, offset:offset + vec.shape[0]].set(vec.astype(F32))


def _hybrid_layer(h, bsz, t_len, w_in, gdn_conv_w, gdn_a_log, gdn_dt_bias, gdn_norm_w, ssd_conv_w,
                  ssd_conv_b, ssd_a_log, ssd_dt_bias, ssd_d, ssd_norm_w, w_out, ln_g, ln_b, cfg):
    b_off = GDN_CONV_DIM + GDN_V_W
    z_off = b_off + 2 * GDN_V_HEADS
    dt_off = z_off + SSD_D_INNER + SSD_CONV_DIM
    w_pad = jnp.concatenate(
        [w_in[:, :b_off], w_in[:, z_off:dt_off], w_in[:, b_off:z_off], w_in[:, dt_off:],
         jnp.zeros((D_MODEL, HYB_PAD - w_in.shape[1]), w_in.dtype)], axis=1).astype(BF16)
    conv_w = jnp.concatenate([gdn_conv_w, ssd_conv_w], axis=1).astype(F32)
    conv_b = jnp.concatenate([jnp.zeros((GDN_CONV_DIM,), F32), ssd_conv_b.astype(F32)])[None]
    proj = _proj_conv(h, w_pad, conv_w, conv_b, t_len, tm=min(cfg["proj_tm"], t_len), tn=cfg["hyb_tn"])
    o_a = _gdn(proj, bsz, t_len, _pad_row(gdn_dt_bias, GDN_V_HEADS),
               _pad_row(gdn_a_log, GDN_V_HEADS), gdn_norm_w[None].astype(F32),
               jnp.asarray(_expand_rows(0, GDN_V_HEADS, GDN_DV)).astype(BF16),
               jnp.asarray(_expand_rows(GDN_V_HEADS, GDN_V_HEADS, GDN_DV)).astype(BF16), tc=cfg["gdn_tc"])
    o_b = _ssd(proj, bsz, t_len, _pad_row(ssd_dt_bias, 16),
               _pad_row(ssd_a_log, 16), jnp.repeat(ssd_d, SSD_HEADDIM)[None], ssd_norm_w[None],
               jnp.asarray(_expand_rows(16, SSD_HEADS, SSD_HEADDIM)).astype(BF16),
               n_sub=cfg["ssd_sub"])
    w_out_b = w_out.astype(BF16)
    return _out_ln([o_a, o_b], [w_out_b[:GDN_V_W], w_out_b[GDN_V_W:]], h, ln_g, ln_b, tm=cfg["out_tm"])


def _nsa_layer(h, pos_row, bsz, t_len, w_in, cmp_pos, cmp_w1, cmp_w2, w_out, ln_g, ln_b, cfg):
    aggt, eselt, freq_col, rope_sel, one_row = _nsa_constants(t_len)
    w_pad = jnp.concatenate(
        [w_in, jnp.zeros((D_MODEL, NSA_PAD - w_in.shape[1]), w_in.dtype)], axis=1).astype(BF16)
    proj = _proj(h, w_pad, tm=cfg["proj_tm"], tn=cfg["nsa_tn"])
    qc, qr, ksz, vszt, kwz, vwzt = _nsa_prep(proj, pos_row, jnp.asarray(freq_col),
                                             jnp.asarray(rope_sel).astype(BF16), jnp.asarray(one_row),
                                             bsz, t_len, tm=cfg["prep_tm"])
    qw = NSA_HEADS * NSA_DK
    nr = t_len // CMP_STRIDE
    width = CMP_STRIDE * LANES
    k16 = proj[:, qw:qw + LANES].reshape(bsz, nr, width)
    v16 = proj[:, qw + LANES:qw + 2 * LANES].reshape(bsz, nr, width)
    pos2 = jnp.broadcast_to(cmp_pos[:, :, None, :], (2, CMP_LEN, NSA_GROUPS, NSA_DK)).reshape(2, 2, width)
    w1 = cmp_w1.reshape(2, 2, CMP_STRIDE, NSA_DK, CMP_HIDDEN)
    eye_g = jnp.eye(NSA_GROUPS, dtype=w1.dtype)
    w1x = jnp.einsum("ksjdh,ge->ksjgdeh", w1, eye_g).reshape(
        2, 2, width, NSA_GROUPS * CMP_HIDDEN).astype(BF16)
    w2x = jnp.einsum("khd,ge->kghed", cmp_w2, eye_g).reshape(
        2, NSA_GROUPS * CMP_HIDDEN, NSA_GROUPS * NSA_DK).astype(BF16)
    kcz, vczt = _nsa_compress(k16, v16, pos2, w1x[:, 0], w1x[:, 1], w2x)
    o = _nsa_attn(qc, qr, proj, kcz, vczt, ksz.reshape(bsz, t_len, -1), vszt,
                  kwz.reshape(bsz, t_len, -1), vwzt,
                  jnp.asarray(aggt).astype(BF16), jnp.asarray(eselt).astype(BF16),
                  bsz, t_len, kb_size=cfg["attn_kb"])
    return _out_ln([o], [w_out.astype(BF16)], h, ln_g, ln_b, tm=cfg["out_tm"])


def _config(m, t_len):
    return dict(ffn_tm=min(1024, m), ffn_tf=256, proj_tm=min(512, m), hyb_tn=512, nsa_tn=NSA_PAD,
                out_tm=min(1024, m), gdn_tc=min(256, t_len), ssd_sub=min(4, t_len // SSD_CHUNK), prep_tm=min(512, m),
                attn_kb=min(256, t_len))


def kernel(x, positions, ln_g, ln_b, ffn_w_in, ffn_w_out, hyb_w_in, gdn_conv_w, gdn_a_log, gdn_dt_bias, gdn_norm_w, ssd_conv_w, ssd_conv_b, ssd_a_log, ssd_dt_bias, ssd_d, ssd_norm_w, hyb_w_out, nsa_w_in, nsa_cmp_pos, nsa_cmp_w1, nsa_cmp_w2, nsa_w_out):
    bsz, t_len, _ = x.shape
    m = bsz * t_len
    cfg = _config(m, t_len)
    h = x.reshape(m, D_MODEL)
    pos_row = positions.reshape(1, m)
    w_in_b = ffn_w_in.astype(BF16)
    w_out_b = ffn_w_out.astype(BF16)
    ln_g = ln_g[:, :, None, :]
    ln_b = ln_b[:, :, None, :]
    for layer in range(DEPTH):
        i = layer // 2
        h = _ffn_ln(h, w_in_b, w_out_b, ln_g[layer, 0], ln_b[layer, 0], layer, 0,
                    tm=cfg["ffn_tm"], tf=cfg["ffn_tf"])
        if layer % 2 == 0:
            h = _hybrid_layer(h, bsz, t_len, hyb_w_in[i], gdn_conv_w[i], gdn_a_log[i], gdn_dt_bias[i],
                              gdn_norm_w[i], ssd_conv_w[i], ssd_conv_b[i], ssd_a_log[i], ssd_dt_bias[i],
                              ssd_d[i], ssd_norm_w[i], hyb_w_out[i], ln_g[layer, 1], ln_b[layer, 1], cfg)
        else:
            h = _nsa_layer(h, pos_row, bsz, t_len, nsa_w_in[i], nsa_cmp_pos[i], nsa_cmp_w1[i],
                           nsa_cmp_w2[i], nsa_w_out[i], ln_g[layer, 1], ln_b[layer, 1], cfg)
        h = _ffn_ln(h, w_in_b, w_out_b, ln_g[layer, 2], ln_b[layer, 2], layer, 1,
                    tm=cfg["ffn_tm"], tf=cfg["ffn_tf"])
    return h.reshape(bsz, t_len, D_MODEL)
```

```python
import functools

import numpy as np
import jax
import jax.numpy as jnp
from jax import lax
from jax.experimental import pallas as pl
from jax.experimental.pallas import tpu as pltpu

F32 = jnp.float32
BF16 = jnp.bfloat16

D_MODEL = 1024
DEPTH = 4
D_FF = 2816
DEEPNORM_ALPHA = (2.0 * DEPTH) ** 0.25
LN_EPS = 1e-5
RMS_EPS = 1e-6
CONV_K = 4

GDN_QK_HEADS = 4
GDN_V_HEADS = 8
GDN_DK = 128
GDN_DV = 128
GDN_CHUNK = 64
GDN_QK_W = GDN_QK_HEADS * GDN_DK
GDN_V_W = GDN_V_HEADS * GDN_DV
GDN_CONV_DIM = 2 * GDN_QK_W + GDN_V_W

SSD_D_INNER = D_MODEL
SSD_HEADDIM = 64
SSD_HEADS = SSD_D_INNER // SSD_HEADDIM
SSD_GROUPS = 2
SSD_STATE = 128
SSD_CHUNK = 128
SSD_BC_W = SSD_GROUPS * SSD_STATE
SSD_CONV_DIM = SSD_D_INNER + 2 * SSD_BC_W

NSA_HEADS = 16
NSA_GROUPS = 2
NSA_DK = 64
NSA_DV = 64
ROPE_DIM = NSA_DK // 4
ROPE_THETA = 500000.0
CMP_LEN = 32
CMP_STRIDE = 16
CMP_HIDDEN = 256
SEL_LEN = 64
SEL_TOPK = 8
SEL_LOCAL = 2
FORCE_SCORE = 1e4
WINDOW = 512
Q_BLOCK = 128

LANES = 128
SUBLANES = 8
VMEM_LIMIT_BYTES = 48 * 1024 * 1024

HYB_SMALL_OFF = GDN_CONV_DIM + GDN_V_W + SSD_D_INNER + SSD_CONV_DIM
HYB_PAD = HYB_SMALL_OFF + LANES
NSA_GATE_OFF = NSA_HEADS * NSA_DK + 3 * NSA_GROUPS * (NSA_DK + NSA_DV)
NSA_PAD = NSA_GATE_OFF + LANES

NEG_BIG = -1e30
LOG2E = 1.4426950408889634
DENOM_FLOOR = 2.0 ** -100
VT_ROWS = LANES + 16


def _cparams(*sem):
    return pltpu.CompilerParams(dimension_semantics=sem, vmem_limit_bytes=VMEM_LIMIT_BYTES)


def _sigmoid(v):
    return 0.5 + 0.5 * jnp.tanh(0.5 * v)


def _silu(v):
    h = 0.5 * v
    return h + h * jnp.tanh(h)


def _softplus(v):
    return jnp.maximum(v, 0.0) + jnp.log1p(jnp.exp(-jnp.abs(v)))


def _layer_norm(y, g, b):
    mu = jnp.mean(y, axis=-1, keepdims=True)
    d = y - mu
    var = jnp.mean(d * d, axis=-1, keepdims=True)
    return d * lax.rsqrt(var + LN_EPS) * g + b


def _dot(a, b):
    return jnp.dot(a, b, preferred_element_type=F32)


def _split3(x):
    hi = x.astype(BF16)
    r1 = x - hi.astype(F32)
    mid = r1.astype(BF16)
    lo = (r1 - mid.astype(F32)).astype(BF16)
    return hi, mid, lo


def _dot_sel_rhs(x, sel):
    sel = sel.astype(BF16)
    hi, mid, lo = _split3(x)
    return _dot(hi, sel) + _dot(mid, sel) + _dot(lo, sel)


def _dot_sel_lhs(sel, x):
    sel = sel.astype(BF16)
    hi, mid, lo = _split3(x)
    return _dot(sel, hi) + _dot(sel, mid) + _dot(sel, lo)


def _dot_nt(a, b):
    return lax.dot_general(a, b, (((1,), (1,)), ((), ())), preferred_element_type=F32)


def _ffn_ln_kernel(x_ref, wi_ref, wo_ref, g_ref, b_ref, o_ref, *, tf):
    x = x_ref[...]
    xb = x.astype(BF16)
    acts = []
    for c in range(D_FF // tf):
        gate = _dot(xb, wi_ref[:, c * tf:(c + 1) * tf])
        up = _dot(xb, wi_ref[:, D_FF + c * tf:D_FF + (c + 1) * tf])
        acts.append((_silu(gate) * up).astype(BF16))
    mix = _dot(jnp.concatenate(acts, axis=1), wo_ref[...])
    y = DEEPNORM_ALPHA * x + 0.5 * mix
    o_ref[...] = _layer_norm(y, g_ref[...], b_ref[...])


def _resident(shape):
    return pl.BlockSpec(shape, lambda *_: (0,) * len(shape), pipeline_mode=pl.Buffered(1))


def _ffn_ln(h, w_in, w_out, g, b, layer, which, *, tm, tf):
    m = h.shape[0]
    pick = lambda i: (layer, which, 0, 0)
    return pl.pallas_call(
        functools.partial(_ffn_ln_kernel, tf=tf),
        grid=(m // tm,),
        in_specs=[
            pl.BlockSpec((tm, D_MODEL), lambda i: (i, 0)),
            pl.BlockSpec((None, None) + w_in.shape[2:], pick, pipeline_mode=pl.Buffered(1)),
            pl.BlockSpec((None, None) + w_out.shape[2:], pick, pipeline_mode=pl.Buffered(1)),
            _resident((1, D_MODEL)),
            _resident((1, D_MODEL)),
        ],
        out_specs=pl.BlockSpec((tm, D_MODEL), lambda i: (i, 0)),
        out_shape=jax.ShapeDtypeStruct((m, D_MODEL), F32),
        compiler_params=_cparams("parallel"),
        name="ffn_ln",
    )(h, w_in, w_out, g, b)


def _proj_kernel(x_ref, w_ref, o_ref, *, tn):
    xb = x_ref[...].astype(BF16)
    for c in range(w_ref.shape[1] // tn):
        o_ref[:, c * tn:(c + 1) * tn] = _dot(xb, w_ref[:, c * tn:(c + 1) * tn])


def _proj(h, w, *, tm, tn):
    m, k = h.shape
    n = w.shape[1]
    return pl.pallas_call(
        functools.partial(_proj_kernel, tn=tn),
        grid=(m // tm,),
        in_specs=[
            pl.BlockSpec((tm, k), lambda i: (i, 0)),
            _resident(w.shape),
        ],
        out_specs=pl.BlockSpec((tm, n), lambda i: (i, 0)),
        out_shape=jax.ShapeDtypeStruct((m, n), F32),
        compiler_params=_cparams("parallel"),
        name="in_proj",
    )(h, w)


def _proj_conv_kernel(x_ref, w_ref, cw_ref, cb_ref, o_ref, hist, *, chunks, tiles_per_seq):
    tm = x_ref.shape[0]

    @pl.when(pl.program_id(0) % tiles_per_seq == 0)
    def _():
        hist[...] = jnp.zeros(hist.shape, F32)

    xb = x_ref[...].astype(BF16)
    row8 = lax.broadcasted_iota(jnp.int32, (SUBLANES, 1), 0)
    for col, width, ccol in chunks:
        y = _dot(xb, w_ref[:, col:col + width])
        if ccol is not None:
            prev = hist[:, ccol:ccol + width]
            acc = cb_ref[:, ccol:ccol + width] + cw_ref[CONV_K - 1:CONV_K, ccol:ccol + width] * y
            for s in range(1, CONV_K):
                y_s = pltpu.roll(y, s, axis=0)
                head = jnp.where(row8 < s, pltpu.roll(prev, s, axis=0), y_s[0:SUBLANES])
                y_s = jnp.concatenate([head, y_s[SUBLANES:]], axis=0)
                acc = acc + cw_ref[CONV_K - 1 - s:CONV_K - s, ccol:ccol + width] * y_s
            hist[:, ccol:ccol + width] = y[tm - SUBLANES:tm]
            y = _silu(acc)
        o_ref[:, col:col + width] = y


def _proj_conv(h, w, cw, cb, t_len, *, tm, tn):
    m, k = h.shape
    n = w.shape[1]
    conv_w = cw.shape[1]
    ssd_lo = GDN_CONV_DIM + GDN_V_W + SSD_D_INNER
    chunks = []
    for col in range(0, n, tn):
        width = min(tn, n - col)
        if col + width <= GDN_CONV_DIM:
            ccol = col
        elif ssd_lo <= col and col + width <= ssd_lo + SSD_CONV_DIM:
            ccol = col - ssd_lo + GDN_CONV_DIM
        else:
            ccol = None
        chunks.append((col, width, ccol))
    return pl.pallas_call(
        functools.partial(_proj_conv_kernel, chunks=tuple(chunks), tiles_per_seq=t_len // tm),
        grid=(m // tm,),
        in_specs=[
            pl.BlockSpec((tm, k), lambda i: (i, 0)),
            _resident(w.shape),
            _resident(cw.shape),
            _resident(cb.shape),
        ],
        out_specs=pl.BlockSpec((tm, n), lambda i: (i, 0)),
        out_shape=jax.ShapeDtypeStruct((m, n), F32),
        scratch_shapes=[pltpu.VMEM((SUBLANES, conv_w), F32)],
        compiler_params=_cparams("arbitrary"),
        name="in_proj_conv",
    )(h, w, cw, cb)


def _out_ln_kernel(*refs, n_parts):
    a_refs = refs[:n_parts]
    w_refs = refs[n_parts:2 * n_parts]
    h_ref, g_ref, b_ref, o_ref = refs[2 * n_parts:]
    mix = _dot(a_refs[0][...], w_refs[0][...])
    for a_ref, w_ref in zip(a_refs[1:], w_refs[1:]):
        mix = mix + _dot(a_ref[...], w_ref[...])
    y = DEEPNORM_ALPHA * h_ref[...] + mix
    o_ref[...] = _layer_norm(y, g_ref[...], b_ref[...])


def _out_ln(parts, weights, h, g, b, *, tm):
    m = h.shape[0]
    n_parts = len(parts)
    in_specs = [pl.BlockSpec((tm, a.shape[1]), lambda i: (i, 0)) for a in parts]
    in_specs += [pl.BlockSpec(w.shape, lambda i: (0, 0)) for w in weights]
    in_specs += [
        pl.BlockSpec((tm, D_MODEL), lambda i: (i, 0)),
        pl.BlockSpec((1, D_MODEL), lambda i: (0, 0)),
        pl.BlockSpec((1, D_MODEL), lambda i: (0, 0)),
    ]
    return pl.pallas_call(
        functools.partial(_out_ln_kernel, n_parts=n_parts),
        grid=(m // tm,),
        in_specs=in_specs,
        out_specs=pl.BlockSpec((tm, D_MODEL), lambda i: (i, 0)),
        out_shape=jax.ShapeDtypeStruct((m, D_MODEL), F32),
        compiler_params=_cparams("parallel"),
        name="out_proj_ln",
    )(*parts, *weights, h, g, b)


def _ssd_kernel(z_ref, x_ref, bm_ref, cm_ref, sm_ref, dtb_ref, alog_ref,
                dskip_ref, nw_ref, eh_ref, o_ref, state, *, n_sub):
    @pl.when(pl.program_id(1) == 0)
    def _():
        state[...] = jnp.zeros_like(state)

    for sub in range(n_sub):
        _ssd_chunk(pl.ds(sub * SSD_CHUNK, SSD_CHUNK), z_ref, x_ref, bm_ref, cm_ref, sm_ref, dtb_ref,
                   alog_ref, dskip_ref, nw_ref, eh_ref, o_ref, state)


def _ssd_chunk(rows, z_ref, x_ref, bm_ref, cm_ref, sm_ref, dtb_ref, alog_ref,
               dskip_ref, nw_ref, eh_ref, o_ref, state):
    L = SSD_CHUNK
    W = SSD_D_INNER
    GW = W // SSD_GROUPS
    N = SSD_STATE

    xs = x_ref[rows, :]
    bm = bm_ref[rows, :].astype(BF16)
    cm = cm_ref[rows, :].astype(BF16)

    lane = lax.broadcasted_iota(jnp.int32, (1, LANES), 1)
    dt_lanes = (lane >= 16) & (lane < 16 + SSD_HEADS)
    dt_full = jnp.where(dt_lanes, _softplus(sm_ref[rows, :] + dtb_ref[...]), 0.0)
    adt_full = dt_full * (-jnp.exp(alog_ref[...]))
    r_i = lax.broadcasted_iota(jnp.int32, (L, L), 0)
    c_i = lax.broadcasted_iota(jnp.int32, (L, L), 1)
    tril = r_i >= c_i
    acs_full = _dot_sel_lhs(jnp.where(tril, 1.0, 0.0), adt_full)
    acs_t = acs_full.T
    eh = eh_ref[...]
    dt_exp = _dot_sel_rhs(dt_full, eh)
    acs_exp = _dot_sel_rhs(acs_full, eh)
    a_last = acs_exp[L - 1:L, :]
    xdt = xs * dt_exp
    xdec = (xdt * jnp.exp(a_last - acs_exp)).astype(BF16)
    e_acs = jnp.exp(acs_exp)
    e_last = jnp.exp(a_last)
    lane_w = lax.broadcasted_iota(jnp.int32, (1, W), 1)
    lo_half = (lane_w & (LANES - 1)) < SSD_HEADDIM
    xdt_b = xdt.astype(BF16)
    xdt_lo = jnp.where(lo_half, xdt_b, jnp.zeros_like(xdt_b))
    xdt_hi = jnp.where(lo_half, jnp.zeros_like(xdt_b), xdt_b)

    heads_per_group = SSD_HEADS // SSD_GROUPS
    y_parts = []
    for g in range(SSD_GROUPS):
        bg = bm[:, g * N:(g + 1) * N]
        cg = cm[:, g * N:(g + 1) * N]
        cb = _dot_nt(cg, bg)
        s_g = state[g]
        y_off = _dot(cg, s_g.astype(BF16)) * e_acs[:, g * GW:(g + 1) * GW]
        for pr in range(heads_per_group // 2):
            y_pair = None
            for var in range(2):
                h = g * heads_per_group + 2 * pr + var
                col = acs_full[:, 16 + h:17 + h]
                row = acs_t[16 + h:17 + h, :]
                seg = jnp.exp(jnp.where(tril, col - row, -jnp.inf))
                mh = (cb * seg).astype(BF16)
                src = xdt_lo if var == 0 else xdt_hi
                lanes0 = (g * heads_per_group + 2 * pr) * SSD_HEADDIM
                d = _dot(mh, src[:, lanes0:lanes0 + LANES])
                y_pair = d if y_pair is None else y_pair + d
            off = pr * LANES
            y_parts.append(y_pair + y_off[:, off:off + LANES])
        upd = lax.dot_general(bg, xdec[:, g * GW:(g + 1) * GW], (((0,), (0,)), ((), ())),
                              preferred_element_type=F32)
        state[g] = s_g * e_last[:, g * GW:(g + 1) * GW] + upd
    y = jnp.concatenate(y_parts, axis=1) + xs * dskip_ref[...]
    y = y * _silu(z_ref[rows, :])
    outs = []
    for g in range(SSD_GROUPS):
        yg = y[:, g * GW:(g + 1) * GW]
        outs.append(yg * lax.rsqrt(jnp.mean(yg * yg, axis=-1, keepdims=True) + RMS_EPS))
    o_ref[rows, :] = (jnp.concatenate(outs, axis=1) * nw_ref[...]).astype(o_ref.dtype)


def _ssd(proj, bsz, t_len, dtb_row, alog_row, dskip_row, nw_row, eh, *, n_sub):
    L = n_sub * SSD_CHUNK
    nc = t_len // L
    W = SSD_D_INNER
    z_blk = (GDN_CONV_DIM + GDN_V_W) // W
    x_blk = (GDN_CONV_DIM + GDN_V_W + W) // W
    b_blk = (GDN_CONV_DIM + GDN_V_W + 2 * W) // SSD_BC_W
    sm_blk = HYB_SMALL_OFF // LANES
    row = lambda b, c: b * nc + c
    const = lambda b, c: (0, 0)
    return pl.pallas_call(
        functools.partial(_ssd_kernel, n_sub=n_sub),
        grid=(bsz, nc),
        in_specs=[
            pl.BlockSpec((L, W), lambda b, c: (row(b, c), z_blk)),
            pl.BlockSpec((L, W), lambda b, c: (row(b, c), x_blk)),
            pl.BlockSpec((L, SSD_BC_W), lambda b, c: (row(b, c), b_blk)),
            pl.BlockSpec((L, SSD_BC_W), lambda b, c: (row(b, c), b_blk + 1)),
            pl.BlockSpec((L, LANES), lambda b, c: (row(b, c), sm_blk)),
            pl.BlockSpec((1, LANES), const),
            pl.BlockSpec((1, LANES), const),
            pl.BlockSpec((1, W), const),
            pl.BlockSpec((1, W), const),
            pl.BlockSpec((LANES, W), const),
        ],
        out_specs=pl.BlockSpec((L, W), lambda b, c: (row(b, c), 0)),
        out_shape=jax.ShapeDtypeStruct((bsz * t_len, W), BF16),
        scratch_shapes=[
            pltpu.VMEM((SSD_GROUPS, SSD_STATE, W // SSD_GROUPS), F32),
        ],
        compiler_params=_cparams("parallel", "arbitrary"),
        name="ssd",
    )(proj, proj, proj, proj, proj, dtb_row, alog_row, dskip_row, nw_row, eh)


def _unit_lower_inverse_minus_eye(a, r_i, c_i):
    n = range(len(a))
    diag8 = (r_i >> 3) == (c_i >> 3)
    x = [jnp.where(diag8, -a[i], 0.0) for i in n]
    x_b = [x[i].astype(BF16) for i in n]
    x2 = [_dot(x_b[i], x_b[i]) for i in n]
    x2_b = [x2[i].astype(BF16) for i in n]
    x4 = [_dot(x2_b[i], x2_b[i]) for i in n]
    e = [x[i] + x2[i] + _dot(x_b[i], x2_b[i]) for i in n]
    e = [e[i] + x4[i] + _dot(e[i].astype(BF16), x4[i].astype(BF16)) for i in n]
    for sh in (3, 4, 5):
        same_pair = (r_i >> (sh + 1)) == (c_i >> (sh + 1))
        lower_left = (((r_i >> sh) & 1) == 1) & (((c_i >> sh) & 1) == 0)
        ms = [jnp.where(same_pair & lower_left, a[i], 0.0) for i in n]
        y = [ms[i] + _dot(ms[i].astype(BF16), e[i].astype(BF16)) for i in n]
        e = [e[i] - y[i] - _dot(e[i].astype(BF16), y[i].astype(BF16)) for i in n]
    return e


def _stack_heads(x, r0, l0):
    return jnp.concatenate([x[r0:r0 + GDN_CHUNK, l0:l0 + GDN_DV],
                            x[r0:r0 + GDN_CHUNK, l0 + GDN_DV:l0 + 2 * GDN_DV]], axis=0)


def _own_head_block(x):
    return jnp.concatenate([x[0:GDN_CHUNK, 0:GDN_DV], x[GDN_CHUNK:, GDN_DV:]], axis=0)


def _gdn_kernel(q_ref, k_ref, v_ref, z_ref, sm_ref, dtb_ref, alog_ref, nw_ref,
                eb_ref, ea_ref, o_ref, state, *, tc):
    C = GDN_CHUNK
    DK = GDN_DK
    DV = GDN_DV
    P = 2 * C

    @pl.when(pl.program_id(1) == 0)
    def _():
        state[...] = jnp.zeros_like(state)

    lane = lax.broadcasted_iota(jnp.int32, (1, LANES), 1)
    sm = sm_ref[...]
    beta_full = jnp.where(lane < GDN_V_HEADS, _sigmoid(sm), 0.0)
    g_lanes = (lane >= GDN_V_HEADS) & (lane < 2 * GDN_V_HEADS)
    g_full = jnp.where(g_lanes, -jnp.exp(alog_ref[...]) * _softplus(sm + dtb_ref[...]), 0.0)
    rt = lax.broadcasted_iota(jnp.int32, (tc, tc), 0)
    ct = lax.broadcasted_iota(jnp.int32, (tc, tc), 1)
    same_chunk_tril = ((rt >> 6) == (ct >> 6)) & (rt >= ct)
    gc_full = _dot_sel_lhs(jnp.where(same_chunk_tril, 1.0, 0.0), g_full)
    b_exp = _dot_sel_rhs(beta_full, eb_ref[...])
    g_exp = _dot_sel_rhs(gc_full, ea_ref[...])
    eg_exp = jnp.exp(g_exp)

    r_i = lax.broadcasted_iota(jnp.int32, (P, P), 0)
    c_i = lax.broadcasted_iota(jnp.int32, (P, P), 1)
    same_head = (r_i >> 6) == (c_i >> 6)
    tril = same_head & (r_i >= c_i)
    strict = same_head & (r_i > c_i)
    head0_rows = lax.broadcasted_iota(jnp.int32, (P, 1), 0) < C
    nw = nw_ref[...]

    n_chunks = tc // C
    units = [(hq, ci) for ci in range(n_chunks) for hq in range(GDN_QK_HEADS)]
    qn, kn = [], []
    for hq in range(GDN_QK_HEADS):
        qh = q_ref[:, hq * DK:(hq + 1) * DK]
        kh = k_ref[:, hq * DK:(hq + 1) * DK]
        qn.append(qh * lax.rsqrt(jnp.sum(qh * qh, axis=-1, keepdims=True) + 1e-6) * (DK ** -0.5))
        kn.append(kh * lax.rsqrt(jnp.sum(kh * kh, axis=-1, keepdims=True) + 1e-6))
    pre = {}
    a_low = []
    for hq, ci in units:
        r0 = ci * C
        l0 = hq * 2 * DV
        q_c = qn[hq][r0:r0 + C]
        k_c = kn[hq][r0:r0 + C]
        q2 = jnp.concatenate([q_c, q_c], axis=0)
        k2 = jnp.concatenate([k_c, k_c], axis=0)
        k2_b = k2.astype(BF16)
        kk = _dot_nt(k2_b, k2_b)
        qk = _dot_nt(q2.astype(BF16), k2_b)
        gcol = _stack_heads(g_exp, r0, l0)
        bcol = _stack_heads(b_exp, r0, l0)
        egc = _stack_heads(eg_exp, r0, l0)
        dmat = jnp.exp(jnp.where(tril, gcol - gcol.T, -jnp.inf))
        a_low.append(jnp.where(strict, kk * bcol * dmat, 0.0))
        g_last = jnp.where(head0_rows, gcol[C - 1:C], gcol[P - 1:P])
        pre[hq, ci] = dict(
            vb=_stack_heads(v_ref, r0, l0) * bcol,
            kb=k2 * (bcol * egc),
            qd=(q2 * egc).astype(BF16),
            attn=(qk * dmat).astype(BF16),
            k_t=k_c.T.astype(BF16),
            v_scale=jnp.exp(g_last - gcol),
            decay_cat=jnp.concatenate([egc[C - 1:C], egc[P - 1:P]], axis=1))
    e_all = _unit_lower_inverse_minus_eye(a_low, r_i, c_i)
    for (hq, ci), e in zip(units, e_all):
        d = pre[hq, ci]
        e_b = e.astype(BF16)
        d["u"] = d["vb"] + _dot(e_b, d["vb"].astype(BF16))
        d["w"] = (d["kb"] + _dot(e_b, d["kb"].astype(BF16))).astype(BF16)

    s_cat = [state[hq] for hq in range(GDN_QK_HEADS)]
    for ci in range(n_chunks):
        r0 = ci * C
        for hq in range(GDN_QK_HEADS):
            d = pre[hq, ci]
            l0 = hq * 2 * DV
            s_b = s_cat[hq].astype(BF16)
            v_new = d["u"] - _own_head_block(_dot(d["w"], s_b))
            o2 = _own_head_block(_dot(d["qd"], s_b)) + _dot(d["attn"], v_new.astype(BF16))
            v_dec = (v_new * d["v_scale"]).astype(BF16)
            v_dec_cat = jnp.concatenate([v_dec[0:C], v_dec[C:]], axis=1)
            s_cat[hq] = s_cat[hq] * d["decay_cat"] + _dot(d["k_t"], v_dec_cat)
            o_n = o2 * lax.rsqrt(jnp.mean(o2 * o2, axis=-1, keepdims=True) + RMS_EPS)
            res = (o_n * nw * _silu(_stack_heads(z_ref, r0, l0))).astype(o_ref.dtype)
            o_ref[r0:r0 + C, l0:l0 + DV] = res[0:C]
            o_ref[r0:r0 + C, l0 + DV:l0 + 2 * DV] = res[C:]
    for hq in range(GDN_QK_HEADS):
        state[hq] = s_cat[hq]


def _gdn(proj, bsz, t_len, dtb_row, alog_row, nw_row, eb, ea, *, tc):
    nt = t_len // tc
    row = lambda b, t: b * nt + t
    const = lambda b, t: (0, 0)
    sm_blk = HYB_SMALL_OFF // LANES
    return pl.pallas_call(
        functools.partial(_gdn_kernel, tc=tc),
        grid=(bsz, nt),
        in_specs=[
            pl.BlockSpec((tc, GDN_QK_W), lambda b, t: (row(b, t), 0)),
            pl.BlockSpec((tc, GDN_QK_W), lambda b, t: (row(b, t), 1)),
            pl.BlockSpec((tc, GDN_V_W), lambda b, t: (row(b, t), 1)),
            pl.BlockSpec((tc, GDN_V_W), lambda b, t: (row(b, t), 2)),
            pl.BlockSpec((tc, LANES), lambda b, t: (row(b, t), sm_blk)),
            pl.BlockSpec((1, LANES), const),
            pl.BlockSpec((1, LANES), const),
            pl.BlockSpec((1, GDN_DV), const),
            pl.BlockSpec((LANES, GDN_V_W), const),
            pl.BlockSpec((LANES, GDN_V_W), const),
        ],
        out_specs=pl.BlockSpec((tc, GDN_V_W), lambda b, t: (row(b, t), 0)),
        out_shape=jax.ShapeDtypeStruct((bsz * t_len, GDN_V_W), BF16),
        scratch_shapes=[
            pltpu.VMEM((GDN_QK_HEADS, GDN_DK, 2 * GDN_DV), F32),
        ],
        compiler_params=_cparams("parallel", "arbitrary"),
        name="gdn",
    )(proj, proj, proj, proj, proj, dtb_row, alog_row, nw_row, eb, ea)


def _group_variants(x):
    lane = lax.broadcasted_iota(jnp.int32, (1, LANES), 1)
    lo = lane < (LANES // 2)
    xr = pltpu.roll(x, LANES // 2, axis=1)
    zero = jnp.zeros_like(x)
    return jnp.concatenate([jnp.where(lo, x, zero), jnp.where(lo, zero, xr),
                            jnp.where(lo, xr, zero), jnp.where(lo, zero, x)], axis=1)


def _values_transposed(v):
    zt = _group_variants(v).T
    extra_r = lax.broadcasted_iota(jnp.int32, (VT_ROWS - LANES, v.shape[0]), 0)
    parts = []
    for var in range(2 * NSA_GROUPS):
        parts.append(zt[var * LANES:(var + 1) * LANES])
        parts.append(jnp.where(extra_r == (var % 2), 1.0, 0.0))
    return jnp.concatenate(parts, axis=0)


def _nsa_prep_kernel(q_ref, ks_ref, vs_ref, kw_ref, vw_ref, pos_ref, freq_ref, sel_ref, one_ref,
                     qc_o, qr_o, ksz_o, vsz_o, kwz_o, vwz_o):
    scale = NSA_DK ** -0.5 * LOG2E
    ang = freq_ref[...] * pos_ref[...].astype(F32)
    half = ROPE_DIM // 2
    tm = ang.shape[1]
    cs = jnp.concatenate([jnp.cos(ang), jnp.sin(ang), jnp.zeros((LANES - 2 * half, tm), F32)], axis=0)
    spread = _dot_sel_rhs(cs.T, sel_ref[...])
    cos = spread[:, 0:LANES] + one_ref[...]
    sin_up = spread[:, LANES:2 * LANES]
    sin_dn = spread[:, 2 * LANES:3 * LANES]

    def rope(x):
        return x * cos + pltpu.roll(x, half, axis=1) * sin_up + pltpu.roll(x, LANES - half, axis=1) * sin_dn

    for p in range(NSA_HEADS * NSA_DK // LANES):
        x = q_ref[:, p * LANES:(p + 1) * LANES]
        qc_o[:, p * LANES:(p + 1) * LANES] = (x * scale).astype(qc_o.dtype)
        qr_o[:, p * LANES:(p + 1) * LANES] = (rope(x) * scale).astype(qr_o.dtype)
    ksz_o[...] = _group_variants(rope(ks_ref[...])).astype(ksz_o.dtype)
    kwz_o[...] = _group_variants(rope(kw_ref[...])).astype(kwz_o.dtype)
    vsz_o[...] = _values_transposed(vs_ref[...]).astype(vsz_o.dtype)
    vwz_o[...] = _values_transposed(vw_ref[...]).astype(vwz_o.dtype)


def _nsa_prep(proj, pos_row, freq_col, rope_sel, one_row, bsz, t_len, *, tm):
    m = proj.shape[0]
    tiles_per_seq = t_len // tm
    seq_t = lambda i: (i // tiles_per_seq, 0, i % tiles_per_seq)
    qw = NSA_HEADS * NSA_DK
    kv0 = qw // LANES
    const = lambda i: (0, 0)
    zw = 4 * LANES
    return pl.pallas_call(
        _nsa_prep_kernel,
        grid=(m // tm,),
        in_specs=[
            pl.BlockSpec((tm, qw), lambda i: (i, 0)),
            pl.BlockSpec((tm, LANES), lambda i: (i, kv0 + 2)),
            pl.BlockSpec((tm, LANES), lambda i: (i, kv0 + 3)),
            pl.BlockSpec((tm, LANES), lambda i: (i, kv0 + 4)),
            pl.BlockSpec((tm, LANES), lambda i: (i, kv0 + 5)),
            pl.BlockSpec((1, tm), lambda i: (0, i)),
            pl.BlockSpec(freq_col.shape, const),
            pl.BlockSpec(rope_sel.shape, const),
            pl.BlockSpec((1, LANES), const),
        ],
        out_specs=[
            pl.BlockSpec((tm, qw), lambda i: (i, 0)),
            pl.BlockSpec((tm, qw), lambda i: (i, 0)),
            pl.BlockSpec((tm, zw), lambda i: (i, 0)),
            pl.BlockSpec((None, 4 * VT_ROWS, tm), seq_t),
            pl.BlockSpec((tm, zw), lambda i: (i, 0)),
            pl.BlockSpec((None, 4 * VT_ROWS, tm), seq_t),
        ],
        out_shape=[
            jax.ShapeDtypeStruct((m, qw), BF16),
            jax.ShapeDtypeStruct((m, qw), BF16),
            jax.ShapeDtypeStruct((m, zw), BF16),
            jax.ShapeDtypeStruct((bsz, 4 * VT_ROWS, t_len), BF16),
            jax.ShapeDtypeStruct((m, zw), BF16),
            jax.ShapeDtypeStruct((bsz, 4 * VT_ROWS, t_len), BF16),
        ],
        compiler_params=_cparams("parallel"),
        name="nsa_prep",
    )(proj, proj, proj, proj, proj, pos_row, freq_col, rope_sel, one_row)


def _nsa_compress_kernel(k16_ref, v16_ref, pos_ref, w1a_ref, w1b_ref, w2_ref, kcz_o, vcz_o):
    nr = k16_ref.shape[0]
    for idx, (x_ref, o_ref) in enumerate(((k16_ref, kcz_o), (v16_ref, vcz_o))):
        x = x_ref[...]
        h_a = _dot((x + pos_ref[idx, 0:1, :]).astype(BF16), w1a_ref[idx])
        h_b = _dot((x + pos_ref[idx, 1:2, :]).astype(BF16), w1b_ref[idx])
        hid = h_a + pltpu.roll(h_b, nr - 1, axis=0)
        out = _dot(_silu(hid).astype(BF16), w2_ref[idx])
        z = _group_variants(out)
        o_ref[...] = (z.T if idx == 1 else z).astype(o_ref.dtype)


def _nsa_compress(k16, v16, pos_ab, w1a, w1b, w2):
    bsz, nr, width = k16.shape
    hid = NSA_GROUPS * CMP_HIDDEN
    zw = 4 * LANES
    c3 = lambda b: (0, 0, 0)
    return pl.pallas_call(
        _nsa_compress_kernel,
        grid=(bsz,),
        in_specs=[
            pl.BlockSpec((None, nr, width), lambda b: (b, 0, 0)),
            pl.BlockSpec((None, nr, width), lambda b: (b, 0, 0)),
            pl.BlockSpec((2, 2, width), c3),
            pl.BlockSpec((2, width, hid), c3),
            pl.BlockSpec((2, width, hid), c3),
            pl.BlockSpec((2, hid, LANES), c3),
        ],
        out_specs=[
            pl.BlockSpec((None, nr, zw), lambda b: (b, 0, 0)),
            pl.BlockSpec((None, zw, nr), lambda b: (b, 0, 0)),
        ],
        out_shape=[
            jax.ShapeDtypeStruct((bsz, nr, zw), BF16),
            jax.ShapeDtypeStruct((bsz, zw, nr), BF16),
        ],
        compiler_params=_cparams("parallel"),
        name="nsa_compress",
    )(k16, v16, pos_ab, w1a, w1b, w2)


def _flash_branch(q_stacks, kz_ref, vzt_ref, lo, hi, bias_fns, kb_size):
    r_acc = lax.broadcasted_iota(jnp.int32, (VT_ROWS, 1), 0)
    even_rows = (r_acc < LANES // 2) | (r_acc == LANES)
    half_w = 2 * Q_BLOCK
    n_half = q_stacks[0].shape[0] // half_w
    units = [(g, h) for g in range(NSA_GROUPS) for h in range(n_half)]
    q_unit = [q_stacks[g][h * half_w:(h + 1) * half_w] for g, h in units]

    def body(kb, carry):
        m_in, acc = carry
        ks = pl.multiple_of(kb * kb_size, kb_size)
        k_both, vt_both, bias = [], [], []
        for g in range(NSA_GROUPS):
            k0 = 2 * g * LANES
            v0 = 2 * g * VT_ROWS
            k_both.append(jnp.concatenate([kz_ref[pl.ds(ks, kb_size), k0:k0 + LANES],
                                           kz_ref[pl.ds(ks, kb_size), k0 + LANES:k0 + 2 * LANES]], axis=0))
            vt_both.append(jnp.concatenate([vzt_ref[v0:v0 + VT_ROWS, pl.ds(ks, kb_size)],
                                            vzt_ref[v0 + VT_ROWS:v0 + 2 * VT_ROWS, pl.ds(ks, kb_size)]],
                                           axis=1))
            b = bias_fns[g](ks)
            bias.append(jnp.concatenate([b, b], axis=1))
        n_u = len(units)
        s, m_out, alpha_rows, p_cat, pv = {}, {}, {}, {}, {}

        def softmax_stage(u):
            g = units[u][0]
            p_parts, alphas, m_news = [], [], []
            for var in range(2):
                s_v = s[u][var * kb_size:(var + 1) * kb_size] + bias[g]
                m_prev = m_in[u][var]
                m_new = jnp.maximum(m_prev, jnp.max(s_v, axis=0, keepdims=True))
                alphas.append(jnp.exp2(m_prev - m_new))
                p_parts.append(jnp.exp2(s_v - m_new).astype(BF16))
                m_news.append(m_new)
            m_out[u] = tuple(m_news)
            alpha_rows[u] = jnp.where(even_rows, alphas[0], alphas[1])
            p_cat[u] = jnp.concatenate(p_parts, axis=0)

        for u in range(n_u):
            s[u] = _dot_nt(k_both[units[u][0]], q_unit[u])
        for u in range(n_u):
            softmax_stage(u)
        for u in range(n_u):
            pv[u] = _dot(vt_both[units[u][0]], p_cat[u])
        acc_out = tuple(acc[u] * alpha_rows[u] + pv[u] for u in range(n_u))
        return tuple(m_out[u] for u in range(n_u)), acc_out

    m_init = tuple((jnp.full((1, half_w), NEG_BIG, F32),) * 2 for _ in units)
    acc_init = tuple(jnp.zeros((VT_ROWS, half_w), F32) for _ in units)
    _, acc = lax.fori_loop(lo, hi, body, (m_init, acc_init))
    outs = []
    for g in range(NSA_GROUPS):
        a = jnp.concatenate(acc[g * n_half:(g + 1) * n_half], axis=1)
        inv_even = 1.0 / a[LANES:LANES + 1, :]
        inv_odd = 1.0 / a[LANES + 1:LANES + 2, :]
        outs.append(a[0:LANES, :] * jnp.where(r_acc[0:LANES] < LANES // 2, inv_even, inv_odd))
    return outs


def _flash_branch_bounded(q_exts, kz_ref, kext_ref, vzt_ref, lo, hi, bias_fns, last_bias_fns, kb_size):
    r_acc = lax.broadcasted_iota(jnp.int32, (VT_ROWS, 1), 0)
    half_w = 2 * Q_BLOCK
    n_half = q_exts[0].shape[0] // half_w
    units = [(g, h) for g in range(NSA_GROUPS) for h in range(n_half)]
    q_unit = [q_exts[g][h * half_w:(h + 1) * half_w] for g, h in units]
    n_u = len(units)

    def step(kbs, acc, fns):
        work = []
        for kb in kbs:
            ks = pl.multiple_of(jnp.maximum(kb, 0) * kb_size, kb_size)
            k_ext = kext_ref[pl.ds(ks, kb_size), :]
            for g in range(NSA_GROUPS):
                k0 = 2 * g * LANES
                v0 = 2 * g * VT_ROWS
                k_both = jnp.concatenate(
                    [jnp.concatenate([kz_ref[pl.ds(ks, kb_size), k0:k0 + LANES], k_ext], axis=1),
                     jnp.concatenate([kz_ref[pl.ds(ks, kb_size), k0 + LANES:k0 + 2 * LANES], k_ext],
                                     axis=1)], axis=0)
                vt_both = jnp.concatenate([vzt_ref[v0:v0 + VT_ROWS, pl.ds(ks, kb_size)],
                                           vzt_ref[v0 + VT_ROWS:v0 + 2 * VT_ROWS, pl.ds(ks, kb_size)]],
                                          axis=1)
                bias = None
                if fns is not None:
                    b = fns[g](kb, ks)
                    b = jnp.concatenate([b, b], axis=1)
                    bias = jnp.concatenate([b, b], axis=0)
                work += [(u, k_both, vt_both, bias) for u in range(n_u) if units[u][0] == g]
        s = [_dot_nt(k, q_unit[u]) for u, k, _, _ in work]
        p = [jnp.exp2(s_i if w[3] is None else s_i + w[3]).astype(BF16) for s_i, w in zip(s, work)]
        pv = [_dot(w[2], p_i) for p_i, w in zip(p, work)]
        out = list(acc)
        for (u, _, _, _), pv_i in zip(work, pv):
            out[u] = out[u] + pv_i
        return tuple(out)

    acc = tuple(jnp.zeros((VT_ROWS, half_w), F32) for _ in units)
    if isinstance(lo, (list, tuple)):
        acc = step(lo, acc, bias_fns)
    elif last_bias_fns is None:
        acc = lax.fori_loop(lo, hi, lambda kb, a: step([kb], a, bias_fns), acc)
    else:
        acc = lax.fori_loop(lo, hi - 1, lambda kb, a: step([kb], a, bias_fns), acc)
        acc = step([hi - 1], acc, last_bias_fns)
    outs = []
    l_min = None
    for g in range(NSA_GROUPS):
        a = jnp.concatenate(acc[g * n_half:(g + 1) * n_half], axis=1)
        l_even = a[LANES:LANES + 1, :]
        l_odd = a[LANES + 1:LANES + 2, :]
        outs.append(a[0:LANES, :] * jnp.where(r_acc[0:LANES] < LANES // 2, 1.0 / l_even, 1.0 / l_odd))
        l_g = jnp.min(jnp.minimum(l_even, l_odd))
        l_min = l_g if l_min is None else jnp.minimum(l_min, l_g)
    return outs, l_min


def _nsa_attn_kernel(qc_ref, qr_ref, gate_ref, kcz_ref, vczt_ref, ksz_ref, vszt_ref, kwz_ref, vwzt_ref,
                     aggt_ref, eselt_ref, o_ref, bias_ref, kmax_ref,
                     *, t_len, kb_size):
    n_cmp_rows = t_len // CMP_STRIDE
    n_cmp = (t_len - CMP_LEN) // CMP_STRIDE + 1
    n_sel = t_len // SEL_LEN
    n_top = min(SEL_TOPK, n_sel)
    pairs = NSA_HEADS // NSA_GROUPS // 2
    q0 = pl.program_id(1) * Q_BLOCK
    t_row = q0 + lax.broadcasted_iota(jnp.int32, (1, Q_BLOCK), 1)
    t_row_stack = q0 + (lax.broadcasted_iota(jnp.int32, (1, pairs * Q_BLOCK), 1) & (Q_BLOCK - 1))
    blk = lax.broadcasted_iota(jnp.int32, (n_sel, 1), 0)
    c_idx = lax.broadcasted_iota(jnp.int32, (n_cmp_rows, 1), 0)
    cmp_mask = ((c_idx * CMP_STRIDE + (CMP_LEN - 1)) <= t_row_stack) & (c_idx < n_cmp)
    key_pos = lax.broadcasted_iota(jnp.int32, (t_len, 1), 0)
    gate_t = _sigmoid(gate_ref[...]).T
    top_half = lax.broadcasted_iota(jnp.int32, (LANES, 1), 0) < (LANES // 2)
    hi = (q0 + (Q_BLOCK - 1)) // kb_size + 1
    win_lo = jnp.maximum(q0 - (WINDOW - 1), 0) // kb_size

    def win_bias(ks):
        kp = ks + lax.broadcasted_iota(jnp.int32, (kb_size, 1), 0)
        return jnp.where((kp <= t_row) & (kp > t_row - WINDOW), 0.0, NEG_BIG)

    def sel_bias(g):
        return lambda ks: bias_ref[g, pl.ds(ks, kb_size), :]

    @pl.when(pl.program_id(1) == 0)
    def _():
        for br, k_ref in enumerate((ksz_ref, kwz_ref)):
            for g in range(NSA_GROUPS):
                k = k_ref[:, 2 * g * LANES:(2 * g + 1) * LANES].astype(F32)
                n2 = jnp.max(jnp.sum(k * k, axis=1, keepdims=True), axis=0, keepdims=True)
                kmax_ref[br * NSA_GROUPS + g] = jnp.broadcast_to(jnp.sqrt(n2), (SUBLANES, LANES))

    qr_stacks, o_cmps, sels = [], [], []
    for g in range(NSA_GROUPS):
        lanes_g = g * pairs * LANES
        qc_stack = jnp.concatenate(
            [qc_ref[:, lanes_g + p * LANES:lanes_g + (p + 1) * LANES] for p in range(pairs)], axis=0)
        qr_stacks.append(jnp.concatenate(
            [qr_ref[:, lanes_g + p * LANES:lanes_g + (p + 1) * LANES] for p in range(pairs)], axis=0))

        o_cmp = None
        p_sum = None
        for var in range(2):
            c0 = (2 * g + var) * LANES
            s = jnp.where(cmp_mask, _dot_nt(kcz_ref[:, c0:c0 + LANES], qc_stack), -jnp.inf)
            mx = jnp.max(s, axis=0, keepdims=True)
            mx = jnp.where(mx > -jnp.inf, mx, 0.0)
            e = jnp.exp2(s - mx)
            p = e * (1.0 / jnp.maximum(jnp.sum(e, axis=0, keepdims=True), 1e-30))
            d = _dot(vczt_ref[c0:c0 + LANES, :], p.astype(BF16))
            o_cmp = d if o_cmp is None else o_cmp + d
            for pr in range(pairs):
                part = p[:, pr * Q_BLOCK:(pr + 1) * Q_BLOCK]
                p_sum = part if p_sum is None else p_sum + part
        importance = _dot_sel_lhs(aggt_ref[...], p_sum)

        cur = t_row >> 6
        causal_blk = blk <= cur
        forced = (blk == 0) | (causal_blk & (blk > cur - SEL_LOCAL))
        score = jnp.where(forced, FORCE_SCORE, jnp.where(causal_blk, importance, -1.0))
        rank = jnp.zeros((n_sel, Q_BLOCK), F32)
        for jj in range(n_sel):
            row = score[jj:jj + 1, :]
            beats = (row > score) | ((row == score) & (blk > jj))
            rank = rank + jnp.where(beats, 1.0, 0.0)
        sel = jnp.where(rank < n_top, 1.0, 0.0)
        sels.append(jnp.concatenate([sel, jnp.zeros((LANES - n_sel, Q_BLOCK), F32)], axis=0))
        o_cmps.append(o_cmp)

    lane = lax.broadcasted_iota(jnp.int32, (1, LANES), 1)
    blk_pad = lax.broadcasted_iota(jnp.int32, (LANES, 1), 0)
    q_ext_sel, q_ext_win = [], []
    for g in range(NSA_GROUPS):
        x = qr_stacks[g].astype(F32)
        sq = x * x
        tot = jnp.sum(sq, axis=1, keepdims=True)
        even = jnp.sum(jnp.where(lane < LANES // 2, sq, 0.0), axis=1, keepdims=True)
        q_norm = jnp.sqrt(jnp.maximum(even, tot - even))
        sel_c = jnp.where(blk_pad * SEL_LEN <= t_row, sels[g], 0.0).T
        sel_b = jnp.where(lane < n_sel, (sel_c - 1.0) * (-NEG_BIG), 0.0)
        sel_b = jnp.concatenate([sel_b] * pairs, axis=0)
        ext_s = jnp.where(lane == n_sel, -q_norm * kmax_ref[g][0:1, 0:1], sel_b)
        ext_w = jnp.where(lane == n_sel, -q_norm * kmax_ref[NSA_GROUPS + g][0:1, 0:1], 0.0)
        q_ext_sel.append(jnp.concatenate([qr_stacks[g], ext_s.astype(BF16)], axis=1))
        q_ext_win.append(jnp.concatenate([qr_stacks[g], ext_w.astype(BF16)], axis=1))

    sel_kb = min(2 * kb_size, t_len)

    def diag_bias(kb, ks):
        kp = ks + lax.broadcasted_iota(jnp.int32, (sel_kb, 1), 0)
        return jnp.where(kp <= t_row, 0.0, NEG_BIG)

    n_win = WINDOW // kb_size + 1
    win_blocks = [hi - n_win + j for j in range(n_win)]

    def win_bias_fast(kb, ks):
        return jnp.where(kb >= 0, win_bias(ks), NEG_BIG)

    fast_sel, l_sel = _flash_branch_bounded(q_ext_sel, ksz_ref, eselt_ref, vszt_ref, 0,
                                            (q0 + (Q_BLOCK - 1)) // sel_kb + 1, None,
                                            [diag_bias] * NSA_GROUPS, sel_kb)
    fast_win, l_win = _flash_branch_bounded(q_ext_win, kwz_ref, eselt_ref, vwzt_ref, win_blocks, None,
                                            [win_bias_fast] * NSA_GROUPS, None, kb_size)

    def robust():
        for g in range(NSA_GROUPS):
            sel_keys = _dot(eselt_ref[...], sels[g].astype(BF16))
            bias_ref[g] = jnp.where((sel_keys > 0.5) & (key_pos <= t_row), 0.0, NEG_BIG)
        o_s = _flash_branch(qr_stacks, ksz_ref, vszt_ref, 0, hi,
                            [sel_bias(g) for g in range(NSA_GROUPS)], kb_size)
        o_w = _flash_branch(qr_stacks, kwz_ref, vwzt_ref, win_lo, hi, [win_bias] * NSA_GROUPS, kb_size)
        return tuple(o_s) + tuple(o_w)

    o_all = lax.cond(jnp.minimum(l_sel, l_win) > DENOM_FLOOR,
                     lambda: tuple(fast_sel) + tuple(fast_win), robust)
    o_sels, o_wins = o_all[:NSA_GROUPS], o_all[NSA_GROUPS:]

    for g in range(NSA_GROUPS):
        lanes_g = g * pairs * LANES
        for pr in range(pairs):
            h_even = (g * pairs + pr) * 2
            cols = slice(pr * Q_BLOCK, (pr + 1) * Q_BLOCK)
            mixed = None
            for j, o_branch in enumerate((o_cmps[g], o_sels[g], o_wins[g])):
                r_e = h_even * 3 + j
                r_o = r_e + 3
                gate_rows = jnp.where(top_half, gate_t[r_e:r_e + 1, :], gate_t[r_o:r_o + 1, :])
                term = gate_rows * o_branch[:, cols]
                mixed = term if mixed is None else mixed + term
            c0 = lanes_g + pr * LANES
            o_ref[:, c0:c0 + LANES] = mixed.T.astype(o_ref.dtype)


def _nsa_attn(qc, qr, proj, kcz, vczt, ksz, vszt, kwz, vwzt, aggt, eselt, bsz, t_len, *, kb_size):
    nq = t_len // Q_BLOCK
    qw = NSA_HEADS * NSA_DK
    zw = 4 * LANES
    nr = t_len // CMP_STRIDE
    pairs = NSA_HEADS // NSA_GROUPS // 2
    row = lambda b, i: (b * nq + i, 0)
    per_b = lambda b, i: (b, 0, 0)
    return pl.pallas_call(
        functools.partial(_nsa_attn_kernel, t_len=t_len, kb_size=kb_size),
        grid=(bsz, nq),
        in_specs=[
            pl.BlockSpec((Q_BLOCK, qw), row),
            pl.BlockSpec((Q_BLOCK, qw), row),
            pl.BlockSpec((Q_BLOCK, LANES), lambda b, i: (b * nq + i, NSA_GATE_OFF // LANES)),
            pl.BlockSpec((None, nr, zw), per_b),
            pl.BlockSpec((None, zw, nr), per_b),
            pl.BlockSpec((None, t_len, zw), per_b),
            pl.BlockSpec((None, 4 * VT_ROWS, t_len), per_b),
            pl.BlockSpec((None, t_len, zw), per_b),
            pl.BlockSpec((None, 4 * VT_ROWS, t_len), per_b),
            pl.BlockSpec(aggt.shape, lambda b, i: (0, 0)),
            pl.BlockSpec(eselt.shape, lambda b, i: (0, 0)),
        ],
        out_specs=pl.BlockSpec((Q_BLOCK, qw), row),
        out_shape=jax.ShapeDtypeStruct((bsz * t_len, qw), BF16),
        scratch_shapes=[
            pltpu.VMEM((NSA_GROUPS, t_len, Q_BLOCK), F32),
            pltpu.VMEM((2 * NSA_GROUPS, SUBLANES, LANES), F32),
        ],
        compiler_params=_cparams("parallel", "arbitrary"),
        name="nsa_attn",
    )(qc, qr, proj, kcz, vczt, ksz, vszt, kwz, vwzt, aggt, eselt)


def _expand_rows(first_row, n_heads, width):
    e = np.zeros((LANES, n_heads * width), np.float32)
    for h in range(n_heads):
        e[first_row + h, h * width:(h + 1) * width] = 1.0
    return e


def _nsa_constants(t_len):
    n_cmp = (t_len - CMP_LEN) // CMP_STRIDE + 1
    n_sel = t_len // SEL_LEN
    nr = t_len // CMP_STRIDE
    c0 = np.arange(n_cmp)[:, None] * CMP_STRIDE
    s0 = np.arange(n_sel)[None, :] * SEL_LEN
    overlap = np.clip(np.minimum(c0 + CMP_LEN, s0 + SEL_LEN) - np.maximum(c0, s0), 0, None) / CMP_LEN
    aggt = np.zeros((n_sel, nr), np.float32)
    aggt[:, :n_cmp] = overlap.T
    eselt = np.zeros((t_len, LANES), np.float32)
    eselt[np.arange(t_len), np.arange(t_len) // SEL_LEN] = 1.0
    eselt[:, n_sel] = 1.0
    half = ROPE_DIM // 2
    inv_freq = (ROPE_THETA ** (-np.arange(half) / half)).astype(np.float32)
    d = np.arange(LANES) % NSA_DK
    rope_sel = np.zeros((LANES, 3 * LANES), np.float32)
    for lane in range(LANES):
        if d[lane] < ROPE_DIM:
            f = d[lane] % half
            rope_sel[f, lane] = 1.0
            if d[lane] >= half:
                rope_sel[half + f, LANES + lane] = 1.0
            else:
                rope_sel[half + f, 2 * LANES + lane] = -1.0
    one_row = (d >= ROPE_DIM).astype(np.float32)[None]
    return aggt, eselt, inv_freq[:, None], rope_sel, one_row


def _pad_row(vec, offset):
    return jnp.zeros((1, LANES), F32).at[0, offset:offset + vec.shape[0]].set(vec.astype(F32))


def _hybrid_layer(h, bsz, t_len, w_in, gdn_conv_w, gdn_a_log, gdn_dt_bias, gdn_norm_w, ssd_conv_w,
                  ssd_conv_b, ssd_a_log, ssd_dt_bias, ssd_d, ssd_norm_w, w_out, ln_g, ln_b, cfg):
    b_off = GDN_CONV_DIM + GDN_V_W
    z_off = b_off + 2 * GDN_V_HEADS
    dt_off = z_off + SSD_D_INNER + SSD_CONV_DIM
    w_pad = jnp.concatenate(
        [w_in[:, :b_off], w_in[:, z_off:dt_off], w_in[:, b_off:z_off], w_in[:, dt_off:],
         jnp.zeros((D_MODEL, HYB_PAD - w_in.shape[1]), w_in.dtype)], axis=1).astype(BF16)
    conv_w = jnp.concatenate([gdn_conv_w, ssd_conv_w], axis=1).astype(F32)
    conv_b = jnp.concatenate([jnp.zeros((GDN_CONV_DIM,), F32), ssd_conv_b.astype(F32)])[None]
    proj = _proj_conv(h, w_pad, conv_w, conv_b, t_len, tm=min(cfg["proj_tm"], t_len), tn=cfg["hyb_tn"])
    o_a = _gdn(proj, bsz, t_len, _pad_row(gdn_dt_bias, GDN_V_HEADS),
               _pad_row(gdn_a_log, GDN_V_HEADS), gdn_norm_w[None].astype(F32),
               jnp.asarray(_expand_rows(0, GDN_V_HEADS, GDN_DV)).astype(BF16),
               jnp.asarray(_expand_rows(GDN_V_HEADS, GDN_V_HEADS, GDN_DV)).astype(BF16), tc=cfg["gdn_tc"])
    o_b = _ssd(proj, bsz, t_len, _pad_row(ssd_dt_bias, 16),
               _pad_row(ssd_a_log, 16), jnp.repeat(ssd_d, SSD_HEADDIM)[None], ssd_norm_w[None],
               jnp.asarray(_expand_rows(16, SSD_HEADS, SSD_HEADDIM)).astype(BF16),
               n_sub=cfg["ssd_sub"])
    w_out_b = w_out.astype(BF16)
    return _out_ln([o_a, o_b], [w_out_b[:GDN_V_W], w_out_b[GDN_V_W:]], h, ln_g, ln_b, tm=cfg["out_tm"])


def _nsa_layer(h, pos_row, bsz, t_len, w_in, cmp_pos, cmp_w1, cmp_w2, w_out, ln_g, ln_b, cfg):
    aggt, eselt, freq_col, rope_sel, one_row = _nsa_constants(t_len)
    w_pad = jnp.concatenate(
        [w_in, jnp.zeros((D_MODEL, NSA_PAD - w_in.shape[1]), w_in.dtype)], axis=1).astype(BF16)
    proj = _proj(h, w_pad, tm=cfg["nsa_proj_tm"], tn=cfg["nsa_tn"])
    qc, qr, ksz, vszt, kwz, vwzt = _nsa_prep(proj, pos_row, jnp.asarray(freq_col),
                                             jnp.asarray(rope_sel).astype(BF16), jnp.asarray(one_row),
                                             bsz, t_len, tm=cfg["prep_tm"])
    qw = NSA_HEADS * NSA_DK
    nr = t_len // CMP_STRIDE
    width = CMP_STRIDE * LANES
    k16 = proj[:, qw:qw + LANES].reshape(bsz, nr, width)
    v16 = proj[:, qw + LANES:qw + 2 * LANES].reshape(bsz, nr, width)
    pos2 = jnp.broadcast_to(cmp_pos[:, :, None, :], (2, CMP_LEN, NSA_GROUPS, NSA_DK)).reshape(2, 2, width)
    w1 = cmp_w1.reshape(2, 2, CMP_STRIDE, NSA_DK, CMP_HIDDEN)
    eye_g = jnp.eye(NSA_GROUPS, dtype=w1.dtype)
    w1x = jnp.einsum("ksjdh,ge->ksjgdeh", w1, eye_g).reshape(
        2, 2, width, NSA_GROUPS * CMP_HIDDEN).astype(BF16)
    w2x = jnp.einsum("khd,ge->kghed", cmp_w2, eye_g).reshape(
        2, NSA_GROUPS * CMP_HIDDEN, NSA_GROUPS * NSA_DK).astype(BF16)
    kcz, vczt = _nsa_compress(k16, v16, pos2, w1x[:, 0], w1x[:, 1], w2x)
    o = _nsa_attn(qc, qr, proj, kcz, vczt, ksz.reshape(bsz, t_len, -1), vszt,
                  kwz.reshape(bsz, t_len, -1), vwzt,
                  jnp.asarray(aggt).astype(BF16), jnp.asarray(eselt).astype(BF16),
                  bsz, t_len, kb_size=cfg["attn_kb"])
    return _out_ln([o], [w_out.astype(BF16)], h, ln_g, ln_b, tm=cfg["out_tm"])


def _config(m, t_len):
    return dict(ffn_tm=min(1024, m), ffn_tf=256, proj_tm=min(512, m), hyb_tn=512, nsa_tn=NSA_PAD,
                out_tm=min(1024, m), gdn_tc=min(256, t_len), ssd_sub=min(4, t_len // SSD_CHUNK), prep_tm=min(512, t_len),
                nsa_proj_tm=min(1024, m),
                attn_kb=min(256, t_len))


def kernel(x, positions, ln_g, ln_b, ffn_w_in, ffn_w_out, hyb_w_in, gdn_conv_w, gdn_a_log, gdn_dt_bias, gdn_norm_w, ssd_conv_w, ssd_conv_b, ssd_a_log, ssd_dt_bias, ssd_d, ssd_norm_w, hyb_w_out, nsa_w_in, nsa_cmp_pos, nsa_cmp_w1, nsa_cmp_w2, nsa_w_out):
    bsz, t_len, _ = x.shape
    m = bsz * t_len
    cfg = _config(m, t_len)
    h = x.reshape(m, D_MODEL)
    pos_row = positions.reshape(1, m)
    w_in_b = ffn_w_in.astype(BF16)
    w_out_b = ffn_w_out.astype(BF16)
    ln_g = ln_g[:, :, None, :]
    ln_b = ln_b[:, :, None, :]
    for layer in range(DEPTH):
        i = layer // 2
        h = _ffn_ln(h, w_in_b, w_out_b, ln_g[layer, 0], ln_b[layer, 0], layer, 0,
                    tm=cfg["ffn_tm"], tf=cfg["ffn_tf"])
        if layer % 2 == 0:
            h = _hybrid_layer(h, bsz, t_len, hyb_w_in[i], gdn_conv_w[i], gdn_a_log[i], gdn_dt_bias[i],
                              gdn_norm_w[i], ssd_conv_w[i], ssd_conv_b[i], ssd_a_log[i], ssd_dt_bias[i],
                              ssd_d[i], ssd_norm_w[i], hyb_w_out[i], ln_g[layer, 1], ln_b[layer, 1], cfg)
        else:
            h = _nsa_layer(h, pos_row, bsz, t_len, nsa_w_in[i], nsa_cmp_pos[i], nsa_cmp_w1[i],
                           nsa_cmp_w2[i], nsa_w_out[i], ln_g[layer, 1], ln_b[layer, 1], cfg)
        h = _ffn_ln(h, w_in_b, w_out_b, ln_g[layer, 2], ln_b[layer, 2], layer, 1,
                    tm=cfg["ffn_tm"], tf=cfg["ffn_tf"])
    return h.reshape(bsz, t_len, D_MODEL)
```

```python
import functools

import numpy as np
import jax
import jax.numpy as jnp
from jax import lax
from jax.experimental import pallas as pl
from jax.experimental.pallas import tpu as pltpu

F32 = jnp.float32
BF16 = jnp.bfloat16

D_MODEL = 1024
DEPTH = 4
D_FF = 2816
DEEPNORM_ALPHA = (2.0 * DEPTH) ** 0.25
LN_EPS = 1e-5
RMS_EPS = 1e-6
CONV_K = 4

GDN_QK_HEADS = 4
GDN_V_HEADS = 8
GDN_DK = 128
GDN_DV = 128
GDN_CHUNK = 64
GDN_QK_W = GDN_QK_HEADS * GDN_DK
GDN_V_W = GDN_V_HEADS * GDN_DV
GDN_CONV_DIM = 2 * GDN_QK_W + GDN_V_W

SSD_D_INNER = D_MODEL
SSD_HEADDIM = 64
SSD_HEADS = SSD_D_INNER // SSD_HEADDIM
SSD_GROUPS = 2
SSD_STATE = 128
SSD_CHUNK = 128
SSD_BC_W = SSD_GROUPS * SSD_STATE
SSD_CONV_DIM = SSD_D_INNER + 2 * SSD_BC_W

NSA_HEADS = 16
NSA_GROUPS = 2
NSA_DK = 64
NSA_DV = 64
ROPE_DIM = NSA_DK // 4
ROPE_THETA = 500000.0
CMP_LEN = 32
CMP_STRIDE = 16
CMP_HIDDEN = 256
SEL_LEN = 64
SEL_TOPK = 8
SEL_LOCAL = 2
FORCE_SCORE = 1e4
WINDOW = 512
Q_BLOCK = 128

LANES = 128
SUBLANES = 8
VMEM_LIMIT_BYTES = 48 * 1024 * 1024

HYB_SMALL_OFF = GDN_CONV_DIM + GDN_V_W + SSD_D_INNER + SSD_CONV_DIM
HYB_PAD = HYB_SMALL_OFF + LANES
NSA_GATE_OFF = NSA_HEADS * NSA_DK + 3 * NSA_GROUPS * (NSA_DK + NSA_DV)
NSA_PAD = NSA_GATE_OFF + LANES

NEG_BIG = -1e30
LOG2E = 1.4426950408889634
DENOM_FLOOR = 2.0 ** -100
VT_ROWS = LANES + 16


def _cparams(*sem):
    return pltpu.CompilerParams(dimension_semantics=sem, vmem_limit_bytes=VMEM_LIMIT_BYTES)


def _sigmoid(v):
    return 0.5 + 0.5 * jnp.tanh(0.5 * v)


def _silu(v):
    h = 0.5 * v
    return h + h * jnp.tanh(h)


def _softplus(v):
    return jnp.maximum(v, 0.0) + jnp.log1p(jnp.exp(-jnp.abs(v)))


def _layer_norm(y, g, b):
    mu = jnp.mean(y, axis=-1, keepdims=True)
    d = y - mu
    var = jnp.mean(d * d, axis=-1, keepdims=True)
    return d * lax.rsqrt(var + LN_EPS) * g + b


def _dot(a, b):
    return jnp.dot(a, b, preferred_element_type=F32)


def _split3(x):
    hi = x.astype(BF16)
    r1 = x - hi.astype(F32)
    mid = r1.astype(BF16)
    lo = (r1 - mid.astype(F32)).astype(BF16)
    return hi, mid, lo


def _dot_sel_rhs(x, sel):
    sel = sel.astype(BF16)
    hi, mid, lo = _split3(x)
    return _dot(hi, sel) + _dot(mid, sel) + _dot(lo, sel)


def _dot_sel_lhs(sel, x):
    sel = sel.astype(BF16)
    hi, mid, lo = _split3(x)
    return _dot(sel, hi) + _dot(sel, mid) + _dot(sel, lo)


def _dot_nt(a, b):
    return lax.dot_general(a, b, (((1,), (1,)), ((), ())), preferred_element_type=F32)


def _ffn_ln_kernel(x_ref, wi_ref, wo_ref, g_ref, b_ref, o_ref, *, tf):
    x = x_ref[...]
    xb = x.astype(BF16)
    acts = []
    for c in range(D_FF // tf):
        gate = _dot(xb, wi_ref[:, c * tf:(c + 1) * tf])
        up = _dot(xb, wi_ref[:, D_FF + c * tf:D_FF + (c + 1) * tf])
        acts.append((_silu(gate) * up).astype(BF16))
    mix = _dot(jnp.concatenate(acts, axis=1), wo_ref[...])
    y = DEEPNORM_ALPHA * x + 0.5 * mix
    o_ref[...] = _layer_norm(y, g_ref[...], b_ref[...])


def _resident(shape):
    return pl.BlockSpec(shape, lambda *_: (0,) * len(shape), pipeline_mode=pl.Buffered(1))


def _ffn_ln(h, w_in, w_out, g, b, layer, which, *, tm, tf):
    m = h.shape[0]
    pick = lambda i: (layer, which, 0, 0)
    return pl.pallas_call(
        functools.partial(_ffn_ln_kernel, tf=tf),
        grid=(m // tm,),
        in_specs=[
            pl.BlockSpec((tm, D_MODEL), lambda i: (i, 0)),
            pl.BlockSpec((None, None) + w_in.shape[2:], pick, pipeline_mode=pl.Buffered(1)),
            pl.BlockSpec((None, None) + w_out.shape[2:], pick, pipeline_mode=pl.Buffered(1)),
            _resident((1, D_MODEL)),
            _resident((1, D_MODEL)),
        ],
        out_specs=pl.BlockSpec((tm, D_MODEL), lambda i: (i, 0)),
        out_shape=jax.ShapeDtypeStruct((m, D_MODEL), F32),
        compiler_params=_cparams("parallel"),
        name="ffn_ln",
    )(h, w_in, w_out, g, b)


def _proj_kernel(x_ref, w_ref, o_ref, *, tn):
    xb = x_ref[...].astype(BF16)
    for c in range(w_ref.shape[1] // tn):
        o_ref[:, c * tn:(c + 1) * tn] = _dot(xb, w_ref[:, c * tn:(c + 1) * tn])


def _proj(h, w, *, tm, tn):
    m, k = h.shape
    n = w.shape[1]
    return pl.pallas_call(
        functools.partial(_proj_kernel, tn=tn),
        grid=(m // tm,),
        in_specs=[
            pl.BlockSpec((tm, k), lambda i: (i, 0)),
            _resident(w.shape),
        ],
        out_specs=pl.BlockSpec((tm, n), lambda i: (i, 0)),
        out_shape=jax.ShapeDtypeStruct((m, n), F32),
        compiler_params=_cparams("parallel"),
        name="in_proj",
    )(h, w)


def _proj_conv_kernel(x_ref, w_ref, cw_ref, cb_ref, o_ref, hist, *, chunks, tiles_per_seq):
    tm = x_ref.shape[0]

    @pl.when(pl.program_id(0) % tiles_per_seq == 0)
    def _():
        hist[...] = jnp.zeros(hist.shape, F32)

    xb = x_ref[...].astype(BF16)
    row8 = lax.broadcasted_iota(jnp.int32, (SUBLANES, 1), 0)
    for col, width, ccol in chunks:
        y = _dot(xb, w_ref[:, col:col + width])
        if ccol is not None:
            prev = hist[:, ccol:ccol + width]
            acc = cb_ref[:, ccol:ccol + width] + cw_ref[CONV_K - 1:CONV_K, ccol:ccol + width] * y
            for s in range(1, CONV_K):
                y_s = pltpu.roll(y, s, axis=0)
                head = jnp.where(row8 < s, pltpu.roll(prev, s, axis=0), y_s[0:SUBLANES])
                y_s = jnp.concatenate([head, y_s[SUBLANES:]], axis=0)
                acc = acc + cw_ref[CONV_K - 1 - s:CONV_K - s, ccol:ccol + width] * y_s
            hist[:, ccol:ccol + width] = y[tm - SUBLANES:tm]
            y = _silu(acc)
        o_ref[:, col:col + width] = y


def _proj_conv(h, w, cw, cb, t_len, *, tm, tn):
    m, k = h.shape
    n = w.shape[1]
    conv_w = cw.shape[1]
    ssd_lo = GDN_CONV_DIM + GDN_V_W + SSD_D_INNER
    chunks = []
    for col in range(0, n, tn):
        width = min(tn, n - col)
        if col + width <= GDN_CONV_DIM:
            ccol = col
        elif ssd_lo <= col and col + width <= ssd_lo + SSD_CONV_DIM:
            ccol = col - ssd_lo + GDN_CONV_DIM
        else:
            ccol = None
        chunks.append((col, width, ccol))
    return pl.pallas_call(
        functools.partial(_proj_conv_kernel, chunks=tuple(chunks), tiles_per_seq=t_len // tm),
        grid=(m // tm,),
        in_specs=[
            pl.BlockSpec((tm, k), lambda i: (i, 0)),
            _resident(w.shape),
            _resident(cw.shape),
            _resident(cb.shape),
        ],
        out_specs=pl.BlockSpec((tm, n), lambda i: (i, 0)),
        out_shape=jax.ShapeDtypeStruct((m, n), F32),
        scratch_shapes=[pltpu.VMEM((SUBLANES, conv_w), F32)],
        compiler_params=_cparams("arbitrary"),
        name="in_proj_conv",
    )(h, w, cw, cb)


def _out_ln_kernel(*refs, n_parts):
    a_refs = refs[:n_parts]
    w_refs = refs[n_parts:2 * n_parts]
    h_ref, g_ref, b_ref, o_ref = refs[2 * n_parts:]
    mix = _dot(a_refs[0][...], w_refs[0][...])
    for a_ref, w_ref in zip(a_refs[1:], w_refs[1:]):
        mix = mix + _dot(a_ref[...], w_ref[...])
    y = DEEPNORM_ALPHA * h_ref[...] + mix
    o_ref[...] = _layer_norm(y, g_ref[...], b_ref[...])


def _out_ln(parts, weights, h, g, b, *, tm):
    m = h.shape[0]
    n_parts = len(parts)
    in_specs = [pl.BlockSpec((tm, a.shape[1]), lambda i: (i, 0)) for a in parts]
    in_specs += [pl.BlockSpec(w.shape, lambda i: (0, 0)) for w in weights]
    in_specs += [
        pl.BlockSpec((tm, D_MODEL), lambda i: (i, 0)),
        pl.BlockSpec((1, D_MODEL), lambda i: (0, 0)),
        pl.BlockSpec((1, D_MODEL), lambda i: (0, 0)),
    ]
    return pl.pallas_call(
        functools.partial(_out_ln_kernel, n_parts=n_parts),
        grid=(m // tm,),
        in_specs=in_specs,
        out_specs=pl.BlockSpec((tm, D_MODEL), lambda i: (i, 0)),
        out_shape=jax.ShapeDtypeStruct((m, D_MODEL), F32),
        compiler_params=_cparams("parallel"),
        name="out_proj_ln",
    )(*parts, *weights, h, g, b)


def _ssd_kernel(z_ref, x_ref, bm_ref, cm_ref, sm_ref, dtb_ref, alog_ref,
                dskip_ref, nw_ref, eh_ref, o_ref, state, *, n_sub):
    @pl.when(pl.program_id(1) == 0)
    def _():
        state[...] = jnp.zeros_like(state)

    for sub in range(n_sub):
        _ssd_chunk(pl.ds(sub * SSD_CHUNK, SSD_CHUNK), z_ref, x_ref, bm_ref, cm_ref, sm_ref, dtb_ref,
                   alog_ref, dskip_ref, nw_ref, eh_ref, o_ref, state)


def _ssd_chunk(rows, z_ref, x_ref, bm_ref, cm_ref, sm_ref, dtb_ref, alog_ref,
               dskip_ref, nw_ref, eh_ref, o_ref, state):
    L = SSD_CHUNK
    W = SSD_D_INNER
    GW = W // SSD_GROUPS
    N = SSD_STATE

    xs = x_ref[rows, :]
    bm = bm_ref[rows, :].astype(BF16)
    cm = cm_ref[rows, :].astype(BF16)

    lane = lax.broadcasted_iota(jnp.int32, (1, LANES), 1)
    dt_lanes = (lane >= 16) & (lane < 16 + SSD_HEADS)
    dt_full = jnp.where(dt_lanes, _softplus(sm_ref[rows, :] + dtb_ref[...]), 0.0)
    adt_full = dt_full * (-jnp.exp(alog_ref[...]))
    r_i = lax.broadcasted_iota(jnp.int32, (L, L), 0)
    c_i = lax.broadcasted_iota(jnp.int32, (L, L), 1)
    tril = r_i >= c_i
    acs_full = _dot_sel_lhs(jnp.where(tril, 1.0, 0.0), adt_full)
    acs_t = acs_full.T
    eh = eh_ref[...]
    dt_exp = _dot_sel_rhs(dt_full, eh)
    acs_exp = _dot_sel_rhs(acs_full, eh)
    a_last = acs_exp[L - 1:L, :]
    xdt = xs * dt_exp
    xdec = (xdt * jnp.exp(a_last - acs_exp)).astype(BF16)
    e_acs = jnp.exp(acs_exp)
    e_last = jnp.exp(a_last)
    lane_w = lax.broadcasted_iota(jnp.int32, (1, W), 1)
    lo_half = (lane_w & (LANES - 1)) < SSD_HEADDIM
    xdt_b = xdt.astype(BF16)
    xdt_lo = jnp.where(lo_half, xdt_b, jnp.zeros_like(xdt_b))
    xdt_hi = jnp.where(lo_half, jnp.zeros_like(xdt_b), xdt_b)

    heads_per_group = SSD_HEADS // SSD_GROUPS
    y_parts = []
    for g in range(SSD_GROUPS):
        bg = bm[:, g * N:(g + 1) * N]
        cg = cm[:, g * N:(g + 1) * N]
        cb = _dot_nt(cg, bg)
        s_g = state[g]
        y_off = _dot(cg, s_g.astype(BF16)) * e_acs[:, g * GW:(g + 1) * GW]
        for pr in range(heads_per_group // 2):
            y_pair = None
            for var in range(2):
                h = g * heads_per_group + 2 * pr + var
                col = acs_full[:, 16 + h:17 + h]
                row = acs_t[16 + h:17 + h, :]
                seg = jnp.exp(jnp.where(tril, col - row, -jnp.inf))
                mh = (cb * seg).astype(BF16)
                src = xdt_lo if var == 0 else xdt_hi
                lanes0 = (g * heads_per_group + 2 * pr) * SSD_HEADDIM
                d = _dot(mh, src[:, lanes0:lanes0 + LANES])
                y_pair = d if y_pair is None else y_pair + d
            off = pr * LANES
            y_parts.append(y_pair + y_off[:, off:off + LANES])
        upd = lax.dot_general(bg, xdec[:, g * GW:(g + 1) * GW], (((0,), (0,)), ((), ())),
                              preferred_element_type=F32)
        state[g] = s_g * e_last[:, g * GW:(g + 1) * GW] + upd
    y = jnp.concatenate(y_parts, axis=1) + xs * dskip_ref[...]
    y = y * _silu(z_ref[rows, :])
    outs = []
    for g in range(SSD_GROUPS):
        yg = y[:, g * GW:(g + 1) * GW]
        outs.append(yg * lax.rsqrt(jnp.mean(yg * yg, axis=-1, keepdims=True) + RMS_EPS))
    o_ref[rows, :] = (jnp.concatenate(outs, axis=1) * nw_ref[...]).astype(o_ref.dtype)


def _ssd(proj, bsz, t_len, dtb_row, alog_row, dskip_row, nw_row, eh, *, n_sub):
    L = n_sub * SSD_CHUNK
    nc = t_len // L
    W = SSD_D_INNER
    z_blk = (GDN_CONV_DIM + GDN_V_W) // W
    x_blk = (GDN_CONV_DIM + GDN_V_W + W) // W
    b_blk = (GDN_CONV_DIM + GDN_V_W + 2 * W) // SSD_BC_W
    sm_blk = HYB_SMALL_OFF // LANES
    row = lambda b, c: b * nc + c
    const = lambda b, c: (0, 0)
    return pl.pallas_call(
        functools.partial(_ssd_kernel, n_sub=n_sub),
        grid=(bsz, nc),
        in_specs=[
            pl.BlockSpec((L, W), lambda b, c: (row(b, c), z_blk)),
            pl.BlockSpec((L, W), lambda b, c: (row(b, c), x_blk)),
            pl.BlockSpec((L, SSD_BC_W), lambda b, c: (row(b, c), b_blk)),
            pl.BlockSpec((L, SSD_BC_W), lambda b, c: (row(b, c), b_blk + 1)),
            pl.BlockSpec((L, LANES), lambda b, c: (row(b, c), sm_blk)),
            pl.BlockSpec((1, LANES), const),
            pl.BlockSpec((1, LANES), const),
            pl.BlockSpec((1, W), const),
            pl.BlockSpec((1, W), const),
            pl.BlockSpec((LANES, W), const),
        ],
        out_specs=pl.BlockSpec((L, W), lambda b, c: (row(b, c), 0)),
        out_shape=jax.ShapeDtypeStruct((bsz * t_len, W), BF16),
        scratch_shapes=[
            pltpu.VMEM((SSD_GROUPS, SSD_STATE, W // SSD_GROUPS), F32),
        ],
        compiler_params=_cparams("parallel", "arbitrary"),
        name="ssd",
    )(proj, proj, proj, proj, proj, dtb_row, alog_row, dskip_row, nw_row, eh)


def _unit_lower_inverse_minus_eye(a, r_i, c_i):
    n = range(len(a))
    diag8 = (r_i >> 3) == (c_i >> 3)
    x = [jnp.where(diag8, -a[i], 0.0) for i in n]
    x_b = [x[i].astype(BF16) for i in n]
    x2 = [_dot(x_b[i], x_b[i]) for i in n]
    x2_b = [x2[i].astype(BF16) for i in n]
    x4 = [_dot(x2_b[i], x2_b[i]) for i in n]
    e = [x[i] + x2[i] + _dot(x_b[i], x2_b[i]) for i in n]
    e = [e[i] + x4[i] + _dot(e[i].astype(BF16), x4[i].astype(BF16)) for i in n]
    for sh in (3, 4, 5):
        same_pair = (r_i >> (sh + 1)) == (c_i >> (sh + 1))
        lower_left = (((r_i >> sh) & 1) == 1) & (((c_i >> sh) & 1) == 0)
        ms = [jnp.where(same_pair & lower_left, a[i], 0.0) for i in n]
        y = [ms[i] + _dot(ms[i].astype(BF16), e[i].astype(BF16)) for i in n]
        e = [e[i] - y[i] - _dot(e[i].astype(BF16), y[i].astype(BF16)) for i in n]
    return e


def _stack_heads(x, r0, l0):
    return jnp.concatenate([x[r0:r0 + GDN_CHUNK, l0:l0 + GDN_DV],
                            x[r0:r0 + GDN_CHUNK, l0 + GDN_DV:l0 + 2 * GDN_DV]], axis=0)


def _own_head_block(x):
    return jnp.concatenate([x[0:GDN_CHUNK, 0:GDN_DV], x[GDN_CHUNK:, GDN_DV:]], axis=0)


def _gdn_kernel(q_ref, k_ref, v_ref, z_ref, sm_ref, dtb_ref, alog_ref, nw_ref,
                eb_ref, ea_ref, o_ref, state, *, tc):
    C = GDN_CHUNK
    DK = GDN_DK
    DV = GDN_DV
    P = 2 * C

    @pl.when(pl.program_id(1) == 0)
    def _():
        state[...] = jnp.zeros_like(state)

    lane = lax.broadcasted_iota(jnp.int32, (1, LANES), 1)
    sm = sm_ref[...]
    beta_full = jnp.where(lane < GDN_V_HEADS, _sigmoid(sm), 0.0)
    g_lanes = (lane >= GDN_V_HEADS) & (lane < 2 * GDN_V_HEADS)
    g_full = jnp.where(g_lanes, -jnp.exp(alog_ref[...]) * _softplus(sm + dtb_ref[...]), 0.0)
    rt = lax.broadcasted_iota(jnp.int32, (tc, tc), 0)
    ct = lax.broadcasted_iota(jnp.int32, (tc, tc), 1)
    same_chunk_tril = ((rt >> 6) == (ct >> 6)) & (rt >= ct)
    gc_full = _dot_sel_lhs(jnp.where(same_chunk_tril, 1.0, 0.0), g_full)
    b_exp = _dot_sel_rhs(beta_full, eb_ref[...])
    g_exp = _dot_sel_rhs(gc_full, ea_ref[...])
    eg_exp = jnp.exp(g_exp)

    r_i = lax.broadcasted_iota(jnp.int32, (P, P), 0)
    c_i = lax.broadcasted_iota(jnp.int32, (P, P), 1)
    same_head = (r_i >> 6) == (c_i >> 6)
    tril = same_head & (r_i >= c_i)
    strict = same_head & (r_i > c_i)
    head0_rows = lax.broadcasted_iota(jnp.int32, (P, 1), 0) < C
    nw = nw_ref[...]

    n_chunks = tc // C
    units = [(hq, ci) for ci in range(n_chunks) for hq in range(GDN_QK_HEADS)]
    qn, kn = [], []
    for hq in range(GDN_QK_HEADS):
        qh = q_ref[:, hq * DK:(hq + 1) * DK]
        kh = k_ref[:, hq * DK:(hq + 1) * DK]
        qn.append(qh * lax.rsqrt(jnp.sum(qh * qh, axis=-1, keepdims=True) + 1e-6) * (DK ** -0.5))
        kn.append(kh * lax.rsqrt(jnp.sum(kh * kh, axis=-1, keepdims=True) + 1e-6))
    pre = {}
    a_low = []
    for hq, ci in units:
        r0 = ci * C
        l0 = hq * 2 * DV
        q_c = qn[hq][r0:r0 + C]
        k_c = kn[hq][r0:r0 + C]
        q2 = jnp.concatenate([q_c, q_c], axis=0)
        k2 = jnp.concatenate([k_c, k_c], axis=0)
        k2_b = k2.astype(BF16)
        kk = _dot_nt(k2_b, k2_b)
        qk = _dot_nt(q2.astype(BF16), k2_b)
        gcol = _stack_heads(g_exp, r0, l0)
        bcol = _stack_heads(b_exp, r0, l0)
        egc = _stack_heads(eg_exp, r0, l0)
        dmat = jnp.exp(jnp.where(tril, gcol - gcol.T, -jnp.inf))
        a_low.append(jnp.where(strict, kk * bcol * dmat, 0.0))
        g_last = jnp.where(head0_rows, gcol[C - 1:C], gcol[P - 1:P])
        pre[hq, ci] = dict(
            vb=_stack_heads(v_ref, r0, l0) * bcol,
            kb=k2 * (bcol * egc),
            qd=(q2 * egc).astype(BF16),
            attn=(qk * dmat).astype(BF16),
            k_t=k_c.T.astype(BF16),
            v_scale=jnp.exp(g_last - gcol),
            decay_cat=jnp.concatenate([egc[C - 1:C], egc[P - 1:P]], axis=1))
    e_all = _unit_lower_inverse_minus_eye(a_low, r_i, c_i)
    for (hq, ci), e in zip(units, e_all):
        d = pre[hq, ci]
        e_b = e.astype(BF16)
        d["u"] = d["vb"] + _dot(e_b, d["vb"].astype(BF16))
        d["w"] = (d["kb"] + _dot(e_b, d["kb"].astype(BF16))).astype(BF16)

    s_cat = [state[hq] for hq in range(GDN_QK_HEADS)]
    for ci in range(n_chunks):
        r0 = ci * C
        for hq in range(GDN_QK_HEADS):
            d = pre[hq, ci]
            l0 = hq * 2 * DV
            s_b = s_cat[hq].astype(BF16)
            v_new = d["u"] - _own_head_block(_dot(d["w"], s_b))
            o2 = _own_head_block(_dot(d["qd"], s_b)) + _dot(d["attn"], v_new.astype(BF16))
            v_dec = (v_new * d["v_scale"]).astype(BF16)
            v_dec_cat = jnp.concatenate([v_dec[0:C], v_dec[C:]], axis=1)
            s_cat[hq] = s_cat[hq] * d["decay_cat"] + _dot(d["k_t"], v_dec_cat)
            o_n = o2 * lax.rsqrt(jnp.mean(o2 * o2, axis=-1, keepdims=True) + RMS_EPS)
            res = (o_n * nw * _silu(_stack_heads(z_ref, r0, l0))).astype(o_ref.dtype)
            o_ref[r0:r0 + C, l0:l0 + DV] = res[0:C]
            o_ref[r0:r0 + C, l0 + DV:l0 + 2 * DV] = res[C:]
    for hq in range(GDN_QK_HEADS):
        state[hq] = s_cat[hq]


def _gdn(proj, bsz, t_len, dtb_row, alog_row, nw_row, eb, ea, *, tc):
    nt = t_len // tc
    row = lambda b, t: b * nt + t
    const = lambda b, t: (0, 0)
    sm_blk = HYB_SMALL_OFF // LANES
    return pl.pallas_call(
        functools.partial(_gdn_kernel, tc=tc),
        grid=(bsz, nt),
        in_specs=[
            pl.BlockSpec((tc, GDN_QK_W), lambda b, t: (row(b, t), 0)),
            pl.BlockSpec((tc, GDN_QK_W), lambda b, t: (row(b, t), 1)),
            pl.BlockSpec((tc, GDN_V_W), lambda b, t: (row(b, t), 1)),
            pl.BlockSpec((tc, GDN_V_W), lambda b, t: (row(b, t), 2)),
            pl.BlockSpec((tc, LANES), lambda b, t: (row(b, t), sm_blk)),
            pl.BlockSpec((1, LANES), const),
            pl.BlockSpec((1, LANES), const),
            pl.BlockSpec((1, GDN_DV), const),
            pl.BlockSpec((LANES, GDN_V_W), const),
            pl.BlockSpec((LANES, GDN_V_W), const),
        ],
        out_specs=pl.BlockSpec((tc, GDN_V_W), lambda b, t: (row(b, t), 0)),
        out_shape=jax.ShapeDtypeStruct((bsz * t_len, GDN_V_W), BF16),
        scratch_shapes=[
            pltpu.VMEM((GDN_QK_HEADS, GDN_DK, 2 * GDN_DV), F32),
        ],
        compiler_params=_cparams("parallel", "arbitrary"),
        name="gdn",
    )(proj, proj, proj, proj, proj, dtb_row, alog_row, nw_row, eb, ea)


def _group_variants(x):
    lane = lax.broadcasted_iota(jnp.int32, (1, LANES), 1)
    lo = lane < (LANES // 2)
    xr = pltpu.roll(x, LANES // 2, axis=1)
    zero = jnp.zeros_like(x)
    return jnp.concatenate([jnp.where(lo, x, zero), jnp.where(lo, zero, xr),
                            jnp.where(lo, xr, zero), jnp.where(lo, zero, x)], axis=1)


def _values_transposed(v):
    zt = _group_variants(v).T
    extra_r = lax.broadcasted_iota(jnp.int32, (VT_ROWS - LANES, v.shape[0]), 0)
    parts = []
    for var in range(2 * NSA_GROUPS):
        parts.append(zt[var * LANES:(var + 1) * LANES])
        parts.append(jnp.where(extra_r == (var % 2), 1.0, 0.0))
    return jnp.concatenate(parts, axis=0)


def _nsa_prep_kernel(q_ref, kc_ref, vc_ref, ks_ref, vs_ref, kw_ref, vw_ref, pos_ref, freq_ref, sel_ref,
                     one_ref, qc_o, qr_o, k16_o, v16_o, ksz_o, vsz_o, kwz_o, vwz_o):
    scale = NSA_DK ** -0.5 * LOG2E
    rows16 = k16_o.shape[0]
    for src, dst in ((kc_ref, k16_o), (vc_ref, v16_o)):
        for j in range(CMP_STRIDE):
            dst[:, j * LANES:(j + 1) * LANES] = src[pl.ds(j, rows16, stride=CMP_STRIDE), :]
    ang = freq_ref[...] * pos_ref[...].astype(F32)
    half = ROPE_DIM // 2
    tm = ang.shape[1]
    cs = jnp.concatenate([jnp.cos(ang), jnp.sin(ang), jnp.zeros((LANES - 2 * half, tm), F32)], axis=0)
    spread = _dot_sel_rhs(cs.T, sel_ref[...])
    cos = spread[:, 0:LANES] + one_ref[...]
    sin_up = spread[:, LANES:2 * LANES]
    sin_dn = spread[:, 2 * LANES:3 * LANES]

    def rope(x):
        return x * cos + pltpu.roll(x, half, axis=1) * sin_up + pltpu.roll(x, LANES - half, axis=1) * sin_dn

    for p in range(NSA_HEADS * NSA_DK // LANES):
        x = q_ref[:, p * LANES:(p + 1) * LANES]
        qc_o[:, p * LANES:(p + 1) * LANES] = (x * scale).astype(qc_o.dtype)
        qr_o[:, p * LANES:(p + 1) * LANES] = (rope(x) * scale).astype(qr_o.dtype)
    ksz_o[...] = _group_variants(rope(ks_ref[...])).astype(ksz_o.dtype)
    kwz_o[...] = _group_variants(rope(kw_ref[...])).astype(kwz_o.dtype)
    vsz_o[...] = _values_transposed(vs_ref[...]).astype(vsz_o.dtype)
    vwz_o[...] = _values_transposed(vw_ref[...]).astype(vwz_o.dtype)


def _nsa_prep(proj, pos_row, freq_col, rope_sel, one_row, bsz, t_len, *, tm):
    m = proj.shape[0]
    tiles_per_seq = t_len // tm
    seq_t = lambda i: (i // tiles_per_seq, 0, i % tiles_per_seq)
    qw = NSA_HEADS * NSA_DK
    kv0 = qw // LANES
    const = lambda i: (0, 0)
    zw = 4 * LANES
    return pl.pallas_call(
        _nsa_prep_kernel,
        grid=(m // tm,),
        in_specs=[
            pl.BlockSpec((tm, qw), lambda i: (i, 0)),
            pl.BlockSpec((tm, LANES), lambda i: (i, kv0)),
            pl.BlockSpec((tm, LANES), lambda i: (i, kv0 + 1)),
            pl.BlockSpec((tm, LANES), lambda i: (i, kv0 + 2)),
            pl.BlockSpec((tm, LANES), lambda i: (i, kv0 + 3)),
            pl.BlockSpec((tm, LANES), lambda i: (i, kv0 + 4)),
            pl.BlockSpec((tm, LANES), lambda i: (i, kv0 + 5)),
            pl.BlockSpec((1, tm), lambda i: (0, i)),
            pl.BlockSpec(freq_col.shape, const),
            pl.BlockSpec(rope_sel.shape, const),
            pl.BlockSpec((1, LANES), const),
        ],
        out_specs=[
            pl.BlockSpec((tm, qw), lambda i: (i, 0)),
            pl.BlockSpec((tm, qw), lambda i: (i, 0)),
            pl.BlockSpec((tm // CMP_STRIDE, CMP_STRIDE * LANES), lambda i: (i, 0)),
            pl.BlockSpec((tm // CMP_STRIDE, CMP_STRIDE * LANES), lambda i: (i, 0)),
            pl.BlockSpec((tm, zw), lambda i: (i, 0)),
            pl.BlockSpec((None, 4 * VT_ROWS, tm), seq_t),
            pl.BlockSpec((tm, zw), lambda i: (i, 0)),
            pl.BlockSpec((None, 4 * VT_ROWS, tm), seq_t),
        ],
        out_shape=[
            jax.ShapeDtypeStruct((m, qw), BF16),
            jax.ShapeDtypeStruct((m, qw), BF16),
            jax.ShapeDtypeStruct((m // CMP_STRIDE, CMP_STRIDE * LANES), F32),
            jax.ShapeDtypeStruct((m // CMP_STRIDE, CMP_STRIDE * LANES), F32),
            jax.ShapeDtypeStruct((m, zw), BF16),
            jax.ShapeDtypeStruct((bsz, 4 * VT_ROWS, t_len), BF16),
            jax.ShapeDtypeStruct((m, zw), BF16),
            jax.ShapeDtypeStruct((bsz, 4 * VT_ROWS, t_len), BF16),
        ],
        compiler_params=_cparams("parallel"),
        name="nsa_prep",
    )(proj, proj, proj, proj, proj, proj, proj, pos_row, freq_col, rope_sel, one_row)


def _nsa_compress_kernel(k16_ref, v16_ref, pos_ref, w1a_ref, w1b_ref, w2_ref, kcz_o, vcz_o):
    nr = k16_ref.shape[0]
    for idx, (x_ref, o_ref) in enumerate(((k16_ref, kcz_o), (v16_ref, vcz_o))):
        x = x_ref[...]
        h_a = _dot((x + pos_ref[idx, 0:1, :]).astype(BF16), w1a_ref[idx])
        h_b = _dot((x + pos_ref[idx, 1:2, :]).astype(BF16), w1b_ref[idx])
        hid = h_a + pltpu.roll(h_b, nr - 1, axis=0)
        out = _dot(_silu(hid).astype(BF16), w2_ref[idx])
        z = _group_variants(out)
        o_ref[...] = (z.T if idx == 1 else z).astype(o_ref.dtype)


def _nsa_compress(k16, v16, pos_ab, w1a, w1b, w2):
    bsz, nr, width = k16.shape
    hid = NSA_GROUPS * CMP_HIDDEN
    zw = 4 * LANES
    c3 = lambda b: (0, 0, 0)
    return pl.pallas_call(
        _nsa_compress_kernel,
        grid=(bsz,),
        in_specs=[
            pl.BlockSpec((None, nr, width), lambda b: (b, 0, 0)),
            pl.BlockSpec((None, nr, width), lambda b: (b, 0, 0)),
            pl.BlockSpec((2, 2, width), c3),
            pl.BlockSpec((2, width, hid), c3),
            pl.BlockSpec((2, width, hid), c3),
            pl.BlockSpec((2, hid, LANES), c3),
        ],
        out_specs=[
            pl.BlockSpec((None, nr, zw), lambda b: (b, 0, 0)),
            pl.BlockSpec((None, zw, nr), lambda b: (b, 0, 0)),
        ],
        out_shape=[
            jax.ShapeDtypeStruct((bsz, nr, zw), BF16),
            jax.ShapeDtypeStruct((bsz, zw, nr), BF16),
        ],
        compiler_params=_cparams("parallel"),
        name="nsa_compress",
    )(k16, v16, pos_ab, w1a, w1b, w2)


def _flash_branch(q_stacks, kz_ref, vzt_ref, lo, hi, bias_fns, kb_size):
    r_acc = lax.broadcasted_iota(jnp.int32, (VT_ROWS, 1), 0)
    even_rows = (r_acc < LANES // 2) | (r_acc == LANES)
    half_w = 2 * Q_BLOCK
    n_half = q_stacks[0].shape[0] // half_w
    units = [(g, h) for g in range(NSA_GROUPS) for h in range(n_half)]
    q_unit = [q_stacks[g][h * half_w:(h + 1) * half_w] for g, h in units]

    def body(kb, carry):
        m_in, acc = carry
        ks = pl.multiple_of(kb * kb_size, kb_size)
        k_both, vt_both, bias = [], [], []
        for g in range(NSA_GROUPS):
            k0 = 2 * g * LANES
            v0 = 2 * g * VT_ROWS
            k_both.append(jnp.concatenate([kz_ref[pl.ds(ks, kb_size), k0:k0 + LANES],
                                           kz_ref[pl.ds(ks, kb_size), k0 + LANES:k0 + 2 * LANES]], axis=0))
            vt_both.append(jnp.concatenate([vzt_ref[v0:v0 + VT_ROWS, pl.ds(ks, kb_size)],
                                            vzt_ref[v0 + VT_ROWS:v0 + 2 * VT_ROWS, pl.ds(ks, kb_size)]],
                                           axis=1))
            b = bias_fns[g](ks)
            bias.append(jnp.concatenate([b, b], axis=1))
        n_u = len(units)
        s, m_out, alpha_rows, p_cat, pv = {}, {}, {}, {}, {}

        def softmax_stage(u):
            g = units[u][0]
            p_parts, alphas, m_news = [], [], []
            for var in range(2):
                s_v = s[u][var * kb_size:(var + 1) * kb_size] + bias[g]
                m_prev = m_in[u][var]
                m_new = jnp.maximum(m_prev, jnp.max(s_v, axis=0, keepdims=True))
                alphas.append(jnp.exp2(m_prev - m_new))
                p_parts.append(jnp.exp2(s_v - m_new).astype(BF16))
                m_news.append(m_new)
            m_out[u] = tuple(m_news)
            alpha_rows[u] = jnp.where(even_rows, alphas[0], alphas[1])
            p_cat[u] = jnp.concatenate(p_parts, axis=0)

        for u in range(n_u):
            s[u] = _dot_nt(k_both[units[u][0]], q_unit[u])
        for u in range(n_u):
            softmax_stage(u)
        for u in range(n_u):
            pv[u] = _dot(vt_both[units[u][0]], p_cat[u])
        acc_out = tuple(acc[u] * alpha_rows[u] + pv[u] for u in range(n_u))
        return tuple(m_out[u] for u in range(n_u)), acc_out

    m_init = tuple((jnp.full((1, half_w), NEG_BIG, F32),) * 2 for _ in units)
    acc_init = tuple(jnp.zeros((VT_ROWS, half_w), F32) for _ in units)
    _, acc = lax.fori_loop(lo, hi, body, (m_init, acc_init))
    outs = []
    for g in range(NSA_GROUPS):
        a = jnp.concatenate(acc[g * n_half:(g + 1) * n_half], axis=1)
        inv_even = 1.0 / a[LANES:LANES + 1, :]
        inv_odd = 1.0 / a[LANES + 1:LANES + 2, :]
        outs.append(a[0:LANES, :] * jnp.where(r_acc[0:LANES] < LANES // 2, inv_even, inv_odd))
    return outs


def _flash_branch_bounded(q_exts, kz_ref, kext_ref, vzt_ref, lo, hi, bias_fns, last_bias_fns, kb_size):
    r_acc = lax.broadcasted_iota(jnp.int32, (VT_ROWS, 1), 0)
    half_w = 2 * Q_BLOCK
    n_half = q_exts[0].shape[0] // half_w
    units = [(g, h) for g in range(NSA_GROUPS) for h in range(n_half)]
    q_unit = [q_exts[g][h * half_w:(h + 1) * half_w] for g, h in units]
    n_u = len(units)

    def step(kbs, acc, fns):
        work = []
        for kb in kbs:
            ks = pl.multiple_of(jnp.maximum(kb, 0) * kb_size, kb_size)
            k_ext = kext_ref[pl.ds(ks, kb_size), :]
            for g in range(NSA_GROUPS):
                k0 = 2 * g * LANES
                v0 = 2 * g * VT_ROWS
                k_both = jnp.concatenate(
                    [jnp.concatenate([kz_ref[pl.ds(ks, kb_size), k0:k0 + LANES], k_ext], axis=1),
                     jnp.concatenate([kz_ref[pl.ds(ks, kb_size), k0 + LANES:k0 + 2 * LANES], k_ext],
                                     axis=1)], axis=0)
                vt_both = jnp.concatenate([vzt_ref[v0:v0 + VT_ROWS, pl.ds(ks, kb_size)],
                                           vzt_ref[v0 + VT_ROWS:v0 + 2 * VT_ROWS, pl.ds(ks, kb_size)]],
                                          axis=1)
                bias = None
                if fns is not None:
                    b = fns[g](kb, ks)
                    b = jnp.concatenate([b, b], axis=1)
                    bias = jnp.concatenate([b, b], axis=0)
                work += [(u, k_both, vt_both, bias) for u in range(n_u) if units[u][0] == g]
        s = [_dot_nt(k, q_unit[u]) for u, k, _, _ in work]
        p = [jnp.exp2(s_i if w[3] is None else s_i + w[3]).astype(BF16) for s_i, w in zip(s, work)]
        pv = [_dot(w[2], p_i) for p_i, w in zip(p, work)]
        out = list(acc)
        for (u, _, _, _), pv_i in zip(work, pv):
            out[u] = out[u] + pv_i
        return tuple(out)

    acc = tuple(jnp.zeros((VT_ROWS, half_w), F32) for _ in units)
    if isinstance(lo, (list, tuple)):
        acc = step(lo, acc, bias_fns)
    elif last_bias_fns is None:
        acc = lax.fori_loop(lo, hi, lambda kb, a: step([kb], a, bias_fns), acc)
    else:
        acc = lax.fori_loop(lo, hi - 1, lambda kb, a: step([kb], a, bias_fns), acc)
        acc = step([hi - 1], acc, last_bias_fns)
    outs = []
    l_min = None
    for g in range(NSA_GROUPS):
        a = jnp.concatenate(acc[g * n_half:(g + 1) * n_half], axis=1)
        l_even = a[LANES:LANES + 1, :]
        l_odd = a[LANES + 1:LANES + 2, :]
        outs.append(a[0:LANES, :] * jnp.where(r_acc[0:LANES] < LANES // 2, 1.0 / l_even, 1.0 / l_odd))
        l_g = jnp.min(jnp.minimum(l_even, l_odd))
        l_min = l_g if l_min is None else jnp.minimum(l_min, l_g)
    return outs, l_min


def _nsa_attn_kernel(qc_ref, qr_ref, gate_ref, kcz_ref, vczt_ref, ksz_ref, vszt_ref, kwz_ref, vwzt_ref,
                     aggt_ref, eselt_ref, o_ref, bias_ref, kmax_ref,
                     *, t_len, kb_size):
    n_cmp_rows = t_len // CMP_STRIDE
    n_cmp = (t_len - CMP_LEN) // CMP_STRIDE + 1
    n_sel = t_len // SEL_LEN
    n_top = min(SEL_TOPK, n_sel)
    pairs = NSA_HEADS // NSA_GROUPS // 2
    q0 = pl.program_id(1) * Q_BLOCK
    t_row = q0 + lax.broadcasted_iota(jnp.int32, (1, Q_BLOCK), 1)
    t_row_stack = q0 + (lax.broadcasted_iota(jnp.int32, (1, pairs * Q_BLOCK), 1) & (Q_BLOCK - 1))
    blk = lax.broadcasted_iota(jnp.int32, (n_sel, 1), 0)
    c_idx = lax.broadcasted_iota(jnp.int32, (n_cmp_rows, 1), 0)
    cmp_mask = ((c_idx * CMP_STRIDE + (CMP_LEN - 1)) <= t_row_stack) & (c_idx < n_cmp)
    key_pos = lax.broadcasted_iota(jnp.int32, (t_len, 1), 0)
    gate_t = _sigmoid(gate_ref[...]).T
    top_half = lax.broadcasted_iota(jnp.int32, (LANES, 1), 0) < (LANES // 2)
    hi = (q0 + (Q_BLOCK - 1)) // kb_size + 1
    win_lo = jnp.maximum(q0 - (WINDOW - 1), 0) // kb_size

    def win_bias(ks):
        kp = ks + lax.broadcasted_iota(jnp.int32, (kb_size, 1), 0)
        return jnp.where((kp <= t_row) & (kp > t_row - WINDOW), 0.0, NEG_BIG)

    def sel_bias(g):
        return lambda ks: bias_ref[g, pl.ds(ks, kb_size), :]

    @pl.when(pl.program_id(1) == 0)
    def _():
        for br, k_ref in enumerate((ksz_ref, kwz_ref)):
            for g in range(NSA_GROUPS):
                k = k_ref[:, 2 * g * LANES:(2 * g + 1) * LANES].astype(F32)
                n2 = jnp.max(jnp.sum(k * k, axis=1, keepdims=True), axis=0, keepdims=True)
                kmax_ref[br * NSA_GROUPS + g] = jnp.broadcast_to(jnp.sqrt(n2), (SUBLANES, LANES))

    qr_stacks, o_cmps, sels = [], [], []
    for g in range(NSA_GROUPS):
        lanes_g = g * pairs * LANES
        qc_stack = jnp.concatenate(
            [qc_ref[:, lanes_g + p * LANES:lanes_g + (p + 1) * LANES] for p in range(pairs)], axis=0)
        qr_stacks.append(jnp.concatenate(
            [qr_ref[:, lanes_g + p * LANES:lanes_g + (p + 1) * LANES] for p in range(pairs)], axis=0))

        o_cmp = None
        p_sum = None
        for var in range(2):
            c0 = (2 * g + var) * LANES
            s = jnp.where(cmp_mask, _dot_nt(kcz_ref[:, c0:c0 + LANES], qc_stack), -jnp.inf)
            mx = jnp.max(s, axis=0, keepdims=True)
            mx = jnp.where(mx > -jnp.inf, mx, 0.0)
            e = jnp.exp2(s - mx)
            p = e * (1.0 / jnp.maximum(jnp.sum(e, axis=0, keepdims=True), 1e-30))
            d = _dot(vczt_ref[c0:c0 + LANES, :], p.astype(BF16))
            o_cmp = d if o_cmp is None else o_cmp + d
            for pr in range(pairs):
                part = p[:, pr * Q_BLOCK:(pr + 1) * Q_BLOCK]
                p_sum = part if p_sum is None else p_sum + part
        importance = _dot_sel_lhs(aggt_ref[...], p_sum)

        cur = t_row >> 6
        causal_blk = blk <= cur
        forced = (blk == 0) | (causal_blk & (blk > cur - SEL_LOCAL))
        score = jnp.where(forced, FORCE_SCORE, jnp.where(causal_blk, importance, -1.0))
        rank = jnp.zeros((n_sel, Q_BLOCK), F32)
        for jj in range(n_sel):
            row = score[jj:jj + 1, :]
            beats = (row > score) | ((row == score) & (blk > jj))
            rank = rank + jnp.where(beats, 1.0, 0.0)
        sel = jnp.where(rank < n_top, 1.0, 0.0)
        sels.append(jnp.concatenate([sel, jnp.zeros((LANES - n_sel, Q_BLOCK), F32)], axis=0))
        o_cmps.append(o_cmp)

    lane = lax.broadcasted_iota(jnp.int32, (1, LANES), 1)
    blk_pad = lax.broadcasted_iota(jnp.int32, (LANES, 1), 0)
    q_ext_sel, q_ext_win = [], []
    for g in range(NSA_GROUPS):
        x = qr_stacks[g].astype(F32)
        sq = x * x
        tot = jnp.sum(sq, axis=1, keepdims=True)
        even = jnp.sum(jnp.where(lane < LANES // 2, sq, 0.0), axis=1, keepdims=True)
        q_norm = jnp.sqrt(jnp.maximum(even, tot - even))
        sel_c = jnp.where(blk_pad * SEL_LEN <= t_row, sels[g], 0.0).T
        sel_b = jnp.where(lane < n_sel, (sel_c - 1.0) * (-NEG_BIG), 0.0)
        sel_b = jnp.concatenate([sel_b] * pairs, axis=0)
        ext_s = jnp.where(lane == n_sel, -q_norm * kmax_ref[g][0:1, 0:1], sel_b)
        ext_w = jnp.where(lane == n_sel, -q_norm * kmax_ref[NSA_GROUPS + g][0:1, 0:1], 0.0)
        q_ext_sel.append(jnp.concatenate([qr_stacks[g], ext_s.astype(BF16)], axis=1))
        q_ext_win.append(jnp.concatenate([qr_stacks[g], ext_w.astype(BF16)], axis=1))

    sel_kb = min(2 * kb_size, t_len)

    def diag_bias(kb, ks):
        kp = ks + lax.broadcasted_iota(jnp.int32, (sel_kb, 1), 0)
        return jnp.where(kp <= t_row, 0.0, NEG_BIG)

    n_win = WINDOW // kb_size + 1
    win_blocks = [hi - n_win + j for j in range(n_win)]

    def win_bias_fast(kb, ks):
        return jnp.where(kb >= 0, win_bias(ks), NEG_BIG)

    fast_sel, l_sel = _flash_branch_bounded(q_ext_sel, ksz_ref, eselt_ref, vszt_ref, 0,
                                            (q0 + (Q_BLOCK - 1)) // sel_kb + 1, None,
                                            [diag_bias] * NSA_GROUPS, sel_kb)
    fast_win, l_win = _flash_branch_bounded(q_ext_win, kwz_ref, eselt_ref, vwzt_ref, win_blocks, None,
                                            [win_bias_fast] * NSA_GROUPS, None, kb_size)

    def robust():
        for g in range(NSA_GROUPS):
            sel_keys = _dot(eselt_ref[...], sels[g].astype(BF16))
            bias_ref[g] = jnp.where((sel_keys > 0.5) & (key_pos <= t_row), 0.0, NEG_BIG)
        o_s = _flash_branch(qr_stacks, ksz_ref, vszt_ref, 0, hi,
                            [sel_bias(g) for g in range(NSA_GROUPS)], kb_size)
        o_w = _flash_branch(qr_stacks, kwz_ref, vwzt_ref, win_lo, hi, [win_bias] * NSA_GROUPS, kb_size)
        return tuple(o_s) + tuple(o_w)

    o_all = lax.cond(jnp.minimum(l_sel, l_win) > DENOM_FLOOR,
                     lambda: tuple(fast_sel) + tuple(fast_win), robust)
    o_sels, o_wins = o_all[:NSA_GROUPS], o_all[NSA_GROUPS:]

    for g in range(NSA_GROUPS):
        lanes_g = g * pairs * LANES
        for pr in range(pairs):
            h_even = (g * pairs + pr) * 2
            cols = slice(pr * Q_BLOCK, (pr + 1) * Q_BLOCK)
            mixed = None
            for j, o_branch in enumerate((o_cmps[g], o_sels[g], o_wins[g])):
                r_e = h_even * 3 + j
                r_o = r_e + 3
                gate_rows = jnp.where(top_half, gate_t[r_e:r_e + 1, :], gate_t[r_o:r_o + 1, :])
                term = gate_rows * o_branch[:, cols]
                mixed = term if mixed is None else mixed + term
            c0 = lanes_g + pr * LANES
            o_ref[:, c0:c0 + LANES] = mixed.T.astype(o_ref.dtype)


def _nsa_attn(qc, qr, proj, kcz, vczt, ksz, vszt, kwz, vwzt, aggt, eselt, bsz, t_len, *, kb_size):
    nq = t_len // Q_BLOCK
    qw = NSA_HEADS * NSA_DK
    zw = 4 * LANES
    nr = t_len // CMP_STRIDE
    pairs = NSA_HEADS // NSA_GROUPS // 2
    row = lambda b, i: (b * nq + i, 0)
    per_b = lambda b, i: (b, 0, 0)
    return pl.pallas_call(
        functools.partial(_nsa_attn_kernel, t_len=t_len, kb_size=kb_size),
        grid=(bsz, nq),
        in_specs=[
            pl.BlockSpec((Q_BLOCK, qw), row),
            pl.BlockSpec((Q_BLOCK, qw), row),
            pl.BlockSpec((Q_BLOCK, LANES), lambda b, i: (b * nq + i, NSA_GATE_OFF // LANES)),
            pl.BlockSpec((None, nr, zw), per_b),
            pl.BlockSpec((None, zw, nr), per_b),
            pl.BlockSpec((None, t_len, zw), per_b),
            pl.BlockSpec((None, 4 * VT_ROWS, t_len), per_b),
            pl.BlockSpec((None, t_len, zw), per_b),
            pl.BlockSpec((None, 4 * VT_ROWS, t_len), per_b),
            pl.BlockSpec(aggt.shape, lambda b, i: (0, 0)),
            pl.BlockSpec(eselt.shape, lambda b, i: (0, 0)),
        ],
        out_specs=pl.BlockSpec((Q_BLOCK, qw), row),
        out_shape=jax.ShapeDtypeStruct((bsz * t_len, qw), BF16),
        scratch_shapes=[
            pltpu.VMEM((NSA_GROUPS, t_len, Q_BLOCK), F32),
            pltpu.VMEM((2 * NSA_GROUPS, SUBLANES, LANES), F32),
        ],
        compiler_params=_cparams("parallel", "arbitrary"),
        name="nsa_attn",
    )(qc, qr, proj, kcz, vczt, ksz, vszt, kwz, vwzt, aggt, eselt)


def _expand_rows(first_row, n_heads, width):
    e = np.zeros((LANES, n_heads * width), np.float32)
    for h in range(n_heads):
        e[first_row + h, h * width:(h + 1) * width] = 1.0
    return e


def _nsa_constants(t_len):
    n_cmp = (t_len - CMP_LEN) // CMP_STRIDE + 1
    n_sel = t_len // SEL_LEN
    nr = t_len // CMP_STRIDE
    c0 = np.arange(n_cmp)[:, None] * CMP_STRIDE
    s0 = np.arange(n_sel)[None, :] * SEL_LEN
    overlap = np.clip(np.minimum(c0 + CMP_LEN, s0 + SEL_LEN) - np.maximum(c0, s0), 0, None) / CMP_LEN
    aggt = np.zeros((n_sel, nr), np.float32)
    aggt[:, :n_cmp] = overlap.T
    eselt = np.zeros((t_len, LANES), np.float32)
    eselt[np.arange(t_len), np.arange(t_len) // SEL_LEN] = 1.0
    eselt[:, n_sel] = 1.0
    half = ROPE_DIM // 2
    inv_freq = (ROPE_THETA ** (-np.arange(half) / half)).astype(np.float32)
    d = np.arange(LANES) % NSA_DK
    rope_sel = np.zeros((LANES, 3 * LANES), np.float32)
    for lane in range(LANES):
        if d[lane] < ROPE_DIM:
            f = d[lane] % half
            rope_sel[f, lane] = 1.0
            if d[lane] >= half:
                rope_sel[half + f, LANES + lane] = 1.0
            else:
                rope_sel[half + f, 2 * LANES + lane] = -1.0
    one_row = (d >= ROPE_DIM).astype(np.float32)[None]
    return aggt, eselt, inv_freq[:, None], rope_sel, one_row


def _pad_row(vec, offset):
    return jnp.zeros((1, LANES), F32).at[0, offset:offset + vec.shape[0]].set(vec.astype(F32))


def _hybrid_layer(h, bsz, t_len, w_in, gdn_conv_w, gdn_a_log, gdn_dt_bias, gdn_norm_w, ssd_conv_w,
                  ssd_conv_b, ssd_a_log, ssd_dt_bias, ssd_d, ssd_norm_w, w_out, ln_g, ln_b, cfg):
    b_off = GDN_CONV_DIM + GDN_V_W
    z_off = b_off + 2 * GDN_V_HEADS
    dt_off = z_off + SSD_D_INNER + SSD_CONV_DIM
    w_pad = jnp.concatenate(
        [w_in[:, :b_off], w_in[:, z_off:dt_off], w_in[:, b_off:z_off], w_in[:, dt_off:],
         jnp.zeros((D_MODEL, HYB_PAD - w_in.shape[1]), w_in.dtype)], axis=1).astype(BF16)
    conv_w = jnp.concatenate([gdn_conv_w, ssd_conv_w], axis=1).astype(F32)
    conv_b = jnp.concatenate([jnp.zeros((GDN_CONV_DIM,), F32), ssd_conv_b.astype(F32)])[None]
    proj = _proj_conv(h, w_pad, conv_w, conv_b, t_len, tm=min(cfg["proj_tm"], t_len), tn=cfg["hyb_tn"])
    o_a = _gdn(proj, bsz, t_len, _pad_row(gdn_dt_bias, GDN_V_HEADS),
               _pad_row(gdn_a_log, GDN_V_HEADS), gdn_norm_w[None].astype(F32),
               jnp.asarray(_expand_rows(0, GDN_V_HEADS, GDN_DV)).astype(BF16),
               jnp.asarray(_expand_rows(GDN_V_HEADS, GDN_V_HEADS, GDN_DV)).astype(BF16), tc=cfg["gdn_tc"])
    o_b = _ssd(proj, bsz, t_len, _pad_row(ssd_dt_bias, 16),
               _pad_row(ssd_a_log, 16), jnp.repeat(ssd_d, SSD_HEADDIM)[None], ssd_norm_w[None],
               jnp.asarray(_expand_rows(16, SSD_HEADS, SSD_HEADDIM)).astype(BF16),
               n_sub=cfg["ssd_sub"])
    w_out_b = w_out.astype(BF16)
    return _out_ln([o_a, o_b], [w_out_b[:GDN_V_W], w_out_b[GDN_V_W:]], h, ln_g, ln_b, tm=cfg["out_tm"])


def _nsa_layer(h, pos_row, bsz, t_len, w_in, cmp_pos, cmp_w1, cmp_w2, w_out, ln_g, ln_b, cfg):
    aggt, eselt, freq_col, rope_sel, one_row = _nsa_constants(t_len)
    w_pad = jnp.concatenate(
        [w_in, jnp.zeros((D_MODEL, NSA_PAD - w_in.shape[1]), w_in.dtype)], axis=1).astype(BF16)
    proj = _proj(h, w_pad, tm=cfg["nsa_proj_tm"], tn=cfg["nsa_tn"])
    qc, qr, k16, v16, ksz, vszt, kwz, vwzt = _nsa_prep(
        proj, pos_row, jnp.asarray(freq_col), jnp.asarray(rope_sel).astype(BF16), jnp.asarray(one_row),
        bsz, t_len, tm=cfg["prep_tm"])
    nr = t_len // CMP_STRIDE
    width = CMP_STRIDE * LANES
    k16 = k16.reshape(bsz, nr, width)
    v16 = v16.reshape(bsz, nr, width)
    pos2 = jnp.broadcast_to(cmp_pos[:, :, None, :], (2, CMP_LEN, NSA_GROUPS, NSA_DK)).reshape(2, 2, width)
    w1 = cmp_w1.reshape(2, 2, CMP_STRIDE, NSA_DK, CMP_HIDDEN)
    eye_g = jnp.eye(NSA_GROUPS, dtype=w1.dtype)
    w1x = jnp.einsum("ksjdh,ge->ksjgdeh", w1, eye_g).reshape(
        2, 2, width, NSA_GROUPS * CMP_HIDDEN).astype(BF16)
    w2x = jnp.einsum("khd,ge->kghed", cmp_w2, eye_g).reshape(
        2, NSA_GROUPS * CMP_HIDDEN, NSA_GROUPS * NSA_DK).astype(BF16)
    kcz, vczt = _nsa_compress(k16, v16, pos2, w1x[:, 0], w1x[:, 1], w2x)
    o = _nsa_attn(qc, qr, proj, kcz, vczt, ksz.reshape(bsz, t_len, -1), vszt,
                  kwz.reshape(bsz, t_len, -1), vwzt,
                  jnp.asarray(aggt).astype(BF16), jnp.asarray(eselt).astype(BF16),
                  bsz, t_len, kb_size=cfg["attn_kb"])
    return _out_ln([o], [w_out.astype(BF16)], h, ln_g, ln_b, tm=cfg["out_tm"])


def _config(m, t_len):
    return dict(ffn_tm=min(1024, m), ffn_tf=256, proj_tm=min(512, m), hyb_tn=512, nsa_tn=NSA_PAD,
                out_tm=min(1024, m), gdn_tc=min(256, t_len), ssd_sub=min(4, t_len // SSD_CHUNK), prep_tm=min(512, t_len),
                nsa_proj_tm=min(1024, m),
                attn_kb=min(256, t_len))


def kernel(x, positions, ln_g, ln_b, ffn_w_in, ffn_w_out, hyb_w_in, gdn_conv_w, gdn_a_log, gdn_dt_bias, gdn_norm_w, ssd_conv_w, ssd_conv_b, ssd_a_log, ssd_dt_bias, ssd_d, ssd_norm_w, hyb_w_out, nsa_w_in, nsa_cmp_pos, nsa_cmp_w1, nsa_cmp_w2, nsa_w_out):
    bsz, t_len, _ = x.shape
    m = bsz * t_len
    cfg = _config(m, t_len)
    h = x.reshape(m, D_MODEL)
    pos_row = positions.reshape(1, m)
    w_in_b = ffn_w_in.astype(BF16)
    w_out_b = ffn_w_out.astype(BF16)
    ln_g = ln_g[:, :, None, :]
    ln_b = ln_b[:, :, None, :]
    for layer in range(DEPTH):
        i = layer // 2
        h = _ffn_ln(h, w_in_b, w_out_b, ln_g[layer, 0], ln_b[layer, 0], layer, 0,
                    tm=cfg["ffn_tm"], tf=cfg["ffn_tf"])
        if layer % 2 == 0:
            h = _hybrid_layer(h, bsz, t_len, hyb_w_in[i], gdn_conv_w[i], gdn_a_log[i], gdn_dt_bias[i],
                              gdn_norm_w[i], ssd_conv_w[i], ssd_conv_b[i], ssd_a_log[i], ssd_dt_bias[i],
                              ssd_d[i], ssd_norm_w[i], hyb_w_out[i], ln_g[layer, 1], ln_b[layer, 1], cfg)
        else:
            h = _nsa_layer(h, pos_row, bsz, t_len, nsa_w_in[i], nsa_cmp_pos[i], nsa_cmp_w1[i],
                           nsa_cmp_w2[i], nsa_w_out[i], ln_g[layer, 1], ln_b[layer, 1], cfg)
        h = _ffn_ln(h, w_in_b, w_out_b, ln_g[layer, 2], ln_b[layer, 2], layer, 1,
                    tm=cfg["ffn_tm"], tf=cfg["ffn_tf"])
    return h.reshape(bsz, t_len, D_MODEL)
```

```python
import functools

import numpy as np
import jax
import jax.numpy as jnp
from jax import lax
from jax.experimental import pallas as pl
from jax.experimental.pallas import tpu as pltpu

F32 = jnp.float32
BF16 = jnp.bfloat16

D_MODEL = 1024
DEPTH = 4
D_FF = 2816
DEEPNORM_ALPHA = (2.0 * DEPTH) ** 0.25
LN_EPS = 1e-5
RMS_EPS = 1e-6
CONV_K = 4

GDN_QK_HEADS = 4
GDN_V_HEADS = 8
GDN_DK = 128
GDN_DV = 128
GDN_CHUNK = 64
GDN_QK_W = GDN_QK_HEADS * GDN_DK
GDN_V_W = GDN_V_HEADS * GDN_DV
GDN_CONV_DIM = 2 * GDN_QK_W + GDN_V_W

SSD_D_INNER = D_MODEL
SSD_HEADDIM = 64
SSD_HEADS = SSD_D_INNER // SSD_HEADDIM
SSD_GROUPS = 2
SSD_STATE = 128
SSD_CHUNK = 128
SSD_BC_W = SSD_GROUPS * SSD_STATE
SSD_CONV_DIM = SSD_D_INNER + 2 * SSD_BC_W

NSA_HEADS = 16
NSA_GROUPS = 2
NSA_DK = 64
NSA_DV = 64
ROPE_DIM = NSA_DK // 4
ROPE_THETA = 500000.0
CMP_LEN = 32
CMP_STRIDE = 16
CMP_HIDDEN = 256
SEL_LEN = 64
SEL_TOPK = 8
SEL_LOCAL = 2
FORCE_SCORE = 1e4
WINDOW = 512
Q_BLOCK = 128

LANES = 128
SUBLANES = 8
VMEM_LIMIT_BYTES = 48 * 1024 * 1024

HYB_SMALL_OFF = GDN_CONV_DIM + GDN_V_W + SSD_D_INNER + SSD_CONV_DIM
HYB_PAD = HYB_SMALL_OFF + LANES
NSA_GATE_OFF = NSA_HEADS * NSA_DK + 3 * NSA_GROUPS * (NSA_DK + NSA_DV)
NSA_PAD = NSA_GATE_OFF + LANES

NEG_BIG = -1e30
LOG2E = 1.4426950408889634
DENOM_FLOOR = 2.0 ** -100
VT_ROWS = LANES + 16


def _cparams(*sem):
    return pltpu.CompilerParams(dimension_semantics=sem, vmem_limit_bytes=VMEM_LIMIT_BYTES)


def _sigmoid(v):
    return 0.5 + 0.5 * jnp.tanh(0.5 * v)


def _silu(v):
    h = 0.5 * v
    return h + h * jnp.tanh(h)


def _softplus(v):
    return jnp.maximum(v, 0.0) + jnp.log1p(jnp.exp(-jnp.abs(v)))


def _layer_norm(y, g, b):
    mu = jnp.mean(y, axis=-1, keepdims=True)
    d = y - mu
    var = jnp.mean(d * d, axis=-1, keepdims=True)
    return d * lax.rsqrt(var + LN_EPS) * g + b


def _dot(a, b):
    return jnp.dot(a, b, preferred_element_type=F32)


def _split3(x):
    hi = x.astype(BF16)
    r1 = x - hi.astype(F32)
    mid = r1.astype(BF16)
    lo = (r1 - mid.astype(F32)).astype(BF16)
    return hi, mid, lo


def _dot_sel_rhs(x, sel):
    sel = sel.astype(BF16)
    hi, mid, lo = _split3(x)
    return _dot(hi, sel) + _dot(mid, sel) + _dot(lo, sel)


def _dot_sel_lhs(sel, x):
    sel = sel.astype(BF16)
    hi, mid, lo = _split3(x)
    return _dot(sel, hi) + _dot(sel, mid) + _dot(sel, lo)


def _dot_nt(a, b):
    return lax.dot_general(a, b, (((1,), (1,)), ((), ())), preferred_element_type=F32)


def _ffn_ln_kernel(x_ref, wi_ref, wo_ref, g_ref, b_ref, o_ref, *, tf):
    x = x_ref[...]
    xb = x.astype(BF16)
    acts = []
    for c in range(D_FF // tf):
        gate = _dot(xb, wi_ref[:, c * tf:(c + 1) * tf])
        up = _dot(xb, wi_ref[:, D_FF + c * tf:D_FF + (c + 1) * tf])
        acts.append((_silu(gate) * up).astype(BF16))
    mix = _dot(jnp.concatenate(acts, axis=1), wo_ref[...])
    y = DEEPNORM_ALPHA * x + 0.5 * mix
    o_ref[...] = _layer_norm(y, g_ref[...], b_ref[...])


def _resident(shape):
    return pl.BlockSpec(shape, lambda *_: (0,) * len(shape), pipeline_mode=pl.Buffered(1))


def _ffn_ln(h, w_in, w_out, g, b, layer, which, *, tm, tf):
    m = h.shape[0]
    pick = lambda i: (layer, which, 0, 0)
    return pl.pallas_call(
        functools.partial(_ffn_ln_kernel, tf=tf),
        grid=(m // tm,),
        in_specs=[
            pl.BlockSpec((tm, D_MODEL), lambda i: (i, 0)),
            pl.BlockSpec((None, None) + w_in.shape[2:], pick, pipeline_mode=pl.Buffered(1)),
            pl.BlockSpec((None, None) + w_out.shape[2:], pick, pipeline_mode=pl.Buffered(1)),
            _resident((1, D_MODEL)),
            _resident((1, D_MODEL)),
        ],
        out_specs=pl.BlockSpec((tm, D_MODEL), lambda i: (i, 0)),
        out_shape=jax.ShapeDtypeStruct((m, D_MODEL), F32),
        compiler_params=_cparams("parallel"),
        name="ffn_ln",
    )(h, w_in, w_out, g, b)


def _proj_kernel(x_ref, w_ref, o_ref, *, tn):
    xb = x_ref[...].astype(BF16)
    for c in range(w_ref.shape[1] // tn):
        o_ref[:, c * tn:(c + 1) * tn] = _dot(xb, w_ref[:, c * tn:(c + 1) * tn])


def _proj(h, w, *, tm, tn):
    m, k = h.shape
    n = w.shape[1]
    return pl.pallas_call(
        functools.partial(_proj_kernel, tn=tn),
        grid=(m // tm,),
        in_specs=[
            pl.BlockSpec((tm, k), lambda i: (i, 0)),
            _resident(w.shape),
        ],
        out_specs=pl.BlockSpec((tm, n), lambda i: (i, 0)),
        out_shape=jax.ShapeDtypeStruct((m, n), F32),
        compiler_params=_cparams("parallel"),
        name="in_proj",
    )(h, w)


def _proj_conv_kernel(x_ref, w_ref, cw_ref, cb_ref, o_ref, hist, *, chunks, tiles_per_seq):
    tm = x_ref.shape[0]

    @pl.when(pl.program_id(0) % tiles_per_seq == 0)
    def _():
        hist[...] = jnp.zeros(hist.shape, F32)

    xb = x_ref[...].astype(BF16)
    row8 = lax.broadcasted_iota(jnp.int32, (SUBLANES, 1), 0)
    for col, width, ccol in chunks:
        y = _dot(xb, w_ref[:, col:col + width])
        if ccol is not None:
            prev = hist[:, ccol:ccol + width]
            acc = cb_ref[:, ccol:ccol + width] + cw_ref[CONV_K - 1:CONV_K, ccol:ccol + width] * y
            for s in range(1, CONV_K):
                y_s = pltpu.roll(y, s, axis=0)
                head = jnp.where(row8 < s, pltpu.roll(prev, s, axis=0), y_s[0:SUBLANES])
                y_s = jnp.concatenate([head, y_s[SUBLANES:]], axis=0)
                acc = acc + cw_ref[CONV_K - 1 - s:CONV_K - s, ccol:ccol + width] * y_s
            hist[:, ccol:ccol + width] = y[tm - SUBLANES:tm]
            y = _silu(acc)
        o_ref[:, col:col + width] = y


def _proj_conv(h, w, cw, cb, t_len, *, tm, tn):
    m, k = h.shape
    n = w.shape[1]
    conv_w = cw.shape[1]
    ssd_lo = GDN_CONV_DIM + GDN_V_W + SSD_D_INNER
    chunks = []
    for col in range(0, n, tn):
        width = min(tn, n - col)
        if col + width <= GDN_CONV_DIM:
            ccol = col
        elif ssd_lo <= col and col + width <= ssd_lo + SSD_CONV_DIM:
            ccol = col - ssd_lo + GDN_CONV_DIM
        else:
            ccol = None
        chunks.append((col, width, ccol))
    return pl.pallas_call(
        functools.partial(_proj_conv_kernel, chunks=tuple(chunks), tiles_per_seq=t_len // tm),
        grid=(m // tm,),
        in_specs=[
            pl.BlockSpec((tm, k), lambda i: (i, 0)),
            _resident(w.shape),
            _resident(cw.shape),
            _resident(cb.shape),
        ],
        out_specs=pl.BlockSpec((tm, n), lambda i: (i, 0)),
        out_shape=jax.ShapeDtypeStruct((m, n), F32),
        scratch_shapes=[pltpu.VMEM((SUBLANES, conv_w), F32)],
        compiler_params=_cparams("arbitrary"),
        name="in_proj_conv",
    )(h, w, cw, cb)


def _out_ln_kernel(*refs, n_parts):
    a_refs = refs[:n_parts]
    w_refs = refs[n_parts:2 * n_parts]
    h_ref, g_ref, b_ref, o_ref = refs[2 * n_parts:]
    mix = _dot(a_refs[0][...], w_refs[0][...])
    for a_ref, w_ref in zip(a_refs[1:], w_refs[1:]):
        mix = mix + _dot(a_ref[...], w_ref[...])
    y = DEEPNORM_ALPHA * h_ref[...] + mix
    o_ref[...] = _layer_norm(y, g_ref[...], b_ref[...])


def _out_ln(parts, weights, h, g, b, *, tm):
    m = h.shape[0]
    n_parts = len(parts)
    in_specs = [pl.BlockSpec((tm, a.shape[1]), lambda i: (i, 0)) for a in parts]
    in_specs += [pl.BlockSpec(w.shape, lambda i: (0, 0)) for w in weights]
    in_specs += [
        pl.BlockSpec((tm, D_MODEL), lambda i: (i, 0)),
        pl.BlockSpec((1, D_MODEL), lambda i: (0, 0)),
        pl.BlockSpec((1, D_MODEL), lambda i: (0, 0)),
    ]
    return pl.pallas_call(
        functools.partial(_out_ln_kernel, n_parts=n_parts),
        grid=(m // tm,),
        in_specs=in_specs,
        out_specs=pl.BlockSpec((tm, D_MODEL), lambda i: (i, 0)),
        out_shape=jax.ShapeDtypeStruct((m, D_MODEL), F32),
        compiler_params=_cparams("parallel"),
        name="out_proj_ln",
    )(*parts, *weights, h, g, b)


def _ssd_kernel(z_ref, x_ref, bm_ref, cm_ref, sm_ref, dtb_ref, alog_ref,
                dskip_ref, nw_ref, eh_ref, o_ref, state, *, n_sub):
    @pl.when(pl.program_id(1) == 0)
    def _():
        state[...] = jnp.zeros_like(state)

    for sub in range(n_sub):
        _ssd_chunk(pl.ds(sub * SSD_CHUNK, SSD_CHUNK), z_ref, x_ref, bm_ref, cm_ref, sm_ref, dtb_ref,
                   alog_ref, dskip_ref, nw_ref, eh_ref, o_ref, state)


def _ssd_chunk(rows, z_ref, x_ref, bm_ref, cm_ref, sm_ref, dtb_ref, alog_ref,
               dskip_ref, nw_ref, eh_ref, o_ref, state):
    L = SSD_CHUNK
    W = SSD_D_INNER
    GW = W // SSD_GROUPS
    N = SSD_STATE

    xs = x_ref[rows, :]
    bm = bm_ref[rows, :].astype(BF16)
    cm = cm_ref[rows, :].astype(BF16)

    lane = lax.broadcasted_iota(jnp.int32, (1, LANES), 1)
    dt_lanes = (lane >= 16) & (lane < 16 + SSD_HEADS)
    dt_full = jnp.where(dt_lanes, _softplus(sm_ref[rows, :] + dtb_ref[...]), 0.0)
    adt_full = dt_full * (-jnp.exp(alog_ref[...]))
    r_i = lax.broadcasted_iota(jnp.int32, (L, L), 0)
    c_i = lax.broadcasted_iota(jnp.int32, (L, L), 1)
    tril = r_i >= c_i
    acs_full = _dot_sel_lhs(jnp.where(tril, 1.0, 0.0), adt_full)
    acs_t = acs_full.T
    eh = eh_ref[...]
    dt_exp = _dot_sel_rhs(dt_full, eh)
    acs_exp = _dot_sel_rhs(acs_full, eh)
    a_last = acs_exp[L - 1:L, :]
    xdt = xs * dt_exp
    xdec = (xdt * jnp.exp(a_last - acs_exp)).astype(BF16)
    e_acs = jnp.exp(acs_exp)
    e_last = jnp.exp(a_last)
    lane_w = lax.broadcasted_iota(jnp.int32, (1, W), 1)
    lo_half = (lane_w & (LANES - 1)) < SSD_HEADDIM
    xdt_b = xdt.astype(BF16)
    xdt_lo = jnp.where(lo_half, xdt_b, jnp.zeros_like(xdt_b))
    xdt_hi = jnp.where(lo_half, jnp.zeros_like(xdt_b), xdt_b)

    heads_per_group = SSD_HEADS // SSD_GROUPS
    y_parts = []
    for g in range(SSD_GROUPS):
        bg = bm[:, g * N:(g + 1) * N]
        cg = cm[:, g * N:(g + 1) * N]
        cb = _dot_nt(cg, bg)
        s_g = state[g]
        y_off = _dot(cg, s_g.astype(BF16)) * e_acs[:, g * GW:(g + 1) * GW]
        for pr in range(heads_per_group // 2):
            y_pair = None
            for var in range(2):
                h = g * heads_per_group + 2 * pr + var
                col = acs_full[:, 16 + h:17 + h]
                row = acs_t[16 + h:17 + h, :]
                seg = jnp.exp(jnp.where(tril, col - row, -jnp.inf))
                mh = (cb * seg).astype(BF16)
                src = xdt_lo if var == 0 else xdt_hi
                lanes0 = (g * heads_per_group + 2 * pr) * SSD_HEADDIM
                d = _dot(mh, src[:, lanes0:lanes0 + LANES])
                y_pair = d if y_pair is None else y_pair + d
            off = pr * LANES
            y_parts.append(y_pair + y_off[:, off:off + LANES])
        upd = lax.dot_general(bg, xdec[:, g * GW:(g + 1) * GW], (((0,), (0,)), ((), ())),
                              preferred_element_type=F32)
        state[g] = s_g * e_last[:, g * GW:(g + 1) * GW] + upd
    y = jnp.concatenate(y_parts, axis=1) + xs * dskip_ref[...]
    y = y * _silu(z_ref[rows, :])
    outs = []
    for g in range(SSD_GROUPS):
        yg = y[:, g * GW:(g + 1) * GW]
        outs.append(yg * lax.rsqrt(jnp.mean(yg * yg, axis=-1, keepdims=True) + RMS_EPS))
    o_ref[rows, :] = (jnp.concatenate(outs, axis=1) * nw_ref[...]).astype(o_ref.dtype)


def _ssd(proj, bsz, t_len, dtb_row, alog_row, dskip_row, nw_row, eh, *, n_sub):
    L = n_sub * SSD_CHUNK
    nc = t_len // L
    W = SSD_D_INNER
    z_blk = (GDN_CONV_DIM + GDN_V_W) // W
    x_blk = (GDN_CONV_DIM + GDN_V_W + W) // W
    b_blk = (GDN_CONV_DIM + GDN_V_W + 2 * W) // SSD_BC_W
    sm_blk = HYB_SMALL_OFF // LANES
    row = lambda b, c: b * nc + c
    const = lambda b, c: (0, 0)
    return pl.pallas_call(
        functools.partial(_ssd_kernel, n_sub=n_sub),
        grid=(bsz, nc),
        in_specs=[
            pl.BlockSpec((L, W), lambda b, c: (row(b, c), z_blk)),
            pl.BlockSpec((L, W), lambda b, c: (row(b, c), x_blk)),
            pl.BlockSpec((L, SSD_BC_W), lambda b, c: (row(b, c), b_blk)),
            pl.BlockSpec((L, SSD_BC_W), lambda b, c: (row(b, c), b_blk + 1)),
            pl.BlockSpec((L, LANES), lambda b, c: (row(b, c), sm_blk)),
            pl.BlockSpec((1, LANES), const),
            pl.BlockSpec((1, LANES), const),
            pl.BlockSpec((1, W), const),
            pl.BlockSpec((1, W), const),
            pl.BlockSpec((LANES, W), const),
        ],
        out_specs=pl.BlockSpec((L, W), lambda b, c: (row(b, c), 0)),
        out_shape=jax.ShapeDtypeStruct((bsz * t_len, W), BF16),
        scratch_shapes=[
            pltpu.VMEM((SSD_GROUPS, SSD_STATE, W // SSD_GROUPS), F32),
        ],
        compiler_params=_cparams("parallel", "arbitrary"),
        name="ssd",
    )(proj, proj, proj, proj, proj, dtb_row, alog_row, dskip_row, nw_row, eh)


def _unit_lower_inverse_minus_eye(a, r_i, c_i):
    n = range(len(a))
    diag8 = (r_i >> 3) == (c_i >> 3)
    x = [jnp.where(diag8, -a[i], 0.0) for i in n]
    x_b = [x[i].astype(BF16) for i in n]
    x2 = [_dot(x_b[i], x_b[i]) for i in n]
    x2_b = [x2[i].astype(BF16) for i in n]
    x4 = [_dot(x2_b[i], x2_b[i]) for i in n]
    e = [x[i] + x2[i] + _dot(x_b[i], x2_b[i]) for i in n]
    e = [e[i] + x4[i] + _dot(e[i].astype(BF16), x4[i].astype(BF16)) for i in n]
    for sh in (3, 4, 5):
        same_pair = (r_i >> (sh + 1)) == (c_i >> (sh + 1))
        lower_left = (((r_i >> sh) & 1) == 1) & (((c_i >> sh) & 1) == 0)
        ms = [jnp.where(same_pair & lower_left, a[i], 0.0) for i in n]
        y = [ms[i] + _dot(ms[i].astype(BF16), e[i].astype(BF16)) for i in n]
        e = [e[i] - y[i] - _dot(e[i].astype(BF16), y[i].astype(BF16)) for i in n]
    return e


def _stack_heads(x, r0, l0):
    return jnp.concatenate([x[r0:r0 + GDN_CHUNK, l0:l0 + GDN_DV],
                            x[r0:r0 + GDN_CHUNK, l0 + GDN_DV:l0 + 2 * GDN_DV]], axis=0)


def _own_head_block(x):
    return jnp.concatenate([x[0:GDN_CHUNK, 0:GDN_DV], x[GDN_CHUNK:, GDN_DV:]], axis=0)


def _gdn_kernel(q_ref, k_ref, v_ref, z_ref, sm_ref, dtb_ref, alog_ref, nw_ref,
                eb_ref, ea_ref, o_ref, state, *, tc):
    C = GDN_CHUNK
    DK = GDN_DK
    DV = GDN_DV
    P = 2 * C

    @pl.when(pl.program_id(1) == 0)
    def _():
        state[...] = jnp.zeros_like(state)

    lane = lax.broadcasted_iota(jnp.int32, (1, LANES), 1)
    sm = sm_ref[...]
    beta_full = jnp.where(lane < GDN_V_HEADS, _sigmoid(sm), 0.0)
    g_lanes = (lane >= GDN_V_HEADS) & (lane < 2 * GDN_V_HEADS)
    g_full = jnp.where(g_lanes, -jnp.exp(alog_ref[...]) * _softplus(sm + dtb_ref[...]), 0.0)
    rt = lax.broadcasted_iota(jnp.int32, (tc, tc), 0)
    ct = lax.broadcasted_iota(jnp.int32, (tc, tc), 1)
    same_chunk_tril = ((rt >> 6) == (ct >> 6)) & (rt >= ct)
    gc_full = _dot_sel_lhs(jnp.where(same_chunk_tril, 1.0, 0.0), g_full)
    b_exp = _dot_sel_rhs(beta_full, eb_ref[...])
    g_exp = _dot_sel_rhs(gc_full, ea_ref[...])
    eg_exp = jnp.exp(g_exp)

    r_i = lax.broadcasted_iota(jnp.int32, (P, P), 0)
    c_i = lax.broadcasted_iota(jnp.int32, (P, P), 1)
    same_head = (r_i >> 6) == (c_i >> 6)
    tril = same_head & (r_i >= c_i)
    strict = same_head & (r_i > c_i)
    head0_rows = lax.broadcasted_iota(jnp.int32, (P, 1), 0) < C
    nw = nw_ref[...]

    n_chunks = tc // C
    units = [(hq, ci) for ci in range(n_chunks) for hq in range(GDN_QK_HEADS)]
    qn, kn = [], []
    for hq in range(GDN_QK_HEADS):
        qh = q_ref[:, hq * DK:(hq + 1) * DK]
        kh = k_ref[:, hq * DK:(hq + 1) * DK]
        qn.append(qh * lax.rsqrt(jnp.sum(qh * qh, axis=-1, keepdims=True) + 1e-6) * (DK ** -0.5))
        kn.append(kh * lax.rsqrt(jnp.sum(kh * kh, axis=-1, keepdims=True) + 1e-6))
    pre = {}
    a_low = []
    for hq, ci in units:
        r0 = ci * C
        l0 = hq * 2 * DV
        q_c = qn[hq][r0:r0 + C]
        k_c = kn[hq][r0:r0 + C]
        q2 = jnp.concatenate([q_c, q_c], axis=0)
        k2 = jnp.concatenate([k_c, k_c], axis=0)
        k2_b = k2.astype(BF16)
        kk = _dot_nt(k2_b, k2_b)
        qk = _dot_nt(q2.astype(BF16), k2_b)
        gcol = _stack_heads(g_exp, r0, l0)
        bcol = _stack_heads(b_exp, r0, l0)
        egc = _stack_heads(eg_exp, r0, l0)
        dmat = jnp.exp(jnp.where(tril, gcol - gcol.T, -jnp.inf))
        a_low.append(jnp.where(strict, kk * bcol * dmat, 0.0))
        g_last = jnp.where(head0_rows, gcol[C - 1:C], gcol[P - 1:P])
        pre[hq, ci] = dict(
            vb=_stack_heads(v_ref, r0, l0) * bcol,
            kb=k2 * (bcol * egc),
            qd=(q2 * egc).astype(BF16),
            attn=(qk * dmat).astype(BF16),
            k_t=k_c.T.astype(BF16),
            v_scale=jnp.exp(g_last - gcol),
            decay_cat=jnp.concatenate([egc[C - 1:C], egc[P - 1:P]], axis=1))
    e_all = _unit_lower_inverse_minus_eye(a_low, r_i, c_i)
    for (hq, ci), e in zip(units, e_all):
        d = pre[hq, ci]
        e_b = e.astype(BF16)
        d["u"] = d["vb"] + _dot(e_b, d["vb"].astype(BF16))
        d["w"] = (d["kb"] + _dot(e_b, d["kb"].astype(BF16))).astype(BF16)

    s_cat = [state[hq] for hq in range(GDN_QK_HEADS)]
    for ci in range(n_chunks):
        r0 = ci * C
        for hq in range(GDN_QK_HEADS):
            d = pre[hq, ci]
            l0 = hq * 2 * DV
            s_b = s_cat[hq].astype(BF16)
            v_new = d["u"] - _own_head_block(_dot(d["w"], s_b))
            o2 = _own_head_block(_dot(d["qd"], s_b)) + _dot(d["attn"], v_new.astype(BF16))
            v_dec = (v_new * d["v_scale"]).astype(BF16)
            v_dec_cat = jnp.concatenate([v_dec[0:C], v_dec[C:]], axis=1)
            s_cat[hq] = s_cat[hq] * d["decay_cat"] + _dot(d["k_t"], v_dec_cat)
            o_n = o2 * lax.rsqrt(jnp.mean(o2 * o2, axis=-1, keepdims=True) + RMS_EPS)
            res = (o_n * nw * _silu(_stack_heads(z_ref, r0, l0))).astype(o_ref.dtype)
            o_ref[r0:r0 + C, l0:l0 + DV] = res[0:C]
            o_ref[r0:r0 + C, l0 + DV:l0 + 2 * DV] = res[C:]
    for hq in range(GDN_QK_HEADS):
        state[hq] = s_cat[hq]


def _gdn(proj, bsz, t_len, dtb_row, alog_row, nw_row, eb, ea, *, tc):
    nt = t_len // tc
    row = lambda b, t: b * nt + t
    const = lambda b, t: (0, 0)
    sm_blk = HYB_SMALL_OFF // LANES
    return pl.pallas_call(
        functools.partial(_gdn_kernel, tc=tc),
        grid=(bsz, nt),
        in_specs=[
            pl.BlockSpec((tc, GDN_QK_W), lambda b, t: (row(b, t), 0)),
            pl.BlockSpec((tc, GDN_QK_W), lambda b, t: (row(b, t), 1)),
            pl.BlockSpec((tc, GDN_V_W), lambda b, t: (row(b, t), 1)),
            pl.BlockSpec((tc, GDN_V_W), lambda b, t: (row(b, t), 2)),
            pl.BlockSpec((tc, LANES), lambda b, t: (row(b, t), sm_blk)),
            pl.BlockSpec((1, LANES), const),
            pl.BlockSpec((1, LANES), const),
            pl.BlockSpec((1, GDN_DV), const),
            pl.BlockSpec((LANES, GDN_V_W), const),
            pl.BlockSpec((LANES, GDN_V_W), const),
        ],
        out_specs=pl.BlockSpec((tc, GDN_V_W), lambda b, t: (row(b, t), 0)),
        out_shape=jax.ShapeDtypeStruct((bsz * t_len, GDN_V_W), BF16),
        scratch_shapes=[
            pltpu.VMEM((GDN_QK_HEADS, GDN_DK, 2 * GDN_DV), F32),
        ],
        compiler_params=_cparams("parallel", "arbitrary"),
        name="gdn",
    )(proj, proj, proj, proj, proj, dtb_row, alog_row, nw_row, eb, ea)


def _group_variants(x):
    lane = lax.broadcasted_iota(jnp.int32, (1, LANES), 1)
    lo = lane < (LANES // 2)
    xr = pltpu.roll(x, LANES // 2, axis=1)
    zero = jnp.zeros_like(x)
    return jnp.concatenate([jnp.where(lo, x, zero), jnp.where(lo, zero, xr),
                            jnp.where(lo, xr, zero), jnp.where(lo, zero, x)], axis=1)


def _values_transposed(v):
    zt = _group_variants(v).T
    extra_r = lax.broadcasted_iota(jnp.int32, (VT_ROWS - LANES, v.shape[0]), 0)
    parts = []
    for var in range(2 * NSA_GROUPS):
        parts.append(zt[var * LANES:(var + 1) * LANES])
        parts.append(jnp.where(extra_r == (var % 2), 1.0, 0.0))
    return jnp.concatenate(parts, axis=0)


def _nsa_prep_kernel(q_ref, kc_ref, vc_ref, ks_ref, vs_ref, kw_ref, vw_ref, pos_ref, freq_ref, sel_ref,
                     one_ref, qc_o, qr_o, k16_o, v16_o, ksz_o, vsz_o, kwz_o, vwz_o):
    scale = NSA_DK ** -0.5 * LOG2E
    rows16 = k16_o.shape[0]
    for src, dst in ((kc_ref, k16_o), (vc_ref, v16_o)):
        for j in range(CMP_STRIDE):
            dst[:, j * LANES:(j + 1) * LANES] = src[pl.ds(j, rows16, stride=CMP_STRIDE), :]
    ang = freq_ref[...] * pos_ref[...].astype(F32)
    half = ROPE_DIM // 2
    tm = ang.shape[1]
    cs = jnp.concatenate([jnp.cos(ang), jnp.sin(ang), jnp.zeros((LANES - 2 * half, tm), F32)], axis=0)
    spread = _dot_sel_rhs(cs.T, sel_ref[...])
    cos = spread[:, 0:LANES] + one_ref[...]
    sin_up = spread[:, LANES:2 * LANES]
    sin_dn = spread[:, 2 * LANES:3 * LANES]

    def rope(x):
        return x * cos + pltpu.roll(x, half, axis=1) * sin_up + pltpu.roll(x, LANES - half, axis=1) * sin_dn

    for p in range(NSA_HEADS * NSA_DK // LANES):
        x = q_ref[:, p * LANES:(p + 1) * LANES]
        qc_o[:, p * LANES:(p + 1) * LANES] = (x * scale).astype(qc_o.dtype)
        qr_o[:, p * LANES:(p + 1) * LANES] = (rope(x) * scale).astype(qr_o.dtype)
    ksz_o[...] = _group_variants(rope(ks_ref[...])).astype(ksz_o.dtype)
    kwz_o[...] = _group_variants(rope(kw_ref[...])).astype(kwz_o.dtype)
    vsz_o[...] = _values_transposed(vs_ref[...]).astype(vsz_o.dtype)
    vwz_o[...] = _values_transposed(vw_ref[...]).astype(vwz_o.dtype)


def _nsa_prep(proj, pos_row, freq_col, rope_sel, one_row, bsz, t_len, *, tm):
    m = proj.shape[0]
    tiles_per_seq = t_len // tm
    seq_t = lambda i: (i // tiles_per_seq, 0, i % tiles_per_seq)
    qw = NSA_HEADS * NSA_DK
    kv0 = qw // LANES
    const = lambda i: (0, 0)
    zw = 4 * LANES
    return pl.pallas_call(
        _nsa_prep_kernel,
        grid=(m // tm,),
        in_specs=[
            pl.BlockSpec((tm, qw), lambda i: (i, 0)),
            pl.BlockSpec((tm, LANES), lambda i: (i, kv0)),
            pl.BlockSpec((tm, LANES), lambda i: (i, kv0 + 1)),
            pl.BlockSpec((tm, LANES), lambda i: (i, kv0 + 2)),
            pl.BlockSpec((tm, LANES), lambda i: (i, kv0 + 3)),
            pl.BlockSpec((tm, LANES), lambda i: (i, kv0 + 4)),
            pl.BlockSpec((tm, LANES), lambda i: (i, kv0 + 5)),
            pl.BlockSpec((1, tm), lambda i: (0, i)),
            pl.BlockSpec(freq_col.shape, const),
            pl.BlockSpec(rope_sel.shape, const),
            pl.BlockSpec((1, LANES), const),
        ],
        out_specs=[
            pl.BlockSpec((tm, qw), lambda i: (i, 0)),
            pl.BlockSpec((tm, qw), lambda i: (i, 0)),
            pl.BlockSpec((tm // CMP_STRIDE, CMP_STRIDE * LANES), lambda i: (i, 0)),
            pl.BlockSpec((tm // CMP_STRIDE, CMP_STRIDE * LANES), lambda i: (i, 0)),
            pl.BlockSpec((tm, zw), lambda i: (i, 0)),
            pl.BlockSpec((None, 4 * VT_ROWS, tm), seq_t),
            pl.BlockSpec((tm, zw), lambda i: (i, 0)),
            pl.BlockSpec((None, 4 * VT_ROWS, tm), seq_t),
        ],
        out_shape=[
            jax.ShapeDtypeStruct((m, qw), BF16),
            jax.ShapeDtypeStruct((m, qw), BF16),
            jax.ShapeDtypeStruct((m // CMP_STRIDE, CMP_STRIDE * LANES), F32),
            jax.ShapeDtypeStruct((m // CMP_STRIDE, CMP_STRIDE * LANES), F32),
            jax.ShapeDtypeStruct((m, zw), BF16),
            jax.ShapeDtypeStruct((bsz, 4 * VT_ROWS, t_len), BF16),
            jax.ShapeDtypeStruct((m, zw), BF16),
            jax.ShapeDtypeStruct((bsz, 4 * VT_ROWS, t_len), BF16),
        ],
        compiler_params=_cparams("parallel"),
        name="nsa_prep",
    )(proj, proj, proj, proj, proj, proj, proj, pos_row, freq_col, rope_sel, one_row)


def _nsa_compress_kernel(k16_ref, v16_ref, pos_ref, w1a_ref, w1b_ref, w2_ref, kcz_o, vcz_o):
    nr = k16_ref.shape[0]
    for idx, (x_ref, o_ref) in enumerate(((k16_ref, kcz_o), (v16_ref, vcz_o))):
        x = x_ref[...]
        h_a = _dot((x + pos_ref[idx, 0:1, :]).astype(BF16), w1a_ref[idx])
        h_b = _dot((x + pos_ref[idx, 1:2, :]).astype(BF16), w1b_ref[idx])
        hid = h_a + pltpu.roll(h_b, nr - 1, axis=0)
        out = _dot(_silu(hid).astype(BF16), w2_ref[idx])
        z = _group_variants(out)
        o_ref[...] = (z.T if idx == 1 else z).astype(o_ref.dtype)


def _nsa_compress(k16, v16, pos_ab, w1a, w1b, w2):
    bsz, nr, width = k16.shape
    hid = NSA_GROUPS * CMP_HIDDEN
    zw = 4 * LANES
    c3 = lambda b: (0, 0, 0)
    return pl.pallas_call(
        _nsa_compress_kernel,
        grid=(bsz,),
        in_specs=[
            pl.BlockSpec((None, nr, width), lambda b: (b, 0, 0)),
            pl.BlockSpec((None, nr, width), lambda b: (b, 0, 0)),
            pl.BlockSpec((2, 2, width), c3),
            pl.BlockSpec((2, width, hid), c3),
            pl.BlockSpec((2, width, hid), c3),
            pl.BlockSpec((2, hid, LANES), c3),
        ],
        out_specs=[
            pl.BlockSpec((None, nr, zw), lambda b: (b, 0, 0)),
            pl.BlockSpec((None, zw, nr), lambda b: (b, 0, 0)),
        ],
        out_shape=[
            jax.ShapeDtypeStruct((bsz, nr, zw), BF16),
            jax.ShapeDtypeStruct((bsz, zw, nr), BF16),
        ],
        compiler_params=_cparams("parallel"),
        name="nsa_compress",
    )(k16, v16, pos_ab, w1a, w1b, w2)


def _flash_branch(q_stacks, kz_ref, vzt_ref, lo, hi, bias_fns, kb_size):
    r_acc = lax.broadcasted_iota(jnp.int32, (VT_ROWS, 1), 0)
    even_rows = (r_acc < LANES // 2) | (r_acc == LANES)
    half_w = 2 * Q_BLOCK
    n_half = q_stacks[0].shape[0] // half_w
    units = [(g, h) for g in range(NSA_GROUPS) for h in range(n_half)]
    q_unit = [q_stacks[g][h * half_w:(h + 1) * half_w] for g, h in units]

    def body(kb, carry):
        m_in, acc = carry
        ks = pl.multiple_of(kb * kb_size, kb_size)
        k_both, vt_both, bias = [], [], []
        for g in range(NSA_GROUPS):
            k0 = 2 * g * LANES
            v0 = 2 * g * VT_ROWS
            k_both.append(jnp.concatenate([kz_ref[pl.ds(ks, kb_size), k0:k0 + LANES],
                                           kz_ref[pl.ds(ks, kb_size), k0 + LANES:k0 + 2 * LANES]], axis=0))
            vt_both.append(jnp.concatenate([vzt_ref[v0:v0 + VT_ROWS, pl.ds(ks, kb_size)],
                                            vzt_ref[v0 + VT_ROWS:v0 + 2 * VT_ROWS, pl.ds(ks, kb_size)]],
                                           axis=1))
            b = bias_fns[g](ks)
            bias.append(jnp.concatenate([b, b], axis=1))
        n_u = len(units)
        s, m_out, alpha_rows, p_cat, pv = {}, {}, {}, {}, {}

        def softmax_stage(u):
            g = units[u][0]
            p_parts, alphas, m_news = [], [], []
            for var in range(2):
                s_v = s[u][var * kb_size:(var + 1) * kb_size] + bias[g]
                m_prev = m_in[u][var]
                m_new = jnp.maximum(m_prev, jnp.max(s_v, axis=0, keepdims=True))
                alphas.append(jnp.exp2(m_prev - m_new))
                p_parts.append(jnp.exp2(s_v - m_new).astype(BF16))
                m_news.append(m_new)
            m_out[u] = tuple(m_news)
            alpha_rows[u] = jnp.where(even_rows, alphas[0], alphas[1])
            p_cat[u] = jnp.concatenate(p_parts, axis=0)

        for u in range(n_u):
            s[u] = _dot_nt(k_both[units[u][0]], q_unit[u])
        for u in range(n_u):
            softmax_stage(u)
        for u in range(n_u):
            pv[u] = _dot(vt_both[units[u][0]], p_cat[u])
        acc_out = tuple(acc[u] * alpha_rows[u] + pv[u] for u in range(n_u))
        return tuple(m_out[u] for u in range(n_u)), acc_out

    m_init = tuple((jnp.full((1, half_w), NEG_BIG, F32),) * 2 for _ in units)
    acc_init = tuple(jnp.zeros((VT_ROWS, half_w), F32) for _ in units)
    _, acc = lax.fori_loop(lo, hi, body, (m_init, acc_init))
    outs = []
    for g in range(NSA_GROUPS):
        a = jnp.concatenate(acc[g * n_half:(g + 1) * n_half], axis=1)
        inv_even = 1.0 / a[LANES:LANES + 1, :]
        inv_odd = 1.0 / a[LANES + 1:LANES + 2, :]
        outs.append(a[0:LANES, :] * jnp.where(r_acc[0:LANES] < LANES // 2, inv_even, inv_odd))
    return outs


def _flash_branch_bounded(q_exts, kz_ref, kext_ref, vzt_ref, lo, hi, bias_fns, last_bias_fns, kb_size):
    r_acc = lax.broadcasted_iota(jnp.int32, (VT_ROWS, 1), 0)
    half_w = 2 * Q_BLOCK
    n_half = q_exts[0].shape[0] // half_w
    units = [(g, h) for g in range(NSA_GROUPS) for h in range(n_half)]
    q_unit = [q_exts[g][h * half_w:(h + 1) * half_w] for g, h in units]
    n_u = len(units)

    def step(kbs, acc, fns):
        work = []
        for kb in kbs:
            ks = pl.multiple_of(jnp.maximum(kb, 0) * kb_size, kb_size)
            k_ext = kext_ref[pl.ds(ks, kb_size), :]
            for g in range(NSA_GROUPS):
                k0 = 2 * g * LANES
                v0 = 2 * g * VT_ROWS
                k_both = jnp.concatenate(
                    [jnp.concatenate([kz_ref[pl.ds(ks, kb_size), k0:k0 + LANES], k_ext], axis=1),
                     jnp.concatenate([kz_ref[pl.ds(ks, kb_size), k0 + LANES:k0 + 2 * LANES], k_ext],
                                     axis=1)], axis=0)
                vt_both = jnp.concatenate([vzt_ref[v0:v0 + VT_ROWS, pl.ds(ks, kb_size)],
                                           vzt_ref[v0 + VT_ROWS:v0 + 2 * VT_ROWS, pl.ds(ks, kb_size)]],
                                          axis=1)
                bias = None
                if fns is not None:
                    b = fns[g](kb, ks)
                    b = jnp.concatenate([b, b], axis=1)
                    bias = jnp.concatenate([b, b], axis=0)
                work += [(u, k_both, vt_both, bias) for u in range(n_u) if units[u][0] == g]
        s = [_dot_nt(k, q_unit[u]) for u, k, _, _ in work]
        p = [jnp.exp2(s_i if w[3] is None else s_i + w[3]).astype(BF16) for s_i, w in zip(s, work)]
        pv = [_dot(w[2], p_i) for p_i, w in zip(p, work)]
        out = list(acc)
        for (u, _, _, _), pv_i in zip(work, pv):
            out[u] = out[u] + pv_i
        return tuple(out)

    acc = tuple(jnp.zeros((VT_ROWS, half_w), F32) for _ in units)
    if isinstance(lo, (list, tuple)):
        acc = step(lo, acc, bias_fns)
    elif last_bias_fns is None:
        acc = lax.fori_loop(lo, hi, lambda kb, a: step([kb], a, bias_fns), acc)
    else:
        acc = lax.fori_loop(lo, hi - 1, lambda kb, a: step([kb], a, bias_fns), acc)
        acc = step([hi - 1], acc, last_bias_fns)
    outs = []
    l_min = None
    for g in range(NSA_GROUPS):
        a = jnp.concatenate(acc[g * n_half:(g + 1) * n_half], axis=1)
        l_even = a[LANES:LANES + 1, :]
        l_odd = a[LANES + 1:LANES + 2, :]
        outs.append(a[0:LANES, :] * jnp.where(r_acc[0:LANES] < LANES // 2, 1.0 / l_even, 1.0 / l_odd))
        l_g = jnp.min(jnp.minimum(l_even, l_odd))
        l_min = l_g if l_min is None else jnp.minimum(l_min, l_g)
    return outs, l_min


def _nsa_attn_kernel(qc_ref, qr_ref, gate_ref, kcz_ref, vczt_ref, ksz_ref, vszt_ref, kwz_ref, vwzt_ref,
                     aggt_ref, eselt_ref, o_ref, bias_ref, kmax_ref,
                     *, t_len, kb_size):
    n_cmp_rows = t_len // CMP_STRIDE
    n_cmp = (t_len - CMP_LEN) // CMP_STRIDE + 1
    n_sel = t_len // SEL_LEN
    n_top = min(SEL_TOPK, n_sel)
    pairs = NSA_HEADS // NSA_GROUPS // 2
    q0 = pl.program_id(1) * Q_BLOCK
    t_row = q0 + lax.broadcasted_iota(jnp.int32, (1, Q_BLOCK), 1)
    t_row_stack = q0 + (lax.broadcasted_iota(jnp.int32, (1, pairs * Q_BLOCK), 1) & (Q_BLOCK - 1))
    blk = lax.broadcasted_iota(jnp.int32, (n_sel, 1), 0)
    c_idx = lax.broadcasted_iota(jnp.int32, (n_cmp_rows, 1), 0)
    cmp_mask = ((c_idx * CMP_STRIDE + (CMP_LEN - 1)) <= t_row_stack) & (c_idx < n_cmp)
    key_pos = lax.broadcasted_iota(jnp.int32, (t_len, 1), 0)
    gate_t = _sigmoid(gate_ref[...]).T
    top_half = lax.broadcasted_iota(jnp.int32, (LANES, 1), 0) < (LANES // 2)
    hi = (q0 + (Q_BLOCK - 1)) // kb_size + 1
    win_lo = jnp.maximum(q0 - (WINDOW - 1), 0) // kb_size

    def win_bias(ks):
        kp = ks + lax.broadcasted_iota(jnp.int32, (kb_size, 1), 0)
        return jnp.where((kp <= t_row) & (kp > t_row - WINDOW), 0.0, NEG_BIG)

    def sel_bias(g):
        return lambda ks: bias_ref[g, pl.ds(ks, kb_size), :]

    @pl.when(pl.program_id(1) == 0)
    def _():
        for br, k_ref in enumerate((ksz_ref, kwz_ref)):
            for g in range(NSA_GROUPS):
                k = k_ref[:, 2 * g * LANES:(2 * g + 1) * LANES].astype(F32)
                n2 = jnp.max(jnp.sum(k * k, axis=1, keepdims=True), axis=0, keepdims=True)
                kmax_ref[br * NSA_GROUPS + g] = jnp.broadcast_to(jnp.sqrt(n2), (SUBLANES, LANES))

    qr_stacks, o_cmps, sels = [], [], []
    for g in range(NSA_GROUPS):
        lanes_g = g * pairs * LANES
        qc_stack = jnp.concatenate(
            [qc_ref[:, lanes_g + p * LANES:lanes_g + (p + 1) * LANES] for p in range(pairs)], axis=0)
        qr_stacks.append(jnp.concatenate(
            [qr_ref[:, lanes_g + p * LANES:lanes_g + (p + 1) * LANES] for p in range(pairs)], axis=0))

        o_cmp = None
        p_sum = None
        for var in range(2):
            c0 = (2 * g + var) * LANES
            s = jnp.where(cmp_mask, _dot_nt(kcz_ref[:, c0:c0 + LANES], qc_stack), -jnp.inf)
            mx = jnp.max(s, axis=0, keepdims=True)
            mx = jnp.where(mx > -jnp.inf, mx, 0.0)
            e = jnp.exp2(s - mx)
            p = e * (1.0 / jnp.maximum(jnp.sum(e, axis=0, keepdims=True), 1e-30))
            d = _dot(vczt_ref[c0:c0 + LANES, :], p.astype(BF16))
            o_cmp = d if o_cmp is None else o_cmp + d
            for pr in range(pairs):
                part = p[:, pr * Q_BLOCK:(pr + 1) * Q_BLOCK]
                p_sum = part if p_sum is None else p_sum + part
        importance = _dot_sel_lhs(aggt_ref[...], p_sum)

        cur = t_row >> 6
        causal_blk = blk <= cur
        forced = (blk == 0) | (causal_blk & (blk > cur - SEL_LOCAL))
        score = jnp.where(forced, FORCE_SCORE, jnp.where(causal_blk, importance, -1.0))
        rank = jnp.zeros((n_sel, Q_BLOCK), F32)
        for jj in range(n_sel):
            row = score[jj:jj + 1, :]
            beats = (row > score) | ((row == score) & (blk > jj))
            rank = rank + jnp.where(beats, 1.0, 0.0)
        sel = jnp.where(rank < n_top, 1.0, 0.0)
        sels.append(jnp.concatenate([sel, jnp.zeros((LANES - n_sel, Q_BLOCK), F32)], axis=0))
        o_cmps.append(o_cmp)

    lane = lax.broadcasted_iota(jnp.int32, (1, LANES), 1)
    blk_pad = lax.broadcasted_iota(jnp.int32, (LANES, 1), 0)
    q_ext_sel, q_ext_win = [], []
    for g in range(NSA_GROUPS):
        x = qr_stacks[g].astype(F32)
        sq = x * x
        tot = jnp.sum(sq, axis=1, keepdims=True)
        even = jnp.sum(jnp.where(lane < LANES // 2, sq, 0.0), axis=1, keepdims=True)
        q_norm = jnp.sqrt(jnp.maximum(even, tot - even))
        sel_c = jnp.where(blk_pad * SEL_LEN <= t_row, sels[g], 0.0).T
        sel_b = jnp.where(lane < n_sel, (sel_c - 1.0) * (-NEG_BIG), 0.0)
        sel_b = jnp.concatenate([sel_b] * pairs, axis=0)
        ext_s = jnp.where(lane == n_sel, -q_norm * kmax_ref[g][0:1, 0:1], sel_b)
        ext_w = jnp.where(lane == n_sel, -q_norm * kmax_ref[NSA_GROUPS + g][0:1, 0:1], 0.0)
        q_ext_sel.append(jnp.concatenate([qr_stacks[g], ext_s.astype(BF16)], axis=1))
        q_ext_win.append(jnp.concatenate([qr_stacks[g], ext_w.astype(BF16)], axis=1))

    sel_kb = min(2 * kb_size, t_len)

    def diag_bias(kb, ks):
        kp = ks + lax.broadcasted_iota(jnp.int32, (sel_kb, 1), 0)
        return jnp.where(kp <= t_row, 0.0, NEG_BIG)

    n_win = WINDOW // kb_size + 1
    win_blocks = [hi - n_win + j for j in range(n_win)]

    def win_bias_fast(kb, ks):
        return jnp.where(kb >= 0, win_bias(ks), NEG_BIG)

    fast_sel, l_sel = _flash_branch_bounded(q_ext_sel, ksz_ref, eselt_ref, vszt_ref, 0,
                                            (q0 + (Q_BLOCK - 1)) // sel_kb + 1, None,
                                            [diag_bias] * NSA_GROUPS, sel_kb)
    fast_win, l_win = _flash_branch_bounded(q_ext_win, kwz_ref, eselt_ref, vwzt_ref, win_blocks, None,
                                            [win_bias_fast] * NSA_GROUPS, None, kb_size)

    def robust():
        for g in range(NSA_GROUPS):
            sel_keys = _dot(eselt_ref[...], sels[g].astype(BF16))
            bias_ref[g] = jnp.where((sel_keys > 0.5) & (key_pos <= t_row), 0.0, NEG_BIG)
        o_s = _flash_branch(qr_stacks, ksz_ref, vszt_ref, 0, hi,
                            [sel_bias(g) for g in range(NSA_GROUPS)], kb_size)
        o_w = _flash_branch(qr_stacks, kwz_ref, vwzt_ref, win_lo, hi, [win_bias] * NSA_GROUPS, kb_size)
        return tuple(o_s) + tuple(o_w)

    o_all = lax.cond(jnp.minimum(l_sel, l_win) > DENOM_FLOOR,
                     lambda: tuple(fast_sel) + tuple(fast_win), robust)
    o_sels, o_wins = o_all[:NSA_GROUPS], o_all[NSA_GROUPS:]

    for g in range(NSA_GROUPS):
        lanes_g = g * pairs * LANES
        for pr in range(pairs):
            h_even = (g * pairs + pr) * 2
            cols = slice(pr * Q_BLOCK, (pr + 1) * Q_BLOCK)
            mixed = None
            for j, o_branch in enumerate((o_cmps[g], o_sels[g], o_wins[g])):
                r_e = h_even * 3 + j
                r_o = r_e + 3
                gate_rows = jnp.where(top_half, gate_t[r_e:r_e + 1, :], gate_t[r_o:r_o + 1, :])
                term = gate_rows * o_branch[:, cols]
                mixed = term if mixed is None else mixed + term
            c0 = lanes_g + pr * LANES
            o_ref[:, c0:c0 + LANES] = mixed.T.astype(o_ref.dtype)


def _nsa_attn(qc, qr, proj, kcz, vczt, ksz, vszt, kwz, vwzt, aggt, eselt, bsz, t_len, *, kb_size):
    nq = t_len // Q_BLOCK
    qw = NSA_HEADS * NSA_DK
    zw = 4 * LANES
    nr = t_len // CMP_STRIDE
    pairs = NSA_HEADS // NSA_GROUPS // 2
    row = lambda b, i: (b * nq + i, 0)
    per_b = lambda b, i: (b, 0, 0)
    return pl.pallas_call(
        functools.partial(_nsa_attn_kernel, t_len=t_len, kb_size=kb_size),
        grid=(bsz, nq),
        in_specs=[
            pl.BlockSpec((Q_BLOCK, qw), row),
            pl.BlockSpec((Q_BLOCK, qw), row),
            pl.BlockSpec((Q_BLOCK, LANES), lambda b, i: (b * nq + i, NSA_GATE_OFF // LANES)),
            pl.BlockSpec((None, nr, zw), per_b),
            pl.BlockSpec((None, zw, nr), per_b),
            pl.BlockSpec((None, t_len, zw), per_b),
            pl.BlockSpec((None, 4 * VT_ROWS, t_len), per_b),
            pl.BlockSpec((None, t_len, zw), per_b),
            pl.BlockSpec((None, 4 * VT_ROWS, t_len), per_b),
            pl.BlockSpec(aggt.shape, lambda b, i: (0, 0)),
            pl.BlockSpec(eselt.shape, lambda b, i: (0, 0)),
        ],
        out_specs=pl.BlockSpec((Q_BLOCK, qw), row),
        out_shape=jax.ShapeDtypeStruct((bsz * t_len, qw), BF16),
        scratch_shapes=[
            pltpu.VMEM((NSA_GROUPS, t_len, Q_BLOCK), F32),
            pltpu.VMEM((2 * NSA_GROUPS, SUBLANES, LANES), F32),
        ],
        compiler_params=_cparams("parallel", "arbitrary"),
        name="nsa_attn",
    )(qc, qr, proj, kcz, vczt, ksz, vszt, kwz, vwzt, aggt, eselt)


def _expand_rows(first_row, n_heads, width):
    e = np.zeros((LANES, n_heads * width), np.float32)
    for h in range(n_heads):
        e[first_row + h, h * width:(h + 1) * width] = 1.0
    return e


def _nsa_constants(t_len):
    n_cmp = (t_len - CMP_LEN) // CMP_STRIDE + 1
    n_sel = t_len // SEL_LEN
    nr = t_len // CMP_STRIDE
    c0 = np.arange(n_cmp)[:, None] * CMP_STRIDE
    s0 = np.arange(n_sel)[None, :] * SEL_LEN
    overlap = np.clip(np.minimum(c0 + CMP_LEN, s0 + SEL_LEN) - np.maximum(c0, s0), 0, None) / CMP_LEN
    aggt = np.zeros((n_sel, nr), np.float32)
    aggt[:, :n_cmp] = overlap.T
    eselt = np.zeros((t_len, LANES), np.float32)
    eselt[np.arange(t_len), np.arange(t_len) // SEL_LEN] = 1.0
    eselt[:, n_sel] = 1.0
    half = ROPE_DIM // 2
    inv_freq = (ROPE_THETA ** (-np.arange(half) / half)).astype(np.float32)
    d = np.arange(LANES) % NSA_DK
    rope_sel = np.zeros((LANES, 3 * LANES), np.float32)
    for lane in range(LANES):
        if d[lane] < ROPE_DIM:
            f = d[lane] % half
            rope_sel[f, lane] = 1.0
            if d[lane] >= half:
                rope_sel[half + f, LANES + lane] = 1.0
            else:
                rope_sel[half + f, 2 * LANES + lane] = -1.0
    one_row = (d >= ROPE_DIM).astype(np.float32)[None]
    return aggt, eselt, inv_freq[:, None], rope_sel, one_row


def _pad_row(vec, offset):
    return jnp.zeros((1, LANES), F32).at[0, offset:offset + vec.shape[0]].set(vec.astype(F32))


def _hybrid_layer(h, bsz, t_len, w_in, gdn_conv_w, gdn_a_log, gdn_dt_bias, gdn_norm_w, ssd_conv_w,
                  ssd_conv_b, ssd_a_log, ssd_dt_bias, ssd_d, ssd_norm_w, w_out, ln_g, ln_b, cfg):
    b_off = GDN_CONV_DIM + GDN_V_W
    z_off = b_off + 2 * GDN_V_HEADS
    dt_off = z_off + SSD_D_INNER + SSD_CONV_DIM
    w_pad = jnp.concatenate(
        [w_in[:, :b_off], w_in[:, z_off:dt_off], w_in[:, b_off:z_off], w_in[:, dt_off:],
         jnp.zeros((D_MODEL, HYB_PAD - w_in.shape[1]), w_in.dtype)], axis=1).astype(BF16)
    conv_w = jnp.concatenate([gdn_conv_w, ssd_conv_w], axis=1).astype(F32)
    conv_b = jnp.concatenate([jnp.zeros((GDN_CONV_DIM,), F32), ssd_conv_b.astype(F32)])[None]
    proj = _proj_conv(h, w_pad, conv_w, conv_b, t_len, tm=min(cfg["proj_tm"], t_len), tn=cfg["hyb_tn"])
    o_a = _gdn(proj, bsz, t_len, _pad_row(gdn_dt_bias, GDN_V_HEADS),
               _pad_row(gdn_a_log, GDN_V_HEADS), gdn_norm_w[None].astype(F32),
               jnp.asarray(_expand_rows(0, GDN_V_HEADS, GDN_DV)).astype(BF16),
               jnp.asarray(_expand_rows(GDN_V_HEADS, GDN_V_HEADS, GDN_DV)).astype(BF16), tc=cfg["gdn_tc"])
    o_b = _ssd(proj, bsz, t_len, _pad_row(ssd_dt_bias, 16),
               _pad_row(ssd_a_log, 16), jnp.repeat(ssd_d, SSD_HEADDIM)[None], ssd_norm_w[None],
               jnp.asarray(_expand_rows(16, SSD_HEADS, SSD_HEADDIM)).astype(BF16),
               n_sub=cfg["ssd_sub"])
    w_out_b = w_out.astype(BF16)
    return _out_ln([o_a, o_b], [w_out_b[:GDN_V_W], w_out_b[GDN_V_W:]], h, ln_g, ln_b, tm=cfg["out_tm"])


def _nsa_layer(h, pos_row, bsz, t_len, w_in, cmp_pos, cmp_w1, cmp_w2, w_out, ln_g, ln_b, cfg):
    aggt, eselt, freq_col, rope_sel, one_row = _nsa_constants(t_len)
    w_pad = jnp.concatenate(
        [w_in, jnp.zeros((D_MODEL, NSA_PAD - w_in.shape[1]), w_in.dtype)], axis=1).astype(BF16)
    proj = _proj(h, w_pad, tm=cfg["nsa_proj_tm"], tn=cfg["nsa_tn"])
    qc, qr, k16, v16, ksz, vszt, kwz, vwzt = _nsa_prep(
        proj, pos_row, jnp.asarray(freq_col), jnp.asarray(rope_sel).astype(BF16), jnp.asarray(one_row),
        bsz, t_len, tm=cfg["prep_tm"])
    nr = t_len // CMP_STRIDE
    width = CMP_STRIDE * LANES
    k16 = k16.reshape(bsz, nr, width)
    v16 = v16.reshape(bsz, nr, width)
    pos2 = jnp.broadcast_to(cmp_pos[:, :, None, :], (2, CMP_LEN, NSA_GROUPS, NSA_DK)).reshape(2, 2, width)
    w1 = cmp_w1.astype(BF16).reshape(2, 2, CMP_STRIDE, NSA_DK, CMP_HIDDEN)
    z1 = jnp.zeros_like(w1)
    w1x = jnp.stack([jnp.concatenate([w1, z1], axis=-1), jnp.concatenate([z1, w1], axis=-1)],
                    axis=3).reshape(2, 2, width, NSA_GROUPS * CMP_HIDDEN)
    w2 = cmp_w2.astype(BF16)
    z2 = jnp.zeros_like(w2)
    w2x = jnp.stack([jnp.concatenate([w2, z2], axis=-1), jnp.concatenate([z2, w2], axis=-1)],
                    axis=1).reshape(2, NSA_GROUPS * CMP_HIDDEN, NSA_GROUPS * NSA_DK)
    kcz, vczt = _nsa_compress(k16, v16, pos2, w1x[:, 0], w1x[:, 1], w2x)
    o = _nsa_attn(qc, qr, proj, kcz, vczt, ksz.reshape(bsz, t_len, -1), vszt,
                  kwz.reshape(bsz, t_len, -1), vwzt,
                  jnp.asarray(aggt).astype(BF16), jnp.asarray(eselt).astype(BF16),
                  bsz, t_len, kb_size=cfg["attn_kb"])
    return _out_ln([o], [w_out.astype(BF16)], h, ln_g, ln_b, tm=cfg["out_tm"])


def _config(m, t_len):
    return dict(ffn_tm=min(1024, m), ffn_tf=256, proj_tm=min(512, m), hyb_tn=512, nsa_tn=NSA_PAD,
                out_tm=min(1024, m), gdn_tc=min(256, t_len), ssd_sub=min(4, t_len // SSD_CHUNK), prep_tm=min(512, t_len),
                nsa_proj_tm=min(1024, m),
                attn_kb=min(256, t_len))


def kernel(x, positions, ln_g, ln_b, ffn_w_in, ffn_w_out, hyb_w_in, gdn_conv_w, gdn_a_log, gdn_dt_bias, gdn_norm_w, ssd_conv_w, ssd_conv_b, ssd_a_log, ssd_dt_bias, ssd_d, ssd_norm_w, hyb_w_out, nsa_w_in, nsa_cmp_pos, nsa_cmp_w1, nsa_cmp_w2, nsa_w_out):
    bsz, t_len, _ = x.shape
    m = bsz * t_len
    cfg = _config(m, t_len)
    h = x.reshape(m, D_MODEL)
    pos_row = positions.reshape(1, m)
    w_in_b = ffn_w_in.astype(BF16)
    w_out_b = ffn_w_out.astype(BF16)
    ln_g = ln_g[:, :, None, :]
    ln_b = ln_b[:, :, None, :]
    for layer in range(DEPTH):
        i = layer // 2
        h = _ffn_ln(h, w_in_b, w_out_b, ln_g[layer, 0], ln_b[layer, 0], layer, 0,
                    tm=cfg["ffn_tm"], tf=cfg["ffn_tf"])
        if layer % 2 == 0:
            h = _hybrid_layer(h, bsz, t_len, hyb_w_in[i], gdn_conv_w[i], gdn_a_log[i], gdn_dt_bias[i],
                              gdn_norm_w[i], ssd_conv_w[i], ssd_conv_b[i], ssd_a_log[i], ssd_dt_bias[i],
                              ssd_d[i], ssd_norm_w[i], hyb_w_out[i], ln_g[layer, 1], ln_b[layer, 1], cfg)
        else:
            h = _nsa_layer(h, pos_row, bsz, t_len, nsa_w_in[i], nsa_cmp_pos[i], nsa_cmp_w1[i],
                           nsa_cmp_w2[i], nsa_w_out[i], ln_g[layer, 1], ln_b[layer, 1], cfg)
        h = _ffn_ln(h, w_in_b, w_out_b, ln_g[layer, 2], ln_b[layer, 2], layer, 1,
                    tm=cfg["ffn_tm"], tf=cfg["ffn_tf"])
    return h.reshape(bsz, t_len, D_MODEL)
```

```python
import functools

import numpy as np
import jax
import jax.numpy as jnp
from jax import lax
from jax.experimental import pallas as pl
from jax.experimental.pallas import tpu as pltpu

F32 = jnp.float32
BF16 = jnp.bfloat16

D_MODEL = 1024
DEPTH = 4
D_FF = 2816
DEEPNORM_ALPHA = (2.0 * DEPTH) ** 0.25
LN_EPS = 1e-5
RMS_EPS = 1e-6
CONV_K = 4

GDN_QK_HEADS = 4
GDN_V_HEADS = 8
GDN_DK = 128
GDN_DV = 128
GDN_CHUNK = 64
GDN_QK_W = GDN_QK_HEADS * GDN_DK
GDN_V_W = GDN_V_HEADS * GDN_DV
GDN_CONV_DIM = 2 * GDN_QK_W + GDN_V_W

SSD_D_INNER = D_MODEL
SSD_HEADDIM = 64
SSD_HEADS = SSD_D_INNER // SSD_HEADDIM
SSD_GROUPS = 2
SSD_STATE = 128
SSD_CHUNK = 128
SSD_BC_W = SSD_GROUPS * SSD_STATE
SSD_CONV_DIM = SSD_D_INNER + 2 * SSD_BC_W

NSA_HEADS = 16
NSA_GROUPS = 2
NSA_DK = 64
NSA_DV = 64
ROPE_DIM = NSA_DK // 4
ROPE_THETA = 500000.0
CMP_LEN = 32
CMP_STRIDE = 16
CMP_HIDDEN = 256
SEL_LEN = 64
SEL_TOPK = 8
SEL_LOCAL = 2
FORCE_SCORE = 1e4
WINDOW = 512
Q_BLOCK = 128

LANES = 128
SUBLANES = 8
VMEM_LIMIT_BYTES = 48 * 1024 * 1024

HYB_SMALL_OFF = GDN_CONV_DIM + GDN_V_W + SSD_D_INNER + SSD_CONV_DIM
HYB_PAD = HYB_SMALL_OFF + LANES
NSA_GATE_OFF = NSA_HEADS * NSA_DK + 3 * NSA_GROUPS * (NSA_DK + NSA_DV)
NSA_PAD = NSA_GATE_OFF + LANES

NEG_BIG = -1e30
LOG2E = 1.4426950408889634
DENOM_FLOOR = 2.0 ** -100
VT_ROWS = LANES + 16


def _cparams(*sem):
    return pltpu.CompilerParams(dimension_semantics=sem, vmem_limit_bytes=VMEM_LIMIT_BYTES)


def _sigmoid(v):
    return 0.5 + 0.5 * jnp.tanh(0.5 * v)


def _silu(v):
    h = 0.5 * v
    return h + h * jnp.tanh(h)


def _softplus(v):
    return jnp.maximum(v, 0.0) + jnp.log1p(jnp.exp(-jnp.abs(v)))


def _layer_norm(y, g, b):
    mu = jnp.mean(y, axis=-1, keepdims=True)
    d = y - mu
    var = jnp.mean(d * d, axis=-1, keepdims=True)
    return d * lax.rsqrt(var + LN_EPS) * g + b


def _dot(a, b):
    return jnp.dot(a, b, preferred_element_type=F32)


def _split3(x):
    hi = x.astype(BF16)
    r1 = x - hi.astype(F32)
    mid = r1.astype(BF16)
    lo = (r1 - mid.astype(F32)).astype(BF16)
    return hi, mid, lo


def _dot_sel_rhs(x, sel):
    sel = sel.astype(BF16)
    hi, mid, lo = _split3(x)
    return _dot(hi, sel) + _dot(mid, sel) + _dot(lo, sel)


def _dot_sel_lhs(sel, x):
    sel = sel.astype(BF16)
    hi, mid, lo = _split3(x)
    return _dot(sel, hi) + _dot(sel, mid) + _dot(sel, lo)


def _dot_nt(a, b):
    return lax.dot_general(a, b, (((1,), (1,)), ((), ())), preferred_element_type=F32)


def _ffn_ln_kernel(x_ref, wi_ref, wo_ref, g_ref, b_ref, o_ref, *, tf):
    halves = 2 if x_ref.shape[0] % (2 * SUBLANES) == 0 else 1
    hr = x_ref.shape[0] // halves
    xs, hidden = [], []
    for h in range(halves):
        x = x_ref[h * hr:(h + 1) * hr, :]
        xb = x.astype(BF16)
        acts = []
        for c in range(D_FF // tf):
            gate = _dot(xb, wi_ref[:, c * tf:(c + 1) * tf])
            up = _dot(xb, wi_ref[:, D_FF + c * tf:D_FF + (c + 1) * tf])
            acts.append((_silu(gate) * up).astype(BF16))
        xs.append(x)
        hidden.append(jnp.concatenate(acts, axis=1))
    mixes = [_dot(hidden[h], wo_ref[...]) for h in range(halves)]
    for h in range(halves):
        y = DEEPNORM_ALPHA * xs[h] + 0.5 * mixes[h]
        o_ref[h * hr:(h + 1) * hr, :] = _layer_norm(y, g_ref[...], b_ref[...])


def _resident(shape):
    return pl.BlockSpec(shape, lambda *_: (0,) * len(shape), pipeline_mode=pl.Buffered(1))


def _ffn_ln(h, w_in, w_out, g, b, layer, which, *, tm, tf):
    m = h.shape[0]
    pick = lambda i: (layer, which, 0, 0)
    return pl.pallas_call(
        functools.partial(_ffn_ln_kernel, tf=tf),
        grid=(m // tm,),
        in_specs=[
            pl.BlockSpec((tm, D_MODEL), lambda i: (i, 0)),
            pl.BlockSpec((None, None) + w_in.shape[2:], pick, pipeline_mode=pl.Buffered(1)),
            pl.BlockSpec((None, None) + w_out.shape[2:], pick, pipeline_mode=pl.Buffered(1)),
            _resident((1, D_MODEL)),
            _resident((1, D_MODEL)),
        ],
        out_specs=pl.BlockSpec((tm, D_MODEL), lambda i: (i, 0)),
        out_shape=jax.ShapeDtypeStruct((m, D_MODEL), F32),
        compiler_params=_cparams("parallel"),
        name="ffn_ln",
    )(h, w_in, w_out, g, b)


def _proj_kernel(x_ref, w_ref, o_ref, *, tn):
    xb = x_ref[...].astype(BF16)
    for c in range(w_ref.shape[1] // tn):
        o_ref[:, c * tn:(c + 1) * tn] = _dot(xb, w_ref[:, c * tn:(c + 1) * tn])


def _proj(h, w, *, tm, tn):
    m, k = h.shape
    n = w.shape[1]
    return pl.pallas_call(
        functools.partial(_proj_kernel, tn=tn),
        grid=(m // tm,),
        in_specs=[
            pl.BlockSpec((tm, k), lambda i: (i, 0)),
            _resident(w.shape),
        ],
        out_specs=pl.BlockSpec((tm, n), lambda i: (i, 0)),
        out_shape=jax.ShapeDtypeStruct((m, n), F32),
        compiler_params=_cparams("parallel"),
        name="in_proj",
    )(h, w)


def _proj_conv_kernel(x_ref, w_ref, cw_ref, cb_ref, o_ref, hist, *, chunks, tiles_per_seq):
    tm = x_ref.shape[0]

    @pl.when(pl.program_id(0) % tiles_per_seq == 0)
    def _():
        hist[...] = jnp.zeros(hist.shape, F32)

    xb = x_ref[...].astype(BF16)
    row8 = lax.broadcasted_iota(jnp.int32, (SUBLANES, 1), 0)
    for col, width, ccol in chunks:
        y = _dot(xb, w_ref[:, col:col + width])
        if ccol is not None:
            prev = hist[:, ccol:ccol + width]
            acc = cb_ref[:, ccol:ccol + width] + cw_ref[CONV_K - 1:CONV_K, ccol:ccol + width] * y
            for s in range(1, CONV_K):
                y_s = pltpu.roll(y, s, axis=0)
                head = jnp.where(row8 < s, pltpu.roll(prev, s, axis=0), y_s[0:SUBLANES])
                y_s = jnp.concatenate([head, y_s[SUBLANES:]], axis=0)
                acc = acc + cw_ref[CONV_K - 1 - s:CONV_K - s, ccol:ccol + width] * y_s
            hist[:, ccol:ccol + width] = y[tm - SUBLANES:tm]
            y = _silu(acc)
        o_ref[:, col:col + width] = y


def _proj_conv(h, w, cw, cb, t_len, *, tm, tn):
    m, k = h.shape
    n = w.shape[1]
    conv_w = cw.shape[1]
    ssd_lo = GDN_CONV_DIM + GDN_V_W + SSD_D_INNER
    chunks = []
    for col in range(0, n, tn):
        width = min(tn, n - col)
        if col + width <= GDN_CONV_DIM:
            ccol = col
        elif ssd_lo <= col and col + width <= ssd_lo + SSD_CONV_DIM:
            ccol = col - ssd_lo + GDN_CONV_DIM
        else:
            ccol = None
        chunks.append((col, width, ccol))
    return pl.pallas_call(
        functools.partial(_proj_conv_kernel, chunks=tuple(chunks), tiles_per_seq=t_len // tm),
        grid=(m // tm,),
        in_specs=[
            pl.BlockSpec((tm, k), lambda i: (i, 0)),
            _resident(w.shape),
            _resident(cw.shape),
            _resident(cb.shape),
        ],
        out_specs=pl.BlockSpec((tm, n), lambda i: (i, 0)),
        out_shape=jax.ShapeDtypeStruct((m, n), F32),
        scratch_shapes=[pltpu.VMEM((SUBLANES, conv_w), F32)],
        compiler_params=_cparams("arbitrary"),
        name="in_proj_conv",
    )(h, w, cw, cb)


def _out_ln_kernel(*refs, n_parts):
    a_refs = refs[:n_parts]
    w_refs = refs[n_parts:2 * n_parts]
    h_ref, g_ref, b_ref, o_ref = refs[2 * n_parts:]
    mix = _dot(a_refs[0][...], w_refs[0][...])
    for a_ref, w_ref in zip(a_refs[1:], w_refs[1:]):
        mix = mix + _dot(a_ref[...], w_ref[...])
    y = DEEPNORM_ALPHA * h_ref[...] + mix
    o_ref[...] = _layer_norm(y, g_ref[...], b_ref[...])


def _out_ln(parts, weights, h, g, b, *, tm):
    m = h.shape[0]
    n_parts = len(parts)
    in_specs = [pl.BlockSpec((tm, a.shape[1]), lambda i: (i, 0)) for a in parts]
    in_specs += [pl.BlockSpec(w.shape, lambda i: (0, 0)) for w in weights]
    in_specs += [
        pl.BlockSpec((tm, D_MODEL), lambda i: (i, 0)),
        pl.BlockSpec((1, D_MODEL), lambda i: (0, 0)),
        pl.BlockSpec((1, D_MODEL), lambda i: (0, 0)),
    ]
    return pl.pallas_call(
        functools.partial(_out_ln_kernel, n_parts=n_parts),
        grid=(m // tm,),
        in_specs=in_specs,
        out_specs=pl.BlockSpec((tm, D_MODEL), lambda i: (i, 0)),
        out_shape=jax.ShapeDtypeStruct((m, D_MODEL), F32),
        compiler_params=_cparams("parallel"),
        name="out_proj_ln",
    )(*parts, *weights, h, g, b)


def _ssd_kernel(z_ref, x_ref, bm_ref, cm_ref, sm_ref, dtb_ref, alog_ref,
                dskip_ref, nw_ref, eh_ref, o_ref, state, *, n_sub):
    @pl.when(pl.program_id(1) == 0)
    def _():
        state[...] = jnp.zeros_like(state)

    for sub in range(n_sub):
        _ssd_chunk(pl.ds(sub * SSD_CHUNK, SSD_CHUNK), z_ref, x_ref, bm_ref, cm_ref, sm_ref, dtb_ref,
                   alog_ref, dskip_ref, nw_ref, eh_ref, o_ref, state)


def _ssd_chunk(rows, z_ref, x_ref, bm_ref, cm_ref, sm_ref, dtb_ref, alog_ref,
               dskip_ref, nw_ref, eh_ref, o_ref, state):
    L = SSD_CHUNK
    W = SSD_D_INNER
    GW = W // SSD_GROUPS
    N = SSD_STATE

    xs = x_ref[rows, :]
    bm = bm_ref[rows, :].astype(BF16)
    cm = cm_ref[rows, :].astype(BF16)

    lane = lax.broadcasted_iota(jnp.int32, (1, LANES), 1)
    dt_lanes = (lane >= 16) & (lane < 16 + SSD_HEADS)
    dt_full = jnp.where(dt_lanes, _softplus(sm_ref[rows, :] + dtb_ref[...]), 0.0)
    adt_full = dt_full * (-jnp.exp(alog_ref[...]))
    r_i = lax.broadcasted_iota(jnp.int32, (L, L), 0)
    c_i = lax.broadcasted_iota(jnp.int32, (L, L), 1)
    tril = r_i >= c_i
    acs_full = _dot_sel_lhs(jnp.where(tril, 1.0, 0.0), adt_full)
    acs_t = acs_full.T
    eh = eh_ref[...]
    dt_exp = _dot_sel_rhs(dt_full, eh)
    acs_exp = _dot_sel_rhs(acs_full, eh)
    a_last = acs_exp[L - 1:L, :]
    xdt = xs * dt_exp
    xdec = (xdt * jnp.exp(a_last - acs_exp)).astype(BF16)
    e_acs = jnp.exp(acs_exp)
    e_last = jnp.exp(a_last)
    lane_w = lax.broadcasted_iota(jnp.int32, (1, W), 1)
    lo_half = (lane_w & (LANES - 1)) < SSD_HEADDIM
    xdt_b = xdt.astype(BF16)
    xdt_lo = jnp.where(lo_half, xdt_b, jnp.zeros_like(xdt_b))
    xdt_hi = jnp.where(lo_half, jnp.zeros_like(xdt_b), xdt_b)

    heads_per_group = SSD_HEADS // SSD_GROUPS
    y_parts = []
    for g in range(SSD_GROUPS):
        bg = bm[:, g * N:(g + 1) * N]
        cg = cm[:, g * N:(g + 1) * N]
        cb = _dot_nt(cg, bg)
        s_g = state[g]
        y_off = _dot(cg, s_g.astype(BF16)) * e_acs[:, g * GW:(g + 1) * GW]
        for pr in range(heads_per_group // 2):
            y_pair = None
            for var in range(2):
                h = g * heads_per_group + 2 * pr + var
                col = acs_full[:, 16 + h:17 + h]
                row = acs_t[16 + h:17 + h, :]
                seg = jnp.exp(jnp.where(tril, col - row, -jnp.inf))
                mh = (cb * seg).astype(BF16)
                src = xdt_lo if var == 0 else xdt_hi
                lanes0 = (g * heads_per_group + 2 * pr) * SSD_HEADDIM
                d = _dot(mh, src[:, lanes0:lanes0 + LANES])
                y_pair = d if y_pair is None else y_pair + d
            off = pr * LANES
            y_parts.append(y_pair + y_off[:, off:off + LANES])
        upd = lax.dot_general(bg, xdec[:, g * GW:(g + 1) * GW], (((0,), (0,)), ((), ())),
                              preferred_element_type=F32)
        state[g] = s_g * e_last[:, g * GW:(g + 1) * GW] + upd
    y = jnp.concatenate(y_parts, axis=1) + xs * dskip_ref[...]
    y = y * _silu(z_ref[rows, :])
    outs = []
    for g in range(SSD_GROUPS):
        yg = y[:, g * GW:(g + 1) * GW]
        outs.append(yg * lax.rsqrt(jnp.mean(yg * yg, axis=-1, keepdims=True) + RMS_EPS))
    o_ref[rows, :] = (jnp.concatenate(outs, axis=1) * nw_ref[...]).astype(o_ref.dtype)


def _ssd(proj, bsz, t_len, dtb_row, alog_row, dskip_row, nw_row, eh, *, n_sub):
    L = n_sub * SSD_CHUNK
    nc = t_len // L
    W = SSD_D_INNER
    z_blk = (GDN_CONV_DIM + GDN_V_W) // W
    x_blk = (GDN_CONV_DIM + GDN_V_W + W) // W
    b_blk = (GDN_CONV_DIM + GDN_V_W + 2 * W) // SSD_BC_W
    sm_blk = HYB_SMALL_OFF // LANES
    row = lambda b, c: b * nc + c
    const = lambda b, c: (0, 0)
    return pl.pallas_call(
        functools.partial(_ssd_kernel, n_sub=n_sub),
        grid=(bsz, nc),
        in_specs=[
            pl.BlockSpec((L, W), lambda b, c: (row(b, c), z_blk)),
            pl.BlockSpec((L, W), lambda b, c: (row(b, c), x_blk)),
            pl.BlockSpec((L, SSD_BC_W), lambda b, c: (row(b, c), b_blk)),
            pl.BlockSpec((L, SSD_BC_W), lambda b, c: (row(b, c), b_blk + 1)),
            pl.BlockSpec((L, LANES), lambda b, c: (row(b, c), sm_blk)),
            pl.BlockSpec((1, LANES), const),
            pl.BlockSpec((1, LANES), const),
            pl.BlockSpec((1, W), const),
            pl.BlockSpec((1, W), const),
            pl.BlockSpec((LANES, W), const),
        ],
        out_specs=pl.BlockSpec((L, W), lambda b, c: (row(b, c), 0)),
        out_shape=jax.ShapeDtypeStruct((bsz * t_len, W), BF16),
        scratch_shapes=[
            pltpu.VMEM((SSD_GROUPS, SSD_STATE, W // SSD_GROUPS), F32),
        ],
        compiler_params=_cparams("parallel", "arbitrary"),
        name="ssd",
    )(proj, proj, proj, proj, proj, dtb_row, alog_row, dskip_row, nw_row, eh)


def _unit_lower_inverse_minus_eye(a, r_i, c_i):
    n = range(len(a))
    diag8 = (r_i >> 3) == (c_i >> 3)
    x = [jnp.where(diag8, -a[i], 0.0) for i in n]
    x_b = [x[i].astype(BF16) for i in n]
    x2 = [_dot(x_b[i], x_b[i]) for i in n]
    x2_b = [x2[i].astype(BF16) for i in n]
    x4 = [_dot(x2_b[i], x2_b[i]) for i in n]
    e = [x[i] + x2[i] + _dot(x_b[i], x2_b[i]) for i in n]
    e = [e[i] + x4[i] + _dot(e[i].astype(BF16), x4[i].astype(BF16)) for i in n]
    for sh in (3, 4, 5):
        same_pair = (r_i >> (sh + 1)) == (c_i >> (sh + 1))
        lower_left = (((r_i >> sh) & 1) == 1) & (((c_i >> sh) & 1) == 0)
        ms = [jnp.where(same_pair & lower_left, a[i], 0.0) for i in n]
        y = [ms[i] + _dot(ms[i].astype(BF16), e[i].astype(BF16)) for i in n]
        e = [e[i] - y[i] - _dot(e[i].astype(BF16), y[i].astype(BF16)) for i in n]
    return e


def _stack_heads(x, r0, l0):
    return jnp.concatenate([x[r0:r0 + GDN_CHUNK, l0:l0 + GDN_DV],
                            x[r0:r0 + GDN_CHUNK, l0 + GDN_DV:l0 + 2 * GDN_DV]], axis=0)


def _own_head_block(x):
    return jnp.concatenate([x[0:GDN_CHUNK, 0:GDN_DV], x[GDN_CHUNK:, GDN_DV:]], axis=0)


def _gdn_kernel(q_ref, k_ref, v_ref, z_ref, sm_ref, dtb_ref, alog_ref, nw_ref,
                eb_ref, ea_ref, o_ref, state, *, tc):
    C = GDN_CHUNK
    DK = GDN_DK
    DV = GDN_DV
    P = 2 * C

    @pl.when(pl.program_id(1) == 0)
    def _():
        state[...] = jnp.zeros_like(state)

    lane = lax.broadcasted_iota(jnp.int32, (1, LANES), 1)
    sm = sm_ref[...]
    beta_full = jnp.where(lane < GDN_V_HEADS, _sigmoid(sm), 0.0)
    g_lanes = (lane >= GDN_V_HEADS) & (lane < 2 * GDN_V_HEADS)
    g_full = jnp.where(g_lanes, -jnp.exp(alog_ref[...]) * _softplus(sm + dtb_ref[...]), 0.0)
    rt = lax.broadcasted_iota(jnp.int32, (tc, tc), 0)
    ct = lax.broadcasted_iota(jnp.int32, (tc, tc), 1)
    same_chunk_tril = ((rt >> 6) == (ct >> 6)) & (rt >= ct)
    gc_full = _dot_sel_lhs(jnp.where(same_chunk_tril, 1.0, 0.0), g_full)
    b_exp = _dot_sel_rhs(beta_full, eb_ref[...])
    g_exp = _dot_sel_rhs(gc_full, ea_ref[...])
    eg_exp = jnp.exp(g_exp)

    r_i = lax.broadcasted_iota(jnp.int32, (P, P), 0)
    c_i = lax.broadcasted_iota(jnp.int32, (P, P), 1)
    same_head = (r_i >> 6) == (c_i >> 6)
    tril = same_head & (r_i >= c_i)
    strict = same_head & (r_i > c_i)
    head0_rows = lax.broadcasted_iota(jnp.int32, (P, 1), 0) < C
    nw = nw_ref[...]

    n_chunks = tc // C
    units = [(hq, ci) for ci in range(n_chunks) for hq in range(GDN_QK_HEADS)]
    qn, kn = [], []
    for hq in range(GDN_QK_HEADS):
        qh = q_ref[:, hq * DK:(hq + 1) * DK]
        kh = k_ref[:, hq * DK:(hq + 1) * DK]
        qn.append(qh * lax.rsqrt(jnp.sum(qh * qh, axis=-1, keepdims=True) + 1e-6) * (DK ** -0.5))
        kn.append(kh * lax.rsqrt(jnp.sum(kh * kh, axis=-1, keepdims=True) + 1e-6))
    pre = {}
    a_low = []
    for hq, ci in units:
        r0 = ci * C
        l0 = hq * 2 * DV
        q_c = qn[hq][r0:r0 + C]
        k_c = kn[hq][r0:r0 + C]
        q2 = jnp.concatenate([q_c, q_c], axis=0)
        k2 = jnp.concatenate([k_c, k_c], axis=0)
        k2_b = k2.astype(BF16)
        kk = _dot_nt(k2_b, k2_b)
        qk = _dot_nt(q2.astype(BF16), k2_b)
        gcol = _stack_heads(g_exp, r0, l0)
        bcol = _stack_heads(b_exp, r0, l0)
        egc = _stack_heads(eg_exp, r0, l0)
        dmat = jnp.exp(jnp.where(tril, gcol - gcol.T, -jnp.inf))
        a_low.append(jnp.where(strict, kk * bcol * dmat, 0.0))
        g_last = jnp.where(head0_rows, gcol[C - 1:C], gcol[P - 1:P])
        pre[hq, ci] = dict(
            vb=_stack_heads(v_ref, r0, l0) * bcol,
            kb=k2 * (bcol * egc),
            qd=(q2 * egc).astype(BF16),
            attn=(qk * dmat).astype(BF16),
            k_t=k_c.T.astype(BF16),
            v_scale=jnp.exp(g_last - gcol),
            decay_cat=jnp.concatenate([egc[C - 1:C], egc[P - 1:P]], axis=1))
    e_all = _unit_lower_inverse_minus_eye(a_low, r_i, c_i)
    for (hq, ci), e in zip(units, e_all):
        d = pre[hq, ci]
        e_b = e.astype(BF16)
        d["u"] = d["vb"] + _dot(e_b, d["vb"].astype(BF16))
        d["w"] = (d["kb"] + _dot(e_b, d["kb"].astype(BF16))).astype(BF16)

    s_cat = [state[hq] for hq in range(GDN_QK_HEADS)]
    for ci in range(n_chunks):
        r0 = ci * C
        for hq in range(GDN_QK_HEADS):
            d = pre[hq, ci]
            l0 = hq * 2 * DV
            s_b = s_cat[hq].astype(BF16)
            v_new = d["u"] - _own_head_block(_dot(d["w"], s_b))
            o2 = _own_head_block(_dot(d["qd"], s_b)) + _dot(d["attn"], v_new.astype(BF16))
            v_dec = (v_new * d["v_scale"]).astype(BF16)
            v_dec_cat = jnp.concatenate([v_dec[0:C], v_dec[C:]], axis=1)
            s_cat[hq] = s_cat[hq] * d["decay_cat"] + _dot(d["k_t"], v_dec_cat)
            o_n = o2 * lax.rsqrt(jnp.mean(o2 * o2, axis=-1, keepdims=True) + RMS_EPS)
            res = (o_n * nw * _silu(_stack_heads(z_ref, r0, l0))).astype(o_ref.dtype)
            o_ref[r0:r0 + C, l0:l0 + DV] = res[0:C]
            o_ref[r0:r0 + C, l0 + DV:l0 + 2 * DV] = res[C:]
    for hq in range(GDN_QK_HEADS):
        state[hq] = s_cat[hq]


def _gdn(proj, bsz, t_len, dtb_row, alog_row, nw_row, eb, ea, *, tc):
    nt = t_len // tc
    row = lambda b, t: b * nt + t
    const = lambda b, t: (0, 0)
    sm_blk = HYB_SMALL_OFF // LANES
    return pl.pallas_call(
        functools.partial(_gdn_kernel, tc=tc),
        grid=(bsz, nt),
        in_specs=[
            pl.BlockSpec((tc, GDN_QK_W), lambda b, t: (row(b, t), 0)),
            pl.BlockSpec((tc, GDN_QK_W), lambda b, t: (row(b, t), 1)),
            pl.BlockSpec((tc, GDN_V_W), lambda b, t: (row(b, t), 1)),
            pl.BlockSpec((tc, GDN_V_W), lambda b, t: (row(b, t), 2)),
            pl.BlockSpec((tc, LANES), lambda b, t: (row(b, t), sm_blk)),
            pl.BlockSpec((1, LANES), const),
            pl.BlockSpec((1, LANES), const),
            pl.BlockSpec((1, GDN_DV), const),
            pl.BlockSpec((LANES, GDN_V_W), const),
            pl.BlockSpec((LANES, GDN_V_W), const),
        ],
        out_specs=pl.BlockSpec((tc, GDN_V_W), lambda b, t: (row(b, t), 0)),
        out_shape=jax.ShapeDtypeStruct((bsz * t_len, GDN_V_W), BF16),
        scratch_shapes=[
            pltpu.VMEM((GDN_QK_HEADS, GDN_DK, 2 * GDN_DV), F32),
        ],
        compiler_params=_cparams("parallel", "arbitrary"),
        name="gdn",
    )(proj, proj, proj, proj, proj, dtb_row, alog_row, nw_row, eb, ea)


def _group_variants(x):
    lane = lax.broadcasted_iota(jnp.int32, (1, LANES), 1)
    lo = lane < (LANES // 2)
    xr = pltpu.roll(x, LANES // 2, axis=1)
    zero = jnp.zeros_like(x)
    return jnp.concatenate([jnp.where(lo, x, zero), jnp.where(lo, zero, xr),
                            jnp.where(lo, xr, zero), jnp.where(lo, zero, x)], axis=1)


def _values_transposed(v):
    zt = _group_variants(v).T
    extra_r = lax.broadcasted_iota(jnp.int32, (VT_ROWS - LANES, v.shape[0]), 0)
    parts = []
    for var in range(2 * NSA_GROUPS):
        parts.append(zt[var * LANES:(var + 1) * LANES])
        parts.append(jnp.where(extra_r == (var % 2), 1.0, 0.0))
    return jnp.concatenate(parts, axis=0)


def _nsa_prep_kernel(q_ref, kc_ref, vc_ref, ks_ref, vs_ref, kw_ref, vw_ref, pos_ref, freq_ref, sel_ref,
                     one_ref, qc_o, qr_o, k16_o, v16_o, ksz_o, vsz_o, kwz_o, vwz_o):
    scale = NSA_DK ** -0.5 * LOG2E
    rows16 = k16_o.shape[0]
    for src, dst in ((kc_ref, k16_o), (vc_ref, v16_o)):
        for j in range(CMP_STRIDE):
            dst[:, j * LANES:(j + 1) * LANES] = src[pl.ds(j, rows16, stride=CMP_STRIDE), :]
    ang = freq_ref[...] * pos_ref[...].astype(F32)
    half = ROPE_DIM // 2
    tm = ang.shape[1]
    cs = jnp.concatenate([jnp.cos(ang), jnp.sin(ang), jnp.zeros((LANES - 2 * half, tm), F32)], axis=0)
    spread = _dot_sel_rhs(cs.T, sel_ref[...])
    cos = spread[:, 0:LANES] + one_ref[...]
    sin_up = spread[:, LANES:2 * LANES]
    sin_dn = spread[:, 2 * LANES:3 * LANES]

    def rope(x):
        return x * cos + pltpu.roll(x, half, axis=1) * sin_up + pltpu.roll(x, LANES - half, axis=1) * sin_dn

    for p in range(NSA_HEADS * NSA_DK // LANES):
        x = q_ref[:, p * LANES:(p + 1) * LANES]
        qc_o[:, p * LANES:(p + 1) * LANES] = (x * scale).astype(qc_o.dtype)
        qr_o[:, p * LANES:(p + 1) * LANES] = (rope(x) * scale).astype(qr_o.dtype)
    ksz_o[...] = _group_variants(rope(ks_ref[...])).astype(ksz_o.dtype)
    kwz_o[...] = _group_variants(rope(kw_ref[...])).astype(kwz_o.dtype)
    vsz_o[...] = _values_transposed(vs_ref[...]).astype(vsz_o.dtype)
    vwz_o[...] = _values_transposed(vw_ref[...]).astype(vwz_o.dtype)


def _nsa_prep(proj, pos_row, freq_col, rope_sel, one_row, bsz, t_len, *, tm):
    m = proj.shape[0]
    tiles_per_seq = t_len // tm
    seq_t = lambda i: (i // tiles_per_seq, 0, i % tiles_per_seq)
    qw = NSA_HEADS * NSA_DK
    kv0 = qw // LANES
    const = lambda i: (0, 0)
    zw = 4 * LANES
    return pl.pallas_call(
        _nsa_prep_kernel,
        grid=(m // tm,),
        in_specs=[
            pl.BlockSpec((tm, qw), lambda i: (i, 0)),
            pl.BlockSpec((tm, LANES), lambda i: (i, kv0)),
            pl.BlockSpec((tm, LANES), lambda i: (i, kv0 + 1)),
            pl.BlockSpec((tm, LANES), lambda i: (i, kv0 + 2)),
            pl.BlockSpec((tm, LANES), lambda i: (i, kv0 + 3)),
            pl.BlockSpec((tm, LANES), lambda i: (i, kv0 + 4)),
            pl.BlockSpec((tm, LANES), lambda i: (i, kv0 + 5)),
            pl.BlockSpec((1, tm), lambda i: (0, i)),
            pl.BlockSpec(freq_col.shape, const),
            pl.BlockSpec(rope_sel.shape, const),
            pl.BlockSpec((1, LANES), const),
        ],
        out_specs=[
            pl.BlockSpec((tm, qw), lambda i: (i, 0)),
            pl.BlockSpec((tm, qw), lambda i: (i, 0)),
            pl.BlockSpec((tm // CMP_STRIDE, CMP_STRIDE * LANES), lambda i: (i, 0)),
            pl.BlockSpec((tm // CMP_STRIDE, CMP_STRIDE * LANES), lambda i: (i, 0)),
            pl.BlockSpec((tm, zw), lambda i: (i, 0)),
            pl.BlockSpec((None, 4 * VT_ROWS, tm), seq_t),
            pl.BlockSpec((tm, zw), lambda i: (i, 0)),
            pl.BlockSpec((None, 4 * VT_ROWS, tm), seq_t),
        ],
        out_shape=[
            jax.ShapeDtypeStruct((m, qw), BF16),
            jax.ShapeDtypeStruct((m, qw), BF16),
            jax.ShapeDtypeStruct((m // CMP_STRIDE, CMP_STRIDE * LANES), F32),
            jax.ShapeDtypeStruct((m // CMP_STRIDE, CMP_STRIDE * LANES), F32),
            jax.ShapeDtypeStruct((m, zw), BF16),
            jax.ShapeDtypeStruct((bsz, 4 * VT_ROWS, t_len), BF16),
            jax.ShapeDtypeStruct((m, zw), BF16),
            jax.ShapeDtypeStruct((bsz, 4 * VT_ROWS, t_len), BF16),
        ],
        compiler_params=_cparams("parallel"),
        name="nsa_prep",
    )(proj, proj, proj, proj, proj, proj, proj, pos_row, freq_col, rope_sel, one_row)


def _nsa_compress_kernel(k16_ref, v16_ref, pos_ref, w1a_ref, w1b_ref, w2_ref, kcz_o, vcz_o):
    nr = k16_ref.shape[0]
    for idx, (x_ref, o_ref) in enumerate(((k16_ref, kcz_o), (v16_ref, vcz_o))):
        x = x_ref[...]
        h_a = _dot((x + pos_ref[idx, 0:1, :]).astype(BF16), w1a_ref[idx])
        h_b = _dot((x + pos_ref[idx, 1:2, :]).astype(BF16), w1b_ref[idx])
        hid = h_a + pltpu.roll(h_b, nr - 1, axis=0)
        out = _dot(_silu(hid).astype(BF16), w2_ref[idx])
        z = _group_variants(out)
        o_ref[...] = (z.T if idx == 1 else z).astype(o_ref.dtype)


def _nsa_compress(k16, v16, pos_ab, w1a, w1b, w2):
    bsz, nr, width = k16.shape
    hid = NSA_GROUPS * CMP_HIDDEN
    zw = 4 * LANES
    c3 = lambda b: (0, 0, 0)
    return pl.pallas_call(
        _nsa_compress_kernel,
        grid=(bsz,),
        in_specs=[
            pl.BlockSpec((None, nr, width), lambda b: (b, 0, 0)),
            pl.BlockSpec((None, nr, width), lambda b: (b, 0, 0)),
            pl.BlockSpec((2, 2, width), c3),
            pl.BlockSpec((2, width, hid), c3),
            pl.BlockSpec((2, width, hid), c3),
            pl.BlockSpec((2, hid, LANES), c3),
        ],
        out_specs=[
            pl.BlockSpec((None, nr, zw), lambda b: (b, 0, 0)),
            pl.BlockSpec((None, zw, nr), lambda b: (b, 0, 0)),
        ],
        out_shape=[
            jax.ShapeDtypeStruct((bsz, nr, zw), BF16),
            jax.ShapeDtypeStruct((bsz, zw, nr), BF16),
        ],
        compiler_params=_cparams("parallel"),
        name="nsa_compress",
    )(k16, v16, pos_ab, w1a, w1b, w2)


def _flash_branch(q_stacks, kz_ref, vzt_ref, lo, hi, bias_fns, kb_size):
    r_acc = lax.broadcasted_iota(jnp.int32, (VT_ROWS, 1), 0)
    even_rows = (r_acc < LANES // 2) | (r_acc == LANES)
    half_w = 2 * Q_BLOCK
    n_half = q_stacks[0].shape[0] // half_w
    units = [(g, h) for g in range(NSA_GROUPS) for h in range(n_half)]
    q_unit = [q_stacks[g][h * half_w:(h + 1) * half_w] for g, h in units]

    def body(kb, carry):
        m_in, acc = carry
        ks = pl.multiple_of(kb * kb_size, kb_size)
        k_both, vt_both, bias = [], [], []
        for g in range(NSA_GROUPS):
            k0 = 2 * g * LANES
            v0 = 2 * g * VT_ROWS
            k_both.append(jnp.concatenate([kz_ref[pl.ds(ks, kb_size), k0:k0 + LANES],
                                           kz_ref[pl.ds(ks, kb_size), k0 + LANES:k0 + 2 * LANES]], axis=0))
            vt_both.append(jnp.concatenate([vzt_ref[v0:v0 + VT_ROWS, pl.ds(ks, kb_size)],
                                            vzt_ref[v0 + VT_ROWS:v0 + 2 * VT_ROWS, pl.ds(ks, kb_size)]],
                                           axis=1))
            b = bias_fns[g](ks)
            bias.append(jnp.concatenate([b, b], axis=1))
        n_u = len(units)
        s, m_out, alpha_rows, p_cat, pv = {}, {}, {}, {}, {}

        def softmax_stage(u):
            g = units[u][0]
            p_parts, alphas, m_news = [], [], []
            for var in range(2):
                s_v = s[u][var * kb_size:(var + 1) * kb_size] + bias[g]
                m_prev = m_in[u][var]
                m_new = jnp.maximum(m_prev, jnp.max(s_v, axis=0, keepdims=True))
                alphas.append(jnp.exp2(m_prev - m_new))
                p_parts.append(jnp.exp2(s_v - m_new).astype(BF16))
                m_news.append(m_new)
            m_out[u] = tuple(m_news)
            alpha_rows[u] = jnp.where(even_rows, alphas[0], alphas[1])
            p_cat[u] = jnp.concatenate(p_parts, axis=0)

        for u in range(n_u):
            s[u] = _dot_nt(k_both[units[u][0]], q_unit[u])
        for u in range(n_u):
            softmax_stage(u)
        for u in range(n_u):
            pv[u] = _dot(vt_both[units[u][0]], p_cat[u])
        acc_out = tuple(acc[u] * alpha_rows[u] + pv[u] for u in range(n_u))
        return tuple(m_out[u] for u in range(n_u)), acc_out

    m_init = tuple((jnp.full((1, half_w), NEG_BIG, F32),) * 2 for _ in units)
    acc_init = tuple(jnp.zeros((VT_ROWS, half_w), F32) for _ in units)
    _, acc = lax.fori_loop(lo, hi, body, (m_init, acc_init))
    outs = []
    for g in range(NSA_GROUPS):
        a = jnp.concatenate(acc[g * n_half:(g + 1) * n_half], axis=1)
        inv_even = 1.0 / a[LANES:LANES + 1, :]
        inv_odd = 1.0 / a[LANES + 1:LANES + 2, :]
        outs.append(a[0:LANES, :] * jnp.where(r_acc[0:LANES] < LANES // 2, inv_even, inv_odd))
    return outs


def _flash_branch_bounded(q_exts, kz_ref, kext_ref, vzt_ref, lo, hi, bias_fns, last_bias_fns, kb_size):
    r_acc = lax.broadcasted_iota(jnp.int32, (VT_ROWS, 1), 0)
    half_w = 2 * Q_BLOCK
    n_half = q_exts[0].shape[0] // half_w
    units = [(g, h) for g in range(NSA_GROUPS) for h in range(n_half)]
    q_unit = [q_exts[g][h * half_w:(h + 1) * half_w] for g, h in units]
    n_u = len(units)

    def step(kbs, acc, fns):
        work = []
        for kb in kbs:
            ks = pl.multiple_of(jnp.maximum(kb, 0) * kb_size, kb_size)
            k_ext = kext_ref[pl.ds(ks, kb_size), :]
            for g in range(NSA_GROUPS):
                k0 = 2 * g * LANES
                v0 = 2 * g * VT_ROWS
                k_both = jnp.concatenate(
                    [jnp.concatenate([kz_ref[pl.ds(ks, kb_size), k0:k0 + LANES], k_ext], axis=1),
                     jnp.concatenate([kz_ref[pl.ds(ks, kb_size), k0 + LANES:k0 + 2 * LANES], k_ext],
                                     axis=1)], axis=0)
                vt_both = jnp.concatenate([vzt_ref[v0:v0 + VT_ROWS, pl.ds(ks, kb_size)],
                                           vzt_ref[v0 + VT_ROWS:v0 + 2 * VT_ROWS, pl.ds(ks, kb_size)]],
                                          axis=1)
                bias = None
                if fns is not None:
                    b = fns[g](kb, ks)
                    b = jnp.concatenate([b, b], axis=1)
                    bias = jnp.concatenate([b, b], axis=0)
                work += [(u, k_both, vt_both, bias) for u in range(n_u) if units[u][0] == g]
        s = [_dot_nt(k, q_unit[u]) for u, k, _, _ in work]
        p = [jnp.exp2(s_i if w[3] is None else s_i + w[3]).astype(BF16) for s_i, w in zip(s, work)]
        pv = [_dot(w[2], p_i) for p_i, w in zip(p, work)]
        out = list(acc)
        for (u, _, _, _), pv_i in zip(work, pv):
            out[u] = out[u] + pv_i
        return tuple(out)

    acc = tuple(jnp.zeros((VT_ROWS, half_w), F32) for _ in units)
    if isinstance(lo, (list, tuple)):
        acc = step(lo, acc, bias_fns)
    elif last_bias_fns is None:
        acc = lax.fori_loop(lo, hi, lambda kb, a: step([kb], a, bias_fns), acc)
    else:
        acc = lax.fori_loop(lo, hi - 1, lambda kb, a: step([kb], a, bias_fns), acc)
        acc = step([hi - 1], acc, last_bias_fns)
    outs = []
    l_min = None
    for g in range(NSA_GROUPS):
        a = jnp.concatenate(acc[g * n_half:(g + 1) * n_half], axis=1)
        l_even = a[LANES:LANES + 1, :]
        l_odd = a[LANES + 1:LANES + 2, :]
        outs.append(a[0:LANES, :] * jnp.where(r_acc[0:LANES] < LANES // 2, 1.0 / l_even, 1.0 / l_odd))
        l_g = jnp.min(jnp.minimum(l_even, l_odd))
        l_min = l_g if l_min is None else jnp.minimum(l_min, l_g)
    return outs, l_min


def _nsa_attn_kernel(qc_ref, qr_ref, gate_ref, kcz_ref, vczt_ref, ksz_ref, vszt_ref, kwz_ref, vwzt_ref,
                     aggt_ref, eselt_ref, o_ref, bias_ref, kmax_ref,
                     *, t_len, kb_size):
    n_cmp_rows = t_len // CMP_STRIDE
    n_cmp = (t_len - CMP_LEN) // CMP_STRIDE + 1
    n_sel = t_len // SEL_LEN
    n_top = min(SEL_TOPK, n_sel)
    pairs = NSA_HEADS // NSA_GROUPS // 2
    q0 = pl.program_id(1) * Q_BLOCK
    t_row = q0 + lax.broadcasted_iota(jnp.int32, (1, Q_BLOCK), 1)
    t_row_stack = q0 + (lax.broadcasted_iota(jnp.int32, (1, pairs * Q_BLOCK), 1) & (Q_BLOCK - 1))
    blk = lax.broadcasted_iota(jnp.int32, (n_sel, 1), 0)
    c_idx = lax.broadcasted_iota(jnp.int32, (n_cmp_rows, 1), 0)
    cmp_mask = ((c_idx * CMP_STRIDE + (CMP_LEN - 1)) <= t_row_stack) & (c_idx < n_cmp)
    key_pos = lax.broadcasted_iota(jnp.int32, (t_len, 1), 0)
    gate_t = _sigmoid(gate_ref[...]).T
    top_half = lax.broadcasted_iota(jnp.int32, (LANES, 1), 0) < (LANES // 2)
    hi = (q0 + (Q_BLOCK - 1)) // kb_size + 1
    win_lo = jnp.maximum(q0 - (WINDOW - 1), 0) // kb_size

    def win_bias(ks):
        kp = ks + lax.broadcasted_iota(jnp.int32, (kb_size, 1), 0)
        return jnp.where((kp <= t_row) & (kp > t_row - WINDOW), 0.0, NEG_BIG)

    def sel_bias(g):
        return lambda ks: bias_ref[g, pl.ds(ks, kb_size), :]

    @pl.when(pl.program_id(1) == 0)
    def _():
        for br, k_ref in enumerate((ksz_ref, kwz_ref)):
            for g in range(NSA_GROUPS):
                k = k_ref[:, 2 * g * LANES:(2 * g + 1) * LANES].astype(F32)
                n2 = jnp.max(jnp.sum(k * k, axis=1, keepdims=True), axis=0, keepdims=True)
                kmax_ref[br * NSA_GROUPS + g] = jnp.broadcast_to(jnp.sqrt(n2), (SUBLANES, LANES))

    qr_stacks, o_cmps, sels = [], [], []
    for g in range(NSA_GROUPS):
        lanes_g = g * pairs * LANES
        qc_stack = jnp.concatenate(
            [qc_ref[:, lanes_g + p * LANES:lanes_g + (p + 1) * LANES] for p in range(pairs)], axis=0)
        qr_stacks.append(jnp.concatenate(
            [qr_ref[:, lanes_g + p * LANES:lanes_g + (p + 1) * LANES] for p in range(pairs)], axis=0))

        o_cmp = None
        p_sum = None
        for var in range(2):
            c0 = (2 * g + var) * LANES
            s = jnp.where(cmp_mask, _dot_nt(kcz_ref[:, c0:c0 + LANES], qc_stack), -jnp.inf)
            mx = jnp.max(s, axis=0, keepdims=True)
            mx = jnp.where(mx > -jnp.inf, mx, 0.0)
            e = jnp.exp2(s - mx)
            p = e * (1.0 / jnp.maximum(jnp.sum(e, axis=0, keepdims=True), 1e-30))
            d = _dot(vczt_ref[c0:c0 + LANES, :], p.astype(BF16))
            o_cmp = d if o_cmp is None else o_cmp + d
            for pr in range(pairs):
                part = p[:, pr * Q_BLOCK:(pr + 1) * Q_BLOCK]
                p_sum = part if p_sum is None else p_sum + part
        importance = _dot_sel_lhs(aggt_ref[...], p_sum)

        cur = t_row >> 6
        causal_blk = blk <= cur
        forced = (blk == 0) | (causal_blk & (blk > cur - SEL_LOCAL))
        score = jnp.where(forced, FORCE_SCORE, jnp.where(causal_blk, importance, -1.0))
        rank = jnp.zeros((n_sel, Q_BLOCK), F32)
        for jj in range(n_sel):
            row = score[jj:jj + 1, :]
            beats = (row > score) | ((row == score) & (blk > jj))
            rank = rank + jnp.where(beats, 1.0, 0.0)
        sel = jnp.where(rank < n_top, 1.0, 0.0)
        sels.append(jnp.concatenate([sel, jnp.zeros((LANES - n_sel, Q_BLOCK), F32)], axis=0))
        o_cmps.append(o_cmp)

    lane = lax.broadcasted_iota(jnp.int32, (1, LANES), 1)
    blk_pad = lax.broadcasted_iota(jnp.int32, (LANES, 1), 0)
    q_ext_sel, q_ext_win = [], []
    for g in range(NSA_GROUPS):
        x = qr_stacks[g].astype(F32)
        sq = x * x
        tot = jnp.sum(sq, axis=1, keepdims=True)
        even = jnp.sum(jnp.where(lane < LANES // 2, sq, 0.0), axis=1, keepdims=True)
        q_norm = jnp.sqrt(jnp.maximum(even, tot - even))
        sel_c = jnp.where(blk_pad * SEL_LEN <= t_row, sels[g], 0.0).T
        sel_b = jnp.where(lane < n_sel, (sel_c - 1.0) * (-NEG_BIG), 0.0)
        sel_b = jnp.concatenate([sel_b] * pairs, axis=0)
        ext_s = jnp.where(lane == n_sel, -q_norm * kmax_ref[g][0:1, 0:1], sel_b)
        ext_w = jnp.where(lane == n_sel, -q_norm * kmax_ref[NSA_GROUPS + g][0:1, 0:1], 0.0)
        q_ext_sel.append(jnp.concatenate([qr_stacks[g], ext_s.astype(BF16)], axis=1))
        q_ext_win.append(jnp.concatenate([qr_stacks[g], ext_w.astype(BF16)], axis=1))

    sel_kb = min(2 * kb_size, t_len)

    def diag_bias(kb, ks):
        kp = ks + lax.broadcasted_iota(jnp.int32, (sel_kb, 1), 0)
        return jnp.where(kp <= t_row, 0.0, NEG_BIG)

    n_win = WINDOW // kb_size + 1
    win_blocks = [hi - n_win + j for j in range(n_win)]

    def win_bias_fast(kb, ks):
        return jnp.where(kb >= 0, win_bias(ks), NEG_BIG)

    fast_sel, l_sel = _flash_branch_bounded(q_ext_sel, ksz_ref, eselt_ref, vszt_ref, 0,
                                            (q0 + (Q_BLOCK - 1)) // sel_kb + 1, None,
                                            [diag_bias] * NSA_GROUPS, sel_kb)
    fast_win, l_win = _flash_branch_bounded(q_ext_win, kwz_ref, eselt_ref, vwzt_ref, win_blocks, None,
                                            [win_bias_fast] * NSA_GROUPS, None, kb_size)

    def robust():
        for g in range(NSA_GROUPS):
            sel_keys = _dot(eselt_ref[...], sels[g].astype(BF16))
            bias_ref[g] = jnp.where((sel_keys > 0.5) & (key_pos <= t_row), 0.0, NEG_BIG)
        o_s = _flash_branch(qr_stacks, ksz_ref, vszt_ref, 0, hi,
                            [sel_bias(g) for g in range(NSA_GROUPS)], kb_size)
        o_w = _flash_branch(qr_stacks, kwz_ref, vwzt_ref, win_lo, hi, [win_bias] * NSA_GROUPS, kb_size)
        return tuple(o_s) + tuple(o_w)

    o_all = lax.cond(jnp.minimum(l_sel, l_win) > DENOM_FLOOR,
                     lambda: tuple(fast_sel) + tuple(fast_win), robust)
    o_sels, o_wins = o_all[:NSA_GROUPS], o_all[NSA_GROUPS:]

    for g in range(NSA_GROUPS):
        lanes_g = g * pairs * LANES
        for pr in range(pairs):
            h_even = (g * pairs + pr) * 2
            cols = slice(pr * Q_BLOCK, (pr + 1) * Q_BLOCK)
            mixed = None
            for j, o_branch in enumerate((o_cmps[g], o_sels[g], o_wins[g])):
                r_e = h_even * 3 + j
                r_o = r_e + 3
                gate_rows = jnp.where(top_half, gate_t[r_e:r_e + 1, :], gate_t[r_o:r_o + 1, :])
                term = gate_rows * o_branch[:, cols]
                mixed = term if mixed is None else mixed + term
            c0 = lanes_g + pr * LANES
            o_ref[:, c0:c0 + LANES] = mixed.T.astype(o_ref.dtype)


def _nsa_attn(qc, qr, proj, kcz, vczt, ksz, vszt, kwz, vwzt, aggt, eselt, bsz, t_len, *, kb_size):
    nq = t_len // Q_BLOCK
    qw = NSA_HEADS * NSA_DK
    zw = 4 * LANES
    nr = t_len // CMP_STRIDE
    pairs = NSA_HEADS // NSA_GROUPS // 2
    row = lambda b, i: (b * nq + i, 0)
    per_b = lambda b, i: (b, 0, 0)
    return pl.pallas_call(
        functools.partial(_nsa_attn_kernel, t_len=t_len, kb_size=kb_size),
        grid=(bsz, nq),
        in_specs=[
            pl.BlockSpec((Q_BLOCK, qw), row),
            pl.BlockSpec((Q_BLOCK, qw), row),
            pl.BlockSpec((Q_BLOCK, LANES), lambda b, i: (b * nq + i, NSA_GATE_OFF // LANES)),
            pl.BlockSpec((None, nr, zw), per_b),
            pl.BlockSpec((None, zw, nr), per_b),
            pl.BlockSpec((None, t_len, zw), per_b),
            pl.BlockSpec((None, 4 * VT_ROWS, t_len), per_b),
            pl.BlockSpec((None, t_len, zw), per_b),
            pl.BlockSpec((None, 4 * VT_ROWS, t_len), per_b),
            pl.BlockSpec(aggt.shape, lambda b, i: (0, 0)),
            pl.BlockSpec(eselt.shape, lambda b, i: (0, 0)),
        ],
        out_specs=pl.BlockSpec((Q_BLOCK, qw), row),
        out_shape=jax.ShapeDtypeStruct((bsz * t_len, qw), BF16),
        scratch_shapes=[
            pltpu.VMEM((NSA_GROUPS, t_len, Q_BLOCK), F32),
            pltpu.VMEM((2 * NSA_GROUPS, SUBLANES, LANES), F32),
        ],
        compiler_params=_cparams("parallel", "arbitrary"),
        name="nsa_attn",
    )(qc, qr, proj, kcz, vczt, ksz, vszt, kwz, vwzt, aggt, eselt)


def _expand_rows(first_row, n_heads, width):
    e = np.zeros((LANES, n_heads * width), np.float32)
    for h in range(n_heads):
        e[first_row + h, h * width:(h + 1) * width] = 1.0
    return e


def _nsa_constants(t_len):
    n_cmp = (t_len - CMP_LEN) // CMP_STRIDE + 1
    n_sel = t_len // SEL_LEN
    nr = t_len // CMP_STRIDE
    c0 = np.arange(n_cmp)[:, None] * CMP_STRIDE
    s0 = np.arange(n_sel)[None, :] * SEL_LEN
    overlap = np.clip(np.minimum(c0 + CMP_LEN, s0 + SEL_LEN) - np.maximum(c0, s0), 0, None) / CMP_LEN
    aggt = np.zeros((n_sel, nr), np.float32)
    aggt[:, :n_cmp] = overlap.T
    eselt = np.zeros((t_len, LANES), np.float32)
    eselt[np.arange(t_len), np.arange(t_len) // SEL_LEN] = 1.0
    eselt[:, n_sel] = 1.0
    half = ROPE_DIM // 2
    inv_freq = (ROPE_THETA ** (-np.arange(half) / half)).astype(np.float32)
    d = np.arange(LANES) % NSA_DK
    rope_sel = np.zeros((LANES, 3 * LANES), np.float32)
    for lane in range(LANES):
        if d[lane] < ROPE_DIM:
            f = d[lane] % half
            rope_sel[f, lane] = 1.0
            if d[lane] >= half:
                rope_sel[half + f, LANES + lane] = 1.0
            else:
                rope_sel[half + f, 2 * LANES + lane] = -1.0
    one_row = (d >= ROPE_DIM).astype(np.float32)[None]
    return aggt, eselt, inv_freq[:, None], rope_sel, one_row


def _pad_row(vec, offset):
    return jnp.zeros((1, LANES), F32).at[0, offset:offset + vec.shape[0]].set(vec.astype(F32))


def _hybrid_layer(h, bsz, t_len, w_in, gdn_conv_w, gdn_a_log, gdn_dt_bias, gdn_norm_w, ssd_conv_w,
                  ssd_conv_b, ssd_a_log, ssd_dt_bias, ssd_d, ssd_norm_w, w_out, ln_g, ln_b, cfg):
    b_off = GDN_CONV_DIM + GDN_V_W
    z_off = b_off + 2 * GDN_V_HEADS
    dt_off = z_off + SSD_D_INNER + SSD_CONV_DIM
    w_pad = jnp.concatenate(
        [w_in[:, :b_off], w_in[:, z_off:dt_off], w_in[:, b_off:z_off], w_in[:, dt_off:],
         jnp.zeros((D_MODEL, HYB_PAD - w_in.shape[1]), w_in.dtype)], axis=1).astype(BF16)
    conv_w = jnp.concatenate([gdn_conv_w, ssd_conv_w], axis=1).astype(F32)
    conv_b = jnp.concatenate([jnp.zeros((GDN_CONV_DIM,), F32), ssd_conv_b.astype(F32)])[None]
    proj = _proj_conv(h, w_pad, conv_w, conv_b, t_len, tm=min(cfg["proj_tm"], t_len), tn=cfg["hyb_tn"])
    o_a = _gdn(proj, bsz, t_len, _pad_row(gdn_dt_bias, GDN_V_HEADS),
               _pad_row(gdn_a_log, GDN_V_HEADS), gdn_norm_w[None].astype(F32),
               jnp.asarray(_expand_rows(0, GDN_V_HEADS, GDN_DV)).astype(BF16),
               jnp.asarray(_expand_rows(GDN_V_HEADS, GDN_V_HEADS, GDN_DV)).astype(BF16), tc=cfg["gdn_tc"])
    o_b = _ssd(proj, bsz, t_len, _pad_row(ssd_dt_bias, 16),
               _pad_row(ssd_a_log, 16), jnp.repeat(ssd_d, SSD_HEADDIM)[None], ssd_norm_w[None],
               jnp.asarray(_expand_rows(16, SSD_HEADS, SSD_HEADDIM)).astype(BF16),
               n_sub=cfg["ssd_sub"])
    w_out_b = w_out.astype(BF16)
    return _out_ln([o_a, o_b], [w_out_b[:GDN_V_W], w_out_b[GDN_V_W:]], h, ln_g, ln_b, tm=cfg["out_tm"])


def _nsa_layer(h, pos_row, bsz, t_len, w_in, cmp_pos, cmp_w1, cmp_w2, w_out, ln_g, ln_b, cfg):
    aggt, eselt, freq_col, rope_sel, one_row = _nsa_constants(t_len)
    w_pad = jnp.concatenate(
        [w_in, jnp.zeros((D_MODEL, NSA_PAD - w_in.shape[1]), w_in.dtype)], axis=1).astype(BF16)
    proj = _proj(h, w_pad, tm=cfg["nsa_proj_tm"], tn=cfg["nsa_tn"])
    qc, qr, k16, v16, ksz, vszt, kwz, vwzt = _nsa_prep(
        proj, pos_row, jnp.asarray(freq_col), jnp.asarray(rope_sel).astype(BF16), jnp.asarray(one_row),
        bsz, t_len, tm=cfg["prep_tm"])
    nr = t_len // CMP_STRIDE
    width = CMP_STRIDE * LANES
    k16 = k16.reshape(bsz, nr, width)
    v16 = v16.reshape(bsz, nr, width)
    pos2 = jnp.broadcast_to(cmp_pos[:, :, None, :], (2, CMP_LEN, NSA_GROUPS, NSA_DK)).reshape(2, 2, width)
    w1 = cmp_w1.astype(BF16).reshape(2, 2, CMP_STRIDE, NSA_DK, CMP_HIDDEN)
    z1 = jnp.zeros_like(w1)
    w1x = jnp.stack([jnp.concatenate([w1, z1], axis=-1), jnp.concatenate([z1, w1], axis=-1)],
                    axis=3).reshape(2, 2, width, NSA_GROUPS * CMP_HIDDEN)
    w2 = cmp_w2.astype(BF16)
    z2 = jnp.zeros_like(w2)
    w2x = jnp.stack([jnp.concatenate([w2, z2], axis=-1), jnp.concatenate([z2, w2], axis=-1)],
                    axis=1).reshape(2, NSA_GROUPS * CMP_HIDDEN, NSA_GROUPS * NSA_DK)
    kcz, vczt = _nsa_compress(k16, v16, pos2, w1x[:, 0], w1x[:, 1], w2x)
    o = _nsa_attn(qc, qr, proj, kcz, vczt, ksz.reshape(bsz, t_len, -1), vszt,
                  kwz.reshape(bsz, t_len, -1), vwzt,
                  jnp.asarray(aggt).astype(BF16), jnp.asarray(eselt).astype(BF16),
                  bsz, t_len, kb_size=cfg["attn_kb"])
    return _out_ln([o], [w_out.astype(BF16)], h, ln_g, ln_b, tm=cfg["out_tm"])


def _config(m, t_len):
    return dict(ffn_tm=min(1024, m), ffn_tf=256, proj_tm=min(512, m), hyb_tn=512, nsa_tn=NSA_PAD,
                out_tm=min(1024, m), gdn_tc=min(256, t_len), ssd_sub=min(4, t_len // SSD_CHUNK), prep_tm=min(512, t_len),
                nsa_proj_tm=min(1024, m),
                attn_kb=min(256, t_len))


def kernel(x, positions, ln_g, ln_b, ffn_w_in, ffn_w_out, hyb_w_in, gdn_conv_w, gdn_a_log, gdn_dt_bias, gdn_norm_w, ssd_conv_w, ssd_conv_b, ssd_a_log, ssd_dt_bias, ssd_d, ssd_norm_w, hyb_w_out, nsa_w_in, nsa_cmp_pos, nsa_cmp_w1, nsa_cmp_w2, nsa_w_out):
    bsz, t_len, _ = x.shape
    m = bsz * t_len
    cfg = _config(m, t_len)
    h = x.reshape(m, D_MODEL)
    pos_row = positions.reshape(1, m)
    w_in_b = ffn_w_in.astype(BF16)
    w_out_b = ffn_w_out.astype(BF16)
    ln_g = ln_g[:, :, None, :]
    ln_b = ln_b[:, :, None, :]
    for layer in range(DEPTH):
        i = layer // 2
        h = _ffn_ln(h, w_in_b, w_out_b, ln_g[layer, 0], ln_b[layer, 0], layer, 0,
                    tm=cfg["ffn_tm"], tf=cfg["ffn_tf"])
        if layer % 2 == 0:
            h = _hybrid_layer(h, bsz, t_len, hyb_w_in[i], gdn_conv_w[i], gdn_a_log[i], gdn_dt_bias[i],
                              gdn_norm_w[i], ssd_conv_w[i], ssd_conv_b[i], ssd_a_log[i], ssd_dt_bias[i],
                              ssd_d[i], ssd_norm_w[i], hyb_w_out[i], ln_g[layer, 1], ln_b[layer, 1], cfg)
        else:
            h = _nsa_layer(h, pos_row, bsz, t_len, nsa_w_in[i], nsa_cmp_pos[i], nsa_cmp_w1[i],
                           nsa_cmp_w2[i], nsa_w_out[i], ln_g[layer, 1], ln_b[layer, 1], cfg)
        h = _ffn_ln(h, w_in_b, w_out_b, ln_g[layer, 2], ln_b[layer, 2], layer, 1,
                    tm=cfg["ffn_tm"], tf=cfg["ffn_tf"])
    return h.reshape(bsz, t_len, D_MODEL)
```
